```python
import jax, jax.numpy as jnp
from jax import lax
import numpy as np

D_MODEL = 1024
BATCH = 8
SEQ = 8192
DEPTH = 4

D_MIX = D_MODEL
D_POOL = D_MIX // 4
POOL_WINDOWS = (2, 4, 8, 16)
N_POOL_GROUPS = len(POOL_WINDOWS)
POOL_GROUP = D_POOL // N_POOL_GROUPS
D_CONF = 3 * D_MIX // 8
D_SCONV = D_MIX - D_POOL - D_CONF
CONF_KERNEL = 31
SCONV_KERNEL = 3
D_IN = D_POOL + 2 * D_CONF + 3 * D_SCONV
N_MEM = 256
XATTN_HEADS = 4
XATTN_HEAD_DIM = D_MODEL // XATTN_HEADS
D_FF = 2816
FFN_CONV_KERNEL = 3
EPS = 1e-6

kernel_name = "hybrid_pool_conformer_shortconv_trunk"


def rms_norm(x, g):
    x32 = x.astype(jnp.float32)
    y = x32 * lax.rsqrt(jnp.mean(x32 * x32, axis=-1, keepdims=True) + EPS)
    return (y * g.astype(jnp.float32)).astype(x.dtype)


def layer_norm(x, g, b):
    x32 = x.astype(jnp.float32)
    mu = jnp.mean(x32, axis=-1, keepdims=True)
    xc = x32 - mu
    y = xc * lax.rsqrt(jnp.mean(xc * xc, axis=-1, keepdims=True) + EPS)
    return (y * g.astype(jnp.float32) + b.astype(jnp.float32)).astype(x.dtype)


def causal_dwconv(u, w):
    k, c = w.shape
    return lax.conv_general_dilated(
        u, w[:, None, :].astype(u.dtype), window_strides=(1,), padding=[(k - 1, 0)],
        dimension_numbers=("NWC", "WIO", "NWC"), feature_group_count=c)


def pool_mixer(u, maps, scale):
    t_len = u.shape[1]
    u32 = u.astype(jnp.float32)
    cs = jnp.cumsum(u32, axis=1)
    pos1 = jnp.arange(1, t_len + 1, dtype=jnp.int32)
    outs = []
    for g, w in enumerate(POOL_WINDOWS):
        sl = slice(g * POOL_GROUP, (g + 1) * POOL_GROUP)
        cs_g = cs[..., sl]
        prev = jnp.pad(cs_g, ((0, 0), (w, 0), (0, 0)))[:, :t_len]
        count = jnp.minimum(pos1, w).astype(jnp.float32)[None, :, None]
        pooled = ((cs_g - prev) / count - u32[..., sl]).astype(u.dtype)
        outs.append(jnp.einsum("btc,cd->btd", pooled, maps[g]))
    return jnp.concatenate(outs, axis=-1) * scale


def conformer_conv(a, gate, w_dw, b_dw, ln_g, ln_b):
    v = a * jax.nn.sigmoid(gate)
    v = causal_dwconv(v, w_dw) + b_dw
    v = layer_norm(v, ln_g, ln_b)
    return jax.nn.silu(v)


def short_gated_conv(bg, cg, xv, w):
    return bg * causal_dwconv(cg * xv, w)


def cross_attention(h, mem_n, wq, wk, wv, wo):
    b, t, _ = h.shape
    m = mem_n.shape[1]
    q = (h @ wq).reshape(b, t, XATTN_HEADS, XATTN_HEAD_DIM)
    k = (mem_n @ wk).reshape(b, m, XATTN_HEADS, XATTN_HEAD_DIM)
    v = (mem_n @ wv).reshape(b, m, XATTN_HEADS, XATTN_HEAD_DIM)
    s = jnp.einsum("bthd,bmhd->bhtm", q, k).astype(jnp.float32) * (XATTN_HEAD_DIM ** -0.5)
    p = jax.nn.softmax(s, axis=-1).astype(h.dtype)
    o = jnp.einsum("bhtm,bmhd->bthd", p, v).reshape(b, t, XATTN_HEADS * XATTN_HEAD_DIM)
    return o @ wo


def conv_ffn(h, w_up, w_conv, w_down):
    u = causal_dwconv(h @ w_up, w_conv)
    gate, val = jnp.split(u, 2, axis=-1)
    return (jax.nn.silu(gate) * val) @ w_down


def _fwd_setup_inputs(seed: int = 0) -> dict:
    key = jax.random.key(seed)
    ks = jax.random.split(key, 26)
    f32 = jnp.float32

    def nrm(k, shape, scale):
        return jax.random.normal(k, shape, f32) * scale

    def gain(k, shape):
        return 1.0 + 0.05 * jax.random.normal(k, shape, f32)

    L, D = DEPTH, D_MODEL
    return {
        "x": nrm(ks[0], (BATCH, SEQ, D), 1.0),
        "mem": nrm(ks[1], (BATCH, N_MEM, D), 1.0),
        "mem_norm": gain(ks[2], (D,)),
        "mix_pre_norm": gain(ks[3], (L, D)),
        "mix_post_norm": gain(ks[4], (L, D)),
        "w_in": nrm(ks[5], (L, D, D_IN), D ** -0.5),
        "pool_maps": nrm(ks[6], (L, N_POOL_GROUPS, POOL_GROUP, POOL_GROUP), POOL_GROUP ** -0.5),
        "pool_scale": gain(ks[7], (L, D_POOL)),
        "conf_dw_w": nrm(ks[8], (L, CONF_KERNEL, D_CONF), CONF_KERNEL ** -0.5),
        "conf_dw_b": nrm(ks[9], (L, D_CONF), 0.01),
        "conf_ln_g": gain(ks[10], (L, D_CONF)),
        "conf_ln_b": nrm(ks[11], (L, D_CONF), 0.01),
        "sconv_w": nrm(ks[12], (L, SCONV_KERNEL, D_SCONV), SCONV_KERNEL ** -0.5),
        "w_out": nrm(ks[13], (L, D_MIX, D), D_MIX ** -0.5),
        "xattn_pre_norm": gain(ks[14], (L, D)),
        "xattn_post_norm": gain(ks[15], (L, D)),
        "xattn_wq": nrm(ks[16], (L, D, D), D ** -0.5),
        "xattn_wk": nrm(ks[17], (L, D, D), D ** -0.5),
        "xattn_wv": nrm(ks[18], (L, D, D), D ** -0.5),
        "xattn_wo": nrm(ks[19], (L, D, D), D ** -0.5),
        "ffn_pre_norm": gain(ks[20], (L, D)),
        "ffn_post_norm": gain(ks[21], (L, D)),
        "ffn_w_up": nrm(ks[22], (L, D, 2 * D_FF), D ** -0.5),
        "ffn_conv_w": nrm(ks[23], (L, FFN_CONV_KERNEL, 2 * D_FF), FFN_CONV_KERNEL ** -0.5),
        "ffn_w_down": nrm(ks[24], (L, D_FF, D), D_FF ** -0.5),
    }


def _fwd_reference(x, mem, mem_norm, mix_pre_norm, mix_post_norm, w_in, pool_maps, pool_scale,
              conf_dw_w, conf_dw_b, conf_ln_g, conf_ln_b, sconv_w, w_out,
              xattn_pre_norm, xattn_post_norm, xattn_wq, xattn_wk, xattn_wv, xattn_wo,
              ffn_pre_norm, ffn_post_norm, ffn_w_up, ffn_conv_w, ffn_w_down):
    split_at = [D_POOL, D_POOL + D_CONF, D_POOL + 2 * D_CONF,
                D_POOL + 2 * D_CONF + D_SCONV, D_POOL + 2 * D_CONF + 2 * D_SCONV]
    mem_n = rms_norm(mem, mem_norm)
    for l in range(DEPTH):
        h = rms_norm(x, mix_pre_norm[l])
        z = h @ w_in[l]
        zp, za, zg, zb, zc, zx = jnp.split(z, split_at, axis=-1)
        ya = pool_mixer(zp, pool_maps[l], pool_scale[l])
        yb = conformer_conv(za, zg, conf_dw_w[l], conf_dw_b[l], conf_ln_g[l], conf_ln_b[l])
        yc = short_gated_conv(zb, zc, zx, sconv_w[l])
        y = jnp.concatenate([ya, yb, yc], axis=-1) @ w_out[l]
        x = x + rms_norm(y, mix_post_norm[l])
        h = rms_norm(x, xattn_pre_norm[l])
        y = cross_attention(h, mem_n, xattn_wq[l], xattn_wk[l], xattn_wv[l], xattn_wo[l])
        x = x + rms_norm(y, xattn_post_norm[l])
        h = rms_norm(x, ffn_pre_norm[l])
        y = conv_ffn(h, ffn_w_up[l], ffn_conv_w[l], ffn_w_down[l])
        x = x + rms_norm(y, ffn_post_norm[l])
    return x


import jax as _jax
import jax.numpy as _jnp

TWIN_FORMAT = 'train_step'
FWD_PARAMS = ['x', 'mem', 'mem_norm', 'mix_pre_norm', 'mix_post_norm', 'w_in', 'pool_maps', 'pool_scale', 'conf_dw_w', 'conf_dw_b', 'conf_ln_g', 'conf_ln_b', 'sconv_w', 'w_out', 'xattn_pre_norm', 'xattn_post_norm', 'xattn_wq', 'xattn_wk', 'xattn_wv', 'xattn_wo', 'ffn_pre_norm', 'ffn_post_norm', 'ffn_w_up', 'ffn_conv_w', 'ffn_w_down']
TWIN_WEIGHTS = ['mem_norm', 'mix_pre_norm', 'mix_post_norm', 'w_in', 'pool_maps', 'pool_scale', 'conf_dw_w', 'conf_dw_b', 'conf_ln_g', 'conf_ln_b', 'sconv_w', 'w_out', 'xattn_pre_norm', 'xattn_post_norm', 'xattn_wq', 'xattn_wk', 'xattn_wv', 'xattn_wo', 'ffn_pre_norm', 'ffn_post_norm', 'ffn_w_up', 'ffn_conv_w', 'ffn_w_down']
TWIN_DIFF_INPUT = 'x'
TWIN_INPUTS = ['x', 'mem', 'mem_norm', 'mix_pre_norm', 'mix_post_norm', 'w_in', 'pool_maps', 'pool_scale', 'conf_dw_w', 'conf_dw_b', 'conf_ln_g', 'conf_ln_b', 'sconv_w', 'w_out', 'xattn_pre_norm', 'xattn_post_norm', 'xattn_wq', 'xattn_wk', 'xattn_wv', 'xattn_wo', 'ffn_pre_norm', 'ffn_post_norm', 'ffn_w_up', 'ffn_conv_w', 'ffn_w_down', 'loss_target', 'm_mem_norm', 'm_mix_pre_norm', 'm_mix_post_norm', 'm_w_in', 'm_pool_maps', 'm_pool_scale', 'm_conf_dw_w', 'm_conf_dw_b', 'm_conf_ln_g', 'm_conf_ln_b', 'm_sconv_w', 'm_w_out', 'm_xattn_pre_norm', 'm_xattn_post_norm', 'm_xattn_wq', 'm_xattn_wk', 'm_xattn_wv', 'm_xattn_wo', 'm_ffn_pre_norm', 'm_ffn_post_norm', 'm_ffn_w_up', 'm_ffn_conv_w', 'm_ffn_w_down', 'v_mem_norm', 'v_mix_pre_norm', 'v_mix_post_norm', 'v_w_in', 'v_pool_maps', 'v_pool_scale', 'v_conf_dw_w', 'v_conf_dw_b', 'v_conf_ln_g', 'v_conf_ln_b', 'v_sconv_w', 'v_w_out', 'v_xattn_pre_norm', 'v_xattn_post_norm', 'v_xattn_wq', 'v_xattn_wk', 'v_xattn_wv', 'v_xattn_wo', 'v_ffn_pre_norm', 'v_ffn_post_norm', 'v_ffn_w_up', 'v_ffn_conv_w', 'v_ffn_w_down']
TWIN_OUTPUTS = ['loss', 'grad_x', 'grad_mem_norm', 'grad_mix_pre_norm', 'grad_mix_post_norm', 'grad_w_in', 'grad_pool_maps', 'grad_pool_scale', 'grad_conf_dw_w', 'grad_conf_dw_b', 'grad_conf_ln_g', 'grad_conf_ln_b', 'grad_sconv_w', 'grad_w_out', 'grad_xattn_pre_norm', 'grad_xattn_post_norm', 'grad_xattn_wq', 'grad_xattn_wk', 'grad_xattn_wv', 'grad_xattn_wo', 'grad_ffn_pre_norm', 'grad_ffn_post_norm', 'grad_ffn_w_up', 'grad_ffn_conv_w', 'grad_ffn_w_down', 'delta_mem_norm', 'delta_mix_pre_norm', 'delta_mix_post_norm', 'delta_w_in', 'delta_pool_maps', 'delta_pool_scale', 'delta_conf_dw_w', 'delta_conf_dw_b', 'delta_conf_ln_g', 'delta_conf_ln_b', 'delta_sconv_w', 'delta_w_out', 'delta_xattn_pre_norm', 'delta_xattn_post_norm', 'delta_xattn_wq', 'delta_xattn_wk', 'delta_xattn_wv', 'delta_xattn_wo', 'delta_ffn_pre_norm', 'delta_ffn_post_norm', 'delta_ffn_w_up', 'delta_ffn_conv_w', 'delta_ffn_w_down', 'new_m_mem_norm', 'new_m_mix_pre_norm', 'new_m_mix_post_norm', 'new_m_w_in', 'new_m_pool_maps', 'new_m_pool_scale', 'new_m_conf_dw_w', 'new_m_conf_dw_b', 'new_m_conf_ln_g', 'new_m_conf_ln_b', 'new_m_sconv_w', 'new_m_w_out', 'new_m_xattn_pre_norm', 'new_m_xattn_post_norm', 'new_m_xattn_wq', 'new_m_xattn_wk', 'new_m_xattn_wv', 'new_m_xattn_wo', 'new_m_ffn_pre_norm', 'new_m_ffn_post_norm', 'new_m_ffn_w_up', 'new_m_ffn_conv_w', 'new_m_ffn_w_down', 'new_v_mem_norm', 'new_v_mix_pre_norm', 'new_v_mix_post_norm', 'new_v_w_in', 'new_v_pool_maps', 'new_v_pool_scale', 'new_v_conf_dw_w', 'new_v_conf_dw_b', 'new_v_conf_ln_g', 'new_v_conf_ln_b', 'new_v_sconv_w', 'new_v_w_out', 'new_v_xattn_pre_norm', 'new_v_xattn_post_norm', 'new_v_xattn_wq', 'new_v_xattn_wk', 'new_v_xattn_wv', 'new_v_xattn_wo', 'new_v_ffn_pre_norm', 'new_v_ffn_post_norm', 'new_v_ffn_w_up', 'new_v_ffn_conv_w', 'new_v_ffn_w_down']
TWIN_LEAF_KINDS = {'loss': 'loss', 'grad_x': 'grad_x', 'grad_mem_norm': 'grad_w', 'grad_mix_pre_norm': 'grad_w', 'grad_mix_post_norm': 'grad_w', 'grad_w_in': 'grad_w', 'grad_pool_maps': 'grad_w', 'grad_pool_scale': 'grad_w', 'grad_conf_dw_w': 'grad_w', 'grad_conf_dw_b': 'grad_w', 'grad_conf_ln_g': 'grad_w', 'grad_conf_ln_b': 'grad_w', 'grad_sconv_w': 'grad_w', 'grad_w_out': 'grad_w', 'grad_xattn_pre_norm': 'grad_w', 'grad_xattn_post_norm': 'grad_w', 'grad_xattn_wq': 'grad_w', 'grad_xattn_wk': 'grad_w', 'grad_xattn_wv': 'grad_w', 'grad_xattn_wo': 'grad_w', 'grad_ffn_pre_norm': 'grad_w', 'grad_ffn_post_norm': 'grad_w', 'grad_ffn_w_up': 'grad_w', 'grad_ffn_conv_w': 'grad_w', 'grad_ffn_w_down': 'grad_w', 'delta_mem_norm': 'delta_w', 'delta_mix_pre_norm': 'delta_w', 'delta_mix_post_norm': 'delta_w', 'delta_w_in': 'delta_w', 'delta_pool_maps': 'delta_w', 'delta_pool_scale': 'delta_w', 'delta_conf_dw_w': 'delta_w', 'delta_conf_dw_b': 'delta_w', 'delta_conf_ln_g': 'delta_w', 'delta_conf_ln_b': 'delta_w', 'delta_sconv_w': 'delta_w', 'delta_w_out': 'delta_w', 'delta_xattn_pre_norm': 'delta_w', 'delta_xattn_post_norm': 'delta_w', 'delta_xattn_wq': 'delta_w', 'delta_xattn_wk': 'delta_w', 'delta_xattn_wv': 'delta_w', 'delta_xattn_wo': 'delta_w', 'delta_ffn_pre_norm': 'delta_w', 'delta_ffn_post_norm': 'delta_w', 'delta_ffn_w_up': 'delta_w', 'delta_ffn_conv_w': 'delta_w', 'delta_ffn_w_down': 'delta_w', 'new_m_mem_norm': 'new_m', 'new_m_mix_pre_norm': 'new_m', 'new_m_mix_post_norm': 'new_m', 'new_m_w_in': 'new_m', 'new_m_pool_maps': 'new_m', 'new_m_pool_scale': 'new_m', 'new_m_conf_dw_w': 'new_m', 'new_m_conf_dw_b': 'new_m', 'new_m_conf_ln_g': 'new_m', 'new_m_conf_ln_b': 'new_m', 'new_m_sconv_w': 'new_m', 'new_m_w_out': 'new_m', 'new_m_xattn_pre_norm': 'new_m', 'new_m_xattn_post_norm': 'new_m', 'new_m_xattn_wq': 'new_m', 'new_m_xattn_wk': 'new_m', 'new_m_xattn_wv': 'new_m', 'new_m_xattn_wo': 'new_m', 'new_m_ffn_pre_norm': 'new_m', 'new_m_ffn_post_norm': 'new_m', 'new_m_ffn_w_up': 'new_m', 'new_m_ffn_conv_w': 'new_m', 'new_m_ffn_w_down': 'new_m', 'new_v_mem_norm': 'new_v', 'new_v_mix_pre_norm': 'new_v', 'new_v_mix_post_norm': 'new_v', 'new_v_w_in': 'new_v', 'new_v_pool_maps': 'new_v', 'new_v_pool_scale': 'new_v', 'new_v_conf_dw_w': 'new_v', 'new_v_conf_dw_b': 'new_v', 'new_v_conf_ln_g': 'new_v', 'new_v_conf_ln_b': 'new_v', 'new_v_sconv_w': 'new_v', 'new_v_w_out': 'new_v', 'new_v_xattn_pre_norm': 'new_v', 'new_v_xattn_post_norm': 'new_v', 'new_v_xattn_wq': 'new_v', 'new_v_xattn_wk': 'new_v', 'new_v_xattn_wv': 'new_v', 'new_v_xattn_wo': 'new_v', 'new_v_ffn_pre_norm': 'new_v', 'new_v_ffn_post_norm': 'new_v', 'new_v_ffn_w_up': 'new_v', 'new_v_ffn_conv_w': 'new_v', 'new_v_ffn_w_down': 'new_v'}


def _forward(args):
    return _fwd_reference(*[args[k] for k in FWD_PARAMS])


def _output_shape():
    def fwd():
        inp = _fwd_setup_inputs(0)
        return _fwd_reference(*[inp[k] for k in FWD_PARAMS])
    out = _jax.eval_shape(fwd)
    return out.shape, out.dtype

N_MICROBATCH = 1
ADAM_LR = 0.001
ADAM_B1 = 0.9
ADAM_B2 = 0.999
ADAM_EPS = 1e-08
ADAM_WD = 0.01
ADAM_STEP = 10
PER_EXAMPLE_BATCH_AXIS = {'x': 0, 'mem': 0, 'loss_target': 0}
SHARED_INPUTS = []
_WEIGHT_DTYPES = {'mem_norm': _jnp.float32, 'mix_pre_norm': _jnp.float32, 'mix_post_norm': _jnp.float32, 'w_in': _jnp.float32, 'pool_maps': _jnp.float32, 'pool_scale': _jnp.float32, 'conf_dw_w': _jnp.float32, 'conf_dw_b': _jnp.float32, 'conf_ln_g': _jnp.float32, 'conf_ln_b': _jnp.float32, 'sconv_w': _jnp.float32, 'w_out': _jnp.float32, 'xattn_pre_norm': _jnp.float32, 'xattn_post_norm': _jnp.float32, 'xattn_wq': _jnp.float32, 'xattn_wk': _jnp.float32, 'xattn_wv': _jnp.float32, 'xattn_wo': _jnp.float32, 'ffn_pre_norm': _jnp.float32, 'ffn_post_norm': _jnp.float32, 'ffn_w_up': _jnp.float32, 'ffn_conv_w': _jnp.float32, 'ffn_w_down': _jnp.float32}
MOMENT_SCALE = {'mem_norm': 3.622760e+01, 'mix_pre_norm': 4.319396e+00, 'mix_post_norm': 6.354227e+01, 'w_in': 2.997994e+00, 'pool_maps': 3.915906e+00, 'pool_scale': 4.064514e+00, 'conf_dw_w': 4.848692e+00, 'conf_dw_b': 3.675924e+01, 'conf_ln_g': 1.642848e+01, 'conf_ln_b': 2.329390e+01, 'sconv_w': 3.007367e+00, 'w_out': 5.821469e+00, 'xattn_pre_norm': 4.498125e+00, 'xattn_post_norm': 6.685732e+01, 'xattn_wq': 4.579075e+00, 'xattn_wk': 4.662170e+00, 'xattn_wv': 1.746757e+01, 'xattn_wo': 1.761472e+01, 'ffn_pre_norm': 5.609536e+00, 'ffn_post_norm': 6.415316e+01, 'ffn_w_up': 2.287080e+00, 'ffn_conv_w': 2.623628e+00, 'ffn_w_down': 4.731051e+00}


def _to_microbatches(a, axis):
    t = _jnp.moveaxis(a, axis, 0)
    t = t.reshape((N_MICROBATCH, t.shape[0] // N_MICROBATCH) + t.shape[1:])
    return _jnp.moveaxis(t, 1, axis + 1)


def setup_inputs(seed: int = 0) -> dict:
    inp = _fwd_setup_inputs(seed)
    key = _jax.random.fold_in(_jax.random.key(seed), 7919)
    shape, _ = _output_shape()
    out = dict(inp)
    out["loss_target"] = _jax.random.normal(_jax.random.fold_in(key, 0), shape, _jnp.float32)
    for i, name in enumerate(TWIN_WEIGHTS):
        w = inp[name].astype(_jnp.float32)
        if MOMENT_SCALE is None:
            s = _jnp.sqrt(_jnp.mean(_jnp.square(w)) + 1e-30)
        else:
            s = MOMENT_SCALE[name]
        km, kv = _jax.random.split(_jax.random.fold_in(key, i + 1))
        out[name] = w
        out["m_" + name] = s * _jax.random.normal(km, w.shape, _jnp.float32)
        out["v_" + name] = (s * s) * _jax.random.uniform(kv, w.shape, _jnp.float32, 0.5, 1.5)
    if N_MICROBATCH > 1:
        for name, axis in PER_EXAMPLE_BATCH_AXIS.items():
            out[name] = _to_microbatches(out[name], axis)
    return {'x': out['x'], 'mem': out['mem'], 'mem_norm': out['mem_norm'], 'mix_pre_norm': out['mix_pre_norm'], 'mix_post_norm': out['mix_post_norm'], 'w_in': out['w_in'], 'pool_maps': out['pool_maps'], 'pool_scale': out['pool_scale'], 'conf_dw_w': out['conf_dw_w'], 'conf_dw_b': out['conf_dw_b'], 'conf_ln_g': out['conf_ln_g'], 'conf_ln_b': out['conf_ln_b'], 'sconv_w': out['sconv_w'], 'w_out': out['w_out'], 'xattn_pre_norm': out['xattn_pre_norm'], 'xattn_post_norm': out['xattn_post_norm'], 'xattn_wq': out['xattn_wq'], 'xattn_wk': out['xattn_wk'], 'xattn_wv': out['xattn_wv'], 'xattn_wo': out['xattn_wo'], 'ffn_pre_norm': out['ffn_pre_norm'], 'ffn_post_norm': out['ffn_post_norm'], 'ffn_w_up': out['ffn_w_up'], 'ffn_conv_w': out['ffn_conv_w'], 'ffn_w_down': out['ffn_w_down'], 'loss_target': out['loss_target'], 'm_mem_norm': out['m_mem_norm'], 'm_mix_pre_norm': out['m_mix_pre_norm'], 'm_mix_post_norm': out['m_mix_post_norm'], 'm_w_in': out['m_w_in'], 'm_pool_maps': out['m_pool_maps'], 'm_pool_scale': out['m_pool_scale'], 'm_conf_dw_w': out['m_conf_dw_w'], 'm_conf_dw_b': out['m_conf_dw_b'], 'm_conf_ln_g': out['m_conf_ln_g'], 'm_conf_ln_b': out['m_conf_ln_b'], 'm_sconv_w': out['m_sconv_w'], 'm_w_out': out['m_w_out'], 'm_xattn_pre_norm': out['m_xattn_pre_norm'], 'm_xattn_post_norm': out['m_xattn_post_norm'], 'm_xattn_wq': out['m_xattn_wq'], 'm_xattn_wk': out['m_xattn_wk'], 'm_xattn_wv': out['m_xattn_wv'], 'm_xattn_wo': out['m_xattn_wo'], 'm_ffn_pre_norm': out['m_ffn_pre_norm'], 'm_ffn_post_norm': out['m_ffn_post_norm'], 'm_ffn_w_up': out['m_ffn_w_up'], 'm_ffn_conv_w': out['m_ffn_conv_w'], 'm_ffn_w_down': out['m_ffn_w_down'], 'v_mem_norm': out['v_mem_norm'], 'v_mix_pre_norm': out['v_mix_pre_norm'], 'v_mix_post_norm': out['v_mix_post_norm'], 'v_w_in': out['v_w_in'], 'v_pool_maps': out['v_pool_maps'], 'v_pool_scale': out['v_pool_scale'], 'v_conf_dw_w': out['v_conf_dw_w'], 'v_conf_dw_b': out['v_conf_dw_b'], 'v_conf_ln_g': out['v_conf_ln_g'], 'v_conf_ln_b': out['v_conf_ln_b'], 'v_sconv_w': out['v_sconv_w'], 'v_w_out': out['v_w_out'], 'v_xattn_pre_norm': out['v_xattn_pre_norm'], 'v_xattn_post_norm': out['v_xattn_post_norm'], 'v_xattn_wq': out['v_xattn_wq'], 'v_xattn_wk': out['v_xattn_wk'], 'v_xattn_wv': out['v_xattn_wv'], 'v_xattn_wo': out['v_xattn_wo'], 'v_ffn_pre_norm': out['v_ffn_pre_norm'], 'v_ffn_post_norm': out['v_ffn_post_norm'], 'v_ffn_w_up': out['v_ffn_w_up'], 'v_ffn_conv_w': out['v_ffn_conv_w'], 'v_ffn_w_down': out['v_ffn_w_down']}


def _loss(weights, diff, rest, loss_target):
    with _jax.named_scope("forward"):
        args = {**rest, TWIN_DIFF_INPUT: diff, **{k: w.astype(_WEIGHT_DTYPES[k]) for k, w in weights.items()}}
        y = _forward(args)
    with _jax.named_scope("loss_head"):
        err = _jnp.square(y.astype(_jnp.float32) - loss_target)
        return 0.5 * _jnp.sum(_jnp.mean(err, axis=-1)) if err.ndim else 0.5 * err


def _adamw(w, g, m, v):
    m = ADAM_B1 * m + (1.0 - ADAM_B1) * g
    v = ADAM_B2 * v + (1.0 - ADAM_B2) * _jnp.square(g)
    m_hat = m / (1.0 - ADAM_B1 ** ADAM_STEP)
    v_hat = v / (1.0 - ADAM_B2 ** ADAM_STEP)
    delta = -ADAM_LR * (m_hat / (_jnp.sqrt(v_hat) + ADAM_EPS) + ADAM_WD * w)
    return delta, m, v


def reference(x, mem, mem_norm, mix_pre_norm, mix_post_norm, w_in, pool_maps, pool_scale, conf_dw_w, conf_dw_b, conf_ln_g, conf_ln_b, sconv_w, w_out, xattn_pre_norm, xattn_post_norm, xattn_wq, xattn_wk, xattn_wv, xattn_wo, ffn_pre_norm, ffn_post_norm, ffn_w_up, ffn_conv_w, ffn_w_down, loss_target, m_mem_norm, m_mix_pre_norm, m_mix_post_norm, m_w_in, m_pool_maps, m_pool_scale, m_conf_dw_w, m_conf_dw_b, m_conf_ln_g, m_conf_ln_b, m_sconv_w, m_w_out, m_xattn_pre_norm, m_xattn_post_norm, m_xattn_wq, m_xattn_wk, m_xattn_wv, m_xattn_wo, m_ffn_pre_norm, m_ffn_post_norm, m_ffn_w_up, m_ffn_conv_w, m_ffn_w_down, v_mem_norm, v_mix_pre_norm, v_mix_post_norm, v_w_in, v_pool_maps, v_pool_scale, v_conf_dw_w, v_conf_dw_b, v_conf_ln_g, v_conf_ln_b, v_sconv_w, v_w_out, v_xattn_pre_norm, v_xattn_post_norm, v_xattn_wq, v_xattn_wk, v_xattn_wv, v_xattn_wo, v_ffn_pre_norm, v_ffn_post_norm, v_ffn_w_up, v_ffn_conv_w, v_ffn_w_down):
    given = dict(x=x, mem=mem, mem_norm=mem_norm, mix_pre_norm=mix_pre_norm, mix_post_norm=mix_post_norm, w_in=w_in, pool_maps=pool_maps, pool_scale=pool_scale, conf_dw_w=conf_dw_w, conf_dw_b=conf_dw_b, conf_ln_g=conf_ln_g, conf_ln_b=conf_ln_b, sconv_w=sconv_w, w_out=w_out, xattn_pre_norm=xattn_pre_norm, xattn_post_norm=xattn_post_norm, xattn_wq=xattn_wq, xattn_wk=xattn_wk, xattn_wv=xattn_wv, xattn_wo=xattn_wo, ffn_pre_norm=ffn_pre_norm, ffn_post_norm=ffn_post_norm, ffn_w_up=ffn_w_up, ffn_conv_w=ffn_conv_w, ffn_w_down=ffn_w_down, loss_target=loss_target, m_mem_norm=m_mem_norm, m_mix_pre_norm=m_mix_pre_norm, m_mix_post_norm=m_mix_post_norm, m_w_in=m_w_in, m_pool_maps=m_pool_maps, m_pool_scale=m_pool_scale, m_conf_dw_w=m_conf_dw_w, m_conf_dw_b=m_conf_dw_b, m_conf_ln_g=m_conf_ln_g, m_conf_ln_b=m_conf_ln_b, m_sconv_w=m_sconv_w, m_w_out=m_w_out, m_xattn_pre_norm=m_xattn_pre_norm, m_xattn_post_norm=m_xattn_post_norm, m_xattn_wq=m_xattn_wq, m_xattn_wk=m_xattn_wk, m_xattn_wv=m_xattn_wv, m_xattn_wo=m_xattn_wo, m_ffn_pre_norm=m_ffn_pre_norm, m_ffn_post_norm=m_ffn_post_norm, m_ffn_w_up=m_ffn_w_up, m_ffn_conv_w=m_ffn_conv_w, m_ffn_w_down=m_ffn_w_down, v_mem_norm=v_mem_norm, v_mix_pre_norm=v_mix_pre_norm, v_mix_post_norm=v_mix_post_norm, v_w_in=v_w_in, v_pool_maps=v_pool_maps, v_pool_scale=v_pool_scale, v_conf_dw_w=v_conf_dw_w, v_conf_dw_b=v_conf_dw_b, v_conf_ln_g=v_conf_ln_g, v_conf_ln_b=v_conf_ln_b, v_sconv_w=v_sconv_w, v_w_out=v_w_out, v_xattn_pre_norm=v_xattn_pre_norm, v_xattn_post_norm=v_xattn_post_norm, v_xattn_wq=v_xattn_wq, v_xattn_wk=v_xattn_wk, v_xattn_wv=v_xattn_wv, v_xattn_wo=v_xattn_wo, v_ffn_pre_norm=v_ffn_pre_norm, v_ffn_post_norm=v_ffn_post_norm, v_ffn_w_up=v_ffn_w_up, v_ffn_conv_w=v_ffn_conv_w, v_ffn_w_down=v_ffn_w_down)
    weights = {n: given[n] for n in TWIN_WEIGHTS}
    shared = {n: given[n] for n in SHARED_INPUTS}
    per_example = {n: given[n] for n in ['x', 'mem']}
    grad_fn = _jax.value_and_grad(_loss, argnums=(0, 1))

    def one_microbatch(ex, loss_target):
        ex = dict(ex)
        diff = ex.pop(TWIN_DIFF_INPUT)
        return grad_fn(weights, diff, {**shared, **ex}, loss_target)

    if N_MICROBATCH == 1:
        loss, (grad_w, grad_x) = one_microbatch(per_example, given["loss_target"])
    else:
        def body(carry, xs):
            loss_sum, grad_sum = carry
            l_k, (gw_k, gx_k) = one_microbatch(xs[0], xs[1])
            with _jax.named_scope("update"):
                return (loss_sum + l_k, _jax.tree.map(_jnp.add, grad_sum, gw_k)), gx_k

        init = (_jnp.zeros((), _jnp.float32), _jax.tree.map(_jnp.zeros_like, weights))
        (loss, grad_w), grad_x = _jax.lax.scan(body, init, (per_example, given["loss_target"]))
    with _jax.named_scope("update"):
        delta_w, new_m, new_v = {}, {}, {}
        for n in TWIN_WEIGHTS:
            delta_w[n], new_m[n], new_v[n] = _adamw(weights[n], grad_w[n], given["m_" + n], given["v_" + n])
    return (loss, grad_x, *[grad_w[n] for n in TWIN_WEIGHTS], *[delta_w[n] for n in TWIN_WEIGHTS],
            *[new_m[n] for n in TWIN_WEIGHTS], *[new_v[n] for n in TWIN_WEIGHTS])
```

```python
import jax
import jax.numpy as jnp
from jax import lax
from jax.experimental import pallas as pl
from jax.experimental.pallas import tpu as pltpu

F32, BF16 = jnp.float32, jnp.bfloat16
EPS = 1e-6
POOL_WINDOWS = (2, 4, 8, 16)
MAX_WINDOW = 16
CONF_TAPS, SHORT_TAPS = 31, 3
CONF_HALO, POOL_HALO, SHORT_HALO = 32, 16, 8
HEADS = 4
N_DEV = 8
ADAM_LR, ADAM_B1, ADAM_B2, ADAM_EPS, ADAM_WD, ADAM_STEP = 0.001, 0.9, 0.999, 1e-08, 0.01, 10
VMEM_LIMIT_V7X = 56 * 2**20
MESH = pl.DeviceIdType.MESH
ANY = pl.BlockSpec(memory_space=pl.ANY)

TILE_NORM, TILE_MIX, TILE_FFN, TILE_ATTN, TILE_ADAM = 256, 256, 128, 512, 256


def _params(*sem):
    return pltpu.CompilerParams(dimension_semantics=sem, vmem_limit_bytes=VMEM_LIMIT_V7X)


def _sig(x):
    return 1.0 / (1.0 + jnp.exp(-x))


def _rms(x):
    r = lax.rsqrt(jnp.mean(x * x, axis=-1, keepdims=True) + EPS)
    return x * r, r


def _rms_bwd(dout, g, n, r):
    dn = dout * g
    return r * (dn - n * jnp.mean(dn * n, axis=-1, keepdims=True))


def _rows(tt, c):
    return pl.BlockSpec((tt, c), lambda i: (i, 0))


def _whole(shape):
    return pl.BlockSpec(shape, lambda i: (0,) * len(shape))


def _layer(shape, l):
    return pl.BlockSpec((None,) + shape, lambda i: (l,) + (0,) * len(shape))


def _prev_rows(h, c, tt):
    return pl.BlockSpec((h, c), lambda i: (jnp.maximum(i * (tt // h) - 1, 0), 0))


def _next_rows(h, c, tt, t):
    return pl.BlockSpec((h, c), lambda i: (jnp.minimum((i + 1) * (tt // h), t // h - 1), 0))


def _colsum(x):
    return jnp.sum(x, axis=0, keepdims=True)


_DIMS = {"nn": (((1,), (0,)), ((), ())), "nt": (((1,), (1,)), ((), ())), "tn": (((0,), (0,)), ((), ()))}


def _matmul(a, b, mode, out_dtype, name, *, tm, tn, tk, lb=None, a_outer=True):
    bs = b.shape[-2:]
    if mode == "nn":
        (m, k), (k2, n) = a.shape, bs
    elif mode == "nt":
        (m, k), (n, k2) = a.shape, bs
    else:
        (k, m), (k2, n) = a.shape, bs
    assert k == k2, (name, a.shape, b.shape)
    tm, tn, tk = min(tm, m), min(tn, n), min(tk, k)
    assert m % tm == 0 and n % tn == 0 and k % tk == 0, (name, m, n, k, tm, tn, tk)
    gm, gn, gk = m // tm, n // tn, k // tk

    def ij(g0, g1):
        return (g0, g1) if a_outer else (g1, g0)

    def a_map(g0, g1, kk):
        i, _ = ij(g0, g1)
        return (kk, i) if mode == "tn" else (i, kk)

    def b_map(g0, g1, kk):
        _, j = ij(g0, g1)
        idx = (j, kk) if mode == "nt" else (kk, j)
        return idx if lb is None else (lb,) + idx

    def o_map(g0, g1, kk):
        return ij(g0, g1)

    a_block = (tk, tm) if mode == "tn" else (tm, tk)
    b_block = (tn, tk) if mode == "nt" else (tk, tn)
    if lb is not None:
        b_block = (None,) + b_block
    dims = _DIMS[mode]

    def body(a_ref, b_ref, o_ref, *acc):
        p = lax.dot_general(a_ref[...].astype(BF16), b_ref[...].astype(BF16), dims, preferred_element_type=F32)
        if gk == 1:
            o_ref[...] = p.astype(o_ref.dtype)
        else:
            kk = pl.program_id(2)

            @pl.when(kk == 0)
            def _():
                acc[0][...] = p

            @pl.when(kk > 0)
            def _():
                acc[0][...] += p

            @pl.when(kk == gk - 1)
            def _():
                o_ref[...] = acc[0][...].astype(o_ref.dtype)

    return pl.pallas_call(
        body, name=name, grid=(gm, gn, gk) if a_outer else (gn, gm, gk),
        in_specs=[pl.BlockSpec(a_block, a_map), pl.BlockSpec(b_block, b_map)],
        out_specs=pl.BlockSpec((tm, tn), o_map),
        out_shape=jax.ShapeDtypeStruct((m, n), out_dtype),
        scratch_shapes=[pltpu.VMEM((tm, tn), F32)] if gk > 1 else [],
        compiler_params=_params("parallel", "parallel", "arbitrary"),
    )(a, b)


def _prenorm(x, g3, l, name):
    t, d = x.shape
    tt = min(TILE_NORM, t)

    def body(x_ref, g_ref, h_ref):
        n, _ = _rms(x_ref[...])
        h_ref[...] = (n * g_ref[...]).astype(BF16)

    return pl.pallas_call(
        body, name=name, grid=(t // tt,),
        in_specs=[_rows(tt, d), _layer((1, d), l)], out_specs=_rows(tt, d),
        out_shape=jax.ShapeDtypeStruct((t, d), BF16), compiler_params=_params("parallel"),
    )(x, g3)


def _resnorm(x, y, gpost3, l, gnext3, l2, name):
    t, d = x.shape
    tt = min(TILE_NORM, t)

    def body(x_ref, y_ref, gp_ref, gn_ref, xo_ref, h_ref):
        n, _ = _rms(y_ref[...])
        xn = x_ref[...] + n * gp_ref[...]
        xo_ref[...] = xn
        n2, _ = _rms(xn)
        h_ref[...] = (n2 * gn_ref[...]).astype(BF16)

    return pl.pallas_call(
        body, name=name, grid=(t // tt,),
        in_specs=[_rows(tt, d), _rows(tt, d), _layer((1, d), l), _layer((1, d), l2)],
        out_specs=[_rows(tt, d), _rows(tt, d)],
        out_shape=[jax.ShapeDtypeStruct((t, d), F32), jax.ShapeDtypeStruct((t, d), BF16)],
        compiler_params=_params("parallel"),
    )(x, y, gpost3, gnext3)


def _loss_head(x, y, gpost3, l, target, name):
    t, d = x.shape
    tt = min(TILE_NORM, t)

    def body(x_ref, y_ref, g_ref, t_ref, dxn_ref, dy_ref, dg_ref, loss_ref):
        @pl.when(pl.program_id(0) == 0)
        def _():
            dg_ref[...] = jnp.zeros_like(dg_ref)
            loss_ref[...] = jnp.zeros_like(loss_ref)

        g = g_ref[...]
        n, r = _rms(y_ref[...])
        diff = x_ref[...] + n * g - t_ref[...]
        loss_ref[...] += 0.5 * jnp.sum(jnp.mean(diff * diff, axis=-1, keepdims=True))
        dxn = diff * (1.0 / d)
        dxn_ref[...] = dxn
        dy_ref[...] = _rms_bwd(dxn, g, n, r).astype(BF16)
        dg_ref[...] += _colsum(dxn * n)

    return pl.pallas_call(
        body, name=name, grid=(t // tt,),
        in_specs=[_rows(tt, d), _rows(tt, d), _layer((1, d), l), _rows(tt, d)],
        out_specs=[_rows(tt, d), _rows(tt, d), _whole((1, d)), _whole((1, 128))],
        out_shape=[jax.ShapeDtypeStruct((t, d), F32), jax.ShapeDtypeStruct((t, d), BF16),
                   jax.ShapeDtypeStruct((1, d), F32), jax.ShapeDtypeStruct((1, 128), F32)],
        compiler_params=_params("arbitrary"),
    )(x, y, gpost3, target)


def _norm_bwd(dxn, dh, x_in, gpre3, l, name, y_prev=None, gpost3=None, l_prev=None):
    t, d = x_in.shape
    tt = min(TILE_NORM, t)
    has_prev = y_prev is not None

    def body(*refs):
        if has_prev:
            dxn_ref, dh_ref, x_ref, g_ref, y_ref, g2_ref, dx_ref, dg_ref, dy_ref, dg2_ref = refs
        else:
            dxn_ref, dh_ref, x_ref, g_ref, dx_ref, dg_ref = refs

        @pl.when(pl.program_id(0) == 0)
        def _():
            dg_ref[...] = jnp.zeros_like(dg_ref)
            if has_prev:
                dg2_ref[...] = jnp.zeros_like(dg2_ref)

        dh_v = dh_ref[...]
        n, r = _rms(x_ref[...])
        dx = dxn_ref[...] + _rms_bwd(dh_v, g_ref[...], n, r)
        dx_ref[...] = dx
        dg_ref[...] += _colsum(dh_v * n)
        if has_prev:
            n2, r2 = _rms(y_ref[...])
            dy_ref[...] = _rms_bwd(dx, g2_ref[...], n2, r2).astype(BF16)
            dg2_ref[...] += _colsum(dx * n2)

    in_specs = [_rows(tt, d), _rows(tt, d), _rows(tt, d), _layer((1, d), l)]
    out_specs = [_rows(tt, d), _whole((1, d))]
    out_shape = [jax.ShapeDtypeStruct((t, d), F32), jax.ShapeDtypeStruct((1, d), F32)]
    args = [dxn, dh, x_in, gpre3]
    if has_prev:
        in_specs += [_rows(tt, d), _layer((1, d), l_prev)]
        out_specs += [_rows(tt, d), _whole((1, d))]
        out_shape += [jax.ShapeDtypeStruct((t, d), BF16), jax.ShapeDtypeStruct((1, d), F32)]
        args += [y_prev, gpost3]
    return pl.pallas_call(
        body, name=name, grid=(t // tt,), in_specs=in_specs, out_specs=out_specs, out_shape=out_shape,
        compiler_params=_params("arbitrary"),
    )(*args)


def _mix_dims(d):
    dp = d // 4
    dc = 3 * d // 8
    ds = d - dp - dc
    oa, og = dp, dp + dc
    ob = dp + 2 * dc
    oc, ox = ob + ds, ob + 2 * ds
    return dp, dc, ds, oa, og, ob, oc, ox, ox + ds


def _pool_consts(dp):
    win = jnp.repeat(jnp.asarray(POOL_WINDOWS, F32), dp // len(POOL_WINDOWS))[None, :]
    mask = (jnp.arange(MAX_WINDOW, dtype=F32)[:, None] < win).astype(F32)
    return mask, win


def _pooled(pbuf, pmask_ref, zp, wlane, row0, tt):
    acc = pmask_ref[0:1, :] * pbuf[pl.ds(POOL_HALO, tt), :]
    for j in range(1, MAX_WINDOW):
        acc = acc + pmask_ref[j:j + 1, :] * pbuf[pl.ds(POOL_HALO - j, tt), :]
    pos1 = (lax.broadcasted_iota(jnp.int32, zp.shape, 0) + (row0 + 1)).astype(F32)
    cnt = jnp.minimum(pos1, wlane)
    return acc / cnt - zp, cnt


def _taps(w_ref, buf, first, taps, tt):
    acc = w_ref[0:1, :] * buf[pl.ds(first, tt), :]
    for k in range(1, taps):
        acc = acc + w_ref[k:k + 1, :] * buf[pl.ds(first + k, tt), :]
    return acc


def _taps_rev(w_ref, buf, taps, tt):
    acc = w_ref[0:1, :] * buf[pl.ds(taps - 1, tt), :]
    for k in range(1, taps):
        acc = acc + w_ref[k:k + 1, :] * buf[pl.ds(taps - 1 - k, tt), :]
    return acc


def _mix_fwd(z, mbd, pscale3, wdw, bdw3, lng3, lnb3, wsc, l, name):
    t, din = z.shape
    dp, dc, ds, oa, og, ob, oc, ox, din2 = _mix_dims(din * 8 // 17)
    assert din2 == din
    d = dp + dc + ds
    tt = min(TILE_MIX, t)
    pmask, wlane = _pool_consts(dp)

    def body(z_ref, zp_ref, mbd_ref, ps_ref, pmask_ref, wl_ref, wdw_ref, bdw_ref, lng_ref, lnb_ref, wsc_ref,
             cat_ref, c_ref, pbuf, vbuf, sbuf):
        i = pl.program_id(0)
        first = i == 0
        zp = z_ref[:, 0:dp]
        pbuf[0:POOL_HALO, :] = jnp.where(first, 0.0, zp_ref[CONF_HALO - POOL_HALO:CONF_HALO, 0:dp])
        pbuf[POOL_HALO:POOL_HALO + tt, :] = zp
        pooled, _ = _pooled(pbuf, pmask_ref, zp, wl_ref[...], i * tt, tt)
        pm = jnp.dot(pooled.astype(BF16), mbd_ref[...], preferred_element_type=F32)
        cat_ref[:, 0:dp] = (pm * ps_ref[...]).astype(BF16)
        vbuf[0:CONF_HALO, :] = jnp.where(first, 0.0, zp_ref[:, oa:oa + dc] * _sig(zp_ref[:, og:og + dc]))
        vbuf[CONF_HALO:CONF_HALO + tt, :] = z_ref[:, oa:oa + dc] * _sig(z_ref[:, og:og + dc])
        c = _taps(wdw_ref, vbuf, CONF_HALO - CONF_TAPS + 1, CONF_TAPS, tt) + bdw_ref[...]
        c_ref[...] = c
        xc = c - jnp.mean(c, axis=-1, keepdims=True)
        nrm = xc * lax.rsqrt(jnp.mean(xc * xc, axis=-1, keepdims=True) + EPS)
        yln = nrm * lng_ref[...] + lnb_ref[...]
        cat_ref[:, dp:dp + dc] = (yln * _sig(yln)).astype(BF16)
        h0 = CONF_HALO - SHORT_HALO
        sbuf[0:SHORT_HALO, :] = jnp.where(first, 0.0, zp_ref[h0:CONF_HALO, oc:oc + ds] * zp_ref[h0:CONF_HALO, ox:ox + ds])
        sbuf[SHORT_HALO:SHORT_HALO + tt, :] = z_ref[:, oc:oc + ds] * z_ref[:, ox:ox + ds]
        cv = _taps(wsc_ref, sbuf, SHORT_HALO - SHORT_TAPS + 1, SHORT_TAPS, tt)
        cat_ref[:, dp + dc:d] = (z_ref[:, ob:ob + ds] * cv).astype(BF16)

    return pl.pallas_call(
        body, name=name, grid=(t // tt,),
        in_specs=[_rows(tt, din), _prev_rows(CONF_HALO, din, tt), _layer((dp, dp), l), _layer((1, dp), l),
                  _whole((MAX_WINDOW, dp)), _whole((1, dp)), _layer((CONF_TAPS, dc), l), _layer((1, dc), l),
                  _layer((1, dc), l), _layer((1, dc), l), _layer((SHORT_TAPS, ds), l)],
        out_specs=[_rows(tt, d), _rows(tt, dc)],
        out_shape=[jax.ShapeDtypeStruct((t, d), BF16), jax.ShapeDtypeStruct((t, dc), F32)],
        scratch_shapes=[pltpu.VMEM((POOL_HALO + tt, dp), F32), pltpu.VMEM((CONF_HALO + tt, dc), F32),
                        pltpu.VMEM((SHORT_HALO + tt, ds), F32)],
        compiler_params=_params("parallel"),
    )(z, z, mbd, pscale3, pmask, wlane, wdw, bdw3, lng3, lnb3, wsc)


def _mix_bwd(dcat, z, c, mbd, pscale3, wdw, lng3, lnb3, wsc, l, name):
    t, din = z.shape
    dp, dc, ds, oa, og, ob, oc, ox, _ = _mix_dims(din * 8 // 17)
    d = dp + dc + ds
    tt = min(TILE_MIX, t)
    nt = t // tt
    hh = CONF_HALO
    pmask, wlane = _pool_consts(dp)

    def body(dcat_ref, dcatn_ref, z_ref, zp_ref, zn_ref, c_ref, cn_ref, mbd_ref, ps_ref, pmask_ref, wl_ref,
             wdw_ref, lng_ref, lnb_ref, wsc_ref,
             dz_ref, dmbd_ref, dps_ref, dwdw_ref, dbdw_ref, dlng_ref, dlnb_ref, dwsc_ref,
             pbuf, qbuf, vbuf, dcbuf, sbuf, dsbuf):
        i = pl.program_id(0)
        first, last = i == 0, i == nt - 1

        @pl.when(first)
        def _():
            for ref in (dmbd_ref, dps_ref, dwdw_ref, dbdw_ref, dlng_ref, dlnb_ref, dwsc_ref):
                ref[...] = jnp.zeros_like(ref)

        mbd_v, ps, wl = mbd_ref[...], ps_ref[...], wl_ref[...]
        zp = z_ref[:, 0:dp]
        pbuf[0:POOL_HALO, :] = jnp.where(first, 0.0, zp_ref[hh - POOL_HALO:hh, 0:dp])
        pbuf[POOL_HALO:POOL_HALO + tt, :] = zp
        pooled, cnt = _pooled(pbuf, pmask_ref, zp, wl, i * tt, tt)
        pb = pooled.astype(BF16)
        pm = jnp.dot(pb, mbd_v, preferred_element_type=F32)
        dya = dcat_ref[:, 0:dp]
        dya_n = jnp.where(last, 0.0, dcatn_ref[0:POOL_HALO, 0:dp])
        dps_ref[...] += _colsum(dya * pm)
        dpm = (dya * ps).astype(BF16)
        dpm_n = (dya_n * ps).astype(BF16)
        dmbd_ref[...] += lax.dot_general(pb, dpm, _DIMS["tn"], preferred_element_type=F32)
        dpool = lax.dot_general(dpm, mbd_v, _DIMS["nt"], preferred_element_type=F32)
        dpool_n = lax.dot_general(dpm_n, mbd_v, _DIMS["nt"], preferred_element_type=F32)
        pos1_n = (lax.broadcasted_iota(jnp.int32, (POOL_HALO, dp), 0) + ((i + 1) * tt + 1)).astype(F32)
        qbuf[0:tt, :] = dpool / cnt
        qbuf[tt:tt + POOL_HALO, :] = dpool_n / jnp.minimum(pos1_n, wl)
        dzp = pmask_ref[0:1, :] * qbuf[pl.ds(0, tt), :]
        for j in range(1, MAX_WINDOW):
            dzp = dzp + pmask_ref[j:j + 1, :] * qbuf[pl.ds(j, tt), :]
        dz_ref[:, 0:dp] = (dzp - dpool).astype(BF16)

        lng, lnb = lng_ref[...], lnb_ref[...]

        def ln_silu_bwd(cc, dyb):
            xc = cc - jnp.mean(cc, axis=-1, keepdims=True)
            rstd = lax.rsqrt(jnp.mean(xc * xc, axis=-1, keepdims=True) + EPS)
            nrm = xc * rstd
            yln = nrm * lng + lnb
            s = _sig(yln)
            dyln = dyb * (s * (1.0 + yln * (1.0 - s)))
            dn = dyln * lng
            dcc = rstd * (dn - jnp.mean(dn, axis=-1, keepdims=True) - nrm * jnp.mean(dn * nrm, axis=-1, keepdims=True))
            return dcc, dyln, nrm

        dcc, dyln, nrm = ln_silu_bwd(c_ref[...], dcat_ref[:, dp:dp + dc])
        dlng_ref[...] += _colsum(dyln * nrm)
        dlnb_ref[...] += _colsum(dyln)
        dbdw_ref[...] += _colsum(dcc)
        dcc_n, _, _ = ln_silu_bwd(cn_ref[...], dcatn_ref[:, dp:dp + dc])
        dcbuf[0:tt, :] = dcc
        dcbuf[tt:tt + hh, :] = jnp.where(last, 0.0, dcc_n)
        za = z_ref[:, oa:oa + dc]
        sg = _sig(z_ref[:, og:og + dc])
        vbuf[0:hh, :] = jnp.where(first, 0.0, zp_ref[:, oa:oa + dc] * _sig(zp_ref[:, og:og + dc]))
        vbuf[hh:hh + tt, :] = za * sg
        dv = _taps_rev(wdw_ref, dcbuf, CONF_TAPS, tt)
        for k in range(CONF_TAPS):
            dwdw_ref[k:k + 1, :] += _colsum(dcc * vbuf[pl.ds(hh - CONF_TAPS + 1 + k, tt), :])
        dz_ref[:, oa:oa + dc] = (dv * sg).astype(BF16)
        dz_ref[:, og:og + dc] = (dv * za * sg * (1.0 - sg)).astype(BF16)

        zb, zc, zx = z_ref[:, ob:ob + ds], z_ref[:, oc:oc + ds], z_ref[:, ox:ox + ds]
        h0 = hh - SHORT_HALO
        sbuf[0:SHORT_HALO, :] = jnp.where(first, 0.0, zp_ref[h0:hh, oc:oc + ds] * zp_ref[h0:hh, ox:ox + ds])
        sbuf[SHORT_HALO:SHORT_HALO + tt, :] = zc * zx
        cv = _taps(wsc_ref, sbuf, SHORT_HALO - SHORT_TAPS + 1, SHORT_TAPS, tt)
        dyc = dcat_ref[:, dp + dc:d]
        dz_ref[:, ob:ob + ds] = (dyc * cv).astype(BF16)
        dcv = dyc * zb
        dsbuf[0:tt, :] = dcv
        dsbuf[tt:tt + SHORT_HALO, :] = jnp.where(
            last, 0.0, dcatn_ref[0:SHORT_HALO, dp + dc:d] * zn_ref[0:SHORT_HALO, ob:ob + ds])
        dpv = _taps_rev(wsc_ref, dsbuf, SHORT_TAPS, tt)
        for k in range(SHORT_TAPS):
            dwsc_ref[k:k + 1, :] += _colsum(dcv * sbuf[pl.ds(SHORT_HALO - SHORT_TAPS + 1 + k, tt), :])
        dz_ref[:, oc:oc + ds] = (dpv * zx).astype(BF16)
        dz_ref[:, ox:ox + ds] = (dpv * zc).astype(BF16)

    return pl.pallas_call(
        body, name=name, grid=(nt,),
        in_specs=[_rows(tt, d), _next_rows(hh, d, tt, t),
                  _rows(tt, din), _prev_rows(hh, din, tt), _next_rows(hh, din, tt, t),
                  _rows(tt, dc), _next_rows(hh, dc, tt, t),
                  _layer((dp, dp), l), _layer((1, dp), l), _whole((MAX_WINDOW, dp)), _whole((1, dp)),
                  _layer((CONF_TAPS, dc), l), _layer((1, dc), l), _layer((1, dc), l), _layer((SHORT_TAPS, ds), l)],
        out_specs=[_rows(tt, din), _whole((dp, dp)), _whole((1, dp)), _whole((CONF_TAPS, dc)), _whole((1, dc)),
                   _whole((1, dc)), _whole((1, dc)), _whole((SHORT_TAPS, ds))],
        out_shape=[jax.ShapeDtypeStruct((t, din), BF16), jax.ShapeDtypeStruct((dp, dp), F32),
                   jax.ShapeDtypeStruct((1, dp), F32), jax.ShapeDtypeStruct((CONF_TAPS, dc), F32),
                   jax.ShapeDtypeStruct((1, dc), F32), jax.ShapeDtypeStruct((1, dc), F32),
                   jax.ShapeDtypeStruct((1, dc), F32), jax.ShapeDtypeStruct((SHORT_TAPS, ds), F32)],
        scratch_shapes=[pltpu.VMEM((POOL_HALO + tt, dp), F32), pltpu.VMEM((tt + POOL_HALO, dp), F32),
                        pltpu.VMEM((hh + tt, dc), F32), pltpu.VMEM((tt + hh, dc), F32),
                        pltpu.VMEM((SHORT_HALO + tt, ds), F32), pltpu.VMEM((tt + SHORT_HALO, ds), F32)],
        compiler_params=_params("arbitrary"),
    )(dcat, dcat, z, z, z, c, c, mbd, pscale3, pmask, wlane, wdw, lng3, lnb3, wsc)


def _softmax_rows(qh, kh, scale):
    s = lax.dot_general(qh, kh, _DIMS["nt"], preferred_element_type=F32) * scale
    e = jnp.exp(s - jnp.max(s, axis=-1, keepdims=True))
    return e / jnp.sum(e, axis=-1, keepdims=True)


def _attn_fwd(q, k, v, name):
    t, d = q.shape
    m = k.shape[0]
    hd = d // HEADS
    scale = hd ** -0.5
    tt = min(TILE_ATTN, t)

    def body(q_ref, k_ref, v_ref, o_ref):
        for h in range(HEADS):
            sl = slice(h * hd, (h + 1) * hd)
            p = _softmax_rows(q_ref[:, sl], k_ref[:, sl], scale)
            o_ref[:, sl] = jnp.dot(p.astype(BF16), v_ref[:, sl], preferred_element_type=F32).astype(BF16)

    return pl.pallas_call(
        body, name=name, grid=(t // tt,),
        in_specs=[_rows(tt, d), _whole((m, d)), _whole((m, d))], out_specs=_rows(tt, d),
        out_shape=jax.ShapeDtypeStruct((t, d), BF16), compiler_params=_params("parallel"),
    )(q, k, v)


def _attn_bwd(q, k, v, do, name):
    t, d = q.shape
    m = k.shape[0]
    hd = d // HEADS
    scale = hd ** -0.5
    tt = min(TILE_ATTN, t)

    def body(q_ref, k_ref, v_ref, do_ref, dq_ref, dk_ref, dv_ref):
        @pl.when(pl.program_id(0) == 0)
        def _():
            dk_ref[...] = jnp.zeros_like(dk_ref)
            dv_ref[...] = jnp.zeros_like(dv_ref)

        for h in range(HEADS):
            sl = slice(h * hd, (h + 1) * hd)
            qh, kh, vh, doh = q_ref[:, sl], k_ref[:, sl], v_ref[:, sl], do_ref[:, sl]
            p = _softmax_rows(qh, kh, scale)
            dv_ref[:, sl] += lax.dot_general(p.astype(BF16), doh, _DIMS["tn"], preferred_element_type=F32)
            dp = lax.dot_general(doh, vh, _DIMS["nt"], preferred_element_type=F32)
            ds = (p * (dp - jnp.sum(dp * p, axis=-1, keepdims=True)) * scale).astype(BF16)
            dq_ref[:, sl] = jnp.dot(ds, kh, preferred_element_type=F32).astype(BF16)
            dk_ref[:, sl] += lax.dot_general(ds, qh, _DIMS["tn"], preferred_element_type=F32)

    return pl.pallas_call(
        body, name=name, grid=(t // tt,),
        in_specs=[_rows(tt, d), _whole((m, d)), _whole((m, d)), _rows(tt, d)],
        out_specs=[_rows(tt, d), _whole((m, d)), _whole((m, d))],
        out_shape=[jax.ShapeDtypeStruct((t, d), BF16), jax.ShapeDtypeStruct((m, d), F32),
                   jax.ShapeDtypeStruct((m, d), F32)],
        compiler_params=_params("arbitrary"),
    )(q, k, v, do)


def _ffn_chunks(f):
    return [(0, f // 2), (f // 2, f // 2)] if (f // 2) % 128 == 0 else [(0, f)]


def _ffn_act_fwd(u, wc, l, name):
    t, f2 = u.shape
    f = f2 // 2
    tt = min(TILE_FFN, t)
    hs = SHORT_HALO

    def body(u_ref, up_ref, wc_ref, a_ref, ubuf):
        first = pl.program_id(0) == 0
        ubuf[0:hs, :] = jnp.where(first, 0.0, up_ref[...])
        ubuf[hs:hs + tt, :] = u_ref[...]
        for c0, cw in _ffn_chunks(f):
            def conv(off):
                acc = wc_ref[0:1, off:off + cw] * ubuf[pl.ds(hs - 2, tt), off:off + cw]
                for kk in range(1, SHORT_TAPS):
                    acc = acc + wc_ref[kk:kk + 1, off:off + cw] * ubuf[pl.ds(hs - 2 + kk, tt), off:off + cw]
                return acc
            g, vv = conv(c0), conv(f + c0)
            a_ref[:, c0:c0 + cw] = (g * _sig(g) * vv).astype(BF16)

    return pl.pallas_call(
        body, name=name, grid=(t // tt,),
        in_specs=[_rows(tt, f2), _prev_rows(hs, f2, tt), _layer((SHORT_TAPS, f2), l)],
        out_specs=_rows(tt, f), out_shape=jax.ShapeDtypeStruct((t, f), BF16),
        scratch_shapes=[pltpu.VMEM((hs + tt, f2), F32)], compiler_params=_params("parallel"),
    )(u, u, wc)


def _ffn_act_bwd(u, da, wc, l, name):
    t, f2 = u.shape
    f = f2 // 2
    tt = min(TILE_FFN, t)
    nt = t // tt
    hs = SHORT_HALO
    n = tt + hs

    def body(u_ref, up_ref, un_ref, da_ref, dan_ref, wc_ref, du_ref, dwc_ref, ubuf, dabuf, dbuf):
        i = pl.program_id(0)
        first, last = i == 0, i == nt - 1

        @pl.when(first)
        def _():
            dwc_ref[...] = jnp.zeros_like(dwc_ref)

        ubuf[0:hs, :] = jnp.where(first, 0.0, up_ref[...])
        ubuf[hs:hs + tt, :] = u_ref[...]
        ubuf[hs + tt:hs + tt + hs, :] = jnp.where(last, 0.0, un_ref[...])
        dabuf[0:tt, :] = da_ref[...]
        dabuf[tt:n, :] = jnp.where(last, 0.0, dan_ref[...])
        for c0, cw in _ffn_chunks(f):
            def conv(off):
                acc = wc_ref[0:1, off:off + cw] * ubuf[pl.ds(hs - 2, n), off:off + cw]
                for kk in range(1, SHORT_TAPS):
                    acc = acc + wc_ref[kk:kk + 1, off:off + cw] * ubuf[pl.ds(hs - 2 + kk, n), off:off + cw]
                return acc
            g, vv = conv(c0), conv(f + c0)
            dav = dabuf[:, c0:c0 + cw]
            sg = _sig(g)
            dbuf[:, c0:c0 + cw] = dav * vv * (sg * (1.0 + g * (1.0 - sg)))
            dbuf[:, f + c0:f + c0 + cw] = dav * (g * sg)
            for off in (c0, f + c0):
                acc = wc_ref[0:1, off:off + cw] * dbuf[pl.ds(2, tt), off:off + cw]
                for kk in range(1, SHORT_TAPS):
                    acc = acc + wc_ref[kk:kk + 1, off:off + cw] * dbuf[pl.ds(2 - kk, tt), off:off + cw]
                du_ref[:, off:off + cw] = acc.astype(BF16)
                for kk in range(SHORT_TAPS):
                    dwc_ref[kk:kk + 1, off:off + cw] += _colsum(
                        dbuf[pl.ds(0, tt), off:off + cw] * ubuf[pl.ds(hs - 2 + kk, tt), off:off + cw])

    return pl.pallas_call(
        body, name=name, grid=(nt,),
        in_specs=[_rows(tt, f2), _prev_rows(hs, f2, tt), _next_rows(hs, f2, tt, t),
                  _rows(tt, f), _next_rows(hs, f, tt, t), _layer((SHORT_TAPS, f2), l)],
        out_specs=[_rows(tt, f2), _whole((SHORT_TAPS, f2))],
        out_shape=[jax.ShapeDtypeStruct((t, f2), BF16), jax.ShapeDtypeStruct((SHORT_TAPS, f2), F32)],
        scratch_shapes=[pltpu.VMEM((hs + tt + hs, f2), F32), pltpu.VMEM((n, f), F32), pltpu.VMEM((n, f2), F32)],
        compiler_params=_params("arbitrary"),
    )(u, u, u, da, da, wc)


def _place():
    return lax.axis_index("x"), lax.axis_index("y"), lax.axis_index("c")


def _flip(v, bit):
    return 1 - v if bit else v


def _allgather(shards, name):
    nt = len(shards)

    def body(*refs):
        srcs, outs = refs[:nt], refs[nt:2 * nt]
        send_sems, recv_sems, local_sems = refs[2 * nt:]
        x, y, c = _place()
        me, sibling = (x, y, c), (x, y, 1 - c)
        chips = [(1 - x, y), (x, 1 - y), (1 - x, 1 - y)]

        def rows(ti, px, py, pc):
            r = srcs[ti].shape[1]
            return outs[ti].at[:, pl.ds((4 * px + 2 * py + pc) * r, r), :]

        def copy(ti, kk, block, to, src=None):
            return pltpu.make_async_remote_copy(
                src_ref=rows(ti, *block) if src is None else src, dst_ref=rows(ti, *block),
                send_sem=send_sems.at[ti, kk], recv_sem=recv_sems.at[ti, kk], device_id=to, device_id_type=MESH)

        mine = [pltpu.make_async_copy(srcs[ti], rows(ti, *me), local_sems.at[ti]) for ti in range(nt)]
        for cp in mine:
            cp.start()
        first = []
        for ti in range(nt):
            first.append(copy(ti, 0, me, sibling, src=srcs[ti]))
            first += [copy(ti, 1 + j, me, (*chip, c), src=srcs[ti]) for j, chip in enumerate(chips)]
        for cp in first:
            cp.start()
        passed = []
        for j, chip in enumerate(chips):
            for ti in range(nt):
                copy(ti, 1 + j, (*chip, c), me).wait_recv()
                fwd = copy(ti, 4 + j, (*chip, c), sibling)
                fwd.start()
                passed.append(fwd)
        for ti in range(nt):
            copy(ti, 0, sibling, me).wait_recv()
            for j, chip in enumerate(chips):
                copy(ti, 4 + j, (*chip, 1 - c), me).wait_recv()
        for cp in first + passed:
            cp.wait_send()
        for cp in mine:
            cp.wait()

    return pl.pallas_call(
        body, name=name,
        in_specs=[ANY] * nt, out_specs=[ANY] * nt,
        out_shape=[jax.ShapeDtypeStruct((s.shape[0], N_DEV * s.shape[1], s.shape[2]), s.dtype) for s in shards],
        scratch_shapes=[pltpu.SemaphoreType.DMA((nt, 7)), pltpu.SemaphoreType.DMA((nt, 7)),
                        pltpu.SemaphoreType.DMA((nt,))],
    )(*shards)


def _scatter_grads(grads, recvs, l, name):
    nt = len(grads)

    def body(*refs):
        gs, outs = refs[:nt], refs[2 * nt:3 * nt]
        send_sems, recv_sems, local_sems = refs[3 * nt:]
        x, y, c = _place()
        me = 4 * x + 2 * y + c

        def copies(ti):
            r = gs[ti].shape[0] // N_DEV
            out = []
            for kk in range(1, N_DEV):
                px, py, pc = _flip(x, kk & 4), _flip(y, kk & 2), _flip(c, kk & 1)
                peer = 4 * px + 2 * py + pc
                out.append((pltpu.make_async_remote_copy(
                    src_ref=gs[ti].at[pl.ds(peer * r, r), :], dst_ref=outs[ti].at[l, me],
                    send_sem=send_sems.at[ti, kk - 1], recv_sem=recv_sems.at[ti, kk - 1],
                    device_id=(px, py, pc), device_id_type=MESH), peer))
            return out

        own = []
        for ti in range(nt):
            r = gs[ti].shape[0] // N_DEV
            own.append(pltpu.make_async_copy(gs[ti].at[pl.ds(me * r, r), :], outs[ti].at[l, me], local_sems.at[ti]))
            own[-1].start()
        sent = [copies(ti) for ti in range(nt)]
        for per_tensor in sent:
            for cp, _ in per_tensor:
                cp.start()
        for ti, per_tensor in enumerate(sent):
            for kk, (cp, peer) in enumerate(per_tensor):
                pltpu.make_async_remote_copy(
                    src_ref=outs[ti].at[l, peer], dst_ref=outs[ti].at[l, peer],
                    send_sem=send_sems.at[ti, kk], recv_sem=recv_sems.at[ti, kk],
                    device_id=(x, y, c), device_id_type=MESH).wait_recv()
        for per_tensor in sent:
            for cp, _ in per_tensor:
                cp.wait_send()
        for cp in own:
            cp.wait()

    return pl.pallas_call(
        body, name=name,
        in_specs=[ANY] * (2 * nt), out_specs=[ANY] * nt,
        out_shape=[jax.ShapeDtypeStruct(r.shape, r.dtype) for r in recvs],
        input_output_aliases={nt + i: i for i in range(nt)},
        scratch_shapes=[pltpu.SemaphoreType.DMA((nt, 7)), pltpu.SemaphoreType.DMA((nt, 7)),
                        pltpu.SemaphoreType.DMA((nt,))],
    )(*grads, *recvs)


def _adam(w, g, m, v):
    m2 = ADAM_B1 * m + (1.0 - ADAM_B1) * g
    v2 = ADAM_B2 * v + (1.0 - ADAM_B2) * (g * g)
    m_hat = m2 / (1.0 - ADAM_B1 ** ADAM_STEP)
    v_hat = v2 / (1.0 - ADAM_B2 ** ADAM_STEP)
    return -ADAM_LR * (m_hat / (jnp.sqrt(v_hat) + ADAM_EPS) + ADAM_WD * w), m2, v2


def _adam_sharded(recv, w, m, v, name):
    nl, _, r, c = recv.shape
    tr = min(TILE_ADAM, r)
    while r % tr:
        tr //= 2

    def body(recv_ref, w_ref, m_ref, v_ref, g_ref, d_ref, m2_ref, v2_ref):
        g = recv_ref[0].astype(F32)
        for s in range(1, N_DEV):
            g = g + recv_ref[s].astype(F32)
        g_ref[...] = g
        d_ref[...], m2_ref[...], v2_ref[...] = _adam(w_ref[...], g, m_ref[...], v_ref[...])

    blk = pl.BlockSpec((None, tr, c), lambda li, i: (li, i, 0))
    return pl.pallas_call(
        body, name=name, grid=(nl, r // tr),
        in_specs=[pl.BlockSpec((None, N_DEV, tr, c), lambda li, i: (li, 0, i, 0)), blk, blk, blk],
        out_specs=[blk] * 4, out_shape=[jax.ShapeDtypeStruct((nl, r, c), F32)] * 4,
        compiler_params=_params("parallel", "parallel"),
    )(recv, w, m, v)


def _sum_sources(parts, name):
    _, r, c = parts.shape

    def body(p_ref, o_ref):
        g = p_ref[0]
        for s in range(1, N_DEV):
            g = g + p_ref[s]
        o_ref[...] = g

    return pl.pallas_call(
        body, name=name, grid=(1,), in_specs=[_whole((N_DEV, r, c))], out_specs=_whole((r, c)),
        out_shape=jax.ShapeDtypeStruct((r, c), F32), compiler_params=_params("arbitrary"),
    )(parts)


def _adam_flat(w, g, m, v, name):
    r, c = w.shape

    def body(w_ref, g_ref, m_ref, v_ref, d_ref, m2_ref, v2_ref):
        d_ref[...], m2_ref[...], v2_ref[...] = _adam(w_ref[...], g_ref[...], m_ref[...], v_ref[...])

    return pl.pallas_call(
        body, name=name, grid=(1,), in_specs=[_whole((r, c))] * 4, out_specs=[_whole((r, c))] * 3,
        out_shape=[jax.ShapeDtypeStruct((r, c), F32)] * 3, compiler_params=_params("arbitrary"),
    )(w, g, m, v)


def _pack(arrays):
    flat = jnp.concatenate([a.reshape(-1).astype(F32) for a in arrays])
    rows = -(-flat.shape[0] // 1024) * 8
    return jnp.pad(flat, (0, rows * 128 - flat.shape[0])).reshape(rows, 128)


def _unpack(slab, like):
    flat = slab.reshape(-1)
    out, at = [], 0
    for a in like:
        out.append(flat[at:at + a.size].reshape(a.shape))
        at += a.size
    return out


def kernel(x, mem, mem_norm, mix_pre_norm, mix_post_norm, w_in, pool_maps, pool_scale, conf_dw_w, conf_dw_b, conf_ln_g, conf_ln_b, sconv_w, w_out, xattn_pre_norm, xattn_post_norm, xattn_wq, xattn_wk, xattn_wv, xattn_wo, ffn_pre_norm, ffn_post_norm, ffn_w_up, ffn_conv_w, ffn_w_down, loss_target, m_mem_norm, m_mix_pre_norm, m_mix_post_norm, m_w_in, m_pool_maps, m_pool_scale, m_conf_dw_w, m_conf_dw_b, m_conf_ln_g, m_conf_ln_b, m_sconv_w, m_w_out, m_xattn_pre_norm, m_xattn_post_norm, m_xattn_wq, m_xattn_wk, m_xattn_wv, m_xattn_wo, m_ffn_pre_norm, m_ffn_post_norm, m_ffn_w_up, m_ffn_conv_w, m_ffn_w_down, v_mem_norm, v_mix_pre_norm, v_mix_post_norm, v_w_in, v_pool_maps, v_pool_scale, v_conf_dw_w, v_conf_dw_b, v_conf_ln_g, v_conf_ln_b, v_sconv_w, v_w_out, v_xattn_pre_norm, v_xattn_post_norm, v_xattn_wq, v_xattn_wk, v_xattn_wv, v_xattn_wo, v_ffn_pre_norm, v_ffn_post_norm, v_ffn_w_up, v_ffn_conv_w, v_ffn_w_down):
    weights = dict(mem_norm=mem_norm, mix_pre_norm=mix_pre_norm, mix_post_norm=mix_post_norm, w_in=w_in, pool_maps=pool_maps, pool_scale=pool_scale, conf_dw_w=conf_dw_w, conf_dw_b=conf_dw_b, conf_ln_g=conf_ln_g, conf_ln_b=conf_ln_b, sconv_w=sconv_w, w_out=w_out, xattn_pre_norm=xattn_pre_norm, xattn_post_norm=xattn_post_norm, xattn_wq=xattn_wq, xattn_wk=xattn_wk, xattn_wv=xattn_wv, xattn_wo=xattn_wo, ffn_pre_norm=ffn_pre_norm, ffn_post_norm=ffn_post_norm, ffn_w_up=ffn_w_up, ffn_conv_w=ffn_conv_w, ffn_w_down=ffn_w_down)
    mom1 = dict(mem_norm=m_mem_norm, mix_pre_norm=m_mix_pre_norm, mix_post_norm=m_mix_post_norm, w_in=m_w_in, pool_maps=m_pool_maps, pool_scale=m_pool_scale, conf_dw_w=m_conf_dw_w, conf_dw_b=m_conf_dw_b, conf_ln_g=m_conf_ln_g, conf_ln_b=m_conf_ln_b, sconv_w=m_sconv_w, w_out=m_w_out, xattn_pre_norm=m_xattn_pre_norm, xattn_post_norm=m_xattn_post_norm, xattn_wq=m_xattn_wq, xattn_wk=m_xattn_wk, xattn_wv=m_xattn_wv, xattn_wo=m_xattn_wo, ffn_pre_norm=m_ffn_pre_norm, ffn_post_norm=m_ffn_post_norm, ffn_w_up=m_ffn_w_up, ffn_conv_w=m_ffn_conv_w, ffn_w_down=m_ffn_w_down)
    mom2 = dict(mem_norm=v_mem_norm, mix_pre_norm=v_mix_pre_norm, mix_post_norm=v_mix_post_norm, w_in=v_w_in, pool_maps=v_pool_maps, pool_scale=v_pool_scale, conf_dw_w=v_conf_dw_w, conf_dw_b=v_conf_dw_b, conf_ln_g=v_conf_ln_g, conf_ln_b=v_conf_ln_b, sconv_w=v_sconv_w, w_out=v_w_out, xattn_pre_norm=v_xattn_pre_norm, xattn_post_norm=v_xattn_post_norm, xattn_wq=v_xattn_wq, xattn_wk=v_xattn_wk, xattn_wv=v_xattn_wv, xattn_wo=v_xattn_wo, ffn_pre_norm=v_ffn_pre_norm, ffn_post_norm=v_ffn_post_norm, ffn_w_up=v_ffn_w_up, ffn_conv_w=v_ffn_conv_w, ffn_w_down=v_ffn_w_down)
    names = list(weights)

    nl, d = mix_pre_norm.shape
    x0, mem0, target = x[0], mem[0], loss_target[0]
    t = x0.shape[0]
    dp, dc, ds, *_ = _mix_dims(d)
    pg = dp // len(POOL_WINDOWS)
    me = 4 * lax.axis_index("x") + 2 * lax.axis_index("y") + lax.axis_index("c")

    big = ["w_in", "w_out", "xattn_wq", "xattn_wk", "xattn_wv", "xattn_wo", "ffn_w_up", "ffn_w_down"]
    transposed = ("w_in", "ffn_w_up")

    def row_shard(n, a):
        return a.transpose(0, 2, 1) if n in transposed else a

    full = dict(zip(big, _allgather([row_shard(n, weights[n]).astype(BF16) for n in big], "gather_weights")))
    taps = ["conf_dw_w", "sconv_w", "ffn_conv_w"]
    tap_slab = _pack([weights[n] for n in taps])
    tap_all = _allgather([tap_slab[None]], "gather_taps")[0][0].reshape(N_DEV, *tap_slab.shape)
    tap_parts = [_unpack(tap_all[p], [weights[n] for n in taps]) for p in range(N_DEV)]
    wdw, wsc, wcf = (jnp.concatenate([tap_parts[p][i] for p in range(N_DEV)], axis=-1) for i in range(3))

    def g3(a):
        return a.reshape(a.shape[0], 1, a.shape[-1])

    mbd = jnp.zeros((nl, dp, dp), F32)
    for gi in range(len(POOL_WINDOWS)):
        mbd = mbd.at[:, gi * pg:(gi + 1) * pg, gi * pg:(gi + 1) * pg].set(pool_maps[:, gi])
    mbd = mbd.astype(BF16)
    pre1, post1, pre2, post2, pre3, post3 = (g3(weights[n]) for n in (
        "mix_pre_norm", "mix_post_norm", "xattn_pre_norm", "xattn_post_norm", "ffn_pre_norm", "ffn_post_norm"))
    pscale3, bdw3, lng3, lnb3 = g3(pool_scale), g3(conf_dw_b), g3(conf_ln_g), g3(conf_ln_b)
    memg3 = mem_norm.reshape(1, 1, d)

    def mm(a, b, mode, dt, name, lb=None, tm=512, tn=1024, tk=1024, a_outer=True):
        return _matmul(a, b, mode, dt, name, tm=tm, tn=tn, tk=tk, lb=lb, a_outer=a_outer)

    mem_n = _prenorm(mem0, memg3, 0, "mem_norm")
    xs = x0
    h = _prenorm(xs, pre1, 0, "pre_norm0")
    saved = []
    for l in range(nl):
        s = {"x": xs, "h": h}
        s["z"] = mm(h, full["w_in"], "nt", F32, f"z{l}", lb=l, tn=4096)
        s["cat"], s["c"] = _mix_fwd(s["z"], mbd, pscale3, wdw, bdw3, lng3, lnb3, wsc, l, f"mix_fwd{l}")
        s["y1"] = mm(s["cat"], full["w_out"], "nn", F32, f"y1_{l}", lb=l)
        s["x1"], s["h1"] = _resnorm(xs, s["y1"], post1, l, pre2, l, f"resnorm1_{l}")
        s["q"] = mm(s["h1"], full["xattn_wq"], "nn", BF16, f"q{l}", lb=l)
        s["k"] = mm(mem_n, full["xattn_wk"], "nn", BF16, f"k{l}", lb=l)
        s["v"] = mm(mem_n, full["xattn_wv"], "nn", BF16, f"v{l}", lb=l)
        s["o"] = _attn_fwd(s["q"], s["k"], s["v"], f"attn_fwd{l}")
        s["y2"] = mm(s["o"], full["xattn_wo"], "nn", F32, f"y2_{l}", lb=l)
        s["x2"], s["h2"] = _resnorm(s["x1"], s["y2"], post2, l, pre3, l, f"resnorm2_{l}")
        s["u"] = mm(s["h2"], full["ffn_w_up"], "nt", F32, f"u{l}", lb=l, tm=1024, tn=1408, a_outer=False)
        s["a"] = _ffn_act_fwd(s["u"], wcf, l, f"ffn_act{l}")
        s["y3"] = mm(s["a"], full["ffn_w_down"], "nn", F32, f"y3_{l}", lb=l, tk=4096)
        if l + 1 < nl:
            xs, h = _resnorm(s["x2"], s["y3"], post3, l, pre1, l + 1, f"resnorm3_{l}")
        saved.append(s)

    last = saved[-1]
    dxn, dy3, dg_post3, loss_lanes = _loss_head(last["x2"], last["y3"], post3, nl - 1, target, "loss_head")
    loss = lax.psum(loss_lanes[0, 0], ("x", "y", "c"))

    rows_of = {n: row_shard(n, weights[n]).shape[1] for n in big}
    recvs = [jnp.zeros((nl, N_DEV, rows_of[n], d), BF16) for n in big]
    small = {n: [None] * nl for n in names if n not in big and n != "mem_norm"}
    small["ffn_post_norm"][nl - 1] = dg_post3
    dmem_n = jnp.zeros(mem0.shape, F32)
    for l in reversed(range(nl)):
        s = saved[l]
        gw = {}
        da = mm(dy3, full["ffn_w_down"], "nt", F32, f"da{l}", lb=l, tn=1408, a_outer=False)
        gw["ffn_w_down"] = mm(s["a"], dy3, "tn", BF16, f"dw_down{l}", tm=1408)
        du, small["ffn_conv_w"][l] = _ffn_act_bwd(s["u"], da, wcf, l, f"ffn_act_bwd{l}")
        dh2 = mm(du, full["ffn_w_up"], "nn", F32, f"dh2_{l}", lb=l, tm=1024, tk=512)
        gw["ffn_w_up"] = mm(du, s["h2"], "tn", BF16, f"dw_up{l}", tm=1408)
        dx2, small["ffn_pre_norm"][l], dy2, small["xattn_post_norm"][l] = _norm_bwd(
            dxn, dh2, s["x2"], pre3, l, f"norm_bwd3_{l}", s["y2"], post2, l)
        do = mm(dy2, full["xattn_wo"], "nt", BF16, f"do{l}", lb=l)
        gw["xattn_wo"] = mm(s["o"], dy2, "tn", BF16, f"dw_o{l}")
        dq, dk, dv = _attn_bwd(s["q"], s["k"], s["v"], do, f"attn_bwd{l}")
        dkb, dvb = dk.astype(BF16), dv.astype(BF16)
        dh1 = mm(dq, full["xattn_wq"], "nt", F32, f"dh1_{l}", lb=l)
        gw["xattn_wq"] = mm(s["h1"], dq, "tn", BF16, f"dw_q{l}")
        gw["xattn_wk"] = mm(mem_n, dkb, "tn", BF16, f"dw_k{l}")
        gw["xattn_wv"] = mm(mem_n, dvb, "tn", BF16, f"dw_v{l}")
        dmem_n = dmem_n + mm(dkb, full["xattn_wk"], "nt", F32, f"dmem_k{l}", lb=l) \
            + mm(dvb, full["xattn_wv"], "nt", F32, f"dmem_v{l}", lb=l)
        dx1, small["xattn_pre_norm"][l], dy1, small["mix_post_norm"][l] = _norm_bwd(
            dx2, dh1, s["x1"], pre2, l, f"norm_bwd2_{l}", s["y1"], post1, l)
        dcat = mm(dy1, full["w_out"], "nt", F32, f"dcat{l}", lb=l)
        gw["w_out"] = mm(s["cat"], dy1, "tn", BF16, f"dw_out{l}")
        dz, dmbd, dps, dwdw, dbdw, dlng, dlnb, dwsc = _mix_bwd(
            dcat, s["z"], s["c"], mbd, pscale3, wdw, lng3, lnb3, wsc, l, f"mix_bwd{l}")
        small["pool_maps"][l] = jnp.stack([dmbd[gi * pg:(gi + 1) * pg, gi * pg:(gi + 1) * pg]
                                           for gi in range(len(POOL_WINDOWS))])
        small["pool_scale"][l], small["conf_dw_w"][l], small["conf_dw_b"][l] = dps, dwdw, dbdw
        small["conf_ln_g"][l], small["conf_ln_b"][l], small["sconv_w"][l] = dlng, dlnb, dwsc
        dh = mm(dz, full["w_in"], "nn", F32, f"dh{l}", lb=l, tk=4096)
        gw["w_in"] = mm(dz, s["h"], "tn", BF16, f"dw_in{l}", tm=4096)
        if l > 0:
            dxn, small["mix_pre_norm"][l], dy3, small["ffn_post_norm"][l - 1] = _norm_bwd(
                dx1, dh, s["x"], pre1, l, f"norm_bwd1_{l}", saved[l - 1]["y3"], post3, l - 1)
        else:
            dxn, small["mix_pre_norm"][l] = _norm_bwd(dx1, dh, s["x"], pre1, l, "norm_bwd1_0")
        recvs = _scatter_grads([gw[n] for n in big], recvs, l, f"scatter_grads{l}")
    grad_x = dxn[None]
    _, dg_mem = _norm_bwd(jnp.zeros(mem0.shape, F32), dmem_n, mem0, memg3, 0, "norm_bwd_mem")

    small_names = [n for n in names if n not in big]
    partial = {n: (dg_mem.reshape(d) if n == "mem_norm" else
                   jnp.stack([g.reshape(g.shape[-1]) if g.shape[0] == 1 and weights[n].ndim == 2 else g
                              for g in small[n]])) for n in small_names}
    slab = _pack([partial[n] for n in small_names])
    gathered = _allgather([slab[None]], "gather_small_grads")[0][0].reshape(N_DEV, *slab.shape)
    summed = dict(zip(small_names, _unpack(_sum_sources(gathered, "sum_small_grads"), [partial[n] for n in small_names])))
    grad = {}
    for n in small_names:
        g = summed[n]
        if n in taps:
            width = weights[n].shape[-1]
            g = lax.dynamic_slice_in_dim(g, me * width, width, axis=g.ndim - 1)
        grad[n] = g

    delta, new_m, new_v = {}, {}, {}
    upd = _adam_flat(_pack([weights[n] for n in small_names]), _pack([grad[n] for n in small_names]),
                     _pack([mom1[n] for n in small_names]), _pack([mom2[n] for n in small_names]), "adam_small")
    for out, slab_o in zip((delta, new_m, new_v), upd):
        out.update(zip(small_names, _unpack(slab_o, [weights[n] for n in small_names])))
    for n, recv in zip(big, recvs):
        res = _adam_sharded(recv, row_shard(n, weights[n]), row_shard(n, mom1[n]), row_shard(n, mom2[n]), f"adam_{n}")
        grad[n], delta[n], new_m[n], new_v[n] = (row_shard(n, r) for r in res)

    return (loss, grad_x, *[grad[n] for n in names], *[delta[n] for n in names],
            *[new_m[n] for n in names], *[new_v[n] for n in names])
```

```python
import jax
import jax.numpy as jnp
from jax import lax
from jax.experimental import pallas as pl
from jax.experimental.pallas import tpu as pltpu

F32, BF16 = jnp.float32, jnp.bfloat16
EPS = 1e-6
POOL_WINDOWS = (2, 4, 8, 16)
MAX_WINDOW = 16
CONF_TAPS, SHORT_TAPS = 31, 3
CONF_HALO, POOL_HALO, SHORT_HALO = 256, 128, 16
ROW_CHUNK = 64
HEADS = 4
N_DEV = 8
ADAM_LR, ADAM_B1, ADAM_B2, ADAM_EPS, ADAM_WD, ADAM_STEP = 0.001, 0.9, 0.999, 1e-08, 0.01, 10
VMEM_LIMIT_V7X = 56 * 2**20
MESH = pl.DeviceIdType.MESH
ANY = pl.BlockSpec(memory_space=pl.ANY)
HBM = pl.BlockSpec(memory_space=pltpu.HBM)
SEM = pl.BlockSpec(memory_space=pltpu.SEMAPHORE)
EFFECT = pltpu.SideEffectType.DATAFLOW_SIDE_EFFECTING

TILE_NORM, TILE_MIX, TILE_FFN, TILE_ATTN, TILE_ADAM = 256, 512, 256, 512, 256


def _params(*sem):
    return pltpu.CompilerParams(dimension_semantics=sem, vmem_limit_bytes=VMEM_LIMIT_V7X)


def _sig(x):
    return 1.0 / (1.0 + jnp.exp(-x))


def _rms(x):
    r = lax.rsqrt(jnp.mean(x * x, axis=-1, keepdims=True) + EPS)
    return x * r, r


def _rms_bwd(dout, g, n, r):
    dn = dout * g
    return r * (dn - n * jnp.mean(dn * n, axis=-1, keepdims=True))


def _rows(tt, c):
    return pl.BlockSpec((tt, c), lambda i: (i, 0))


def _whole(shape):
    return pl.BlockSpec(shape, lambda i: (0,) * len(shape))


def _layer(shape, l):
    return pl.BlockSpec((None,) + shape, lambda i: (l,) + (0,) * len(shape))


def _colsum(x):
    return jnp.sum(x, axis=0, keepdims=True)


_DIMS = {"nn": (((1,), (0,)), ((), ())), "nt": (((1,), (1,)), ((), ())), "tn": (((0,), (0,)), ((), ()))}


def _matmul(a, b, mode, out_dtype, name, *, tm, tn, tk, a_outer=True):
    if mode == "nn":
        (m, k), (k2, n) = a.shape, b.shape
    elif mode == "nt":
        (m, k), (n, k2) = a.shape, b.shape
    else:
        (k, m), (k2, n) = a.shape, b.shape
    assert k == k2, (name, a.shape, b.shape)
    tm, tn, tk = min(tm, m), min(tn, n), min(tk, k)
    assert m % tm == 0 and n % tn == 0 and k % tk == 0, (name, m, n, k, tm, tn, tk)
    gm, gn, gk = m // tm, n // tn, k // tk

    def ij(g0, g1):
        return (g0, g1) if a_outer else (g1, g0)

    def a_map(g0, g1, kk):
        i, _ = ij(g0, g1)
        return (kk, i) if mode == "tn" else (i, kk)

    def b_map(g0, g1, kk):
        _, j = ij(g0, g1)
        return (j, kk) if mode == "nt" else (kk, j)

    def o_map(g0, g1, kk):
        return ij(g0, g1)

    a_block = (tk, tm) if mode == "tn" else (tm, tk)
    b_block = (tn, tk) if mode == "nt" else (tk, tn)
    dims = _DIMS[mode]

    def body(a_ref, b_ref, o_ref, *acc):
        p = lax.dot_general(a_ref[...].astype(BF16), b_ref[...].astype(BF16), dims, preferred_element_type=F32)
        if gk == 1:
            o_ref[...] = p.astype(o_ref.dtype)
        else:
            kk = pl.program_id(2)

            @pl.when(kk == 0)
            def _():
                acc[0][...] = p

            @pl.when(kk > 0)
            def _():
                acc[0][...] += p

            @pl.when(kk == gk - 1)
            def _():
                o_ref[...] = acc[0][...].astype(o_ref.dtype)

    return pl.pallas_call(
        body, name=name, grid=(gm, gn, gk) if a_outer else (gn, gm, gk),
        in_specs=[pl.BlockSpec(a_block, a_map), pl.BlockSpec(b_block, b_map)],
        out_specs=pl.BlockSpec((tm, tn), o_map),
        out_shape=jax.ShapeDtypeStruct((m, n), out_dtype),
        scratch_shapes=[pltpu.VMEM((tm, tn), F32)] if gk > 1 else [],
        compiler_params=_params("parallel", "parallel", "arbitrary"),
    )(a, b)


def _prenorm(x, g3, l, name):
    t, d = x.shape
    tt = min(TILE_NORM, t)

    def body(x_ref, g_ref, h_ref):
        n, _ = _rms(x_ref[...])
        h_ref[...] = (n * g_ref[...]).astype(BF16)

    return pl.pallas_call(
        body, name=name, grid=(t // tt,),
        in_specs=[_rows(tt, d), _layer((1, d), l)], out_specs=_rows(tt, d),
        out_shape=jax.ShapeDtypeStruct((t, d), BF16), compiler_params=_params("parallel"),
    )(x, g3)


def _resnorm(x, y, gpost3, l, gnext3, l2, name):
    t, d = x.shape
    tt = min(TILE_NORM, t)

    def body(x_ref, y_ref, gp_ref, gn_ref, xo_ref, h_ref):
        n, _ = _rms(y_ref[...])
        xn = x_ref[...] + n * gp_ref[...]
        xo_ref[...] = xn
        n2, _ = _rms(xn)
        h_ref[...] = (n2 * gn_ref[...]).astype(BF16)

    return pl.pallas_call(
        body, name=name, grid=(t // tt,),
        in_specs=[_rows(tt, d), _rows(tt, d), _layer((1, d), l), _layer((1, d), l2)],
        out_specs=[_rows(tt, d), _rows(tt, d)],
        out_shape=[jax.ShapeDtypeStruct((t, d), F32), jax.ShapeDtypeStruct((t, d), BF16)],
        compiler_params=_params("parallel"),
    )(x, y, gpost3, gnext3)


def _loss_head(x, y, gpost3, l, target, name):
    t, d = x.shape
    tt = min(TILE_NORM, t)

    def body(x_ref, y_ref, g_ref, t_ref, dxn_ref, dy_ref, dg_ref, loss_ref):
        @pl.when(pl.program_id(0) == 0)
        def _():
            dg_ref[...] = jnp.zeros_like(dg_ref)
            loss_ref[...] = jnp.zeros_like(loss_ref)

        g = g_ref[...]
        n, r = _rms(y_ref[...])
        diff = x_ref[...] + n * g - t_ref[...]
        loss_ref[...] += 0.5 * jnp.sum(jnp.mean(diff * diff, axis=-1, keepdims=True))
        dxn = diff * (1.0 / d)
        dxn_ref[...] = dxn
        dy_ref[...] = _rms_bwd(dxn, g, n, r).astype(BF16)
        dg_ref[...] += _colsum(dxn * n)

    return pl.pallas_call(
        body, name=name, grid=(t // tt,),
        in_specs=[_rows(tt, d), _rows(tt, d), _layer((1, d), l), _rows(tt, d)],
        out_specs=[_rows(tt, d), _rows(tt, d), _whole((1, d)), _whole((1, 128))],
        out_shape=[jax.ShapeDtypeStruct((t, d), F32), jax.ShapeDtypeStruct((t, d), BF16),
                   jax.ShapeDtypeStruct((1, d), F32), jax.ShapeDtypeStruct((1, 128), F32)],
        compiler_params=_params("arbitrary"),
    )(x, y, gpost3, target)


def _norm_bwd(dxn, dh, x_in, gpre3, l, name, y_prev=None, gpost3=None, l_prev=None):
    t, d = x_in.shape
    tt = min(TILE_NORM, t)
    has_prev = y_prev is not None

    def body(*refs):
        if has_prev:
            dxn_ref, dh_ref, x_ref, g_ref, y_ref, g2_ref, dx_ref, dg_ref, dy_ref, dg2_ref = refs
        else:
            dxn_ref, dh_ref, x_ref, g_ref, dx_ref, dg_ref = refs

        @pl.when(pl.program_id(0) == 0)
        def _():
            dg_ref[...] = jnp.zeros_like(dg_ref)
            if has_prev:
                dg2_ref[...] = jnp.zeros_like(dg2_ref)

        dh_v = dh_ref[...]
        n, r = _rms(x_ref[...])
        dx = dxn_ref[...] + _rms_bwd(dh_v, g_ref[...], n, r)
        dx_ref[...] = dx
        dg_ref[...] += _colsum(dh_v * n)
        if has_prev:
            n2, r2 = _rms(y_ref[...])
            dy_ref[...] = _rms_bwd(dx, g2_ref[...], n2, r2).astype(BF16)
            dg2_ref[...] += _colsum(dx * n2)

    in_specs = [_rows(tt, d), _rows(tt, d), _rows(tt, d), _layer((1, d), l)]
    out_specs = [_rows(tt, d), _whole((1, d))]
    out_shape = [jax.ShapeDtypeStruct((t, d), F32), jax.ShapeDtypeStruct((1, d), F32)]
    args = [dxn, dh, x_in, gpre3]
    if has_prev:
        in_specs += [_rows(tt, d), _layer((1, d), l_prev)]
        out_specs += [_rows(tt, d), _whole((1, d))]
        out_shape += [jax.ShapeDtypeStruct((t, d), BF16), jax.ShapeDtypeStruct((1, d), F32)]
        args += [y_prev, gpost3]
    return pl.pallas_call(
        body, name=name, grid=(t // tt,), in_specs=in_specs, out_specs=out_specs, out_shape=out_shape,
        compiler_params=_params("arbitrary"),
    )(*args)


def _to_steps(a):
    t = a.shape[0]
    return a.reshape(8, t // 8, -1).transpose(1, 0, 2).reshape(a.shape)


def _from_steps(a):
    t = a.shape[0]
    return a.reshape(t // 8, 8, -1).transpose(1, 0, 2).reshape(a.shape)


def _al(v):
    return v if isinstance(v, int) else pl.multiple_of(v, 8)


def _chunks(n_rows, fn):
    def step(r, carry):
        fn(pl.multiple_of(r * ROW_CHUNK, ROW_CHUNK))
        return carry
    lax.fori_loop(0, n_rows // ROW_CHUNK, step, 0)


def _fold8(a):
    return a.reshape(a.shape[0] // 8, 8, a.shape[1]).sum(axis=0)


def _shift_down(a):
    row = lax.broadcasted_iota(jnp.int32, a.shape, 0)
    return jnp.where(row % 8 == 0, 0.0, pltpu.roll(a, 1, 0))


def _shift_up(a):
    row = lax.broadcasted_iota(jnp.int32, a.shape, 0)
    return jnp.where(row % 8 == 7, 0.0, pltpu.roll(a, a.shape[0] - 1, 0))


def _prev_block(h, c, tt, t):
    return pl.BlockSpec((h, c), lambda i: (jnp.where(i == 0, t // h - 1, i * (tt // h) - 1), 0))


def _next_block(h, c, tt, t):
    return pl.BlockSpec((h, c), lambda i: (jnp.where(i == t // tt - 1, 0, (i + 1) * (tt // h)), 0))


def _taps(w_ref, buf, start, taps, rc, lanes=slice(None)):
    acc = w_ref[0:1, lanes] * buf[pl.ds(_al(start), rc), lanes]
    for k in range(1, taps):
        acc = acc + w_ref[k:k + 1, lanes] * buf[pl.ds(_al(start + 8 * k), rc), lanes]
    return acc


def _taps_rev(w_ref, buf, start, taps, rc, lanes=slice(None)):
    acc = w_ref[0:1, lanes] * buf[pl.ds(_al(start + 8 * (taps - 1)), rc), lanes]
    for k in range(1, taps):
        acc = acc + w_ref[k:k + 1, lanes] * buf[pl.ds(_al(start + 8 * (taps - 1 - k)), rc), lanes]
    return acc


def _mix_dims(d):
    dp = d // 4
    dc = 3 * d // 8
    ds = d - dp - dc
    oa, og = dp, dp + dc
    ob = dp + 2 * dc
    oc, ox = ob + ds, ob + 2 * ds
    return dp, dc, ds, oa, og, ob, oc, ox, ox + ds


def _pool_consts(dp):
    win = jnp.repeat(jnp.asarray(POOL_WINDOWS, F32), dp // len(POOL_WINDOWS))[None, :]
    mask = (jnp.arange(MAX_WINDOW, dtype=F32)[:, None] < win).astype(F32)
    return mask, win


def _pool_count(row0, rc, dp, seg, wl):
    r = lax.broadcasted_iota(jnp.int32, (rc, dp), 0) + row0
    return jnp.minimum(((r & 7) * seg + (r >> 3) + 1).astype(F32), wl)


def _mix_fwd(z, mbd, pscale3, wdw, bdw3, lng3, lnb3, wsc, l, name):
    t, din = z.shape
    dp, dc, ds, oa, og, ob, oc, ox, din2 = _mix_dims(din * 8 // 17)
    assert din2 == din
    d = ob
    tt = min(TILE_MIX, t)
    hp, hc, hs, rc = POOL_HALO, CONF_HALO, SHORT_HALO, ROW_CHUNK
    assert tt % hc == 0 and t % tt == 0
    seg = t // 8
    pmask, wlane = _pool_consts(dp)

    def body(z_ref, zpa_ref, zpb_ref, mbd_ref, ps_ref, pmask_ref, wl_ref, wdw_ref, bdw_ref, lng_ref, lnb_ref, wsc_ref,
             cat_ref, c_ref, pbuf, vbuf, sbuf):
        i = pl.program_id(0)
        pbuf[0:hp, :] = zpa_ref[hc - hp:hc, 0:dp]

        def prev(r0):
            rows = pl.ds(r0, rc)
            vbuf[rows, :] = zpa_ref[rows, oa:oa + dc] * _sig(zpa_ref[rows, og:og + dc])
        _chunks(hc, prev)
        sbuf[0:hs, :] = zpb_ref[:, oc:oc + ds] * zpb_ref[:, ox:ox + ds]

        @pl.when(i == 0)
        def _():
            pbuf[0:hp, :] = _shift_down(pbuf[0:hp, :])
            vbuf[0:hc, :] = _shift_down(vbuf[0:hc, :])
            sbuf[0:hs, :] = _shift_down(sbuf[0:hs, :])

        mbd_v, ps, wl = mbd_ref[...], ps_ref[...], wl_ref[...]
        bdw, lng, lnb = bdw_ref[...], lng_ref[...], lnb_ref[...]

        def step(r0):
            rows = pl.ds(r0, rc)
            zp = z_ref[rows, 0:dp]
            pbuf[pl.ds(_al(hp + r0), rc), :] = zp
            vbuf[pl.ds(_al(hc + r0), rc), :] = z_ref[rows, oa:oa + dc] * _sig(z_ref[rows, og:og + dc])
            sbuf[pl.ds(_al(hs + r0), rc), :] = z_ref[rows, oc:oc + ds] * z_ref[rows, ox:ox + ds]
            pooled = _taps_rev(pmask_ref, pbuf, r0 + hp - 8 * (MAX_WINDOW - 1), MAX_WINDOW, rc)
            pooled = pooled / _pool_count(i * tt + r0, rc, dp, seg, wl) - zp
            pm = jnp.dot(pooled.astype(BF16), mbd_v, preferred_element_type=F32)
            cat_ref[rows, 0:dp] = (pm * ps).astype(BF16)
            c = _taps(wdw_ref, vbuf, r0 + hc - 8 * (CONF_TAPS - 1), CONF_TAPS, rc) + bdw
            c_ref[rows, :] = c
            xc = c - jnp.mean(c, axis=-1, keepdims=True)
            nrm = xc * lax.rsqrt(jnp.mean(xc * xc, axis=-1, keepdims=True) + EPS)
            yln = nrm * lng + lnb
            cat_ref[rows, dp:dp + dc] = (yln * _sig(yln)).astype(BF16)
            cv = _taps(wsc_ref, sbuf, r0 + hs - 8 * (SHORT_TAPS - 1), SHORT_TAPS, rc)
            cat_ref[rows, dp + dc:d] = (z_ref[rows, ob:ob + ds] * cv).astype(BF16)
        _chunks(tt, step)

    return pl.pallas_call(
        body, name=name, grid=(t // tt,),
        in_specs=[_rows(tt, din), _prev_block(hc, d, tt, t), _prev_block(hs, din, tt, t),
                  _layer((dp, dp), l), _layer((1, dp), l), _whole((MAX_WINDOW, dp)), _whole((1, dp)),
                  _layer((CONF_TAPS, dc), l), _layer((1, dc), l), _layer((1, dc), l), _layer((1, dc), l),
                  _layer((SHORT_TAPS, ds), l)],
        out_specs=[_rows(tt, d), _rows(tt, dc)],
        out_shape=[jax.ShapeDtypeStruct((t, d), BF16), jax.ShapeDtypeStruct((t, dc), F32)],
        scratch_shapes=[pltpu.VMEM((hp + tt, dp), F32), pltpu.VMEM((hc + tt, dc), F32), pltpu.VMEM((hs + tt, ds), F32)],
        compiler_params=_params("parallel"),
    )(z, z, z, mbd, pscale3, pmask, wlane, wdw, bdw3, lng3, lnb3, wsc)


def _mix_bwd(dcat, z, c, mbd, pscale3, wdw, lng3, lnb3, wsc, l, name):
    t, din = z.shape
    dp, dc, ds, oa, og, ob, oc, ox, _ = _mix_dims(din * 8 // 17)
    d = ob
    tt = min(TILE_MIX, t)
    nt = t // tt
    hp, hc, hs, rc = POOL_HALO, CONF_HALO, SHORT_HALO, ROW_CHUNK
    assert tt % hc == 0 and t % tt == 0 and tt >= 8 * MAX_WINDOW
    seg = t // 8
    pmask, wlane = _pool_consts(dp)

    def body(dcat_ref, dcn_ref, z_ref, zpa_ref, zpb_ref, znb_ref, c_ref, cn_ref, mbd_ref, ps_ref, pmask_ref, wl_ref,
             wdw_ref, lng_ref, lnb_ref, wsc_ref,
             dz_ref, dmbd_ref, dps_ref, dwdw_ref, dbdw_ref, dlng_ref, dlnb_ref, dwsc_ref,
             pbuf, qbuf, dpbuf, pbf, vbuf, dcbuf, sbuf, dsbuf, dw8, ds8, ln8, ps8):
        i = pl.program_id(0)
        first, last = i == 0, i == nt - 1

        @pl.when(first)
        def _():
            for ref in (dmbd_ref, dw8, ds8, ln8, ps8):
                ref[...] = jnp.zeros_like(ref)

        mbd_v, ps, wl = mbd_ref[...], ps_ref[...], wl_ref[...]
        lng, lnb = lng_ref[...], lnb_ref[...]

        def ln_silu_bwd(cc, dyb):
            xc = cc - jnp.mean(cc, axis=-1, keepdims=True)
            rstd = lax.rsqrt(jnp.mean(xc * xc, axis=-1, keepdims=True) + EPS)
            nrm = xc * rstd
            yln = nrm * lng + lnb
            s = _sig(yln)
            dyln = dyb * (s * (1.0 + yln * (1.0 - s)))
            dn = dyln * lng
            dcc = rstd * (dn - jnp.mean(dn, axis=-1, keepdims=True) - nrm * jnp.mean(dn * nrm, axis=-1, keepdims=True))
            return dcc, dyln, nrm

        pbuf[0:hp, :] = zpa_ref[hc - hp:hc, 0:dp]

        def prev(r0):
            rows = pl.ds(r0, rc)
            vbuf[rows, :] = zpa_ref[rows, oa:oa + dc] * _sig(zpa_ref[rows, og:og + dc])
        _chunks(hc, prev)
        sbuf[0:hs, :] = zpb_ref[:, oc:oc + ds] * zpb_ref[:, ox:ox + ds]

        @pl.when(first)
        def _():
            pbuf[0:hp, :] = _shift_down(pbuf[0:hp, :])
            vbuf[0:hc, :] = _shift_down(vbuf[0:hc, :])
            sbuf[0:hs, :] = _shift_down(sbuf[0:hs, :])

        def nxt(r0):
            rows = pl.ds(r0, rc)
            dcc, _, _ = ln_silu_bwd(cn_ref[rows, :], dcn_ref[rows, dp:dp + dc])
            dcbuf[pl.ds(_al(tt + r0), rc), :] = dcc
        _chunks(hc, nxt)
        dpm_n = (dcn_ref[0:hp, 0:dp] * ps).astype(BF16)
        qbuf[tt:tt + hp, :] = lax.dot_general(dpm_n, mbd_v, _DIMS["nt"], preferred_element_type=F32) / wl
        dsbuf[tt:tt + hs, :] = dcn_ref[0:hs, dp + dc:d] * znb_ref[:, ob:ob + ds]

        @pl.when(last)
        def _():
            dcbuf[tt:tt + hc, :] = _shift_up(dcbuf[tt:tt + hc, :])
            qbuf[tt:tt + hp, :] = _shift_up(qbuf[tt:tt + hp, :])
            dsbuf[tt:tt + hs, :] = _shift_up(dsbuf[tt:tt + hs, :])

        def fill(r0):
            rows = pl.ds(r0, rc)
            zp = z_ref[rows, 0:dp]
            pbuf[pl.ds(_al(hp + r0), rc), :] = zp
            vbuf[pl.ds(_al(hc + r0), rc), :] = z_ref[rows, oa:oa + dc] * _sig(z_ref[rows, og:og + dc])
            sbuf[pl.ds(_al(hs + r0), rc), :] = z_ref[rows, oc:oc + ds] * z_ref[rows, ox:ox + ds]
            pooled = _taps_rev(pmask_ref, pbuf, r0 + hp - 8 * (MAX_WINDOW - 1), MAX_WINDOW, rc)
            pbf[rows, :] = (pooled / _pool_count(i * tt + r0, rc, dp, seg, wl) - zp).astype(BF16)
            dcc, dyln, nrm = ln_silu_bwd(c_ref[rows, :], dcat_ref[rows, dp:dp + dc])
            dcbuf[rows, :] = dcc
            ln8[0] += _fold8(dyln * nrm)
            ln8[1] += _fold8(dyln)
            ln8[2] += _fold8(dcc)
            dsbuf[rows, :] = dcat_ref[rows, dp + dc:d] * z_ref[rows, ob:ob + ds]
        _chunks(tt, fill)

        pb = pbf[...]
        dya = dcat_ref[:, 0:dp]
        ps8[...] += _fold8(dya * jnp.dot(pb, mbd_v, preferred_element_type=F32))
        dpm = (dya * ps).astype(BF16)
        dmbd_ref[...] += lax.dot_general(pb, dpm, _DIMS["tn"], preferred_element_type=F32)
        dpbuf[...] = lax.dot_general(dpm, mbd_v, _DIMS["nt"], preferred_element_type=F32)

        def quot(r0):
            rows = pl.ds(r0, rc)
            qbuf[rows, :] = dpbuf[rows, :] / _pool_count(i * tt + r0, rc, dp, seg, wl)
        _chunks(tt, quot)

        def back(r0):
            rows = pl.ds(r0, rc)
            dzp = _taps(pmask_ref, qbuf, r0, MAX_WINDOW, rc) - dpbuf[rows, :]
            dz_ref[rows, 0:dp] = dzp.astype(BF16)
            dcc = dcbuf[rows, :]
            for k in range(CONF_TAPS):
                dw8[k] += _fold8(dcc * vbuf[pl.ds(_al(r0 + hc - 8 * (CONF_TAPS - 1 - k)), rc), :])
            dv = _taps_rev(wdw_ref, dcbuf, r0, CONF_TAPS, rc)
            za = z_ref[rows, oa:oa + dc]
            sg = _sig(z_ref[rows, og:og + dc])
            dz_ref[rows, oa:oa + dc] = (dv * sg).astype(BF16)
            dz_ref[rows, og:og + dc] = (dv * za * sg * (1.0 - sg)).astype(BF16)
            cv = _taps(wsc_ref, sbuf, r0 + hs - 8 * (SHORT_TAPS - 1), SHORT_TAPS, rc)
            dz_ref[rows, ob:ob + ds] = (dcat_ref[rows, dp + dc:d] * cv).astype(BF16)
            dcv = dsbuf[rows, :]
            for k in range(SHORT_TAPS):
                ds8[k] += _fold8(dcv * sbuf[pl.ds(_al(r0 + hs - 8 * (SHORT_TAPS - 1 - k)), rc), :])
            dpv = _taps_rev(wsc_ref, dsbuf, r0, SHORT_TAPS, rc)
            dz_ref[rows, oc:oc + ds] = (dpv * z_ref[rows, ox:ox + ds]).astype(BF16)
            dz_ref[rows, ox:ox + ds] = (dpv * z_ref[rows, oc:oc + ds]).astype(BF16)
        _chunks(tt, back)

        @pl.when(last)
        def _():
            dps_ref[...] = jnp.sum(ps8[...], axis=0, keepdims=True)
            dwdw_ref[...] = jnp.sum(dw8[...], axis=1)
            dwsc_ref[...] = jnp.sum(ds8[...], axis=1)
            dlng_ref[...] = jnp.sum(ln8[0], axis=0, keepdims=True)
            dlnb_ref[...] = jnp.sum(ln8[1], axis=0, keepdims=True)
            dbdw_ref[...] = jnp.sum(ln8[2], axis=0, keepdims=True)

    return pl.pallas_call(
        body, name=name, grid=(nt,),
        in_specs=[_rows(tt, d), _next_block(hc, d, tt, t),
                  _rows(tt, din), _prev_block(hc, d, tt, t), _prev_block(hs, din, tt, t), _next_block(hs, din, tt, t),
                  _rows(tt, dc), _next_block(hc, dc, tt, t),
                  _layer((dp, dp), l), _layer((1, dp), l), _whole((MAX_WINDOW, dp)), _whole((1, dp)),
                  _layer((CONF_TAPS, dc), l), _layer((1, dc), l), _layer((1, dc), l), _layer((SHORT_TAPS, ds), l)],
        out_specs=[_rows(tt, din), _whole((dp, dp)), _whole((1, dp)), _whole((CONF_TAPS, dc)), _whole((1, dc)),
                   _whole((1, dc)), _whole((1, dc)), _whole((SHORT_TAPS, ds))],
        out_shape=[jax.ShapeDtypeStruct((t, din), BF16), jax.ShapeDtypeStruct((dp, dp), F32),
                   jax.ShapeDtypeStruct((1, dp), F32), jax.ShapeDtypeStruct((CONF_TAPS, dc), F32),
                   jax.ShapeDtypeStruct((1, dc), F32), jax.ShapeDtypeStruct((1, dc), F32),
                   jax.ShapeDtypeStruct((1, dc), F32), jax.ShapeDtypeStruct((SHORT_TAPS, ds), F32)],
        scratch_shapes=[pltpu.VMEM((hp + tt, dp), F32), pltpu.VMEM((tt + hp, dp), F32), pltpu.VMEM((tt, dp), F32),
                        pltpu.VMEM((tt, dp), BF16), pltpu.VMEM((hc + tt, dc), F32), pltpu.VMEM((tt + hc, dc), F32),
                        pltpu.VMEM((hs + tt, ds), F32), pltpu.VMEM((tt + hs, ds), F32),
                        pltpu.VMEM((CONF_TAPS, 8, dc), F32), pltpu.VMEM((SHORT_TAPS, 8, ds), F32),
                        pltpu.VMEM((3, 8, dc), F32), pltpu.VMEM((8, dp), F32)],
        compiler_params=_params("arbitrary"),
    )(dcat, dcat, z, z, z, z, c, c, mbd, pscale3, pmask, wlane, wdw, lng3, lnb3, wsc)


def _softmax_rows(qh, kh, scale):
    s = lax.dot_general(qh, kh, _DIMS["nt"], preferred_element_type=F32) * scale
    e = jnp.exp(s - jnp.max(s, axis=-1, keepdims=True))
    return e / jnp.sum(e, axis=-1, keepdims=True)


def _attn_fwd(q, k, v, name):
    t, d = q.shape
    m = k.shape[0]
    hd = d // HEADS
    scale = hd ** -0.5
    tt = min(TILE_ATTN, t)

    def body(q_ref, k_ref, v_ref, o_ref):
        for h in range(HEADS):
            sl = slice(h * hd, (h + 1) * hd)
            p = _softmax_rows(q_ref[:, sl], k_ref[:, sl], scale)
            o_ref[:, sl] = jnp.dot(p.astype(BF16), v_ref[:, sl], preferred_element_type=F32).astype(BF16)

    return pl.pallas_call(
        body, name=name, grid=(t // tt,),
        in_specs=[_rows(tt, d), _whole((m, d)), _whole((m, d))], out_specs=_rows(tt, d),
        out_shape=jax.ShapeDtypeStruct((t, d), BF16), compiler_params=_params("parallel"),
    )(q, k, v)


def _attn_bwd(q, k, v, do, name):
    t, d = q.shape
    m = k.shape[0]
    hd = d // HEADS
    scale = hd ** -0.5
    tt = min(TILE_ATTN, t)

    def body(q_ref, k_ref, v_ref, do_ref, dq_ref, dk_ref, dv_ref):
        @pl.when(pl.program_id(0) == 0)
        def _():
            dk_ref[...] = jnp.zeros_like(dk_ref)
            dv_ref[...] = jnp.zeros_like(dv_ref)

        for h in range(HEADS):
            sl = slice(h * hd, (h + 1) * hd)
            qh, kh, vh, doh = q_ref[:, sl], k_ref[:, sl], v_ref[:, sl], do_ref[:, sl]
            p = _softmax_rows(qh, kh, scale)
            dv_ref[:, sl] += lax.dot_general(p.astype(BF16), doh, _DIMS["tn"], preferred_element_type=F32)
            dp = lax.dot_general(doh, vh, _DIMS["nt"], preferred_element_type=F32)
            ds = (p * (dp - jnp.sum(dp * p, axis=-1, keepdims=True)) * scale).astype(BF16)
            dq_ref[:, sl] = jnp.dot(ds, kh, preferred_element_type=F32).astype(BF16)
            dk_ref[:, sl] += lax.dot_general(ds, qh, _DIMS["tn"], preferred_element_type=F32)

    return pl.pallas_call(
        body, name=name, grid=(t // tt,),
        in_specs=[_rows(tt, d), _whole((m, d)), _whole((m, d)), _rows(tt, d)],
        out_specs=[_rows(tt, d), _whole((m, d)), _whole((m, d))],
        out_shape=[jax.ShapeDtypeStruct((t, d), BF16), jax.ShapeDtypeStruct((m, d), F32),
                   jax.ShapeDtypeStruct((m, d), F32)],
        compiler_params=_params("arbitrary"),
    )(q, k, v, do)


def _lane_chunks(f):
    w = 256 if f % 256 == 0 else 128 if f % 128 == 0 else f
    return [(c0, w) for c0 in range(0, f, w)]


def _ffn_act_fwd(u, wc, l, name):
    t, f2 = u.shape
    f = f2 // 2
    tt = min(TILE_FFN, t)
    hs, rc = SHORT_HALO, ROW_CHUNK
    lanes = _lane_chunks(f)

    def body(u_ref, up_ref, wc_ref, a_ref, ubuf):
        ubuf[0:hs, :] = up_ref[...]

        @pl.when(pl.program_id(0) == 0)
        def _():
            ubuf[0:hs, :] = _shift_down(ubuf[0:hs, :])

        def step(r0):
            rows = pl.ds(r0, rc)
            ubuf[pl.ds(_al(hs + r0), rc), :] = u_ref[rows, :]
            start = r0 + hs - 8 * (SHORT_TAPS - 1)
            for c0, cw in lanes:
                g = _taps(wc_ref, ubuf, start, SHORT_TAPS, rc, slice(c0, c0 + cw))
                vv = _taps(wc_ref, ubuf, start, SHORT_TAPS, rc, slice(f + c0, f + c0 + cw))
                a_ref[rows, c0:c0 + cw] = (g * _sig(g) * vv).astype(BF16)
        _chunks(tt, step)

    return pl.pallas_call(
        body, name=name, grid=(t // tt,),
        in_specs=[_rows(tt, f2), _prev_block(hs, f2, tt, t), _layer((SHORT_TAPS, f2), l)],
        out_specs=_rows(tt, f), out_shape=jax.ShapeDtypeStruct((t, f), BF16),
        scratch_shapes=[pltpu.VMEM((hs + tt, f2), F32)], compiler_params=_params("parallel"),
    )(u, u, wc)


def _ffn_act_bwd(u, da, wc, l, name):
    t, f2 = u.shape
    f = f2 // 2
    tt = min(TILE_FFN, t)
    nt = t // tt
    hs, rc = SHORT_HALO, ROW_CHUNK
    lanes = _lane_chunks(f)

    def body(u_ref, up_ref, un_ref, da_ref, dan_ref, wc_ref, du_ref, dwc_ref, ubuf, danbuf, dbuf, dw8):
        i = pl.program_id(0)
        first, last = i == 0, i == nt - 1
        ubuf[0:hs, :] = up_ref[...]
        ubuf[hs + tt:hs + tt + hs, :] = un_ref[...]
        danbuf[...] = dan_ref[...]

        @pl.when(first)
        def _():
            dw8[...] = jnp.zeros_like(dw8)
            ubuf[0:hs, :] = _shift_down(ubuf[0:hs, :])

        @pl.when(last)
        def _():
            ubuf[hs + tt:hs + tt + hs, :] = _shift_up(ubuf[hs + tt:hs + tt + hs, :])
            danbuf[...] = _shift_up(danbuf[...])

        def fill(r0):
            ubuf[pl.ds(_al(hs + r0), rc), :] = u_ref[pl.ds(r0, rc), :]
        _chunks(tt, fill)

        def conv_grads(r0, n, da_rows):
            start = r0 + hs - 8 * (SHORT_TAPS - 1)
            for c0, cw in lanes:
                sl_g, sl_v = slice(c0, c0 + cw), slice(f + c0, f + c0 + cw)
                g = _taps(wc_ref, ubuf, start, SHORT_TAPS, n, sl_g)
                vv = _taps(wc_ref, ubuf, start, SHORT_TAPS, n, sl_v)
                dav = da_rows(c0, cw)
                sg = _sig(g)
                dbuf[pl.ds(_al(r0), n), sl_g] = dav * vv * (sg * (1.0 + g * (1.0 - sg)))
                dbuf[pl.ds(_al(r0), n), sl_v] = dav * (g * sg)

        _chunks(tt, lambda r0: conv_grads(r0, rc, lambda c0, cw: da_ref[pl.ds(r0, rc), c0:c0 + cw]))
        conv_grads(tt, hs, lambda c0, cw: danbuf[:, c0:c0 + cw])

        def back(r0):
            rows = pl.ds(r0, rc)
            for c0, cw in lanes:
                for off in (c0, f + c0):
                    sl = slice(off, off + cw)
                    du_ref[rows, sl] = _taps_rev(wc_ref, dbuf, r0, SHORT_TAPS, rc, sl).astype(BF16)
                    dd = dbuf[rows, sl]
                    for k in range(SHORT_TAPS):
                        dw8[k, :, sl] += _fold8(dd * ubuf[pl.ds(_al(r0 + hs - 8 * (SHORT_TAPS - 1 - k)), rc), sl])
        _chunks(tt, back)

        @pl.when(last)
        def _():
            dwc_ref[...] = jnp.sum(dw8[...], axis=1)

    return pl.pallas_call(
        body, name=name, grid=(nt,),
        in_specs=[_rows(tt, f2), _prev_block(hs, f2, tt, t), _next_block(hs, f2, tt, t),
                  _rows(tt, f), _next_block(hs, f, tt, t), _layer((SHORT_TAPS, f2), l)],
        out_specs=[_rows(tt, f2), _whole((SHORT_TAPS, f2))],
        out_shape=[jax.ShapeDtypeStruct((t, f2), BF16), jax.ShapeDtypeStruct((SHORT_TAPS, f2), F32)],
        scratch_shapes=[pltpu.VMEM((hs + tt + hs, f2), F32), pltpu.VMEM((hs, f), F32), pltpu.VMEM((tt + hs, f2), F32),
                        pltpu.VMEM((SHORT_TAPS, 8, f2), F32)],
        compiler_params=_params("arbitrary"),
    )(u, u, u, da, da, wc)


def _place():
    return lax.axis_index("x"), lax.axis_index("y"), lax.axis_index("c")


def _flip(v, bit):
    return 1 - v if bit else v


def _peers(x, y, c):
    out = []
    for kk in range(1, N_DEV):
        px, py, pc = _flip(x, kk & 4), _flip(y, kk & 2), _flip(c, kk & 1)
        out.append((kk - 1, (px, py, pc), 4 * px + 2 * py + pc))
    return out


def _allgather(shards, name):
    nt = len(shards)

    def body(*refs):
        srcs, outs = refs[:nt], refs[nt:2 * nt]
        send_sems, recv_sems, local_sems = refs[2 * nt:]
        x, y, c = _place()
        me, sibling = (x, y, c), (x, y, 1 - c)
        chips = [(1 - x, y), (x, 1 - y), (1 - x, 1 - y)]

        def rows(ti, px, py, pc):
            r = srcs[ti].shape[1]
            return outs[ti].at[:, pl.ds((4 * px + 2 * py + pc) * r, r), :]

        def copy(ti, kk, block, to, src=None):
            return pltpu.make_async_remote_copy(
                src_ref=rows(ti, *block) if src is None else src, dst_ref=rows(ti, *block),
                send_sem=send_sems.at[ti, kk], recv_sem=recv_sems.at[ti, kk], device_id=to, device_id_type=MESH)

        mine = [pltpu.make_async_copy(srcs[ti], rows(ti, *me), local_sems.at[ti]) for ti in range(nt)]
        for cp in mine:
            cp.start()
        first = []
        for ti in range(nt):
            first.append(copy(ti, 0, me, sibling, src=srcs[ti]))
            first += [copy(ti, 1 + j, me, (*chip, c), src=srcs[ti]) for j, chip in enumerate(chips)]
        for cp in first:
            cp.start()
        passed = []
        for j, chip in enumerate(chips):
            for ti in range(nt):
                copy(ti, 1 + j, (*chip, c), me).wait_recv()
                fwd = copy(ti, 4 + j, (*chip, c), sibling)
                fwd.start()
                passed.append(fwd)
        for ti in range(nt):
            copy(ti, 0, sibling, me).wait_recv()
            for j, chip in enumerate(chips):
                copy(ti, 4 + j, (*chip, 1 - c), me).wait_recv()
        for cp in first + passed:
            cp.wait_send()
        for cp in mine:
            cp.wait()

    return pl.pallas_call(
        body, name=name,
        in_specs=[ANY] * nt, out_specs=[ANY] * nt,
        out_shape=[jax.ShapeDtypeStruct((s.shape[0], N_DEV * s.shape[1], s.shape[2]), s.dtype) for s in shards],
        scratch_shapes=[pltpu.SemaphoreType.DMA((nt, 7)), pltpu.SemaphoreType.DMA((nt, 7)),
                        pltpu.SemaphoreType.DMA((nt,))],
    )(*shards)


def _gather_piece(src, land, me, to):
    r = src.shape[0]
    return src, land.at[pl.ds(me * r, r), :]


def _scatter_piece(l):
    def piece(src, land, me, to):
        r = src.shape[0] // N_DEV
        return src.at[pl.ds(to * r, r), :], land.at[l, me]
    return piece


def _split_start(srcs, lands, piece, name):
    nt = len(srcs)

    def body(*refs):
        src_refs, land_refs = refs[:nt], refs[nt:2 * nt]
        send_sems, recv_sems = refs[2 * nt], refs[2 * nt + 1]
        token, local_sems = refs[4 * nt + 2], refs[4 * nt + 3]
        x, y, c = _place()
        me = 4 * x + 2 * y + c
        own = [pltpu.make_async_copy(*piece(src_refs[ti], land_refs[ti], me, me), local_sems.at[ti]) for ti in range(nt)]
        for cp in own:
            cp.start()
        for ti in range(nt):
            for slot, peer, flat in _peers(x, y, c):
                src, dst = piece(src_refs[ti], land_refs[ti], me, flat)
                pltpu.make_async_remote_copy(
                    src_ref=src, dst_ref=dst, send_sem=send_sems.at[7 * ti + slot], recv_sem=recv_sems.at[7 * ti + slot],
                    device_id=peer, device_id_type=MESH).start()
        for cp in own:
            cp.wait()
        token[...] = jnp.zeros_like(token)

    both = list(srcs) + list(lands)
    return pl.pallas_call(
        body, name=name,
        in_specs=[HBM] * (2 * nt),
        out_specs=[SEM, SEM] + [HBM] * (2 * nt) + [pl.BlockSpec(memory_space=pltpu.VMEM)],
        out_shape=[pltpu.SemaphoreType.DMA((7 * nt,)), pltpu.SemaphoreType.DMA((7 * nt,))]
        + [pltpu.HBM(a.shape, a.dtype) for a in both] + [jax.ShapeDtypeStruct((8, 128), F32)],
        input_output_aliases={i: i + 2 for i in range(2 * nt)},
        scratch_shapes=[pltpu.SemaphoreType.DMA((nt,))],
        compiler_params=pltpu.CompilerParams(has_side_effects=EFFECT),
    )(*[pltpu.with_memory_space_constraint(a, pltpu.HBM) for a in both])


def _split_wait(started, after, piece, name):
    send_sems, recv_sems, *both = started[:-1]
    nt = len(both) // 2

    def body(*refs):
        src_refs, land_refs = refs[:nt], refs[nt:2 * nt]
        send_ref, recv_ref = refs[2 * nt], refs[2 * nt + 1]
        x, y, c = _place()
        me = 4 * x + 2 * y + c
        for ti in range(nt):
            src, dst = piece(src_refs[ti], land_refs[ti], me, me)
            for slot in range(N_DEV - 1):
                cp = pltpu.make_async_remote_copy(
                    src_ref=src, dst_ref=dst, send_sem=send_ref.at[7 * ti + slot], recv_sem=recv_ref.at[7 * ti + slot],
                    device_id=(x, y, c), device_id_type=MESH)
                cp.wait_send()
                cp.wait_recv()

    outs = pl.pallas_call(
        body, name=name,
        in_specs=[HBM] * (2 * nt) + [SEM, SEM, ANY], out_specs=[HBM] * (2 * nt),
        out_shape=[pltpu.HBM(a.shape, a.dtype) for a in both],
        input_output_aliases={i: i for i in range(2 * nt)},
        compiler_params=pltpu.CompilerParams(has_side_effects=EFFECT),
    )(*both, send_sems, recv_sems, after)
    return outs[nt:]


def _adam(w, g, m, v):
    m2 = ADAM_B1 * m + (1.0 - ADAM_B1) * g
    v2 = ADAM_B2 * v + (1.0 - ADAM_B2) * (g * g)
    m_hat = m2 / (1.0 - ADAM_B1 ** ADAM_STEP)
    v_hat = v2 / (1.0 - ADAM_B2 ** ADAM_STEP)
    return -ADAM_LR * (m_hat / (jnp.sqrt(v_hat) + ADAM_EPS) + ADAM_WD * w), m2, v2


def _adam_sharded(recv, w, m, v, name):
    nl, _, r, c = recv.shape
    tr = min(TILE_ADAM, r)
    while r % tr:
        tr //= 2

    def body(recv_ref, w_ref, m_ref, v_ref, g_ref, d_ref, m2_ref, v2_ref):
        g = recv_ref[0].astype(F32)
        for s in range(1, N_DEV):
            g = g + recv_ref[s].astype(F32)
        g_ref[...] = g
        d_ref[...], m2_ref[...], v2_ref[...] = _adam(w_ref[...], g, m_ref[...], v_ref[...])

    blk = pl.BlockSpec((None, tr, c), lambda li, i: (li, i, 0))
    return pl.pallas_call(
        body, name=name, grid=(nl, r // tr),
        in_specs=[pl.BlockSpec((None, N_DEV, tr, c), lambda li, i: (li, 0, i, 0)), blk, blk, blk],
        out_specs=[blk] * 4, out_shape=[jax.ShapeDtypeStruct((nl, r, c), F32)] * 4,
        compiler_params=_params("parallel", "parallel"),
    )(recv, w, m, v)


def _sum_sources(parts, name):
    _, r, c = parts.shape

    def body(p_ref, o_ref):
        g = p_ref[0]
        for s in range(1, N_DEV):
            g = g + p_ref[s]
        o_ref[...] = g

    return pl.pallas_call(
        body, name=name, grid=(1,), in_specs=[_whole((N_DEV, r, c))], out_specs=_whole((r, c)),
        out_shape=jax.ShapeDtypeStruct((r, c), F32), compiler_params=_params("arbitrary"),
    )(parts)


def _adam_flat(w, g, m, v, name):
    r, c = w.shape

    def body(w_ref, g_ref, m_ref, v_ref, d_ref, m2_ref, v2_ref):
        d_ref[...], m2_ref[...], v2_ref[...] = _adam(w_ref[...], g_ref[...], m_ref[...], v_ref[...])

    return pl.pallas_call(
        body, name=name, grid=(1,), in_specs=[_whole((r, c))] * 4, out_specs=[_whole((r, c))] * 3,
        out_shape=[jax.ShapeDtypeStruct((r, c), F32)] * 3, compiler_params=_params("arbitrary"),
    )(w, g, m, v)


def _pack(arrays):
    flat = jnp.concatenate([a.reshape(-1).astype(F32) for a in arrays])
    rows = -(-flat.shape[0] // 1024) * 8
    return jnp.pad(flat, (0, rows * 128 - flat.shape[0])).reshape(rows, 128)


def _unpack(slab, like):
    flat = slab.reshape(-1)
    out, at = [], 0
    for a in like:
        out.append(flat[at:at + a.size].reshape(a.shape))
        at += a.size
    return out


def kernel(x, mem, mem_norm, mix_pre_norm, mix_post_norm, w_in, pool_maps, pool_scale, conf_dw_w, conf_dw_b, conf_ln_g, conf_ln_b, sconv_w, w_out, xattn_pre_norm, xattn_post_norm, xattn_wq, xattn_wk, xattn_wv, xattn_wo, ffn_pre_norm, ffn_post_norm, ffn_w_up, ffn_conv_w, ffn_w_down, loss_target, m_mem_norm, m_mix_pre_norm, m_mix_post_norm, m_w_in, m_pool_maps, m_pool_scale, m_conf_dw_w, m_conf_dw_b, m_conf_ln_g, m_conf_ln_b, m_sconv_w, m_w_out, m_xattn_pre_norm, m_xattn_post_norm, m_xattn_wq, m_xattn_wk, m_xattn_wv, m_xattn_wo, m_ffn_pre_norm, m_ffn_post_norm, m_ffn_w_up, m_ffn_conv_w, m_ffn_w_down, v_mem_norm, v_mix_pre_norm, v_mix_post_norm, v_w_in, v_pool_maps, v_pool_scale, v_conf_dw_w, v_conf_dw_b, v_conf_ln_g, v_conf_ln_b, v_sconv_w, v_w_out, v_xattn_pre_norm, v_xattn_post_norm, v_xattn_wq, v_xattn_wk, v_xattn_wv, v_xattn_wo, v_ffn_pre_norm, v_ffn_post_norm, v_ffn_w_up, v_ffn_conv_w, v_ffn_w_down):
    weights = dict(mem_norm=mem_norm, mix_pre_norm=mix_pre_norm, mix_post_norm=mix_post_norm, w_in=w_in, pool_maps=pool_maps, pool_scale=pool_scale, conf_dw_w=conf_dw_w, conf_dw_b=conf_dw_b, conf_ln_g=conf_ln_g, conf_ln_b=conf_ln_b, sconv_w=sconv_w, w_out=w_out, xattn_pre_norm=xattn_pre_norm, xattn_post_norm=xattn_post_norm, xattn_wq=xattn_wq, xattn_wk=xattn_wk, xattn_wv=xattn_wv, xattn_wo=xattn_wo, ffn_pre_norm=ffn_pre_norm, ffn_post_norm=ffn_post_norm, ffn_w_up=ffn_w_up, ffn_conv_w=ffn_conv_w, ffn_w_down=ffn_w_down)
    mom1 = dict(mem_norm=m_mem_norm, mix_pre_norm=m_mix_pre_norm, mix_post_norm=m_mix_post_norm, w_in=m_w_in, pool_maps=m_pool_maps, pool_scale=m_pool_scale, conf_dw_w=m_conf_dw_w, conf_dw_b=m_conf_dw_b, conf_ln_g=m_conf_ln_g, conf_ln_b=m_conf_ln_b, sconv_w=m_sconv_w, w_out=m_w_out, xattn_pre_norm=m_xattn_pre_norm, xattn_post_norm=m_xattn_post_norm, xattn_wq=m_xattn_wq, xattn_wk=m_xattn_wk, xattn_wv=m_xattn_wv, xattn_wo=m_xattn_wo, ffn_pre_norm=m_ffn_pre_norm, ffn_post_norm=m_ffn_post_norm, ffn_w_up=m_ffn_w_up, ffn_conv_w=m_ffn_conv_w, ffn_w_down=m_ffn_w_down)
    mom2 = dict(mem_norm=v_mem_norm, mix_pre_norm=v_mix_pre_norm, mix_post_norm=v_mix_post_norm, w_in=v_w_in, pool_maps=v_pool_maps, pool_scale=v_pool_scale, conf_dw_w=v_conf_dw_w, conf_dw_b=v_conf_dw_b, conf_ln_g=v_conf_ln_g, conf_ln_b=v_conf_ln_b, sconv_w=v_sconv_w, w_out=v_w_out, xattn_pre_norm=v_xattn_pre_norm, xattn_post_norm=v_xattn_post_norm, xattn_wq=v_xattn_wq, xattn_wk=v_xattn_wk, xattn_wv=v_xattn_wv, xattn_wo=v_xattn_wo, ffn_pre_norm=v_ffn_pre_norm, ffn_post_norm=v_ffn_post_norm, ffn_w_up=v_ffn_w_up, ffn_conv_w=v_ffn_conv_w, ffn_w_down=v_ffn_w_down)
    names = list(weights)

    nl, d = mix_pre_norm.shape
    x0, mem0, target = _to_steps(x[0]), mem[0], _to_steps(loss_target[0])
    dp, dc, ds, *_ = _mix_dims(d)
    pg = dp // len(POOL_WINDOWS)
    me = 4 * lax.axis_index("x") + 2 * lax.axis_index("y") + lax.axis_index("c")

    big = ["w_in", "w_out", "xattn_wq", "xattn_wk", "xattn_wv", "xattn_wo", "ffn_w_up", "ffn_w_down"]
    transposed = ("w_in", "ffn_w_up")

    def row_shard(n, a):
        return a.transpose(0, 2, 1) if n in transposed else a

    shards = [row_shard(n, weights[n]).astype(BF16) for n in big]
    layer_w = [dict(zip(big, [a[0] for a in _allgather([s[0:1] for s in shards], "gather_weights0")]))]
    taps = ["conf_dw_w", "sconv_w", "ffn_conv_w"]
    tap_slab = _pack([weights[n] for n in taps])
    tap_all = _allgather([tap_slab[None]], "gather_taps")[0][0].reshape(N_DEV, *tap_slab.shape)
    tap_parts = [_unpack(tap_all[p], [weights[n] for n in taps]) for p in range(N_DEV)]
    wdw, wsc, wcf = (jnp.concatenate([tap_parts[p][i] for p in range(N_DEV)], axis=-1) for i in range(3))

    def g3(a):
        return a.reshape(a.shape[0], 1, a.shape[-1])

    mbd = jnp.zeros((nl, dp, dp), F32)
    for gi in range(len(POOL_WINDOWS)):
        mbd = mbd.at[:, gi * pg:(gi + 1) * pg, gi * pg:(gi + 1) * pg].set(pool_maps[:, gi])
    mbd = mbd.astype(BF16)
    pre1, post1, pre2, post2, pre3, post3 = (g3(weights[n]) for n in (
        "mix_pre_norm", "mix_post_norm", "xattn_pre_norm", "xattn_post_norm", "ffn_pre_norm", "ffn_post_norm"))
    pscale3, bdw3, lng3, lnb3 = g3(pool_scale), g3(conf_dw_b), g3(conf_ln_g), g3(conf_ln_b)
    memg3 = mem_norm.reshape(1, 1, d)

    def mm(a, b, mode, dt, name, tm=512, tn=1024, tk=1024, a_outer=True):
        return _matmul(a, b, mode, dt, name, tm=tm, tn=tn, tk=tk, a_outer=a_outer)

    mem_n = _prenorm(mem0, memg3, 0, "mem_norm")
    xs = x0
    h = _prenorm(xs, pre1, 0, "pre_norm0")
    saved = []
    for l in range(nl):
        ps_l = pscale3
        if l + 1 < nl:
            lands = [lax.empty((N_DEV * s.shape[1], s.shape[2]), BF16) for s in shards]
            flying = _split_start([s[l + 1] for s in shards], lands, _gather_piece, f"gather_start{l + 1}")
            ps_l = pscale3 + flying[-1][0, 0]
        w = layer_w[l]
        s = {"x": xs, "h": h}
        s["z"] = mm(h, w["w_in"], "nt", F32, f"z{l}", tn=4096)
        s["cat"], s["c"] = _mix_fwd(s["z"], mbd, ps_l, wdw, bdw3, lng3, lnb3, wsc, l, f"mix_fwd{l}")
        s["y1"] = mm(s["cat"], w["w_out"], "nn", F32, f"y1_{l}")
        s["x1"], s["h1"] = _resnorm(xs, s["y1"], post1, l, pre2, l, f"resnorm1_{l}")
        s["q"] = mm(s["h1"], w["xattn_wq"], "nn", BF16, f"q{l}")
        s["k"] = mm(mem_n, w["xattn_wk"], "nn", BF16, f"k{l}")
        s["v"] = mm(mem_n, w["xattn_wv"], "nn", BF16, f"v{l}")
        s["o"] = _attn_fwd(s["q"], s["k"], s["v"], f"attn_fwd{l}")
        s["y2"] = mm(s["o"], w["xattn_wo"], "nn", F32, f"y2_{l}")
        s["x2"], s["h2"] = _resnorm(s["x1"], s["y2"], post2, l, pre3, l, f"resnorm2_{l}")
        s["u"] = mm(s["h2"], w["ffn_w_up"], "nt", F32, f"u{l}", tm=1024, tn=1408, a_outer=False)
        s["a"] = _ffn_act_fwd(s["u"], wcf, l, f"ffn_act{l}")
        s["y3"] = mm(s["a"], w["ffn_w_down"], "nn", F32, f"y3_{l}", tk=4096)
        if l + 1 < nl:
            xs, h = _resnorm(s["x2"], s["y3"], post3, l, pre1, l + 1, f"resnorm3_{l}")
            layer_w.append(dict(zip(big, _split_wait(flying, xs, _gather_piece, f"gather_wait{l + 1}"))))
        saved.append(s)

    last = saved[-1]
    dxn, dy3, dg_post3, loss_lanes = _loss_head(last["x2"], last["y3"], post3, nl - 1, target, "loss_head")
    loss = lax.psum(loss_lanes[0, 0], ("x", "y", "c"))

    recvs = [lax.empty((nl, N_DEV, s.shape[1], d), BF16) for s in shards]
    small = {n: [None] * nl for n in names if n not in big and n != "mem_norm"}
    small["ffn_post_norm"][nl - 1] = dg_post3
    dmem_n = jnp.zeros(mem0.shape, F32)
    flying = None
    for l in reversed(range(nl)):
        s, w = saved[l], layer_w[l]
        wc_l = wcf if flying is None else wcf + flying[-1][0, 0]
        gw = {}
        da = mm(dy3, w["ffn_w_down"], "nt", F32, f"da{l}", tn=1408, a_outer=False)
        gw["ffn_w_down"] = mm(s["a"], dy3, "tn", BF16, f"dw_down{l}", tm=1408)
        du, small["ffn_conv_w"][l] = _ffn_act_bwd(s["u"], da, wc_l, l, f"ffn_act_bwd{l}")
        dh2 = mm(du, w["ffn_w_up"], "nn", F32, f"dh2_{l}", tk=8192)
        gw["ffn_w_up"] = mm(du, s["h2"], "tn", BF16, f"dw_up{l}", tm=2816)
        dx2, small["ffn_pre_norm"][l], dy2, small["xattn_post_norm"][l] = _norm_bwd(
            dxn, dh2, s["x2"], pre3, l, f"norm_bwd3_{l}", s["y2"], post2, l)
        do = mm(dy2, w["xattn_wo"], "nt", BF16, f"do{l}")
        gw["xattn_wo"] = mm(s["o"], dy2, "tn", BF16, f"dw_o{l}")
        dq, dk, dv = _attn_bwd(s["q"], s["k"], s["v"], do, f"attn_bwd{l}")
        dkb, dvb = dk.astype(BF16), dv.astype(BF16)
        dh1 = mm(dq, w["xattn_wq"], "nt", F32, f"dh1_{l}")
        gw["xattn_wq"] = mm(s["h1"], dq, "tn", BF16, f"dw_q{l}")
        gw["xattn_wk"] = mm(mem_n, dkb, "tn", BF16, f"dw_k{l}")
        gw["xattn_wv"] = mm(mem_n, dvb, "tn", BF16, f"dw_v{l}")
        dmem_n = dmem_n + mm(dkb, w["xattn_wk"], "nt", F32, f"dmem_k{l}") \
            + mm(dvb, w["xattn_wv"], "nt", F32, f"dmem_v{l}")
        dx1, small["xattn_pre_norm"][l], dy1, small["mix_post_norm"][l] = _norm_bwd(
            dx2, dh1, s["x1"], pre2, l, f"norm_bwd2_{l}", s["y1"], post1, l)
        dcat = mm(dy1, w["w_out"], "nt", F32, f"dcat{l}")
        gw["w_out"] = mm(s["cat"], dy1, "tn", BF16, f"dw_out{l}")
        dz, dmbd, dps, dwdw, dbdw, dlng, dlnb, dwsc = _mix_bwd(
            dcat, s["z"], s["c"], mbd, pscale3, wdw, lng3, lnb3, wsc, l, f"mix_bwd{l}")
        small["pool_maps"][l] = jnp.stack([dmbd[gi * pg:(gi + 1) * pg, gi * pg:(gi + 1) * pg]
                                           for gi in range(len(POOL_WINDOWS))])
        small["pool_scale"][l], small["conf_dw_w"][l], small["conf_dw_b"][l] = dps, dwdw, dbdw
        small["conf_ln_g"][l], small["conf_ln_b"][l], small["sconv_w"][l] = dlng, dlnb, dwsc
        dh = mm(dz, w["w_in"], "nn", F32, f"dh{l}", tk=4096)
        gw["w_in"] = mm(dz, s["h"], "tn", BF16, f"dw_in{l}", tm=4096)
        if l > 0:
            dxn, small["mix_pre_norm"][l], dy3, small["ffn_post_norm"][l - 1] = _norm_bwd(
                dx1, dh, s["x"], pre1, l, f"norm_bwd1_{l}", saved[l - 1]["y3"], post3, l - 1)
        else:
            dxn, small["mix_pre_norm"][l] = _norm_bwd(dx1, dh, s["x"], pre1, l, "norm_bwd1_0")
        if flying is not None:
            recvs = _split_wait(flying, dxn, _scatter_piece(l + 1), f"scatter_wait{l + 1}")
        flying = _split_start([gw[n] for n in big], recvs, _scatter_piece(l), f"scatter_start{l}")
    grad_x = _from_steps(dxn)[None]
    _, dg_mem = _norm_bwd(jnp.zeros(mem0.shape, F32), dmem_n, mem0, memg3, 0, "norm_bwd_mem")

    small_names = [n for n in names if n not in big]
    partial = {n: (dg_mem.reshape(d) if n == "mem_norm" else
                   jnp.stack([g.reshape(g.shape[-1]) if g.shape[0] == 1 and weights[n].ndim == 2 else g
                              for g in small[n]])) for n in small_names}
    slab = _pack([partial[n] for n in small_names])
    gathered = _allgather([slab[None]], "gather_small_grads")[0][0].reshape(N_DEV, *slab.shape)
    summed = dict(zip(small_names, _unpack(_sum_sources(gathered, "sum_small_grads"), [partial[n] for n in small_names])))
    grad = {}
    for n in small_names:
        g = summed[n]
        if n in taps:
            width = weights[n].shape[-1]
            g = lax.dynamic_slice_in_dim(g, me * width, width, axis=g.ndim - 1)
        grad[n] = g

    delta, new_m, new_v = {}, {}, {}
    upd = _adam_flat(_pack([weights[n] for n in small_names]), _pack([grad[n] for n in small_names]),
                     _pack([mom1[n] for n in small_names]), _pack([mom2[n] for n in small_names]), "adam_small")
    for out, slab_o in zip((delta, new_m, new_v), upd):
        out.update(zip(small_names, _unpack(slab_o, [weights[n] for n in small_names])))
    recvs = _split_wait(flying, upd[0], _scatter_piece(0), "scatter_wait0")
    for n, recv in zip(big, recvs):
        res = _adam_sharded(recv, row_shard(n, weights[n]), row_shard(n, mom1[n]), row_shard(n, mom2[n]), f"adam_{n}")
        grad[n], delta[n], new_m[n], new_v[n] = (row_shard(n, r) for r in res)

    return (loss, grad_x, *[grad[n] for n in names], *[delta[n] for n in names],
            *[new_m[n] for n in names], *[new_v[n] for n in names])
```

```python
import jax
import jax.numpy as jnp
from jax import lax
from jax.experimental import pallas as pl
from jax.experimental.pallas import tpu as pltpu

F32, BF16 = jnp.float32, jnp.bfloat16
EPS = 1e-6
POOL_WINDOWS = (2, 4, 8, 16)
MAX_WINDOW = 16
CONF_TAPS, SHORT_TAPS = 31, 3
CONF_HALO, POOL_HALO, SHORT_HALO = 256, 128, 16
ROW_CHUNK = 64
HEADS = 4
N_DEV = 8
ADAM_LR, ADAM_B1, ADAM_B2, ADAM_EPS, ADAM_WD, ADAM_STEP = 0.001, 0.9, 0.999, 1e-08, 0.01, 10
VMEM_LIMIT_V7X = 56 * 2**20
MESH = pl.DeviceIdType.MESH
ANY = pl.BlockSpec(memory_space=pl.ANY)
HBM = pl.BlockSpec(memory_space=pltpu.HBM)
SEM = pl.BlockSpec(memory_space=pltpu.SEMAPHORE)
EFFECT = pltpu.SideEffectType.DATAFLOW_SIDE_EFFECTING

TILE_NORM, TILE_MIX, TILE_FFN, TILE_ATTN, TILE_ADAM = 256, 512, 256, 512, 256


def _params(*sem):
    return pltpu.CompilerParams(dimension_semantics=sem, vmem_limit_bytes=VMEM_LIMIT_V7X)


def _sig(x):
    return 1.0 / (1.0 + jnp.exp(-x))


def _rms(x):
    r = lax.rsqrt(jnp.mean(x * x, axis=-1, keepdims=True) + EPS)
    return x * r, r


def _rms_bwd(dout, g, n, r):
    dn = dout * g
    return r * (dn - n * jnp.mean(dn * n, axis=-1, keepdims=True))


def _rows(tt, c):
    return pl.BlockSpec((tt, c), lambda i: (i, 0))


def _whole(shape):
    return pl.BlockSpec(shape, lambda i: (0,) * len(shape))


def _layer(shape, l):
    return pl.BlockSpec((None,) + shape, lambda i: (l,) + (0,) * len(shape))


def _colsum(x):
    return jnp.sum(x, axis=0, keepdims=True)


_DIMS = {"nn": (((1,), (0,)), ((), ())), "nt": (((1,), (1,)), ((), ())), "tn": (((0,), (0,)), ((), ()))}


def _matmul(a, b, mode, out_dtype, name, *, tm, tn, tk, a_outer=True):
    if mode == "nn":
        (m, k), (k2, n) = a.shape, b.shape
    elif mode == "nt":
        (m, k), (n, k2) = a.shape, b.shape
    else:
        (k, m), (k2, n) = a.shape, b.shape
    assert k == k2, (name, a.shape, b.shape)
    tm, tn, tk = min(tm, m), min(tn, n), min(tk, k)
    assert m % tm == 0 and n % tn == 0 and k % tk == 0, (name, m, n, k, tm, tn, tk)
    gm, gn, gk = m // tm, n // tn, k // tk

    def ij(g0, g1):
        return (g0, g1) if a_outer else (g1, g0)

    def a_map(g0, g1, kk):
        i, _ = ij(g0, g1)
        return (kk, i) if mode == "tn" else (i, kk)

    def b_map(g0, g1, kk):
        _, j = ij(g0, g1)
        return (j, kk) if mode == "nt" else (kk, j)

    def o_map(g0, g1, kk):
        return ij(g0, g1)

    a_block = (tk, tm) if mode == "tn" else (tm, tk)
    b_block = (tn, tk) if mode == "nt" else (tk, tn)
    dims = _DIMS[mode]

    def body(a_ref, b_ref, o_ref, *acc):
        p = lax.dot_general(a_ref[...].astype(BF16), b_ref[...].astype(BF16), dims, preferred_element_type=F32)
        if gk == 1:
            o_ref[...] = p.astype(o_ref.dtype)
        else:
            kk = pl.program_id(2)

            @pl.when(kk == 0)
            def _():
                acc[0][...] = p

            @pl.when(kk > 0)
            def _():
                acc[0][...] += p

            @pl.when(kk == gk - 1)
            def _():
                o_ref[...] = acc[0][...].astype(o_ref.dtype)

    return pl.pallas_call(
        body, name=name, grid=(gm, gn, gk) if a_outer else (gn, gm, gk),
        in_specs=[pl.BlockSpec(a_block, a_map), pl.BlockSpec(b_block, b_map)],
        out_specs=pl.BlockSpec((tm, tn), o_map),
        out_shape=jax.ShapeDtypeStruct((m, n), out_dtype),
        scratch_shapes=[pltpu.VMEM((tm, tn), F32)] if gk > 1 else [],
        compiler_params=_params("parallel", "parallel", "arbitrary"),
    )(a, b)


def _prenorm(x, g3, l, name):
    t, d = x.shape
    tt = min(TILE_NORM, t)

    def body(x_ref, g_ref, h_ref):
        n, _ = _rms(x_ref[...])
        h_ref[...] = (n * g_ref[...]).astype(BF16)

    return pl.pallas_call(
        body, name=name, grid=(t // tt,),
        in_specs=[_rows(tt, d), _layer((1, d), l)], out_specs=_rows(tt, d),
        out_shape=jax.ShapeDtypeStruct((t, d), BF16), compiler_params=_params("parallel"),
    )(x, g3)


def _resnorm(x, y, gpost3, l, gnext3, l2, name):
    t, d = x.shape
    tt = min(TILE_NORM, t)

    def body(x_ref, y_ref, gp_ref, gn_ref, xo_ref, h_ref):
        n, _ = _rms(y_ref[...])
        xn = x_ref[...] + n * gp_ref[...]
        xo_ref[...] = xn
        n2, _ = _rms(xn)
        h_ref[...] = (n2 * gn_ref[...]).astype(BF16)

    return pl.pallas_call(
        body, name=name, grid=(t // tt,),
        in_specs=[_rows(tt, d), _rows(tt, d), _layer((1, d), l), _layer((1, d), l2)],
        out_specs=[_rows(tt, d), _rows(tt, d)],
        out_shape=[jax.ShapeDtypeStruct((t, d), F32), jax.ShapeDtypeStruct((t, d), BF16)],
        compiler_params=_params("parallel"),
    )(x, y, gpost3, gnext3)


def _loss_head(x, y, gpost3, l, target, name):
    t, d = x.shape
    tt = min(TILE_NORM, t)

    def body(x_ref, y_ref, g_ref, t_ref, dxn_ref, dy_ref, dg_ref, loss_ref):
        @pl.when(pl.program_id(0) == 0)
        def _():
            dg_ref[...] = jnp.zeros_like(dg_ref)
            loss_ref[...] = jnp.zeros_like(loss_ref)

        g = g_ref[...]
        n, r = _rms(y_ref[...])
        diff = x_ref[...] + n * g - t_ref[...]
        loss_ref[...] += 0.5 * jnp.sum(jnp.mean(diff * diff, axis=-1, keepdims=True))
        dxn = diff * (1.0 / d)
        dxn_ref[...] = dxn
        dy_ref[...] = _rms_bwd(dxn, g, n, r).astype(BF16)
        dg_ref[...] += _colsum(dxn * n)

    return pl.pallas_call(
        body, name=name, grid=(t // tt,),
        in_specs=[_rows(tt, d), _rows(tt, d), _layer((1, d), l), _rows(tt, d)],
        out_specs=[_rows(tt, d), _rows(tt, d), _whole((1, d)), _whole((1, 128))],
        out_shape=[jax.ShapeDtypeStruct((t, d), F32), jax.ShapeDtypeStruct((t, d), BF16),
                   jax.ShapeDtypeStruct((1, d), F32), jax.ShapeDtypeStruct((1, 128), F32)],
        compiler_params=_params("arbitrary"),
    )(x, y, gpost3, target)


def _norm_bwd(dxn, dh, x_in, gpre3, l, name, y_prev=None, gpost3=None, l_prev=None):
    t, d = x_in.shape
    tt = min(TILE_NORM, t)
    has_prev = y_prev is not None

    def body(*refs):
        if has_prev:
            dxn_ref, dh_ref, x_ref, g_ref, y_ref, g2_ref, dx_ref, dg_ref, dy_ref, dg2_ref = refs
        else:
            dxn_ref, dh_ref, x_ref, g_ref, dx_ref, dg_ref = refs

        @pl.when(pl.program_id(0) == 0)
        def _():
            dg_ref[...] = jnp.zeros_like(dg_ref)
            if has_prev:
                dg2_ref[...] = jnp.zeros_like(dg2_ref)

        dh_v = dh_ref[...]
        n, r = _rms(x_ref[...])
        dx = dxn_ref[...] + _rms_bwd(dh_v, g_ref[...], n, r)
        dx_ref[...] = dx
        dg_ref[...] += _colsum(dh_v * n)
        if has_prev:
            n2, r2 = _rms(y_ref[...])
            dy_ref[...] = _rms_bwd(dx, g2_ref[...], n2, r2).astype(BF16)
            dg2_ref[...] += _colsum(dx * n2)

    in_specs = [_rows(tt, d), _rows(tt, d), _rows(tt, d), _layer((1, d), l)]
    out_specs = [_rows(tt, d), _whole((1, d))]
    out_shape = [jax.ShapeDtypeStruct((t, d), F32), jax.ShapeDtypeStruct((1, d), F32)]
    args = [dxn, dh, x_in, gpre3]
    if has_prev:
        in_specs += [_rows(tt, d), _layer((1, d), l_prev)]
        out_specs += [_rows(tt, d), _whole((1, d))]
        out_shape += [jax.ShapeDtypeStruct((t, d), BF16), jax.ShapeDtypeStruct((1, d), F32)]
        args += [y_prev, gpost3]
    return pl.pallas_call(
        body, name=name, grid=(t // tt,), in_specs=in_specs, out_specs=out_specs, out_shape=out_shape,
        compiler_params=_params("arbitrary"),
    )(*args)


def _to_steps(a):
    t = a.shape[0]
    return a.reshape(8, t // 8, -1).transpose(1, 0, 2).reshape(a.shape)


def _from_steps(a):
    t = a.shape[0]
    return a.reshape(t // 8, 8, -1).transpose(1, 0, 2).reshape(a.shape)


def _al(v):
    return v if isinstance(v, int) else pl.multiple_of(v, 8)


def _chunks(n_rows, fn, unroll=1):
    def step(r, carry):
        fn(pl.multiple_of(r * ROW_CHUNK, ROW_CHUNK))
        return carry
    lax.fori_loop(0, n_rows // ROW_CHUNK, step, 0, unroll=unroll)


def _fold8(a):
    return a.reshape(a.shape[0] // 8, 8, a.shape[1]).sum(axis=0)


def _shift_down(a):
    row = lax.broadcasted_iota(jnp.int32, a.shape, 0)
    return jnp.where(row % 8 == 0, 0.0, pltpu.roll(a, 1, 0))


def _shift_up(a):
    row = lax.broadcasted_iota(jnp.int32, a.shape, 0)
    return jnp.where(row % 8 == 7, 0.0, pltpu.roll(a, a.shape[0] - 1, 0))


def _prev_block(h, c, tt, t):
    return pl.BlockSpec((h, c), lambda i: (jnp.where(i == 0, t // h - 1, i * (tt // h) - 1), 0))


def _next_block(h, c, tt, t):
    return pl.BlockSpec((h, c), lambda i: (jnp.where(i == t // tt - 1, 0, (i + 1) * (tt // h)), 0))


def _taps(w_ref, buf, start, taps, rc, lanes=slice(None)):
    acc = w_ref[0:1, lanes] * buf[pl.ds(_al(start), rc), lanes]
    for k in range(1, taps):
        acc = acc + w_ref[k:k + 1, lanes] * buf[pl.ds(_al(start + 8 * k), rc), lanes]
    return acc


def _taps_rev(w_ref, buf, start, taps, rc, lanes=slice(None)):
    acc = w_ref[0:1, lanes] * buf[pl.ds(_al(start + 8 * (taps - 1)), rc), lanes]
    for k in range(1, taps):
        acc = acc + w_ref[k:k + 1, lanes] * buf[pl.ds(_al(start + 8 * (taps - 1 - k)), rc), lanes]
    return acc


def _mix_dims(d):
    dp = d // 4
    dc = 3 * d // 8
    ds = d - dp - dc
    oa, og = dp, dp + dc
    ob = dp + 2 * dc
    oc, ox = ob + ds, ob + 2 * ds
    return dp, dc, ds, oa, og, ob, oc, ox, ox + ds


def _pool_consts(dp):
    win = jnp.repeat(jnp.asarray(POOL_WINDOWS, F32), dp // len(POOL_WINDOWS))[None, :]
    mask = (jnp.arange(MAX_WINDOW, dtype=F32)[:, None] < win).astype(F32)
    return mask, win


def _pool_count(row0, rc, dp, seg, wl):
    r = lax.broadcasted_iota(jnp.int32, (rc, dp), 0) + row0
    return jnp.minimum(((r & 7) * seg + (r >> 3) + 1).astype(F32), wl)


def _mix_fwd(z, mbd, pscale3, wdw, bdw3, lng3, lnb3, wsc, l, name):
    t, din = z.shape
    dp, dc, ds, oa, og, ob, oc, ox, din2 = _mix_dims(din * 8 // 17)
    assert din2 == din
    d = ob
    tt = min(TILE_MIX, t)
    hp, hc, hs, rc = POOL_HALO, CONF_HALO, SHORT_HALO, ROW_CHUNK
    assert tt % hc == 0 and t % tt == 0
    seg = t // 8
    pmask, wlane = _pool_consts(dp)

    def body(z_ref, zpa_ref, zpb_ref, mbd_ref, ps_ref, pmask_ref, wl_ref, wdw_ref, bdw_ref, lng_ref, lnb_ref, wsc_ref,
             cat_ref, c_ref, pbuf, vbuf, sbuf):
        i = pl.program_id(0)
        pbuf[0:hp, :] = zpa_ref[hc - hp:hc, 0:dp]

        def prev(r0):
            rows = pl.ds(r0, rc)
            vbuf[rows, :] = zpa_ref[rows, oa:oa + dc] * _sig(zpa_ref[rows, og:og + dc])
        _chunks(hc, prev)
        sbuf[0:hs, :] = zpb_ref[:, oc:oc + ds] * zpb_ref[:, ox:ox + ds]

        @pl.when(i == 0)
        def _():
            pbuf[0:hp, :] = _shift_down(pbuf[0:hp, :])
            vbuf[0:hc, :] = _shift_down(vbuf[0:hc, :])
            sbuf[0:hs, :] = _shift_down(sbuf[0:hs, :])

        mbd_v, ps, wl = mbd_ref[...], ps_ref[...], wl_ref[...]
        bdw, lng, lnb = bdw_ref[...], lng_ref[...], lnb_ref[...]

        def step(r0):
            rows = pl.ds(r0, rc)
            zp = z_ref[rows, 0:dp]
            pbuf[pl.ds(_al(hp + r0), rc), :] = zp
            vbuf[pl.ds(_al(hc + r0), rc), :] = z_ref[rows, oa:oa + dc] * _sig(z_ref[rows, og:og + dc])
            sbuf[pl.ds(_al(hs + r0), rc), :] = z_ref[rows, oc:oc + ds] * z_ref[rows, ox:ox + ds]
            pooled = _taps_rev(pmask_ref, pbuf, r0 + hp - 8 * (MAX_WINDOW - 1), MAX_WINDOW, rc)
            pooled = pooled / _pool_count(i * tt + r0, rc, dp, seg, wl) - zp
            pm = jnp.dot(pooled.astype(BF16), mbd_v, preferred_element_type=F32)
            cat_ref[rows, 0:dp] = (pm * ps).astype(BF16)
            c = _taps(wdw_ref, vbuf, r0 + hc - 8 * (CONF_TAPS - 1), CONF_TAPS, rc) + bdw
            c_ref[rows, :] = c
            xc = c - jnp.mean(c, axis=-1, keepdims=True)
            nrm = xc * lax.rsqrt(jnp.mean(xc * xc, axis=-1, keepdims=True) + EPS)
            yln = nrm * lng + lnb
            cat_ref[rows, dp:dp + dc] = (yln * _sig(yln)).astype(BF16)
            cv = _taps(wsc_ref, sbuf, r0 + hs - 8 * (SHORT_TAPS - 1), SHORT_TAPS, rc)
            cat_ref[rows, dp + dc:d] = (z_ref[rows, ob:ob + ds] * cv).astype(BF16)
        _chunks(tt, step, unroll=2)

    return pl.pallas_call(
        body, name=name, grid=(t // tt,),
        in_specs=[_rows(tt, din), _prev_block(hc, d, tt, t), _prev_block(hs, din, tt, t),
                  _layer((dp, dp), l), _layer((1, dp), l), _whole((MAX_WINDOW, dp)), _whole((1, dp)),
                  _layer((CONF_TAPS, dc), l), _layer((1, dc), l), _layer((1, dc), l), _layer((1, dc), l),
                  _layer((SHORT_TAPS, ds), l)],
        out_specs=[_rows(tt, d), _rows(tt, dc)],
        out_shape=[jax.ShapeDtypeStruct((t, d), BF16), jax.ShapeDtypeStruct((t, dc), F32)],
        scratch_shapes=[pltpu.VMEM((hp + tt, dp), F32), pltpu.VMEM((hc + tt, dc), F32), pltpu.VMEM((hs + tt, ds), F32)],
        compiler_params=_params("parallel"),
    )(z, z, z, mbd, pscale3, pmask, wlane, wdw, bdw3, lng3, lnb3, wsc)


def _mix_bwd(dcat, z, c, mbd, pscale3, wdw, lng3, lnb3, wsc, l, name):
    t, din = z.shape
    dp, dc, ds, oa, og, ob, oc, ox, _ = _mix_dims(din * 8 // 17)
    d = ob
    tt = min(TILE_MIX, t)
    nt = t // tt
    hp, hc, hs, rc = POOL_HALO, CONF_HALO, SHORT_HALO, ROW_CHUNK
    assert tt % hc == 0 and t % tt == 0 and tt >= 8 * MAX_WINDOW
    seg = t // 8
    pmask, wlane = _pool_consts(dp)

    def body(dcat_ref, dcn_ref, z_ref, zpa_ref, zpb_ref, znb_ref, c_ref, cn_ref, mbd_ref, ps_ref, pmask_ref, wl_ref,
             wdw_ref, lng_ref, lnb_ref, wsc_ref,
             dz_ref, dmbd_ref, dps_ref, dwdw_ref, dbdw_ref, dlng_ref, dlnb_ref, dwsc_ref,
             pbuf, qbuf, dpbuf, pbf, vbuf, dcbuf, sbuf, dsbuf, dw8, ds8, ln8, ps8):
        i = pl.program_id(0)
        first, last = i == 0, i == nt - 1

        @pl.when(first)
        def _():
            for ref in (dmbd_ref, dw8, ds8, ln8, ps8):
                ref[...] = jnp.zeros_like(ref)

        mbd_v, ps, wl = mbd_ref[...], ps_ref[...], wl_ref[...]
        lng, lnb = lng_ref[...], lnb_ref[...]

        def ln_silu_bwd(cc, dyb):
            xc = cc - jnp.mean(cc, axis=-1, keepdims=True)
            rstd = lax.rsqrt(jnp.mean(xc * xc, axis=-1, keepdims=True) + EPS)
            nrm = xc * rstd
            yln = nrm * lng + lnb
            s = _sig(yln)
            dyln = dyb * (s * (1.0 + yln * (1.0 - s)))
            dn = dyln * lng
            dcc = rstd * (dn - jnp.mean(dn, axis=-1, keepdims=True) - nrm * jnp.mean(dn * nrm, axis=-1, keepdims=True))
            return dcc, dyln, nrm

        pbuf[0:hp, :] = zpa_ref[hc - hp:hc, 0:dp]

        def prev(r0):
            rows = pl.ds(r0, rc)
            vbuf[rows, :] = zpa_ref[rows, oa:oa + dc] * _sig(zpa_ref[rows, og:og + dc])
        _chunks(hc, prev)
        sbuf[0:hs, :] = zpb_ref[:, oc:oc + ds] * zpb_ref[:, ox:ox + ds]

        @pl.when(first)
        def _():
            pbuf[0:hp, :] = _shift_down(pbuf[0:hp, :])
            vbuf[0:hc, :] = _shift_down(vbuf[0:hc, :])
            sbuf[0:hs, :] = _shift_down(sbuf[0:hs, :])

        def nxt(r0):
            rows = pl.ds(r0, rc)
            dcc, _, _ = ln_silu_bwd(cn_ref[rows, :], dcn_ref[rows, dp:dp + dc])
            dcbuf[pl.ds(_al(tt + r0), rc), :] = dcc
        _chunks(hc, nxt, unroll=2)
        dpm_n = (dcn_ref[0:hp, 0:dp] * ps).astype(BF16)
        qbuf[tt:tt + hp, :] = lax.dot_general(dpm_n, mbd_v, _DIMS["nt"], preferred_element_type=F32) / wl
        dsbuf[tt:tt + hs, :] = dcn_ref[0:hs, dp + dc:d] * znb_ref[:, ob:ob + ds]

        @pl.when(last)
        def _():
            dcbuf[tt:tt + hc, :] = _shift_up(dcbuf[tt:tt + hc, :])
            qbuf[tt:tt + hp, :] = _shift_up(qbuf[tt:tt + hp, :])
            dsbuf[tt:tt + hs, :] = _shift_up(dsbuf[tt:tt + hs, :])

        def fill(r0):
            rows = pl.ds(r0, rc)
            zp = z_ref[rows, 0:dp]
            pbuf[pl.ds(_al(hp + r0), rc), :] = zp
            vbuf[pl.ds(_al(hc + r0), rc), :] = z_ref[rows, oa:oa + dc] * _sig(z_ref[rows, og:og + dc])
            sbuf[pl.ds(_al(hs + r0), rc), :] = z_ref[rows, oc:oc + ds] * z_ref[rows, ox:ox + ds]
            pooled = _taps_rev(pmask_ref, pbuf, r0 + hp - 8 * (MAX_WINDOW - 1), MAX_WINDOW, rc)
            pbf[rows, :] = (pooled / _pool_count(i * tt + r0, rc, dp, seg, wl) - zp).astype(BF16)
            dcc, dyln, nrm = ln_silu_bwd(c_ref[rows, :], dcat_ref[rows, dp:dp + dc])
            dcbuf[rows, :] = dcc
            ln8[0] += _fold8(dyln * nrm)
            ln8[1] += _fold8(dyln)
            ln8[2] += _fold8(dcc)
            dsbuf[rows, :] = dcat_ref[rows, dp + dc:d] * z_ref[rows, ob:ob + ds]
        _chunks(tt, fill, unroll=2)

        pb = pbf[...]
        dya = dcat_ref[:, 0:dp]
        ps8[...] += _fold8(dya * jnp.dot(pb, mbd_v, preferred_element_type=F32))
        dpm = (dya * ps).astype(BF16)
        dmbd_ref[...] += lax.dot_general(pb, dpm, _DIMS["tn"], preferred_element_type=F32)
        dpbuf[...] = lax.dot_general(dpm, mbd_v, _DIMS["nt"], preferred_element_type=F32)

        def quot(r0):
            rows = pl.ds(r0, rc)
            qbuf[rows, :] = dpbuf[rows, :] / _pool_count(i * tt + r0, rc, dp, seg, wl)
        _chunks(tt, quot)

        def back(r0):
            rows = pl.ds(r0, rc)
            dzp = _taps(pmask_ref, qbuf, r0, MAX_WINDOW, rc) - dpbuf[rows, :]
            dz_ref[rows, 0:dp] = dzp.astype(BF16)
            dcc = dcbuf[rows, :]
            for k in range(CONF_TAPS):
                dw8[k] += _fold8(dcc * vbuf[pl.ds(_al(r0 + hc - 8 * (CONF_TAPS - 1 - k)), rc), :])
            dv = _taps_rev(wdw_ref, dcbuf, r0, CONF_TAPS, rc)
            za = z_ref[rows, oa:oa + dc]
            sg = _sig(z_ref[rows, og:og + dc])
            dz_ref[rows, oa:oa + dc] = (dv * sg).astype(BF16)
            dz_ref[rows, og:og + dc] = (dv * za * sg * (1.0 - sg)).astype(BF16)
            cv = _taps(wsc_ref, sbuf, r0 + hs - 8 * (SHORT_TAPS - 1), SHORT_TAPS, rc)
            dz_ref[rows, ob:ob + ds] = (dcat_ref[rows, dp + dc:d] * cv).astype(BF16)
            dcv = dsbuf[rows, :]
            for k in range(SHORT_TAPS):
                ds8[k] += _fold8(dcv * sbuf[pl.ds(_al(r0 + hs - 8 * (SHORT_TAPS - 1 - k)), rc), :])
            dpv = _taps_rev(wsc_ref, dsbuf, r0, SHORT_TAPS, rc)
            dz_ref[rows, oc:oc + ds] = (dpv * z_ref[rows, ox:ox + ds]).astype(BF16)
            dz_ref[rows, ox:ox + ds] = (dpv * z_ref[rows, oc:oc + ds]).astype(BF16)
        _chunks(tt, back)

        @pl.when(last)
        def _():
            dps_ref[...] = jnp.sum(ps8[...], axis=0, keepdims=True)
            dwdw_ref[...] = jnp.sum(dw8[...], axis=1)
            dwsc_ref[...] = jnp.sum(ds8[...], axis=1)
            dlng_ref[...] = jnp.sum(ln8[0], axis=0, keepdims=True)
            dlnb_ref[...] = jnp.sum(ln8[1], axis=0, keepdims=True)
            dbdw_ref[...] = jnp.sum(ln8[2], axis=0, keepdims=True)

    return pl.pallas_call(
        body, name=name, grid=(nt,),
        in_specs=[_rows(tt, d), _next_block(hc, d, tt, t),
                  _rows(tt, din), _prev_block(hc, d, tt, t), _prev_block(hs, din, tt, t), _next_block(hs, din, tt, t),
                  _rows(tt, dc), _next_block(hc, dc, tt, t),
                  _layer((dp, dp), l), _layer((1, dp), l), _whole((MAX_WINDOW, dp)), _whole((1, dp)),
                  _layer((CONF_TAPS, dc), l), _layer((1, dc), l), _layer((1, dc), l), _layer((SHORT_TAPS, ds), l)],
        out_specs=[_rows(tt, din), _whole((dp, dp)), _whole((1, dp)), _whole((CONF_TAPS, dc)), _whole((1, dc)),
                   _whole((1, dc)), _whole((1, dc)), _whole((SHORT_TAPS, ds))],
        out_shape=[jax.ShapeDtypeStruct((t, din), BF16), jax.ShapeDtypeStruct((dp, dp), F32),
                   jax.ShapeDtypeStruct((1, dp), F32), jax.ShapeDtypeStruct((CONF_TAPS, dc), F32),
                   jax.ShapeDtypeStruct((1, dc), F32), jax.ShapeDtypeStruct((1, dc), F32),
                   jax.ShapeDtypeStruct((1, dc), F32), jax.ShapeDtypeStruct((SHORT_TAPS, ds), F32)],
        scratch_shapes=[pltpu.VMEM((hp + tt, dp), F32), pltpu.VMEM((tt + hp, dp), F32), pltpu.VMEM((tt, dp), F32),
                        pltpu.VMEM((tt, dp), BF16), pltpu.VMEM((hc + tt, dc), F32), pltpu.VMEM((tt + hc, dc), F32),
                        pltpu.VMEM((hs + tt, ds), F32), pltpu.VMEM((tt + hs, ds), F32),
                        pltpu.VMEM((CONF_TAPS, 8, dc), F32), pltpu.VMEM((SHORT_TAPS, 8, ds), F32),
                        pltpu.VMEM((3, 8, dc), F32), pltpu.VMEM((8, dp), F32)],
        compiler_params=_params("arbitrary"),
    )(dcat, dcat, z, z, z, z, c, c, mbd, pscale3, pmask, wlane, wdw, lng3, lnb3, wsc)


def _softmax_rows(qh, kh, scale):
    s = lax.dot_general(qh, kh, _DIMS["nt"], preferred_element_type=F32) * scale
    e = jnp.exp(s - jnp.max(s, axis=-1, keepdims=True))
    return e / jnp.sum(e, axis=-1, keepdims=True)


def _attn_fwd(q, k, v, name):
    t, d = q.shape
    m = k.shape[0]
    hd = d // HEADS
    scale = hd ** -0.5
    tt = min(TILE_ATTN, t)

    def body(q_ref, k_ref, v_ref, o_ref):
        for h in range(HEADS):
            sl = slice(h * hd, (h + 1) * hd)
            p = _softmax_rows(q_ref[:, sl], k_ref[:, sl], scale)
            o_ref[:, sl] = jnp.dot(p.astype(BF16), v_ref[:, sl], preferred_element_type=F32).astype(BF16)

    return pl.pallas_call(
        body, name=name, grid=(t // tt,),
        in_specs=[_rows(tt, d), _whole((m, d)), _whole((m, d))], out_specs=_rows(tt, d),
        out_shape=jax.ShapeDtypeStruct((t, d), BF16), compiler_params=_params("parallel"),
    )(q, k, v)


def _attn_bwd(q, k, v, do, name):
    t, d = q.shape
    m = k.shape[0]
    hd = d // HEADS
    scale = hd ** -0.5
    tt = min(TILE_ATTN, t)

    def body(q_ref, k_ref, v_ref, do_ref, dq_ref, dk_ref, dv_ref):
        @pl.when(pl.program_id(0) == 0)
        def _():
            dk_ref[...] = jnp.zeros_like(dk_ref)
            dv_ref[...] = jnp.zeros_like(dv_ref)

        for h in range(HEADS):
            sl = slice(h * hd, (h + 1) * hd)
            qh, kh, vh, doh = q_ref[:, sl], k_ref[:, sl], v_ref[:, sl], do_ref[:, sl]
            p = _softmax_rows(qh, kh, scale)
            dv_ref[:, sl] += lax.dot_general(p.astype(BF16), doh, _DIMS["tn"], preferred_element_type=F32)
            dp = lax.dot_general(doh, vh, _DIMS["nt"], preferred_element_type=F32)
            ds = (p * (dp - jnp.sum(dp * p, axis=-1, keepdims=True)) * scale).astype(BF16)
            dq_ref[:, sl] = jnp.dot(ds, kh, preferred_element_type=F32).astype(BF16)
            dk_ref[:, sl] += lax.dot_general(ds, qh, _DIMS["tn"], preferred_element_type=F32)

    return pl.pallas_call(
        body, name=name, grid=(t // tt,),
        in_specs=[_rows(tt, d), _whole((m, d)), _whole((m, d)), _rows(tt, d)],
        out_specs=[_rows(tt, d), _whole((m, d)), _whole((m, d))],
        out_shape=[jax.ShapeDtypeStruct((t, d), BF16), jax.ShapeDtypeStruct((m, d), F32),
                   jax.ShapeDtypeStruct((m, d), F32)],
        compiler_params=_params("arbitrary"),
    )(q, k, v, do)


def _lane_chunks(f):
    w = 256 if f % 256 == 0 else 128 if f % 128 == 0 else f
    return [(c0, w) for c0 in range(0, f, w)]


def _ffn_act_fwd(u, wc, l, name):
    t, f2 = u.shape
    f = f2 // 2
    tt = min(TILE_FFN, t)
    hs, rc = SHORT_HALO, ROW_CHUNK
    lanes = _lane_chunks(f)

    def body(u_ref, up_ref, wc_ref, a_ref, ubuf):
        ubuf[0:hs, :] = up_ref[...]

        @pl.when(pl.program_id(0) == 0)
        def _():
            ubuf[0:hs, :] = _shift_down(ubuf[0:hs, :])

        def step(r0):
            rows = pl.ds(r0, rc)
            ubuf[pl.ds(_al(hs + r0), rc), :] = u_ref[rows, :]
            start = r0 + hs - 8 * (SHORT_TAPS - 1)
            for c0, cw in lanes:
                g = _taps(wc_ref, ubuf, start, SHORT_TAPS, rc, slice(c0, c0 + cw))
                vv = _taps(wc_ref, ubuf, start, SHORT_TAPS, rc, slice(f + c0, f + c0 + cw))
                a_ref[rows, c0:c0 + cw] = (g * _sig(g) * vv).astype(BF16)
        _chunks(tt, step)

    return pl.pallas_call(
        body, name=name, grid=(t // tt,),
        in_specs=[_rows(tt, f2), _prev_block(hs, f2, tt, t), _layer((SHORT_TAPS, f2), l)],
        out_specs=_rows(tt, f), out_shape=jax.ShapeDtypeStruct((t, f), BF16),
        scratch_shapes=[pltpu.VMEM((hs + tt, f2), F32)], compiler_params=_params("parallel"),
    )(u, u, wc)


def _ffn_act_bwd(u, da, wc, l, name):
    t, f2 = u.shape
    f = f2 // 2
    tt = min(TILE_FFN, t)
    nt = t // tt
    hs, rc = SHORT_HALO, ROW_CHUNK
    lanes = _lane_chunks(f)

    def body(u_ref, up_ref, un_ref, da_ref, dan_ref, wc_ref, du_ref, dwc_ref, ubuf, danbuf, dbuf, dw8):
        i = pl.program_id(0)
        first, last = i == 0, i == nt - 1
        ubuf[0:hs, :] = up_ref[...]
        ubuf[hs + tt:hs + tt + hs, :] = un_ref[...]
        danbuf[...] = dan_ref[...]

        @pl.when(first)
        def _():
            dw8[...] = jnp.zeros_like(dw8)
            ubuf[0:hs, :] = _shift_down(ubuf[0:hs, :])

        @pl.when(last)
        def _():
            ubuf[hs + tt:hs + tt + hs, :] = _shift_up(ubuf[hs + tt:hs + tt + hs, :])
            danbuf[...] = _shift_up(danbuf[...])

        def fill(r0):
            ubuf[pl.ds(_al(hs + r0), rc), :] = u_ref[pl.ds(r0, rc), :]
        _chunks(tt, fill)

        def conv_grads(r0, n, da_rows):
            start = r0 + hs - 8 * (SHORT_TAPS - 1)
            for c0, cw in lanes:
                sl_g, sl_v = slice(c0, c0 + cw), slice(f + c0, f + c0 + cw)
                g = _taps(wc_ref, ubuf, start, SHORT_TAPS, n, sl_g)
                vv = _taps(wc_ref, ubuf, start, SHORT_TAPS, n, sl_v)
                dav = da_rows(c0, cw)
                sg = _sig(g)
                dbuf[pl.ds(_al(r0), n), sl_g] = dav * vv * (sg * (1.0 + g * (1.0 - sg)))
                dbuf[pl.ds(_al(r0), n), sl_v] = dav * (g * sg)

        _chunks(tt, lambda r0: conv_grads(r0, rc, lambda c0, cw: da_ref[pl.ds(r0, rc), c0:c0 + cw]))
        conv_grads(tt, hs, lambda c0, cw: danbuf[:, c0:c0 + cw])

        def back(r0):
            rows = pl.ds(r0, rc)
            for c0, cw in lanes:
                for off in (c0, f + c0):
                    sl = slice(off, off + cw)
                    du_ref[rows, sl] = _taps_rev(wc_ref, dbuf, r0, SHORT_TAPS, rc, sl).astype(BF16)
                    dd = dbuf[rows, sl]
                    for k in range(SHORT_TAPS):
                        dw8[k, :, sl] += _fold8(dd * ubuf[pl.ds(_al(r0 + hs - 8 * (SHORT_TAPS - 1 - k)), rc), sl])
        _chunks(tt, back)

        @pl.when(last)
        def _():
            dwc_ref[...] = jnp.sum(dw8[...], axis=1)

    return pl.pallas_call(
        body, name=name, grid=(nt,),
        in_specs=[_rows(tt, f2), _prev_block(hs, f2, tt, t), _next_block(hs, f2, tt, t),
                  _rows(tt, f), _next_block(hs, f, tt, t), _layer((SHORT_TAPS, f2), l)],
        out_specs=[_rows(tt, f2), _whole((SHORT_TAPS, f2))],
        out_shape=[jax.ShapeDtypeStruct((t, f2), BF16), jax.ShapeDtypeStruct((SHORT_TAPS, f2), F32)],
        scratch_shapes=[pltpu.VMEM((hs + tt + hs, f2), F32), pltpu.VMEM((hs, f), F32), pltpu.VMEM((tt + hs, f2), F32),
                        pltpu.VMEM((SHORT_TAPS, 8, f2), F32)],
        compiler_params=_params("arbitrary"),
    )(u, u, u, da, da, wc)


def _place():
    return lax.axis_index("x"), lax.axis_index("y"), lax.axis_index("c")


def _flip(v, bit):
    return 1 - v if bit else v


def _peers(x, y, c):
    out = []
    for kk in range(1, N_DEV):
        px, py, pc = _flip(x, kk & 4), _flip(y, kk & 2), _flip(c, kk & 1)
        out.append((kk - 1, (px, py, pc), 4 * px + 2 * py + pc))
    return out


def _allgather(shards, name):
    nt = len(shards)

    def body(*refs):
        srcs, outs = refs[:nt], refs[nt:2 * nt]
        send_sems, recv_sems, local_sems = refs[2 * nt:]
        x, y, c = _place()
        me, sibling = (x, y, c), (x, y, 1 - c)
        chips = [(1 - x, y), (x, 1 - y), (1 - x, 1 - y)]

        def rows(ti, px, py, pc):
            r = srcs[ti].shape[1]
            return outs[ti].at[:, pl.ds((4 * px + 2 * py + pc) * r, r), :]

        def copy(ti, kk, block, to, src=None):
            return pltpu.make_async_remote_copy(
                src_ref=rows(ti, *block) if src is None else src, dst_ref=rows(ti, *block),
                send_sem=send_sems.at[ti, kk], recv_sem=recv_sems.at[ti, kk], device_id=to, device_id_type=MESH)

        mine = [pltpu.make_async_copy(srcs[ti], rows(ti, *me), local_sems.at[ti]) for ti in range(nt)]
        for cp in mine:
            cp.start()
        first = []
        for ti in range(nt):
            first.append(copy(ti, 0, me, sibling, src=srcs[ti]))
            first += [copy(ti, 1 + j, me, (*chip, c), src=srcs[ti]) for j, chip in enumerate(chips)]
        for cp in first:
            cp.start()
        passed = []
        for j, chip in enumerate(chips):
            for ti in range(nt):
                copy(ti, 1 + j, (*chip, c), me).wait_recv()
                fwd = copy(ti, 4 + j, (*chip, c), sibling)
                fwd.start()
                passed.append(fwd)
        for ti in range(nt):
            copy(ti, 0, sibling, me).wait_recv()
            for j, chip in enumerate(chips):
                copy(ti, 4 + j, (*chip, 1 - c), me).wait_recv()
        for cp in first + passed:
            cp.wait_send()
        for cp in mine:
            cp.wait()

    return pl.pallas_call(
        body, name=name,
        in_specs=[ANY] * nt, out_specs=[ANY] * nt,
        out_shape=[jax.ShapeDtypeStruct((s.shape[0], N_DEV * s.shape[1], s.shape[2]), s.dtype) for s in shards],
        scratch_shapes=[pltpu.SemaphoreType.DMA((nt, 7)), pltpu.SemaphoreType.DMA((nt, 7)),
                        pltpu.SemaphoreType.DMA((nt,))],
    )(*shards)


def _gather_piece(src, land, me, to):
    r = src.shape[0]
    return src, land.at[pl.ds(me * r, r), :]


def _scatter_piece(l):
    def piece(src, land, me, to):
        r = src.shape[0] // N_DEV
        return src.at[pl.ds(to * r, r), :], land.at[l, me]
    return piece


def _split_start(srcs, lands, piece, after, name):
    nt = len(srcs)

    def body(*refs):
        src_refs, land_refs = refs[:nt], refs[nt:2 * nt]
        send_sems, recv_sems, token = refs[2 * nt + 1], refs[2 * nt + 2], refs[4 * nt + 3]
        x, y, c = _place()
        me = 4 * x + 2 * y + c
        for ti in range(nt):
            for slot, peer, flat in _peers(x, y, c):
                src, dst = piece(src_refs[ti], land_refs[ti], me, flat)
                pltpu.make_async_remote_copy(
                    src_ref=src, dst_ref=dst, send_sem=send_sems.at[7 * ti + slot], recv_sem=recv_sems.at[7 * ti + slot],
                    device_id=peer, device_id_type=MESH).start()
        token[...] = jnp.zeros_like(token)

    both = list(srcs) + list(lands)
    return pl.pallas_call(
        body, name=name,
        in_specs=[HBM] * (2 * nt) + [ANY],
        out_specs=[SEM, SEM] + [HBM] * (2 * nt) + [pl.BlockSpec(memory_space=pltpu.VMEM)],
        out_shape=[pltpu.SemaphoreType.DMA((7 * nt,)), pltpu.SemaphoreType.DMA((7 * nt,))]
        + [pltpu.HBM(a.shape, a.dtype) for a in both] + [jax.ShapeDtypeStruct((8, 128), F32)],
        input_output_aliases={i: i + 2 for i in range(2 * nt)},
        compiler_params=pltpu.CompilerParams(has_side_effects=EFFECT),
    )(*[pltpu.with_memory_space_constraint(a, pltpu.HBM) for a in both], after)


def _split_wait(started, after, piece, name):
    send_sems, recv_sems, *both = started[:-1]
    nt = len(both) // 2

    def body(*refs):
        src_refs, land_refs = refs[:nt], refs[nt:2 * nt]
        send_ref, recv_ref, local_sems = refs[2 * nt], refs[2 * nt + 1], refs[4 * nt + 3]
        x, y, c = _place()
        me = 4 * x + 2 * y + c
        own = [pltpu.make_async_copy(*piece(src_refs[ti], land_refs[ti], me, me), local_sems.at[ti]) for ti in range(nt)]
        for cp in own:
            cp.start()
        for ti in range(nt):
            src, dst = piece(src_refs[ti], land_refs[ti], me, me)
            for slot in range(N_DEV - 1):
                cp = pltpu.make_async_remote_copy(
                    src_ref=src, dst_ref=dst, send_sem=send_ref.at[7 * ti + slot], recv_sem=recv_ref.at[7 * ti + slot],
                    device_id=(x, y, c), device_id_type=MESH)
                cp.wait_send()
                cp.wait_recv()
        for cp in own:
            cp.wait()

    outs = pl.pallas_call(
        body, name=name,
        in_specs=[HBM] * (2 * nt) + [SEM, SEM, ANY], out_specs=[HBM] * (2 * nt),
        out_shape=[pltpu.HBM(a.shape, a.dtype) for a in both],
        input_output_aliases={i: i for i in range(2 * nt)},
        scratch_shapes=[pltpu.SemaphoreType.DMA((nt,))],
        compiler_params=pltpu.CompilerParams(has_side_effects=EFFECT),
    )(*both, send_sems, recv_sems, after)
    return outs[nt:]


def _adam(w, g, m, v):
    m2 = ADAM_B1 * m + (1.0 - ADAM_B1) * g
    v2 = ADAM_B2 * v + (1.0 - ADAM_B2) * (g * g)
    m_hat = m2 / (1.0 - ADAM_B1 ** ADAM_STEP)
    v_hat = v2 / (1.0 - ADAM_B2 ** ADAM_STEP)
    return -ADAM_LR * (m_hat / (jnp.sqrt(v_hat) + ADAM_EPS) + ADAM_WD * w), m2, v2


def _adam_sharded(recv, recv_first, w, m, v, lo, hi, name, prev=None):
    nl, r, c = w.shape
    tr = min(TILE_ADAM, r)
    while r % tr:
        tr //= 2

    def body(recv_ref, w_ref, m_ref, v_ref, *rest):
        g_ref, d_ref, m2_ref, v2_ref = rest[-4:]
        g = recv_ref[0].astype(F32)
        for s in range(1, N_DEV):
            g = g + recv_ref[s].astype(F32)
        g_ref[...] = g
        d_ref[...], m2_ref[...], v2_ref[...] = _adam(w_ref[...], g, m_ref[...], v_ref[...])

    blk = pl.BlockSpec((None, tr, c), lambda li, i: (li + lo, i, 0))
    extra = [] if prev is None else list(prev)
    return pl.pallas_call(
        body, name=name, grid=(hi - lo, r // tr),
        in_specs=[pl.BlockSpec((None, N_DEV, tr, c), lambda li, i: (li + lo - recv_first, 0, i, 0)), blk, blk, blk]
        + [ANY] * len(extra),
        out_specs=[blk] * 4, out_shape=[jax.ShapeDtypeStruct((nl, r, c), F32)] * 4,
        input_output_aliases={4 + i: i for i in range(len(extra))},
        compiler_params=_params("parallel", "parallel"),
    )(recv, w, m, v, *extra)


def _sum_sources(parts, name):
    _, r, c = parts.shape

    def body(p_ref, o_ref):
        g = p_ref[0]
        for s in range(1, N_DEV):
            g = g + p_ref[s]
        o_ref[...] = g

    return pl.pallas_call(
        body, name=name, grid=(1,), in_specs=[_whole((N_DEV, r, c))], out_specs=_whole((r, c)),
        out_shape=jax.ShapeDtypeStruct((r, c), F32), compiler_params=_params("arbitrary"),
    )(parts)


def _adam_flat(w, g, m, v, name):
    r, c = w.shape

    def body(w_ref, g_ref, m_ref, v_ref, d_ref, m2_ref, v2_ref):
        d_ref[...], m2_ref[...], v2_ref[...] = _adam(w_ref[...], g_ref[...], m_ref[...], v_ref[...])

    return pl.pallas_call(
        body, name=name, grid=(1,), in_specs=[_whole((r, c))] * 4, out_specs=[_whole((r, c))] * 3,
        out_shape=[jax.ShapeDtypeStruct((r, c), F32)] * 3, compiler_params=_params("arbitrary"),
    )(w, g, m, v)


def _pack(arrays):
    flat = jnp.concatenate([a.reshape(-1).astype(F32) for a in arrays])
    rows = -(-flat.shape[0] // 1024) * 8
    return jnp.pad(flat, (0, rows * 128 - flat.shape[0])).reshape(rows, 128)


def _unpack(slab, like):
    flat = slab.reshape(-1)
    out, at = [], 0
    for a in like:
        out.append(flat[at:at + a.size].reshape(a.shape))
        at += a.size
    return out


def kernel(x, mem, mem_norm, mix_pre_norm, mix_post_norm, w_in, pool_maps, pool_scale, conf_dw_w, conf_dw_b, conf_ln_g, conf_ln_b, sconv_w, w_out, xattn_pre_norm, xattn_post_norm, xattn_wq, xattn_wk, xattn_wv, xattn_wo, ffn_pre_norm, ffn_post_norm, ffn_w_up, ffn_conv_w, ffn_w_down, loss_target, m_mem_norm, m_mix_pre_norm, m_mix_post_norm, m_w_in, m_pool_maps, m_pool_scale, m_conf_dw_w, m_conf_dw_b, m_conf_ln_g, m_conf_ln_b, m_sconv_w, m_w_out, m_xattn_pre_norm, m_xattn_post_norm, m_xattn_wq, m_xattn_wk, m_xattn_wv, m_xattn_wo, m_ffn_pre_norm, m_ffn_post_norm, m_ffn_w_up, m_ffn_conv_w, m_ffn_w_down, v_mem_norm, v_mix_pre_norm, v_mix_post_norm, v_w_in, v_pool_maps, v_pool_scale, v_conf_dw_w, v_conf_dw_b, v_conf_ln_g, v_conf_ln_b, v_sconv_w, v_w_out, v_xattn_pre_norm, v_xattn_post_norm, v_xattn_wq, v_xattn_wk, v_xattn_wv, v_xattn_wo, v_ffn_pre_norm, v_ffn_post_norm, v_ffn_w_up, v_ffn_conv_w, v_ffn_w_down):
    weights = dict(mem_norm=mem_norm, mix_pre_norm=mix_pre_norm, mix_post_norm=mix_post_norm, w_in=w_in, pool_maps=pool_maps, pool_scale=pool_scale, conf_dw_w=conf_dw_w, conf_dw_b=conf_dw_b, conf_ln_g=conf_ln_g, conf_ln_b=conf_ln_b, sconv_w=sconv_w, w_out=w_out, xattn_pre_norm=xattn_pre_norm, xattn_post_norm=xattn_post_norm, xattn_wq=xattn_wq, xattn_wk=xattn_wk, xattn_wv=xattn_wv, xattn_wo=xattn_wo, ffn_pre_norm=ffn_pre_norm, ffn_post_norm=ffn_post_norm, ffn_w_up=ffn_w_up, ffn_conv_w=ffn_conv_w, ffn_w_down=ffn_w_down)
    mom1 = dict(mem_norm=m_mem_norm, mix_pre_norm=m_mix_pre_norm, mix_post_norm=m_mix_post_norm, w_in=m_w_in, pool_maps=m_pool_maps, pool_scale=m_pool_scale, conf_dw_w=m_conf_dw_w, conf_dw_b=m_conf_dw_b, conf_ln_g=m_conf_ln_g, conf_ln_b=m_conf_ln_b, sconv_w=m_sconv_w, w_out=m_w_out, xattn_pre_norm=m_xattn_pre_norm, xattn_post_norm=m_xattn_post_norm, xattn_wq=m_xattn_wq, xattn_wk=m_xattn_wk, xattn_wv=m_xattn_wv, xattn_wo=m_xattn_wo, ffn_pre_norm=m_ffn_pre_norm, ffn_post_norm=m_ffn_post_norm, ffn_w_up=m_ffn_w_up, ffn_conv_w=m_ffn_conv_w, ffn_w_down=m_ffn_w_down)
    mom2 = dict(mem_norm=v_mem_norm, mix_pre_norm=v_mix_pre_norm, mix_post_norm=v_mix_post_norm, w_in=v_w_in, pool_maps=v_pool_maps, pool_scale=v_pool_scale, conf_dw_w=v_conf_dw_w, conf_dw_b=v_conf_dw_b, conf_ln_g=v_conf_ln_g, conf_ln_b=v_conf_ln_b, sconv_w=v_sconv_w, w_out=v_w_out, xattn_pre_norm=v_xattn_pre_norm, xattn_post_norm=v_xattn_post_norm, xattn_wq=v_xattn_wq, xattn_wk=v_xattn_wk, xattn_wv=v_xattn_wv, xattn_wo=v_xattn_wo, ffn_pre_norm=v_ffn_pre_norm, ffn_post_norm=v_ffn_post_norm, ffn_w_up=v_ffn_w_up, ffn_conv_w=v_ffn_conv_w, ffn_w_down=v_ffn_w_down)
    names = list(weights)

    nl, d = mix_pre_norm.shape
    x0, mem0, target = _to_steps(x[0]), mem[0], _to_steps(loss_target[0])
    dp, dc, ds, *_ = _mix_dims(d)
    pg = dp // len(POOL_WINDOWS)
    me = 4 * lax.axis_index("x") + 2 * lax.axis_index("y") + lax.axis_index("c")

    big = ["w_in", "w_out", "xattn_wq", "xattn_wk", "xattn_wv", "xattn_wo", "ffn_w_up", "ffn_w_down"]
    transposed = ("w_in", "ffn_w_up")

    def row_shard(n, a):
        return a.transpose(0, 2, 1) if n in transposed else a

    shards = [row_shard(n, weights[n]).astype(BF16) for n in big]
    taps = ["conf_dw_w", "sconv_w", "ffn_conv_w"]
    tap_slab = _pack([weights[n] for n in taps])
    *first_w, tap_all = _allgather([s[0:1] for s in shards] + [tap_slab[None]], "gather_weights0")
    layer_w = [dict(zip(big, [a[0] for a in first_w]))]
    tap_all = tap_all[0].reshape(N_DEV, *tap_slab.shape)
    tap_parts = [_unpack(tap_all[p], [weights[n] for n in taps]) for p in range(N_DEV)]
    wdw, wsc, wcf = (jnp.concatenate([tap_parts[p][i] for p in range(N_DEV)], axis=-1) for i in range(3))

    def g3(a):
        return a.reshape(a.shape[0], 1, a.shape[-1])

    mbd = jnp.zeros((nl, dp, dp), F32)
    for gi in range(len(POOL_WINDOWS)):
        mbd = mbd.at[:, gi * pg:(gi + 1) * pg, gi * pg:(gi + 1) * pg].set(pool_maps[:, gi])
    mbd = mbd.astype(BF16)
    pre1, post1, pre2, post2, pre3, post3 = (g3(weights[n]) for n in (
        "mix_pre_norm", "mix_post_norm", "xattn_pre_norm", "xattn_post_norm", "ffn_pre_norm", "ffn_post_norm"))
    pscale3, bdw3, lng3, lnb3 = g3(pool_scale), g3(conf_dw_b), g3(conf_ln_g), g3(conf_ln_b)
    memg3 = mem_norm.reshape(1, 1, d)

    def mm(a, b, mode, dt, name, tm=1024, tn=1024, tk=1024, a_outer=True):
        return _matmul(a, b, mode, dt, name, tm=tm, tn=tn, tk=tk, a_outer=a_outer)

    mem_n = _prenorm(mem0, memg3, 0, "mem_norm")
    xs = x0
    h = _prenorm(xs, pre1, 0, "pre_norm0")
    saved = []
    for l in range(nl):
        ps_l = pscale3
        if l + 1 < nl:
            lands = [lax.empty((N_DEV * s.shape[1], s.shape[2]), BF16) for s in shards]
            flying = _split_start([s[l + 1] for s in shards], lands, _gather_piece,
                                  xs if l else layer_w[0]["w_in"], f"gather_start{l + 1}")
            ps_l = pscale3 + flying[-1][0, 0]
        w = layer_w[l]
        s = {"x": xs, "h": h}
        s["z"] = mm(h, w["w_in"], "nt", F32, f"z{l}", tn=4096)
        s["cat"], s["c"] = _mix_fwd(s["z"], mbd, ps_l, wdw, bdw3, lng3, lnb3, wsc, l, f"mix_fwd{l}")
        s["y1"] = mm(s["cat"], w["w_out"], "nn", F32, f"y1_{l}")
        s["x1"], s["h1"] = _resnorm(xs, s["y1"], post1, l, pre2, l, f"resnorm1_{l}")
        s["q"] = mm(s["h1"], w["xattn_wq"], "nn", BF16, f"q{l}")
        s["k"] = mm(mem_n, w["xattn_wk"], "nn", BF16, f"k{l}")
        s["v"] = mm(mem_n, w["xattn_wv"], "nn", BF16, f"v{l}")
        s["o"] = _attn_fwd(s["q"], s["k"], s["v"], f"attn_fwd{l}")
        s["y2"] = mm(s["o"], w["xattn_wo"], "nn", F32, f"y2_{l}")
        s["x2"], s["h2"] = _resnorm(s["x1"], s["y2"], post2, l, pre3, l, f"resnorm2_{l}")
        s["u"] = mm(s["h2"], w["ffn_w_up"], "nt", F32, f"u{l}", tm=1024, tn=1408, a_outer=False)
        s["a"] = _ffn_act_fwd(s["u"], wcf, l, f"ffn_act{l}")
        s["y3"] = mm(s["a"], w["ffn_w_down"], "nn", F32, f"y3_{l}", tk=4096)
        if l + 1 < nl:
            xs, h = _resnorm(s["x2"], s["y3"], post3, l, pre1, l + 1, f"resnorm3_{l}")
            layer_w.append(dict(zip(big, _split_wait(flying, xs, _gather_piece, f"gather_wait{l + 1}"))))
        saved.append(s)

    last = saved[-1]
    dxn, dy3, dg_post3, loss_lanes = _loss_head(last["x2"], last["y3"], post3, nl - 1, target, "loss_head")
    loss = lax.psum(loss_lanes[0, 0], ("x", "y", "c"))

    recvs = [lax.empty((max(nl - 1, 1), N_DEV, s.shape[1], d), BF16) for s in shards]
    recv0 = [lax.empty((1, N_DEV, s.shape[1], d), BF16) for s in shards]
    small = {n: [None] * nl for n in names if n not in big and n != "mem_norm"}
    small["ffn_post_norm"][nl - 1] = dg_post3
    dmem_n = jnp.zeros(mem0.shape, F32)
    flying = None
    for l in reversed(range(nl)):
        s, w = saved[l], layer_w[l]
        wc_l = wcf if flying is None else wcf + flying[-1][0, 0]
        gw = {}
        da = mm(dy3, w["ffn_w_down"], "nt", F32, f"da{l}", tn=1408, a_outer=False)
        gw["ffn_w_down"] = mm(s["a"], dy3, "tn", BF16, f"dw_down{l}", tm=1408)
        du, small["ffn_conv_w"][l] = _ffn_act_bwd(s["u"], da, wc_l, l, f"ffn_act_bwd{l}")
        dh2 = mm(du, w["ffn_w_up"], "nn", F32, f"dh2_{l}", tm=512, tk=8192)
        gw["ffn_w_up"] = mm(du, s["h2"], "tn", BF16, f"dw_up{l}", tm=2816)
        dx2, small["ffn_pre_norm"][l], dy2, small["xattn_post_norm"][l] = _norm_bwd(
            dxn, dh2, s["x2"], pre3, l, f"norm_bwd3_{l}", s["y2"], post2, l)
        do = mm(dy2, w["xattn_wo"], "nt", BF16, f"do{l}")
        gw["xattn_wo"] = mm(s["o"], dy2, "tn", BF16, f"dw_o{l}")
        dq, dk, dv = _attn_bwd(s["q"], s["k"], s["v"], do, f"attn_bwd{l}")
        dkb, dvb = dk.astype(BF16), dv.astype(BF16)
        dh1 = mm(dq, w["xattn_wq"], "nt", F32, f"dh1_{l}")
        gw["xattn_wq"] = mm(s["h1"], dq, "tn", BF16, f"dw_q{l}")
        gw["xattn_wk"] = mm(mem_n, dkb, "tn", BF16, f"dw_k{l}")
        gw["xattn_wv"] = mm(mem_n, dvb, "tn", BF16, f"dw_v{l}")
        dmem_n = dmem_n + mm(dkb, w["xattn_wk"], "nt", F32, f"dmem_k{l}") \
            + mm(dvb, w["xattn_wv"], "nt", F32, f"dmem_v{l}")
        dx1, small["xattn_pre_norm"][l], dy1, small["mix_post_norm"][l] = _norm_bwd(
            dx2, dh1, s["x1"], pre2, l, f"norm_bwd2_{l}", s["y1"], post1, l)
        dcat = mm(dy1, w["w_out"], "nt", F32, f"dcat{l}")
        gw["w_out"] = mm(s["cat"], dy1, "tn", BF16, f"dw_out{l}")
        dz, dmbd, dps, dwdw, dbdw, dlng, dlnb, dwsc = _mix_bwd(
            dcat, s["z"], s["c"], mbd, pscale3, wdw, lng3, lnb3, wsc, l, f"mix_bwd{l}")
        small["pool_maps"][l] = jnp.stack([dmbd[gi * pg:(gi + 1) * pg, gi * pg:(gi + 1) * pg]
                                           for gi in range(len(POOL_WINDOWS))])
        small["pool_scale"][l], small["conf_dw_w"][l], small["conf_dw_b"][l] = dps, dwdw, dbdw
        small["conf_ln_g"][l], small["conf_ln_b"][l], small["sconv_w"][l] = dlng, dlnb, dwsc
        dh = mm(dz, w["w_in"], "nn", F32, f"dh{l}", tk=4096)
        gw["w_in"] = mm(dz, s["h"], "tn", BF16, f"dw_in{l}", tm=4096)
        if l > 0:
            dxn, small["mix_pre_norm"][l], dy3, small["ffn_post_norm"][l - 1] = _norm_bwd(
                dx1, dh, s["x"], pre1, l, f"norm_bwd1_{l}", saved[l - 1]["y3"], post3, l - 1)
        else:
            dxn, small["mix_pre_norm"][l] = _norm_bwd(dx1, dh, s["x"], pre1, l, "norm_bwd1_0")
        if flying is not None:
            recvs = _split_wait(flying, dxn, _scatter_piece(l), f"scatter_wait{l + 1}")
        flying = _split_start([gw[n] for n in big], recvs if l else recv0, _scatter_piece(max(l - 1, 0)), dxn,
                              f"scatter_start{l}")
    grad_x = _from_steps(dxn)[None]
    _, dg_mem = _norm_bwd(jnp.zeros(mem0.shape, F32), dmem_n, mem0, memg3, 0, "norm_bwd_mem")

    small_names = [n for n in names if n not in big]
    partial = {n: (dg_mem.reshape(d) if n == "mem_norm" else
                   jnp.stack([g.reshape(g.shape[-1]) if g.shape[0] == 1 and weights[n].ndim == 2 else g
                              for g in small[n]])) for n in small_names}
    slab = _pack([partial[n] for n in small_names])
    gathered = _allgather([slab[None]], "gather_small_grads")[0][0].reshape(N_DEV, *slab.shape)
    summed = dict(zip(small_names, _unpack(_sum_sources(gathered, "sum_small_grads"), [partial[n] for n in small_names])))
    grad = {}
    for n in small_names:
        g = summed[n]
        if n in taps:
            width = weights[n].shape[-1]
            g = lax.dynamic_slice_in_dim(g, me * width, width, axis=g.ndim - 1)
        grad[n] = g

    delta, new_m, new_v = {}, {}, {}
    wmv = {n: [row_shard(n, a[n]) for a in (weights, mom1, mom2)] for n in big}
    upper = {n: _adam_sharded(recv, 1, *wmv[n], 1, nl, f"adam_{n}") for n, recv in zip(big, recvs)} if nl > 1 else {}
    upd = _adam_flat(_pack([weights[n] for n in small_names]), _pack([grad[n] for n in small_names]),
                     _pack([mom1[n] for n in small_names]), _pack([mom2[n] for n in small_names]), "adam_small")
    for out, slab_o in zip((delta, new_m, new_v), upd):
        out.update(zip(small_names, _unpack(slab_o, [weights[n] for n in small_names])))
    recv0 = _split_wait(flying, sum(r[3][0, 0, :1] for r in upper.values()) + upd[0][0, :1], _scatter_piece(0),
                        "scatter_wait0")
    for n, recv in zip(big, recv0):
        res = _adam_sharded(recv, 0, *wmv[n], 0, 1, f"adam0_{n}", prev=upper.get(n))
        grad[n], delta[n], new_m[n], new_v[n] = (row_shard(n, r) for r in res)

    return (loss, grad_x, *[grad[n] for n in names], *[delta[n] for n in names],
            *[new_m[n] for n in names], *[new_v[n] for n in names])
```

```python
import jax
import jax.numpy as jnp
from jax import lax
from jax.experimental import pallas as pl
from jax.experimental.pallas import tpu as pltpu

F32, BF16 = jnp.float32, jnp.bfloat16
EPS = 1e-6
POOL_WINDOWS = (2, 4, 8, 16)
MAX_WINDOW = 16
CONF_TAPS, SHORT_TAPS = 31, 3
CONF_HALO, POOL_HALO, SHORT_HALO = 256, 128, 16
ROW_CHUNK = 64
HEADS = 4
N_DEV = 8
ADAM_LR, ADAM_B1, ADAM_B2, ADAM_EPS, ADAM_WD, ADAM_STEP = 0.001, 0.9, 0.999, 1e-08, 0.01, 10
VMEM_LIMIT_V7X = 56 * 2**20
MESH = pl.DeviceIdType.MESH
ANY = pl.BlockSpec(memory_space=pl.ANY)
HBM = pl.BlockSpec(memory_space=pltpu.HBM)
SEM = pl.BlockSpec(memory_space=pltpu.SEMAPHORE)
EFFECT = pltpu.SideEffectType.DATAFLOW_SIDE_EFFECTING

TILE_NORM, TILE_MIX, TILE_FFN, TILE_ATTN, TILE_ADAM = 256, 512, 256, 512, 256


def _params(*sem):
    return pltpu.CompilerParams(dimension_semantics=sem, vmem_limit_bytes=VMEM_LIMIT_V7X)


def _sig(x):
    return 1.0 / (1.0 + jnp.exp(-x))


def _rms(x):
    r = lax.rsqrt(jnp.mean(x * x, axis=-1, keepdims=True) + EPS)
    return x * r, r


def _rms_bwd(dout, g, n, r):
    dn = dout * g
    return r * (dn - n * jnp.mean(dn * n, axis=-1, keepdims=True))


def _rows(tt, c):
    return pl.BlockSpec((tt, c), lambda i: (i, 0))


def _whole(shape):
    return pl.BlockSpec(shape, lambda i: (0,) * len(shape))


def _layer(shape, l):
    return pl.BlockSpec((None,) + shape, lambda i: (l,) + (0,) * len(shape))


def _colsum(x):
    return jnp.sum(x, axis=0, keepdims=True)


_DIMS = {"nn": (((1,), (0,)), ((), ())), "nt": (((1,), (1,)), ((), ())), "tn": (((0,), (0,)), ((), ()))}


def _matmul(a, b, mode, out_dtype, name, *, tm, tn, tk, a_outer=True):
    if mode == "nn":
        (m, k), (k2, n) = a.shape, b.shape
    elif mode == "nt":
        (m, k), (n, k2) = a.shape, b.shape
    else:
        (k, m), (k2, n) = a.shape, b.shape
    assert k == k2, (name, a.shape, b.shape)
    tm, tn, tk = min(tm, m), min(tn, n), min(tk, k)
    assert m % tm == 0 and n % tn == 0 and k % tk == 0, (name, m, n, k, tm, tn, tk)
    gm, gn, gk = m // tm, n // tn, k // tk

    def ij(g0, g1):
        return (g0, g1) if a_outer else (g1, g0)

    def a_map(g0, g1, kk):
        i, _ = ij(g0, g1)
        return (kk, i) if mode == "tn" else (i, kk)

    def b_map(g0, g1, kk):
        _, j = ij(g0, g1)
        return (j, kk) if mode == "nt" else (kk, j)

    def o_map(g0, g1, kk):
        return ij(g0, g1)

    a_block = (tk, tm) if mode == "tn" else (tm, tk)
    b_block = (tn, tk) if mode == "nt" else (tk, tn)
    dims = _DIMS[mode]

    def body(a_ref, b_ref, o_ref, *acc):
        p = lax.dot_general(a_ref[...].astype(BF16), b_ref[...].astype(BF16), dims, preferred_element_type=F32)
        if gk == 1:
            o_ref[...] = p.astype(o_ref.dtype)
        else:
            kk = pl.program_id(2)

            @pl.when(kk == 0)
            def _():
                acc[0][...] = p

            @pl.when(kk > 0)
            def _():
                acc[0][...] += p

            @pl.when(kk == gk - 1)
            def _():
                o_ref[...] = acc[0][...].astype(o_ref.dtype)

    return pl.pallas_call(
        body, name=name, grid=(gm, gn, gk) if a_outer else (gn, gm, gk),
        in_specs=[pl.BlockSpec(a_block, a_map), pl.BlockSpec(b_block, b_map)],
        out_specs=pl.BlockSpec((tm, tn), o_map),
        out_shape=jax.ShapeDtypeStruct((m, n), out_dtype),
        scratch_shapes=[pltpu.VMEM((tm, tn), F32)] if gk > 1 else [],
        compiler_params=_params("parallel", "parallel", "arbitrary"),
    )(a, b)


def _prenorm(x, g3, l, name):
    t, d = x.shape
    tt = min(TILE_NORM, t)

    def body(x_ref, g_ref, h_ref):
        n, _ = _rms(x_ref[...])
        h_ref[...] = (n * g_ref[...]).astype(BF16)

    return pl.pallas_call(
        body, name=name, grid=(t // tt,),
        in_specs=[_rows(tt, d), _layer((1, d), l)], out_specs=_rows(tt, d),
        out_shape=jax.ShapeDtypeStruct((t, d), BF16), compiler_params=_params("parallel"),
    )(x, g3)


def _resnorm(x, y, gpost3, l, gnext3, l2, name):
    t, d = x.shape
    tt = min(TILE_NORM, t)

    def body(x_ref, y_ref, gp_ref, gn_ref, xo_ref, h_ref):
        n, _ = _rms(y_ref[...])
        xn = x_ref[...] + n * gp_ref[...]
        xo_ref[...] = xn
        n2, _ = _rms(xn)
        h_ref[...] = (n2 * gn_ref[...]).astype(BF16)

    return pl.pallas_call(
        body, name=name, grid=(t // tt,),
        in_specs=[_rows(tt, d), _rows(tt, d), _layer((1, d), l), _layer((1, d), l2)],
        out_specs=[_rows(tt, d), _rows(tt, d)],
        out_shape=[jax.ShapeDtypeStruct((t, d), F32), jax.ShapeDtypeStruct((t, d), BF16)],
        compiler_params=_params("parallel"),
    )(x, y, gpost3, gnext3)


def _loss_head(x, y, gpost3, l, target, name):
    t, d = x.shape
    tt = min(TILE_NORM, t)

    def body(x_ref, y_ref, g_ref, t_ref, dxn_ref, dy_ref, dg_ref, loss_ref):
        @pl.when(pl.program_id(0) == 0)
        def _():
            dg_ref[...] = jnp.zeros_like(dg_ref)
            loss_ref[...] = jnp.zeros_like(loss_ref)

        g = g_ref[...]
        n, r = _rms(y_ref[...])
        diff = x_ref[...] + n * g - t_ref[...]
        loss_ref[...] += 0.5 * jnp.sum(jnp.mean(diff * diff, axis=-1, keepdims=True))
        dxn = diff * (1.0 / d)
        dxn_ref[...] = dxn
        dy_ref[...] = _rms_bwd(dxn, g, n, r).astype(BF16)
        dg_ref[...] += _colsum(dxn * n)

    return pl.pallas_call(
        body, name=name, grid=(t // tt,),
        in_specs=[_rows(tt, d), _rows(tt, d), _layer((1, d), l), _rows(tt, d)],
        out_specs=[_rows(tt, d), _rows(tt, d), _whole((1, d)), _whole((1, 128))],
        out_shape=[jax.ShapeDtypeStruct((t, d), F32), jax.ShapeDtypeStruct((t, d), BF16),
                   jax.ShapeDtypeStruct((1, d), F32), jax.ShapeDtypeStruct((1, 128), F32)],
        compiler_params=_params("arbitrary"),
    )(x, y, gpost3, target)


def _norm_bwd(dxn, dh, x_in, gpre3, l, name, y_prev=None, gpost3=None, l_prev=None):
    t, d = x_in.shape
    tt = min(TILE_NORM, t)
    has_prev = y_prev is not None

    def body(*refs):
        if has_prev:
            dxn_ref, dh_ref, x_ref, g_ref, y_ref, g2_ref, dx_ref, dg_ref, dy_ref, dg2_ref = refs
        else:
            dxn_ref, dh_ref, x_ref, g_ref, dx_ref, dg_ref = refs

        @pl.when(pl.program_id(0) == 0)
        def _():
            dg_ref[...] = jnp.zeros_like(dg_ref)
            if has_prev:
                dg2_ref[...] = jnp.zeros_like(dg2_ref)

        dh_v = dh_ref[...]
        n, r = _rms(x_ref[...])
        dx = dxn_ref[...] + _rms_bwd(dh_v, g_ref[...], n, r)
        dx_ref[...] = dx
        dg_ref[...] += _colsum(dh_v * n)
        if has_prev:
            n2, r2 = _rms(y_ref[...])
            dy_ref[...] = _rms_bwd(dx, g2_ref[...], n2, r2).astype(BF16)
            dg2_ref[...] += _colsum(dx * n2)

    in_specs = [_rows(tt, d), _rows(tt, d), _rows(tt, d), _layer((1, d), l)]
    out_specs = [_rows(tt, d), _whole((1, d))]
    out_shape = [jax.ShapeDtypeStruct((t, d), F32), jax.ShapeDtypeStruct((1, d), F32)]
    args = [dxn, dh, x_in, gpre3]
    if has_prev:
        in_specs += [_rows(tt, d), _layer((1, d), l_prev)]
        out_specs += [_rows(tt, d), _whole((1, d))]
        out_shape += [jax.ShapeDtypeStruct((t, d), BF16), jax.ShapeDtypeStruct((1, d), F32)]
        args += [y_prev, gpost3]
    return pl.pallas_call(
        body, name=name, grid=(t // tt,), in_specs=in_specs, out_specs=out_specs, out_shape=out_shape,
        compiler_params=_params("arbitrary"),
    )(*args)


def _to_steps(a):
    t = a.shape[0]
    return a.reshape(8, t // 8, -1).transpose(1, 0, 2).reshape(a.shape)


def _from_steps(a):
    t = a.shape[0]
    return a.reshape(t // 8, 8, -1).transpose(1, 0, 2).reshape(a.shape)


def _al(v):
    return v if isinstance(v, int) else pl.multiple_of(v, 8)


def _chunks(n_rows, fn, unroll=1):
    def step(r, carry):
        fn(pl.multiple_of(r * ROW_CHUNK, ROW_CHUNK))
        return carry
    lax.fori_loop(0, n_rows // ROW_CHUNK, step, 0, unroll=unroll)


def _fold8(a):
    return a.reshape(a.shape[0] // 8, 8, a.shape[1]).sum(axis=0)


def _shift_down(a):
    row = lax.broadcasted_iota(jnp.int32, a.shape, 0)
    return jnp.where(row % 8 == 0, 0.0, pltpu.roll(a, 1, 0))


def _shift_up(a):
    row = lax.broadcasted_iota(jnp.int32, a.shape, 0)
    return jnp.where(row % 8 == 7, 0.0, pltpu.roll(a, a.shape[0] - 1, 0))


def _prev_block(h, c, tt, t):
    return pl.BlockSpec((h, c), lambda i: (jnp.where(i == 0, t // h - 1, i * (tt // h) - 1), 0))


def _next_block(h, c, tt, t):
    return pl.BlockSpec((h, c), lambda i: (jnp.where(i == t // tt - 1, 0, (i + 1) * (tt // h)), 0))


def _taps(w_ref, buf, start, taps, rc, lanes=slice(None)):
    acc = w_ref[0:1, lanes] * buf[pl.ds(_al(start), rc), lanes]
    for k in range(1, taps):
        acc = acc + w_ref[k:k + 1, lanes] * buf[pl.ds(_al(start + 8 * k), rc), lanes]
    return acc


def _taps_rev(w_ref, buf, start, taps, rc, lanes=slice(None)):
    acc = w_ref[0:1, lanes] * buf[pl.ds(_al(start + 8 * (taps - 1)), rc), lanes]
    for k in range(1, taps):
        acc = acc + w_ref[k:k + 1, lanes] * buf[pl.ds(_al(start + 8 * (taps - 1 - k)), rc), lanes]
    return acc


def _mix_dims(d):
    dp = d // 4
    dc = 3 * d // 8
    ds = d - dp - dc
    oa, og = dp, dp + dc
    ob = dp + 2 * dc
    oc, ox = ob + ds, ob + 2 * ds
    return dp, dc, ds, oa, og, ob, oc, ox, ox + ds


def _pool_consts(dp):
    win = jnp.repeat(jnp.asarray(POOL_WINDOWS, F32), dp // len(POOL_WINDOWS))[None, :]
    mask = (jnp.arange(MAX_WINDOW, dtype=F32)[:, None] < win).astype(F32)
    return mask, win


def _pool_count(row0, rc, dp, seg, wl):
    r = lax.broadcasted_iota(jnp.int32, (rc, dp), 0) + row0
    return jnp.minimum(((r & 7) * seg + (r >> 3) + 1).astype(F32), wl)


def _mix_fwd(z, mbd, pscale3, wdw, bdw3, lng3, lnb3, wsc, l, name):
    t, din = z.shape
    dp, dc, ds, oa, og, ob, oc, ox, din2 = _mix_dims(din * 8 // 17)
    assert din2 == din
    d = ob
    tt = min(TILE_MIX, t)
    hp, hc, hs, rc = POOL_HALO, CONF_HALO, SHORT_HALO, ROW_CHUNK
    assert tt % hc == 0 and t % tt == 0
    seg = t // 8
    pmask, wlane = _pool_consts(dp)

    def body(z_ref, zpa_ref, zpb_ref, mbd_ref, ps_ref, pmask_ref, wl_ref, wdw_ref, bdw_ref, lng_ref, lnb_ref, wsc_ref,
             cat_ref, c_ref, pbuf, vbuf, sbuf):
        i = pl.program_id(0)
        pbuf[0:hp, :] = zpa_ref[hc - hp:hc, 0:dp]

        def prev(r0):
            rows = pl.ds(r0, rc)
            vbuf[rows, :] = zpa_ref[rows, oa:oa + dc] * _sig(zpa_ref[rows, og:og + dc])
        _chunks(hc, prev)
        sbuf[0:hs, :] = zpb_ref[:, oc:oc + ds] * zpb_ref[:, ox:ox + ds]

        @pl.when(i == 0)
        def _():
            pbuf[0:hp, :] = _shift_down(pbuf[0:hp, :])
            vbuf[0:hc, :] = _shift_down(vbuf[0:hc, :])
            sbuf[0:hs, :] = _shift_down(sbuf[0:hs, :])

        mbd_v, ps, wl = mbd_ref[...], ps_ref[...], wl_ref[...]
        bdw, lng, lnb = bdw_ref[...], lng_ref[...], lnb_ref[...]

        def step(r0):
            rows = pl.ds(r0, rc)
            zp = z_ref[rows, 0:dp]
            pbuf[pl.ds(_al(hp + r0), rc), :] = zp
            vbuf[pl.ds(_al(hc + r0), rc), :] = z_ref[rows, oa:oa + dc] * _sig(z_ref[rows, og:og + dc])
            sbuf[pl.ds(_al(hs + r0), rc), :] = z_ref[rows, oc:oc + ds] * z_ref[rows, ox:ox + ds]
            pooled = _taps_rev(pmask_ref, pbuf, r0 + hp - 8 * (MAX_WINDOW - 1), MAX_WINDOW, rc)
            pooled = pooled / _pool_count(i * tt + r0, rc, dp, seg, wl) - zp
            pm = jnp.dot(pooled.astype(BF16), mbd_v, preferred_element_type=F32)
            cat_ref[rows, 0:dp] = (pm * ps).astype(BF16)
            c = _taps(wdw_ref, vbuf, r0 + hc - 8 * (CONF_TAPS - 1), CONF_TAPS, rc) + bdw
            c_ref[rows, :] = c
            xc = c - jnp.mean(c, axis=-1, keepdims=True)
            nrm = xc * lax.rsqrt(jnp.mean(xc * xc, axis=-1, keepdims=True) + EPS)
            yln = nrm * lng + lnb
            cat_ref[rows, dp:dp + dc] = (yln * _sig(yln)).astype(BF16)
            cv = _taps(wsc_ref, sbuf, r0 + hs - 8 * (SHORT_TAPS - 1), SHORT_TAPS, rc)
            cat_ref[rows, dp + dc:d] = (z_ref[rows, ob:ob + ds] * cv).astype(BF16)
        _chunks(tt, step, unroll=2)

    return pl.pallas_call(
        body, name=name, grid=(t // tt,),
        in_specs=[_rows(tt, din), _prev_block(hc, d, tt, t), _prev_block(hs, din, tt, t),
                  _layer((dp, dp), l), _layer((1, dp), l), _whole((MAX_WINDOW, dp)), _whole((1, dp)),
                  _layer((CONF_TAPS, dc), l), _layer((1, dc), l), _layer((1, dc), l), _layer((1, dc), l),
                  _layer((SHORT_TAPS, ds), l)],
        out_specs=[_rows(tt, d), _rows(tt, dc)],
        out_shape=[jax.ShapeDtypeStruct((t, d), BF16), jax.ShapeDtypeStruct((t, dc), F32)],
        scratch_shapes=[pltpu.VMEM((hp + tt, dp), F32), pltpu.VMEM((hc + tt, dc), F32), pltpu.VMEM((hs + tt, ds), F32)],
        compiler_params=_params("parallel"),
    )(z, z, z, mbd, pscale3, pmask, wlane, wdw, bdw3, lng3, lnb3, wsc)


def _mix_bwd(dcat, z, c, mbd, pscale3, wdw, lng3, lnb3, wsc, l, name):
    t, din = z.shape
    dp, dc, ds, oa, og, ob, oc, ox, _ = _mix_dims(din * 8 // 17)
    d = ob
    tt = min(TILE_MIX, t)
    nt = t // tt
    hp, hc, hs, rc = POOL_HALO, CONF_HALO, SHORT_HALO, ROW_CHUNK
    assert tt % hc == 0 and t % tt == 0 and tt >= 8 * MAX_WINDOW
    seg = t // 8
    pmask, wlane = _pool_consts(dp)

    def body(dcat_ref, dcn_ref, z_ref, zpa_ref, zpb_ref, znb_ref, c_ref, cn_ref, mbd_ref, ps_ref, pmask_ref, wl_ref,
             wdw_ref, lng_ref, lnb_ref, wsc_ref,
             dz_ref, dmbd_ref, dps_ref, dwdw_ref, dbdw_ref, dlng_ref, dlnb_ref, dwsc_ref,
             pbuf, qbuf, dpbuf, pbf, vbuf, dcbuf, sbuf, dsbuf, dw8, ds8, ln8, ps8):
        i = pl.program_id(0)
        first, last = i == 0, i == nt - 1

        @pl.when(first)
        def _():
            for ref in (dmbd_ref, dw8, ds8, ln8, ps8):
                ref[...] = jnp.zeros_like(ref)

        mbd_v, ps, wl = mbd_ref[...], ps_ref[...], wl_ref[...]
        lng, lnb = lng_ref[...], lnb_ref[...]

        def ln_silu_bwd(cc, dyb):
            xc = cc - jnp.mean(cc, axis=-1, keepdims=True)
            rstd = lax.rsqrt(jnp.mean(xc * xc, axis=-1, keepdims=True) + EPS)
            nrm = xc * rstd
            yln = nrm * lng + lnb
            s = _sig(yln)
            dyln = dyb * (s * (1.0 + yln * (1.0 - s)))
            dn = dyln * lng
            dcc = rstd * (dn - jnp.mean(dn, axis=-1, keepdims=True) - nrm * jnp.mean(dn * nrm, axis=-1, keepdims=True))
            return dcc, dyln, nrm

        pbuf[0:hp, :] = zpa_ref[hc - hp:hc, 0:dp]

        def prev(r0):
            rows = pl.ds(r0, rc)
            vbuf[rows, :] = zpa_ref[rows, oa:oa + dc] * _sig(zpa_ref[rows, og:og + dc])
        _chunks(hc, prev)
        sbuf[0:hs, :] = zpb_ref[:, oc:oc + ds] * zpb_ref[:, ox:ox + ds]

        @pl.when(first)
        def _():
            pbuf[0:hp, :] = _shift_down(pbuf[0:hp, :])
            vbuf[0:hc, :] = _shift_down(vbuf[0:hc, :])
            sbuf[0:hs, :] = _shift_down(sbuf[0:hs, :])

        def nxt(r0):
            rows = pl.ds(r0, rc)
            dcc, _, _ = ln_silu_bwd(cn_ref[rows, :], dcn_ref[rows, dp:dp + dc])
            dcbuf[pl.ds(_al(tt + r0), rc), :] = dcc
        _chunks(hc, nxt, unroll=2)
        dpm_n = (dcn_ref[0:hp, 0:dp] * ps).astype(BF16)
        qbuf[tt:tt + hp, :] = lax.dot_general(dpm_n, mbd_v, _DIMS["nt"], preferred_element_type=F32) / wl
        dsbuf[tt:tt + hs, :] = dcn_ref[0:hs, dp + dc:d] * znb_ref[:, ob:ob + ds]

        @pl.when(last)
        def _():
            dcbuf[tt:tt + hc, :] = _shift_up(dcbuf[tt:tt + hc, :])
            qbuf[tt:tt + hp, :] = _shift_up(qbuf[tt:tt + hp, :])
            dsbuf[tt:tt + hs, :] = _shift_up(dsbuf[tt:tt + hs, :])

        def fill(r0):
            rows = pl.ds(r0, rc)
            zp = z_ref[rows, 0:dp]
            pbuf[pl.ds(_al(hp + r0), rc), :] = zp
            vbuf[pl.ds(_al(hc + r0), rc), :] = z_ref[rows, oa:oa + dc] * _sig(z_ref[rows, og:og + dc])
            sbuf[pl.ds(_al(hs + r0), rc), :] = z_ref[rows, oc:oc + ds] * z_ref[rows, ox:ox + ds]
            pooled = _taps_rev(pmask_ref, pbuf, r0 + hp - 8 * (MAX_WINDOW - 1), MAX_WINDOW, rc)
            pbf[rows, :] = (pooled / _pool_count(i * tt + r0, rc, dp, seg, wl) - zp).astype(BF16)
            dcc, dyln, nrm = ln_silu_bwd(c_ref[rows, :], dcat_ref[rows, dp:dp + dc])
            dcbuf[rows, :] = dcc
            ln8[0] += _fold8(dyln * nrm)
            ln8[1] += _fold8(dyln)
            ln8[2] += _fold8(dcc)
            dsbuf[rows, :] = dcat_ref[rows, dp + dc:d] * z_ref[rows, ob:ob + ds]
        _chunks(tt, fill, unroll=2)

        pb = pbf[...]
        dya = dcat_ref[:, 0:dp]
        ps8[...] += _fold8(dya * jnp.dot(pb, mbd_v, preferred_element_type=F32))
        dpm = (dya * ps).astype(BF16)
        dmbd_ref[...] += lax.dot_general(pb, dpm, _DIMS["tn"], preferred_element_type=F32)
        dpbuf[...] = lax.dot_general(dpm, mbd_v, _DIMS["nt"], preferred_element_type=F32)

        def quot(r0):
            rows = pl.ds(r0, rc)
            qbuf[rows, :] = dpbuf[rows, :] / _pool_count(i * tt + r0, rc, dp, seg, wl)
        _chunks(tt, quot)

        def back(r0):
            rows = pl.ds(r0, rc)
            dzp = _taps(pmask_ref, qbuf, r0, MAX_WINDOW, rc) - dpbuf[rows, :]
            dz_ref[rows, 0:dp] = dzp.astype(BF16)
            dcc = dcbuf[rows, :]
            for k in range(CONF_TAPS):
                dw8[k] += _fold8(dcc * vbuf[pl.ds(_al(r0 + hc - 8 * (CONF_TAPS - 1 - k)), rc), :])
            dv = _taps_rev(wdw_ref, dcbuf, r0, CONF_TAPS, rc)
            za = z_ref[rows, oa:oa + dc]
            sg = _sig(z_ref[rows, og:og + dc])
            dz_ref[rows, oa:oa + dc] = (dv * sg).astype(BF16)
            dz_ref[rows, og:og + dc] = (dv * za * sg * (1.0 - sg)).astype(BF16)
            cv = _taps(wsc_ref, sbuf, r0 + hs - 8 * (SHORT_TAPS - 1), SHORT_TAPS, rc)
            dz_ref[rows, ob:ob + ds] = (dcat_ref[rows, dp + dc:d] * cv).astype(BF16)
            dcv = dsbuf[rows, :]
            for k in range(SHORT_TAPS):
                ds8[k] += _fold8(dcv * sbuf[pl.ds(_al(r0 + hs - 8 * (SHORT_TAPS - 1 - k)), rc), :])
            dpv = _taps_rev(wsc_ref, dsbuf, r0, SHORT_TAPS, rc)
            dz_ref[rows, oc:oc + ds] = (dpv * z_ref[rows, ox:ox + ds]).astype(BF16)
            dz_ref[rows, ox:ox + ds] = (dpv * z_ref[rows, oc:oc + ds]).astype(BF16)
        _chunks(tt, back)

        @pl.when(last)
        def _():
            dps_ref[...] = jnp.sum(ps8[...], axis=0, keepdims=True)
            dwdw_ref[...] = jnp.sum(dw8[...], axis=1)
            dwsc_ref[...] = jnp.sum(ds8[...], axis=1)
            dlng_ref[...] = jnp.sum(ln8[0], axis=0, keepdims=True)
            dlnb_ref[...] = jnp.sum(ln8[1], axis=0, keepdims=True)
            dbdw_ref[...] = jnp.sum(ln8[2], axis=0, keepdims=True)

    return pl.pallas_call(
        body, name=name, grid=(nt,),
        in_specs=[_rows(tt, d), _next_block(hc, d, tt, t),
                  _rows(tt, din), _prev_block(hc, d, tt, t), _prev_block(hs, din, tt, t), _next_block(hs, din, tt, t),
                  _rows(tt, dc), _next_block(hc, dc, tt, t),
                  _layer((dp, dp), l), _layer((1, dp), l), _whole((MAX_WINDOW, dp)), _whole((1, dp)),
                  _layer((CONF_TAPS, dc), l), _layer((1, dc), l), _layer((1, dc), l), _layer((SHORT_TAPS, ds), l)],
        out_specs=[_rows(tt, din), _whole((dp, dp)), _whole((1, dp)), _whole((CONF_TAPS, dc)), _whole((1, dc)),
                   _whole((1, dc)), _whole((1, dc)), _whole((SHORT_TAPS, ds))],
        out_shape=[jax.ShapeDtypeStruct((t, din), BF16), jax.ShapeDtypeStruct((dp, dp), F32),
                   jax.ShapeDtypeStruct((1, dp), F32), jax.ShapeDtypeStruct((CONF_TAPS, dc), F32),
                   jax.ShapeDtypeStruct((1, dc), F32), jax.ShapeDtypeStruct((1, dc), F32),
                   jax.ShapeDtypeStruct((1, dc), F32), jax.ShapeDtypeStruct((SHORT_TAPS, ds), F32)],
        scratch_shapes=[pltpu.VMEM((hp + tt, dp), F32), pltpu.VMEM((tt + hp, dp), F32), pltpu.VMEM((tt, dp), F32),
                        pltpu.VMEM((tt, dp), BF16), pltpu.VMEM((hc + tt, dc), F32), pltpu.VMEM((tt + hc, dc), F32),
                        pltpu.VMEM((hs + tt, ds), F32), pltpu.VMEM((tt + hs, ds), F32),
                        pltpu.VMEM((CONF_TAPS, 8, dc), F32), pltpu.VMEM((SHORT_TAPS, 8, ds), F32),
                        pltpu.VMEM((3, 8, dc), F32), pltpu.VMEM((8, dp), F32)],
        compiler_params=_params("arbitrary"),
    )(dcat, dcat, z, z, z, z, c, c, mbd, pscale3, pmask, wlane, wdw, lng3, lnb3, wsc)


def _softmax_rows(qh, kh, scale):
    s = lax.dot_general(qh, kh, _DIMS["nt"], preferred_element_type=F32) * scale
    e = jnp.exp(s - jnp.max(s, axis=-1, keepdims=True))
    return e / jnp.sum(e, axis=-1, keepdims=True)


def _attn_fwd(q, k, v, name):
    t, d = q.shape
    m = k.shape[0]
    hd = d // HEADS
    scale = hd ** -0.5
    tt = min(TILE_ATTN, t)

    def body(q_ref, k_ref, v_ref, o_ref):
        for h in range(HEADS):
            sl = slice(h * hd, (h + 1) * hd)
            p = _softmax_rows(q_ref[:, sl], k_ref[:, sl], scale)
            o_ref[:, sl] = jnp.dot(p.astype(BF16), v_ref[:, sl], preferred_element_type=F32).astype(BF16)

    return pl.pallas_call(
        body, name=name, grid=(t // tt,),
        in_specs=[_rows(tt, d), _whole((m, d)), _whole((m, d))], out_specs=_rows(tt, d),
        out_shape=jax.ShapeDtypeStruct((t, d), BF16), compiler_params=_params("parallel"),
    )(q, k, v)


def _attn_bwd(q, k, v, do, name):
    t, d = q.shape
    m = k.shape[0]
    hd = d // HEADS
    scale = hd ** -0.5
    tt = min(TILE_ATTN, t)

    def body(q_ref, k_ref, v_ref, do_ref, dq_ref, dk_ref, dv_ref):
        @pl.when(pl.program_id(0) == 0)
        def _():
            dk_ref[...] = jnp.zeros_like(dk_ref)
            dv_ref[...] = jnp.zeros_like(dv_ref)

        for h in range(HEADS):
            sl = slice(h * hd, (h + 1) * hd)
            qh, kh, vh, doh = q_ref[:, sl], k_ref[:, sl], v_ref[:, sl], do_ref[:, sl]
            p = _softmax_rows(qh, kh, scale)
            dv_ref[:, sl] += lax.dot_general(p.astype(BF16), doh, _DIMS["tn"], preferred_element_type=F32)
            dp = lax.dot_general(doh, vh, _DIMS["nt"], preferred_element_type=F32)
            ds = (p * (dp - jnp.sum(dp * p, axis=-1, keepdims=True)) * scale).astype(BF16)
            dq_ref[:, sl] = jnp.dot(ds, kh, preferred_element_type=F32).astype(BF16)
            dk_ref[:, sl] += lax.dot_general(ds, qh, _DIMS["tn"], preferred_element_type=F32)

    return pl.pallas_call(
        body, name=name, grid=(t // tt,),
        in_specs=[_rows(tt, d), _whole((m, d)), _whole((m, d)), _rows(tt, d)],
        out_specs=[_rows(tt, d), _whole((m, d)), _whole((m, d))],
        out_shape=[jax.ShapeDtypeStruct((t, d), BF16), jax.ShapeDtypeStruct((m, d), F32),
                   jax.ShapeDtypeStruct((m, d), F32)],
        compiler_params=_params("arbitrary"),
    )(q, k, v, do)


def _lane_chunks(f):
    w = 256 if f % 256 == 0 else 128 if f % 128 == 0 else f
    return [(c0, w) for c0 in range(0, f, w)]


def _ffn_act_fwd(u, wc, l, name):
    t, f2 = u.shape
    f = f2 // 2
    tt = min(TILE_FFN, t)
    hs, rc = SHORT_HALO, ROW_CHUNK
    lanes = _lane_chunks(f)

    def body(u_ref, up_ref, wc_ref, a_ref, ubuf):
        ubuf[0:hs, :] = up_ref[...]

        @pl.when(pl.program_id(0) == 0)
        def _():
            ubuf[0:hs, :] = _shift_down(ubuf[0:hs, :])

        def step(r0):
            rows = pl.ds(r0, rc)
            ubuf[pl.ds(_al(hs + r0), rc), :] = u_ref[rows, :]
            start = r0 + hs - 8 * (SHORT_TAPS - 1)
            for c0, cw in lanes:
                g = _taps(wc_ref, ubuf, start, SHORT_TAPS, rc, slice(c0, c0 + cw))
                vv = _taps(wc_ref, ubuf, start, SHORT_TAPS, rc, slice(f + c0, f + c0 + cw))
                a_ref[rows, c0:c0 + cw] = (g * _sig(g) * vv).astype(BF16)
        _chunks(tt, step)

    return pl.pallas_call(
        body, name=name, grid=(t // tt,),
        in_specs=[_rows(tt, f2), _prev_block(hs, f2, tt, t), _layer((SHORT_TAPS, f2), l)],
        out_specs=_rows(tt, f), out_shape=jax.ShapeDtypeStruct((t, f), BF16),
        scratch_shapes=[pltpu.VMEM((hs + tt, f2), F32)], compiler_params=_params("parallel"),
    )(u, u, wc)


def _ffn_act_bwd(u, da, wc, l, name):
    t, f2 = u.shape
    f = f2 // 2
    tt = min(TILE_FFN, t)
    nt = t // tt
    hs, rc = SHORT_HALO, ROW_CHUNK
    lanes = _lane_chunks(f)

    def body(u_ref, up_ref, un_ref, da_ref, dan_ref, wc_ref, du_ref, dwc_ref, ubuf, danbuf, dbuf, dw8):
        i = pl.program_id(0)
        first, last = i == 0, i == nt - 1
        ubuf[0:hs, :] = up_ref[...]
        ubuf[hs + tt:hs + tt + hs, :] = un_ref[...]
        danbuf[...] = dan_ref[...]

        @pl.when(first)
        def _():
            dw8[...] = jnp.zeros_like(dw8)
            ubuf[0:hs, :] = _shift_down(ubuf[0:hs, :])

        @pl.when(last)
        def _():
            ubuf[hs + tt:hs + tt + hs, :] = _shift_up(ubuf[hs + tt:hs + tt + hs, :])
            danbuf[...] = _shift_up(danbuf[...])

        def fill(r0):
            ubuf[pl.ds(_al(hs + r0), rc), :] = u_ref[pl.ds(r0, rc), :]
        _chunks(tt, fill)

        def conv_grads(r0, n, da_rows):
            start = r0 + hs - 8 * (SHORT_TAPS - 1)
            for c0, cw in lanes:
                sl_g, sl_v = slice(c0, c0 + cw), slice(f + c0, f + c0 + cw)
                g = _taps(wc_ref, ubuf, start, SHORT_TAPS, n, sl_g)
                vv = _taps(wc_ref, ubuf, start, SHORT_TAPS, n, sl_v)
                dav = da_rows(c0, cw)
                sg = _sig(g)
                dbuf[pl.ds(_al(r0), n), sl_g] = dav * vv * (sg * (1.0 + g * (1.0 - sg)))
                dbuf[pl.ds(_al(r0), n), sl_v] = dav * (g * sg)

        _chunks(tt, lambda r0: conv_grads(r0, rc, lambda c0, cw: da_ref[pl.ds(r0, rc), c0:c0 + cw]))
        conv_grads(tt, hs, lambda c0, cw: danbuf[:, c0:c0 + cw])

        def back(r0):
            rows = pl.ds(r0, rc)
            for c0, cw in lanes:
                for off in (c0, f + c0):
                    sl = slice(off, off + cw)
                    du_ref[rows, sl] = _taps_rev(wc_ref, dbuf, r0, SHORT_TAPS, rc, sl).astype(BF16)
                    dd = dbuf[rows, sl]
                    for k in range(SHORT_TAPS):
                        dw8[k, :, sl] += _fold8(dd * ubuf[pl.ds(_al(r0 + hs - 8 * (SHORT_TAPS - 1 - k)), rc), sl])
        _chunks(tt, back)

        @pl.when(last)
        def _():
            dwc_ref[...] = jnp.sum(dw8[...], axis=1)

    return pl.pallas_call(
        body, name=name, grid=(nt,),
        in_specs=[_rows(tt, f2), _prev_block(hs, f2, tt, t), _next_block(hs, f2, tt, t),
                  _rows(tt, f), _next_block(hs, f, tt, t), _layer((SHORT_TAPS, f2), l)],
        out_specs=[_rows(tt, f2), _whole((SHORT_TAPS, f2))],
        out_shape=[jax.ShapeDtypeStruct((t, f2), BF16), jax.ShapeDtypeStruct((SHORT_TAPS, f2), F32)],
        scratch_shapes=[pltpu.VMEM((hs + tt + hs, f2), F32), pltpu.VMEM((hs, f), F32), pltpu.VMEM((tt + hs, f2), F32),
                        pltpu.VMEM((SHORT_TAPS, 8, f2), F32)],
        compiler_params=_params("arbitrary"),
    )(u, u, u, da, da, wc)


def _place():
    return lax.axis_index("x"), lax.axis_index("y"), lax.axis_index("c")


def _flip(v, bit):
    return 1 - v if bit else v


def _peers(x, y, c):
    out = []
    for kk in range(1, N_DEV):
        px, py, pc = _flip(x, kk & 4), _flip(y, kk & 2), _flip(c, kk & 1)
        out.append((kk - 1, (px, py, pc), 4 * px + 2 * py + pc))
    return out


def _allgather(shards, name):
    nt = len(shards)

    def body(*refs):
        srcs, outs = refs[:nt], refs[nt:2 * nt]
        send_sems, recv_sems, local_sems = refs[2 * nt:]
        x, y, c = _place()
        me, sibling = (x, y, c), (x, y, 1 - c)
        chips = [(1 - x, y), (x, 1 - y), (1 - x, 1 - y)]

        def rows(ti, px, py, pc):
            r = srcs[ti].shape[1]
            return outs[ti].at[:, pl.ds((4 * px + 2 * py + pc) * r, r), :]

        def copy(ti, kk, block, to, src=None):
            return pltpu.make_async_remote_copy(
                src_ref=rows(ti, *block) if src is None else src, dst_ref=rows(ti, *block),
                send_sem=send_sems.at[ti, kk], recv_sem=recv_sems.at[ti, kk], device_id=to, device_id_type=MESH)

        mine = [pltpu.make_async_copy(srcs[ti], rows(ti, *me), local_sems.at[ti]) for ti in range(nt)]
        for cp in mine:
            cp.start()
        first = []
        for ti in range(nt):
            first.append(copy(ti, 0, me, sibling, src=srcs[ti]))
            first += [copy(ti, 1 + j, me, (*chip, c), src=srcs[ti]) for j, chip in enumerate(chips)]
        for cp in first:
            cp.start()
        passed = []
        for j, chip in enumerate(chips):
            for ti in range(nt):
                copy(ti, 1 + j, (*chip, c), me).wait_recv()
                fwd = copy(ti, 4 + j, (*chip, c), sibling)
                fwd.start()
                passed.append(fwd)
        for ti in range(nt):
            copy(ti, 0, sibling, me).wait_recv()
            for j, chip in enumerate(chips):
                copy(ti, 4 + j, (*chip, 1 - c), me).wait_recv()
        for cp in first + passed:
            cp.wait_send()
        for cp in mine:
            cp.wait()

    return pl.pallas_call(
        body, name=name,
        in_specs=[ANY] * nt, out_specs=[ANY] * nt,
        out_shape=[jax.ShapeDtypeStruct((s.shape[0], N_DEV * s.shape[1], s.shape[2]), s.dtype) for s in shards],
        scratch_shapes=[pltpu.SemaphoreType.DMA((nt, 7)), pltpu.SemaphoreType.DMA((nt, 7)),
                        pltpu.SemaphoreType.DMA((nt,))],
    )(*shards)


def _gather_piece(src, land, me, to):
    r = src.shape[0]
    return src, land.at[pl.ds(me * r, r), :]


def _scatter_piece(l):
    def piece(src, land, me, to):
        r = src.shape[0] // N_DEV
        return src.at[pl.ds(to * r, r), :], land.at[l, me]
    return piece


def _split_start(srcs, lands, piece, after, name):
    nt = len(srcs)

    def body(*refs):
        src_refs, land_refs = refs[:nt], refs[nt:2 * nt]
        send_sems, recv_sems, local_sems, token = refs[2 * nt + 1], refs[2 * nt + 2], refs[2 * nt + 3], refs[4 * nt + 4]
        x, y, c = _place()
        me = 4 * x + 2 * y + c
        for ti in range(nt):
            for slot, peer, flat in _peers(x, y, c):
                src, dst = piece(src_refs[ti], land_refs[ti], me, flat)
                pltpu.make_async_remote_copy(
                    src_ref=src, dst_ref=dst, send_sem=send_sems.at[7 * ti + slot], recv_sem=recv_sems.at[7 * ti + slot],
                    device_id=peer, device_id_type=MESH).start()
        for ti in range(nt):
            pltpu.make_async_copy(*piece(src_refs[ti], land_refs[ti], me, me), local_sems.at[ti]).start()
        token[...] = jnp.zeros_like(token)

    both = list(srcs) + list(lands)
    return pl.pallas_call(
        body, name=name,
        in_specs=[HBM] * (2 * nt) + [ANY],
        out_specs=[SEM, SEM, SEM] + [HBM] * (2 * nt) + [pl.BlockSpec(memory_space=pltpu.VMEM)],
        out_shape=[pltpu.SemaphoreType.DMA((7 * nt,)), pltpu.SemaphoreType.DMA((7 * nt,)), pltpu.SemaphoreType.DMA((nt,))]
        + [pltpu.HBM(a.shape, a.dtype) for a in both] + [jax.ShapeDtypeStruct((8, 128), F32)],
        input_output_aliases={i: i + 3 for i in range(2 * nt)},
        compiler_params=pltpu.CompilerParams(has_side_effects=EFFECT),
    )(*[pltpu.with_memory_space_constraint(a, pltpu.HBM) for a in both], after)


def _split_wait(started, after, piece, name):
    send_sems, recv_sems, local_sems, *both = started[:-1]
    nt = len(both) // 2

    def body(*refs):
        src_refs, land_refs = refs[:nt], refs[nt:2 * nt]
        send_ref, recv_ref, local_ref = refs[2 * nt], refs[2 * nt + 1], refs[2 * nt + 2]
        x, y, c = _place()
        me = 4 * x + 2 * y + c
        for ti in range(nt):
            src, dst = piece(src_refs[ti], land_refs[ti], me, me)
            for slot in range(N_DEV - 1):
                cp = pltpu.make_async_remote_copy(
                    src_ref=src, dst_ref=dst, send_sem=send_ref.at[7 * ti + slot], recv_sem=recv_ref.at[7 * ti + slot],
                    device_id=(x, y, c), device_id_type=MESH)
                cp.wait_send()
                cp.wait_recv()
            pltpu.make_async_copy(src, dst, local_ref.at[ti]).wait()

    outs = pl.pallas_call(
        body, name=name,
        in_specs=[HBM] * (2 * nt) + [SEM, SEM, SEM, ANY], out_specs=[HBM] * (2 * nt),
        out_shape=[pltpu.HBM(a.shape, a.dtype) for a in both],
        input_output_aliases={i: i for i in range(2 * nt)},
        compiler_params=pltpu.CompilerParams(has_side_effects=EFFECT),
    )(*both, send_sems, recv_sems, local_sems, after)
    return outs[nt:]


def _adam(w, g, m, v):
    m2 = ADAM_B1 * m + (1.0 - ADAM_B1) * g
    v2 = ADAM_B2 * v + (1.0 - ADAM_B2) * (g * g)
    m_hat = m2 / (1.0 - ADAM_B1 ** ADAM_STEP)
    v_hat = v2 / (1.0 - ADAM_B2 ** ADAM_STEP)
    return -ADAM_LR * (m_hat / (jnp.sqrt(v_hat) + ADAM_EPS) + ADAM_WD * w), m2, v2


def _adam_sharded(recv, recv_first, w, m, v, lo, hi, name, prev=None):
    nl, r, c = w.shape
    tr = min(TILE_ADAM, r)
    while r % tr:
        tr //= 2

    def body(recv_ref, w_ref, m_ref, v_ref, *rest):
        g_ref, d_ref, m2_ref, v2_ref = rest[-4:]
        g = recv_ref[0].astype(F32)
        for s in range(1, N_DEV):
            g = g + recv_ref[s].astype(F32)
        g_ref[...] = g
        d_ref[...], m2_ref[...], v2_ref[...] = _adam(w_ref[...], g, m_ref[...], v_ref[...])

    blk = pl.BlockSpec((None, tr, c), lambda li, i: (li + lo, i, 0))
    extra = [] if prev is None else list(prev)
    return pl.pallas_call(
        body, name=name, grid=(hi - lo, r // tr),
        in_specs=[pl.BlockSpec((None, N_DEV, tr, c), lambda li, i: (li + lo - recv_first, 0, i, 0)), blk, blk, blk]
        + [ANY] * len(extra),
        out_specs=[blk] * 4, out_shape=[jax.ShapeDtypeStruct((nl, r, c), F32)] * 4,
        input_output_aliases={4 + i: i for i in range(len(extra))},
        compiler_params=_params("parallel", "parallel"),
    )(recv, w, m, v, *extra)


def _sum_sources(parts, name):
    _, r, c = parts.shape

    def body(p_ref, o_ref):
        g = p_ref[0]
        for s in range(1, N_DEV):
            g = g + p_ref[s]
        o_ref[...] = g

    return pl.pallas_call(
        body, name=name, grid=(1,), in_specs=[_whole((N_DEV, r, c))], out_specs=_whole((r, c)),
        out_shape=jax.ShapeDtypeStruct((r, c), F32), compiler_params=_params("arbitrary"),
    )(parts)


def _adam_flat(w, g, m, v, name):
    r, c = w.shape

    def body(w_ref, g_ref, m_ref, v_ref, d_ref, m2_ref, v2_ref):
        d_ref[...], m2_ref[...], v2_ref[...] = _adam(w_ref[...], g_ref[...], m_ref[...], v_ref[...])

    return pl.pallas_call(
        body, name=name, grid=(1,), in_specs=[_whole((r, c))] * 4, out_specs=[_whole((r, c))] * 3,
        out_shape=[jax.ShapeDtypeStruct((r, c), F32)] * 3, compiler_params=_params("arbitrary"),
    )(w, g, m, v)


def _pack(arrays):
    flat = jnp.concatenate([a.reshape(-1).astype(F32) for a in arrays])
    rows = -(-flat.shape[0] // 1024) * 8
    return jnp.pad(flat, (0, rows * 128 - flat.shape[0])).reshape(rows, 128)


def _unpack(slab, like):
    flat = slab.reshape(-1)
    out, at = [], 0
    for a in like:
        out.append(flat[at:at + a.size].reshape(a.shape))
        at += a.size
    return out


def kernel(x, mem, mem_norm, mix_pre_norm, mix_post_norm, w_in, pool_maps, pool_scale, conf_dw_w, conf_dw_b, conf_ln_g, conf_ln_b, sconv_w, w_out, xattn_pre_norm, xattn_post_norm, xattn_wq, xattn_wk, xattn_wv, xattn_wo, ffn_pre_norm, ffn_post_norm, ffn_w_up, ffn_conv_w, ffn_w_down, loss_target, m_mem_norm, m_mix_pre_norm, m_mix_post_norm, m_w_in, m_pool_maps, m_pool_scale, m_conf_dw_w, m_conf_dw_b, m_conf_ln_g, m_conf_ln_b, m_sconv_w, m_w_out, m_xattn_pre_norm, m_xattn_post_norm, m_xattn_wq, m_xattn_wk, m_xattn_wv, m_xattn_wo, m_ffn_pre_norm, m_ffn_post_norm, m_ffn_w_up, m_ffn_conv_w, m_ffn_w_down, v_mem_norm, v_mix_pre_norm, v_mix_post_norm, v_w_in, v_pool_maps, v_pool_scale, v_conf_dw_w, v_conf_dw_b, v_conf_ln_g, v_conf_ln_b, v_sconv_w, v_w_out, v_xattn_pre_norm, v_xattn_post_norm, v_xattn_wq, v_xattn_wk, v_xattn_wv, v_xattn_wo, v_ffn_pre_norm, v_ffn_post_norm, v_ffn_w_up, v_ffn_conv_w, v_ffn_w_down):
    weights = dict(mem_norm=mem_norm, mix_pre_norm=mix_pre_norm, mix_post_norm=mix_post_norm, w_in=w_in, pool_maps=pool_maps, pool_scale=pool_scale, conf_dw_w=conf_dw_w, conf_dw_b=conf_dw_b, conf_ln_g=conf_ln_g, conf_ln_b=conf_ln_b, sconv_w=sconv_w, w_out=w_out, xattn_pre_norm=xattn_pre_norm, xattn_post_norm=xattn_post_norm, xattn_wq=xattn_wq, xattn_wk=xattn_wk, xattn_wv=xattn_wv, xattn_wo=xattn_wo, ffn_pre_norm=ffn_pre_norm, ffn_post_norm=ffn_post_norm, ffn_w_up=ffn_w_up, ffn_conv_w=ffn_conv_w, ffn_w_down=ffn_w_down)
    mom1 = dict(mem_norm=m_mem_norm, mix_pre_norm=m_mix_pre_norm, mix_post_norm=m_mix_post_norm, w_in=m_w_in, pool_maps=m_pool_maps, pool_scale=m_pool_scale, conf_dw_w=m_conf_dw_w, conf_dw_b=m_conf_dw_b, conf_ln_g=m_conf_ln_g, conf_ln_b=m_conf_ln_b, sconv_w=m_sconv_w, w_out=m_w_out, xattn_pre_norm=m_xattn_pre_norm, xattn_post_norm=m_xattn_post_norm, xattn_wq=m_xattn_wq, xattn_wk=m_xattn_wk, xattn_wv=m_xattn_wv, xattn_wo=m_xattn_wo, ffn_pre_norm=m_ffn_pre_norm, ffn_post_norm=m_ffn_post_norm, ffn_w_up=m_ffn_w_up, ffn_conv_w=m_ffn_conv_w, ffn_w_down=m_ffn_w_down)
    mom2 = dict(mem_norm=v_mem_norm, mix_pre_norm=v_mix_pre_norm, mix_post_norm=v_mix_post_norm, w_in=v_w_in, pool_maps=v_pool_maps, pool_scale=v_pool_scale, conf_dw_w=v_conf_dw_w, conf_dw_b=v_conf_dw_b, conf_ln_g=v_conf_ln_g, conf_ln_b=v_conf_ln_b, sconv_w=v_sconv_w, w_out=v_w_out, xattn_pre_norm=v_xattn_pre_norm, xattn_post_norm=v_xattn_post_norm, xattn_wq=v_xattn_wq, xattn_wk=v_xattn_wk, xattn_wv=v_xattn_wv, xattn_wo=v_xattn_wo, ffn_pre_norm=v_ffn_pre_norm, ffn_post_norm=v_ffn_post_norm, ffn_w_up=v_ffn_w_up, ffn_conv_w=v_ffn_conv_w, ffn_w_down=v_ffn_w_down)
    names = list(weights)

    nl, d = mix_pre_norm.shape
    x0, mem0, target = _to_steps(x[0]), mem[0], _to_steps(loss_target[0])
    dp, dc, ds, *_ = _mix_dims(d)
    pg = dp // len(POOL_WINDOWS)
    me = 4 * lax.axis_index("x") + 2 * lax.axis_index("y") + lax.axis_index("c")

    big = ["w_in", "w_out", "xattn_wq", "xattn_wk", "xattn_wv", "xattn_wo", "ffn_w_up", "ffn_w_down"]
    transposed = ("w_in", "ffn_w_up")

    def row_shard(n, a):
        return a.transpose(0, 2, 1) if n in transposed else a

    shards = [row_shard(n, weights[n]).astype(BF16) for n in big]
    taps = ["conf_dw_w", "sconv_w", "ffn_conv_w"]
    tap_slab = _pack([weights[n] for n in taps])
    *first_w, tap_all = _allgather([s[0:1] for s in shards] + [tap_slab[None]], "gather_weights0")
    layer_w = [dict(zip(big, [a[0] for a in first_w]))]
    tap_all = tap_all[0].reshape(N_DEV, *tap_slab.shape)
    tap_parts = [_unpack(tap_all[p], [weights[n] for n in taps]) for p in range(N_DEV)]
    wdw, wsc, wcf = (jnp.concatenate([tap_parts[p][i] for p in range(N_DEV)], axis=-1) for i in range(3))

    def g3(a):
        return a.reshape(a.shape[0], 1, a.shape[-1])

    mbd = jnp.zeros((nl, dp, dp), F32)
    for gi in range(len(POOL_WINDOWS)):
        mbd = mbd.at[:, gi * pg:(gi + 1) * pg, gi * pg:(gi + 1) * pg].set(pool_maps[:, gi])
    mbd = mbd.astype(BF16)
    pre1, post1, pre2, post2, pre3, post3 = (g3(weights[n]) for n in (
        "mix_pre_norm", "mix_post_norm", "xattn_pre_norm", "xattn_post_norm", "ffn_pre_norm", "ffn_post_norm"))
    pscale3, bdw3, lng3, lnb3 = g3(pool_scale), g3(conf_dw_b), g3(conf_ln_g), g3(conf_ln_b)
    memg3 = mem_norm.reshape(1, 1, d)

    def mm(a, b, mode, dt, name, tm=2048, tn=1024, tk=1024, a_outer=True):
        return _matmul(a, b, mode, dt, name, tm=tm, tn=tn, tk=tk, a_outer=a_outer)

    mem_n = _prenorm(mem0, memg3, 0, "mem_norm")
    xs = x0
    h = _prenorm(xs, pre1, 0, "pre_norm0")
    saved = []
    for l in range(nl):
        ps_l = pscale3
        if l + 1 < nl:
            lands = [lax.empty((N_DEV * s.shape[1], s.shape[2]), BF16) for s in shards]
            flying = _split_start([s[l + 1] for s in shards], lands, _gather_piece,
                                  xs if l else layer_w[0]["w_in"], f"gather_start{l + 1}")
            ps_l = pscale3 + flying[-1][0, 0]
        w = layer_w[l]
        s = {"x": xs, "h": h}
        s["z"] = mm(h, w["w_in"], "nt", F32, f"z{l}", tm=1024, tn=4096)
        s["cat"], s["c"] = _mix_fwd(s["z"], mbd, ps_l, wdw, bdw3, lng3, lnb3, wsc, l, f"mix_fwd{l}")
        s["y1"] = mm(s["cat"], w["w_out"], "nn", F32, f"y1_{l}")
        s["x1"], s["h1"] = _resnorm(xs, s["y1"], post1, l, pre2, l, f"resnorm1_{l}")
        s["q"] = mm(s["h1"], w["xattn_wq"], "nn", BF16, f"q{l}")
        s["k"] = mm(mem_n, w["xattn_wk"], "nn", BF16, f"k{l}")
        s["v"] = mm(mem_n, w["xattn_wv"], "nn", BF16, f"v{l}")
        s["o"] = _attn_fwd(s["q"], s["k"], s["v"], f"attn_fwd{l}")
        s["y2"] = mm(s["o"], w["xattn_wo"], "nn", F32, f"y2_{l}")
        s["x2"], s["h2"] = _resnorm(s["x1"], s["y2"], post2, l, pre3, l, f"resnorm2_{l}")
        s["u"] = mm(s["h2"], w["ffn_w_up"], "nt", F32, f"u{l}", tn=1408, a_outer=False)
        s["a"] = _ffn_act_fwd(s["u"], wcf, l, f"ffn_act{l}")
        s["y3"] = mm(s["a"], w["ffn_w_down"], "nn", F32, f"y3_{l}", tm=1024, tk=4096)
        if l + 1 < nl:
            xs, h = _resnorm(s["x2"], s["y3"], post3, l, pre1, l + 1, f"resnorm3_{l}")
            layer_w.append(dict(zip(big, _split_wait(flying, xs, _gather_piece, f"gather_wait{l + 1}"))))
        saved.append(s)

    last = saved[-1]
    dxn, dy3, dg_post3, loss_lanes = _loss_head(last["x2"], last["y3"], post3, nl - 1, target, "loss_head")
    loss = lax.psum(loss_lanes[0, 0], ("x", "y", "c"))

    recvs = [lax.empty((max(nl - 1, 1), N_DEV, s.shape[1], d), BF16) for s in shards]
    recv0 = [lax.empty((1, N_DEV, s.shape[1], d), BF16) for s in shards]
    small = {n: [None] * nl for n in names if n not in big and n != "mem_norm"}
    small["ffn_post_norm"][nl - 1] = dg_post3
    dmem_n = jnp.zeros(mem0.shape, F32)
    flying = None
    for l in reversed(range(nl)):
        s, w = saved[l], layer_w[l]
        wc_l = wcf if flying is None else wcf + flying[-1][0, 0]
        gw = {}
        da = mm(dy3, w["ffn_w_down"], "nt", F32, f"da{l}", tn=1408, a_outer=False)
        gw["ffn_w_down"] = mm(s["a"], dy3, "tn", BF16, f"dw_down{l}", tm=1408, tk=2048)
        du, small["ffn_conv_w"][l] = _ffn_act_bwd(s["u"], da, wc_l, l, f"ffn_act_bwd{l}")
        dh2 = mm(du, w["ffn_w_up"], "nn", F32, f"dh2_{l}", tm=512, tk=8192)
        gw["ffn_w_up"] = mm(du, s["h2"], "tn", BF16, f"dw_up{l}", tm=1408, tk=2048)
        dx2, small["ffn_pre_norm"][l], dy2, small["xattn_post_norm"][l] = _norm_bwd(
            dxn, dh2, s["x2"], pre3, l, f"norm_bwd3_{l}", s["y2"], post2, l)
        do = mm(dy2, w["xattn_wo"], "nt", BF16, f"do{l}")
        gw["xattn_wo"] = mm(s["o"], dy2, "tn", BF16, f"dw_o{l}", tk=4096)
        dq, dk, dv = _attn_bwd(s["q"], s["k"], s["v"], do, f"attn_bwd{l}")
        dkb, dvb = dk.astype(BF16), dv.astype(BF16)
        dh1 = mm(dq, w["xattn_wq"], "nt", F32, f"dh1_{l}")
        gw["xattn_wq"] = mm(s["h1"], dq, "tn", BF16, f"dw_q{l}", tk=4096)
        gw["xattn_wk"] = mm(mem_n, dkb, "tn", BF16, f"dw_k{l}")
        gw["xattn_wv"] = mm(mem_n, dvb, "tn", BF16, f"dw_v{l}")
        dmem_n = dmem_n + mm(dkb, w["xattn_wk"], "nt", F32, f"dmem_k{l}") \
            + mm(dvb, w["xattn_wv"], "nt", F32, f"dmem_v{l}")
        pre2_l = pre2
        if l == 0:
            flying0 = _split_start([gw[n] for n in big[2:]], recv0[2:], _scatter_piece(0), dh1, "scatter_start0a")
            pre2_l = pre2 + flying0[-1][0, 0]
        dx1, small["xattn_pre_norm"][l], dy1, small["mix_post_norm"][l] = _norm_bwd(
            dx2, dh1, s["x1"], pre2_l, l, f"norm_bwd2_{l}", s["y1"], post1, l)
        dcat = mm(dy1, w["w_out"], "nt", F32, f"dcat{l}")
        gw["w_out"] = mm(s["cat"], dy1, "tn", BF16, f"dw_out{l}", tk=4096)
        dz, dmbd, dps, dwdw, dbdw, dlng, dlnb, dwsc = _mix_bwd(
            dcat, s["z"], s["c"], mbd, pscale3, wdw, lng3, lnb3, wsc, l, f"mix_bwd{l}")
        small["pool_maps"][l] = jnp.stack([dmbd[gi * pg:(gi + 1) * pg, gi * pg:(gi + 1) * pg]
                                           for gi in range(len(POOL_WINDOWS))])
        small["pool_scale"][l], small["conf_dw_w"][l], small["conf_dw_b"][l] = dps, dwdw, dbdw
        small["conf_ln_g"][l], small["conf_ln_b"][l], small["sconv_w"][l] = dlng, dlnb, dwsc
        dh = mm(dz, w["w_in"], "nn", F32, f"dh{l}", tk=4096)
        gw["w_in"] = mm(dz, s["h"], "tn", BF16, f"dw_in{l}", tm=4096, tk=2048)
        if l > 0:
            dxn, small["mix_pre_norm"][l], dy3, small["ffn_post_norm"][l - 1] = _norm_bwd(
                dx1, dh, s["x"], pre1, l, f"norm_bwd1_{l}", saved[l - 1]["y3"], post3, l - 1)
        else:
            dxn, small["mix_pre_norm"][l] = _norm_bwd(dx1, dh, s["x"], pre1, l, "norm_bwd1_0")
        if flying is not None:
            recvs = _split_wait(flying, dxn, _scatter_piece(l), f"scatter_wait{l + 1}")
        if l:
            flying = _split_start([gw[n] for n in big], recvs, _scatter_piece(l - 1), dxn, f"scatter_start{l}")
        else:
            flying = _split_start([gw[n] for n in big[:2]], recv0[:2], _scatter_piece(0), dxn, "scatter_start0b")
    grad_x = _from_steps(dxn)[None]
    _, dg_mem = _norm_bwd(jnp.zeros(mem0.shape, F32), dmem_n, mem0, memg3, 0, "norm_bwd_mem")

    small_names = [n for n in names if n not in big]
    partial = {n: (dg_mem.reshape(d) if n == "mem_norm" else
                   jnp.stack([g.reshape(g.shape[-1]) if g.shape[0] == 1 and weights[n].ndim == 2 else g
                              for g in small[n]])) for n in small_names}
    slab = _pack([partial[n] for n in small_names])
    gathered = _allgather([slab[None]], "gather_small_grads")[0][0].reshape(N_DEV, *slab.shape)
    summed = dict(zip(small_names, _unpack(_sum_sources(gathered, "sum_small_grads"), [partial[n] for n in small_names])))
    grad = {}
    for n in small_names:
        g = summed[n]
        if n in taps:
            width = weights[n].shape[-1]
            g = lax.dynamic_slice_in_dim(g, me * width, width, axis=g.ndim - 1)
        grad[n] = g

    delta, new_m, new_v = {}, {}, {}
    wmv = {n: [row_shard(n, a[n]) for a in (weights, mom1, mom2)] for n in big}
    upper = {n: _adam_sharded(recv, 1, *wmv[n], 1, nl, f"adam_{n}") for n, recv in zip(big, recvs)} if nl > 1 else {}
    upd = _adam_flat(_pack([weights[n] for n in small_names]), _pack([grad[n] for n in small_names]),
                     _pack([mom1[n] for n in small_names]), _pack([mom2[n] for n in small_names]), "adam_small")
    for out, slab_o in zip((delta, new_m, new_v), upd):
        out.update(zip(small_names, _unpack(slab_o, [weights[n] for n in small_names])))
    done = sum(r[3][0, 0, :1] for r in upper.values()) + upd[0][0, :1]
    recv0 = _split_wait(flying, done, _scatter_piece(0), "scatter_wait0b") \
        + _split_wait(flying0, done, _scatter_piece(0), "scatter_wait0a")
    for n, recv in zip(big, recv0):
        res = _adam_sharded(recv, 0, *wmv[n], 0, 1, f"adam0_{n}", prev=upper.get(n))
        grad[n], delta[n], new_m[n], new_v[n] = (row_shard(n, r) for r in res)

    return (loss, grad_x, *[grad[n] for n in names], *[delta[n] for n in names],
            *[new_m[n] for n in names], *[new_v[n] for n in names])
```

```python
import jax
import jax.numpy as jnp
from jax import lax
from jax.experimental import pallas as pl
from jax.experimental.pallas import tpu as pltpu

F32, BF16 = jnp.float32, jnp.bfloat16
EPS = 1e-6
POOL_WINDOWS = (2, 4, 8, 16)
MAX_WINDOW = 16
CONF_TAPS, SHORT_TAPS = 31, 3
CONF_HALO, POOL_HALO, SHORT_HALO = 256, 128, 16
ROW_CHUNK = 64
HEADS = 4
N_DEV = 8
ADAM_LR, ADAM_B1, ADAM_B2, ADAM_EPS, ADAM_WD, ADAM_STEP = 0.001, 0.9, 0.999, 1e-08, 0.01, 10
VMEM_LIMIT_V7X = 56 * 2**20
MESH = pl.DeviceIdType.MESH
ANY = pl.BlockSpec(memory_space=pl.ANY)
HBM = pl.BlockSpec(memory_space=pltpu.HBM)
SEM = pl.BlockSpec(memory_space=pltpu.SEMAPHORE)
EFFECT = pltpu.SideEffectType.DATAFLOW_SIDE_EFFECTING

TILE_NORM, TILE_MIX, TILE_FFN, TILE_ATTN, TILE_ADAM = 256, 512, 256, 512, 352


def _params(*sem):
    return pltpu.CompilerParams(dimension_semantics=sem, vmem_limit_bytes=VMEM_LIMIT_V7X)


def _sig(x):
    return 1.0 / (1.0 + jnp.exp(-x))


def _rms(x):
    r = lax.rsqrt(jnp.mean(x * x, axis=-1, keepdims=True) + EPS)
    return x * r, r


def _rms_bwd(dout, g, n, r):
    dn = dout * g
    return r * (dn - n * jnp.mean(dn * n, axis=-1, keepdims=True))


def _rows(tt, c):
    return pl.BlockSpec((tt, c), lambda i: (i, 0))


def _whole(shape):
    return pl.BlockSpec(shape, lambda i: (0,) * len(shape))


def _layer(shape, l):
    return pl.BlockSpec((None,) + shape, lambda i: (l,) + (0,) * len(shape))


def _colsum(x):
    return jnp.sum(x, axis=0, keepdims=True)


_DIMS = {"nn": (((1,), (0,)), ((), ())), "nt": (((1,), (1,)), ((), ())), "tn": (((0,), (0,)), ((), ()))}


def _matmul(a, b, mode, out_dtype, name, *, tm, tn, tk, a_outer=True):
    if mode == "nn":
        (m, k), (k2, n) = a.shape, b.shape
    elif mode == "nt":
        (m, k), (n, k2) = a.shape, b.shape
    else:
        (k, m), (k2, n) = a.shape, b.shape
    assert k == k2, (name, a.shape, b.shape)
    tm, tn, tk = min(tm, m), min(tn, n), min(tk, k)
    assert m % tm == 0 and n % tn == 0 and k % tk == 0, (name, m, n, k, tm, tn, tk)
    gm, gn, gk = m // tm, n // tn, k // tk

    def ij(g0, g1):
        return (g0, g1) if a_outer else (g1, g0)

    def a_map(g0, g1, kk):
        i, _ = ij(g0, g1)
        return (kk, i) if mode == "tn" else (i, kk)

    def b_map(g0, g1, kk):
        _, j = ij(g0, g1)
        return (j, kk) if mode == "nt" else (kk, j)

    def o_map(g0, g1, kk):
        return ij(g0, g1)

    a_block = (tk, tm) if mode == "tn" else (tm, tk)
    b_block = (tn, tk) if mode == "nt" else (tk, tn)
    dims = _DIMS[mode]

    def body(a_ref, b_ref, o_ref, *acc):
        p = lax.dot_general(a_ref[...].astype(BF16), b_ref[...].astype(BF16), dims, preferred_element_type=F32)
        if gk == 1:
            o_ref[...] = p.astype(o_ref.dtype)
        else:
            kk = pl.program_id(2)

            @pl.when(kk == 0)
            def _():
                acc[0][...] = p

            @pl.when(kk > 0)
            def _():
                acc[0][...] += p

            @pl.when(kk == gk - 1)
            def _():
                o_ref[...] = acc[0][...].astype(o_ref.dtype)

    return pl.pallas_call(
        body, name=name, grid=(gm, gn, gk) if a_outer else (gn, gm, gk),
        in_specs=[pl.BlockSpec(a_block, a_map), pl.BlockSpec(b_block, b_map)],
        out_specs=pl.BlockSpec((tm, tn), o_map),
        out_shape=jax.ShapeDtypeStruct((m, n), out_dtype),
        scratch_shapes=[pltpu.VMEM((tm, tn), F32)] if gk > 1 else [],
        compiler_params=_params("parallel", "parallel", "arbitrary"),
    )(a, b)


def _prenorm(x, g3, l, name):
    t, d = x.shape
    tt = min(TILE_NORM, t)

    def body(x_ref, g_ref, h_ref):
        n, _ = _rms(x_ref[...])
        h_ref[...] = (n * g_ref[...]).astype(BF16)

    return pl.pallas_call(
        body, name=name, grid=(t // tt,),
        in_specs=[_rows(tt, d), _layer((1, d), l)], out_specs=_rows(tt, d),
        out_shape=jax.ShapeDtypeStruct((t, d), BF16), compiler_params=_params("parallel"),
    )(x, g3)


def _resnorm(x, y, gpost3, l, gnext3, l2, name):
    t, d = x.shape
    tt = min(TILE_NORM, t)

    def body(x_ref, y_ref, gp_ref, gn_ref, xo_ref, h_ref):
        n, _ = _rms(y_ref[...])
        xn = x_ref[...] + n * gp_ref[...]
        xo_ref[...] = xn
        n2, _ = _rms(xn)
        h_ref[...] = (n2 * gn_ref[...]).astype(BF16)

    return pl.pallas_call(
        body, name=name, grid=(t // tt,),
        in_specs=[_rows(tt, d), _rows(tt, d), _layer((1, d), l), _layer((1, d), l2)],
        out_specs=[_rows(tt, d), _rows(tt, d)],
        out_shape=[jax.ShapeDtypeStruct((t, d), F32), jax.ShapeDtypeStruct((t, d), BF16)],
        compiler_params=_params("parallel"),
    )(x, y, gpost3, gnext3)


def _loss_head(x, y, gpost3, l, target, name):
    t, d = x.shape
    tt = min(TILE_NORM, t)

    def body(x_ref, y_ref, g_ref, t_ref, dxn_ref, dy_ref, dg_ref, loss_ref):
        @pl.when(pl.program_id(0) == 0)
        def _():
            dg_ref[...] = jnp.zeros_like(dg_ref)
            loss_ref[...] = jnp.zeros_like(loss_ref)

        g = g_ref[...]
        n, r = _rms(y_ref[...])
        diff = x_ref[...] + n * g - t_ref[...]
        loss_ref[...] += 0.5 * jnp.sum(jnp.mean(diff * diff, axis=-1, keepdims=True))
        dxn = diff * (1.0 / d)
        dxn_ref[...] = dxn
        dy_ref[...] = _rms_bwd(dxn, g, n, r).astype(BF16)
        dg_ref[...] += _colsum(dxn * n)

    return pl.pallas_call(
        body, name=name, grid=(t // tt,),
        in_specs=[_rows(tt, d), _rows(tt, d), _layer((1, d), l), _rows(tt, d)],
        out_specs=[_rows(tt, d), _rows(tt, d), _whole((1, d)), _whole((1, 128))],
        out_shape=[jax.ShapeDtypeStruct((t, d), F32), jax.ShapeDtypeStruct((t, d), BF16),
                   jax.ShapeDtypeStruct((1, d), F32), jax.ShapeDtypeStruct((1, 128), F32)],
        compiler_params=_params("arbitrary"),
    )(x, y, gpost3, target)


def _norm_bwd(dxn, dh, x_in, gpre3, l, name, y_prev=None, gpost3=None, l_prev=None):
    t, d = x_in.shape
    tt = min(TILE_NORM, t)
    has_prev = y_prev is not None

    def body(*refs):
        if has_prev:
            dxn_ref, dh_ref, x_ref, g_ref, y_ref, g2_ref, dx_ref, dg_ref, dy_ref, dg2_ref = refs
        else:
            dxn_ref, dh_ref, x_ref, g_ref, dx_ref, dg_ref = refs

        @pl.when(pl.program_id(0) == 0)
        def _():
            dg_ref[...] = jnp.zeros_like(dg_ref)
            if has_prev:
                dg2_ref[...] = jnp.zeros_like(dg2_ref)

        dh_v = dh_ref[...]
        n, r = _rms(x_ref[...])
        dx = dxn_ref[...] + _rms_bwd(dh_v, g_ref[...], n, r)
        dx_ref[...] = dx
        dg_ref[...] += _colsum(dh_v * n)
        if has_prev:
            n2, r2 = _rms(y_ref[...])
            dy_ref[...] = _rms_bwd(dx, g2_ref[...], n2, r2).astype(BF16)
            dg2_ref[...] += _colsum(dx * n2)

    in_specs = [_rows(tt, d), _rows(tt, d), _rows(tt, d), _layer((1, d), l)]
    out_specs = [_rows(tt, d), _whole((1, d))]
    out_shape = [jax.ShapeDtypeStruct((t, d), F32), jax.ShapeDtypeStruct((1, d), F32)]
    args = [dxn, dh, x_in, gpre3]
    if has_prev:
        in_specs += [_rows(tt, d), _layer((1, d), l_prev)]
        out_specs += [_rows(tt, d), _whole((1, d))]
        out_shape += [jax.ShapeDtypeStruct((t, d), BF16), jax.ShapeDtypeStruct((1, d), F32)]
        args += [y_prev, gpost3]
    return pl.pallas_call(
        body, name=name, grid=(t // tt,), in_specs=in_specs, out_specs=out_specs, out_shape=out_shape,
        compiler_params=_params("arbitrary"),
    )(*args)


def _halves(tt):
    return [slice(0, tt // 2), slice(tt // 2, tt)] if tt % 32 == 0 else [slice(0, tt)]


def _matmul_resnorm(a, w, x, gpost3, l, gnext3, l2, name, tm=512):
    t, k = a.shape
    d = w.shape[1]
    tm = min(tm, t)

    def body(a_ref, w_ref, x_ref, gp_ref, gn_ref, y_ref, xo_ref, h_ref):
        wv, gp, gn = w_ref[...], gp_ref[...], gn_ref[...]
        for rows in _halves(tm):
            y = jnp.dot(a_ref[rows, :], wv, preferred_element_type=F32)
            y_ref[rows, :] = y
            n, _ = _rms(y)
            xn = x_ref[rows, :] + n * gp
            xo_ref[rows, :] = xn
            n2, _ = _rms(xn)
            h_ref[rows, :] = (n2 * gn).astype(BF16)

    return pl.pallas_call(
        body, name=name, grid=(t // tm,),
        in_specs=[_rows(tm, k), _whole((k, d)), _rows(tm, d), _layer((1, d), l), _layer((1, d), l2)],
        out_specs=[_rows(tm, d)] * 3,
        out_shape=[jax.ShapeDtypeStruct((t, d), F32), jax.ShapeDtypeStruct((t, d), F32),
                   jax.ShapeDtypeStruct((t, d), BF16)],
        compiler_params=_params("parallel"),
    )(a, w, x, gpost3, gnext3)


def _matmul_norm_bwd(a, w, mode, dxn, x_in, gpre3, l, name, y_prev=None, gpost3=None, l_prev=None, tm=512):
    t, k = a.shape
    d = x_in.shape[1]
    tm = min(tm, t)
    has_prev = y_prev is not None
    dims = _DIMS[mode]

    def body(*refs):
        if has_prev:
            a_ref, w_ref, dxn_ref, x_ref, g_ref, y_ref, g2_ref, dx_ref, dg_ref, dy_ref, dg2_ref = refs
        else:
            a_ref, w_ref, dxn_ref, x_ref, g_ref, dx_ref, dg_ref = refs

        @pl.when(pl.program_id(0) == 0)
        def _():
            dg_ref[...] = jnp.zeros_like(dg_ref)
            if has_prev:
                dg2_ref[...] = jnp.zeros_like(dg2_ref)

        wv, g = w_ref[...], g_ref[...]
        for rows in _halves(tm):
            dh = lax.dot_general(a_ref[rows, :], wv, dims, preferred_element_type=F32)
            n, r = _rms(x_ref[rows, :])
            dx = dxn_ref[rows, :] + _rms_bwd(dh, g, n, r)
            dx_ref[rows, :] = dx
            dg_ref[...] += _colsum(dh * n)
            if has_prev:
                n2, r2 = _rms(y_ref[rows, :])
                dy_ref[rows, :] = _rms_bwd(dx, g2_ref[...], n2, r2).astype(BF16)
                dg2_ref[...] += _colsum(dx * n2)

    in_specs = [_rows(tm, k), _whole(w.shape), _rows(tm, d), _rows(tm, d), _layer((1, d), l)]
    out_specs = [_rows(tm, d), _whole((1, d))]
    out_shape = [jax.ShapeDtypeStruct((t, d), F32), jax.ShapeDtypeStruct((1, d), F32)]
    args = [a, w, dxn, x_in, gpre3]
    if has_prev:
        in_specs += [_rows(tm, d), _layer((1, d), l_prev)]
        out_specs += [_rows(tm, d), _whole((1, d))]
        out_shape += [jax.ShapeDtypeStruct((t, d), BF16), jax.ShapeDtypeStruct((1, d), F32)]
        args += [y_prev, gpost3]
    return pl.pallas_call(
        body, name=name, grid=(t // tm,), in_specs=in_specs, out_specs=out_specs, out_shape=out_shape,
        compiler_params=_params("arbitrary"),
    )(*args)


def _to_steps(a):
    t = a.shape[0]
    return a.reshape(8, t // 8, -1).transpose(1, 0, 2).reshape(a.shape)


def _from_steps(a):
    t = a.shape[0]
    return a.reshape(t // 8, 8, -1).transpose(1, 0, 2).reshape(a.shape)


def _al(v):
    return v if isinstance(v, int) else pl.multiple_of(v, 8)


def _chunks(n_rows, fn, unroll=1):
    def step(r, carry):
        fn(pl.multiple_of(r * ROW_CHUNK, ROW_CHUNK))
        return carry
    lax.fori_loop(0, n_rows // ROW_CHUNK, step, 0, unroll=unroll)


def _fold8(a):
    return a.reshape(a.shape[0] // 8, 8, a.shape[1]).sum(axis=0)


def _shift_down(a):
    row = lax.broadcasted_iota(jnp.int32, a.shape, 0)
    return jnp.where(row % 8 == 0, 0.0, pltpu.roll(a, 1, 0))


def _shift_up(a):
    row = lax.broadcasted_iota(jnp.int32, a.shape, 0)
    return jnp.where(row % 8 == 7, 0.0, pltpu.roll(a, a.shape[0] - 1, 0))


def _prev_block(h, c, tt, t):
    return pl.BlockSpec((h, c), lambda i: (jnp.where(i == 0, t // h - 1, i * (tt // h) - 1), 0))


def _next_block(h, c, tt, t):
    return pl.BlockSpec((h, c), lambda i: (jnp.where(i == t // tt - 1, 0, (i + 1) * (tt // h)), 0))


def _taps(w_ref, buf, start, taps, rc, lanes=slice(None)):
    acc = w_ref[0:1, lanes] * buf[pl.ds(_al(start), rc), lanes]
    for k in range(1, taps):
        acc = acc + w_ref[k:k + 1, lanes] * buf[pl.ds(_al(start + 8 * k), rc), lanes]
    return acc


def _taps_rev(w_ref, buf, start, taps, rc, lanes=slice(None)):
    acc = w_ref[0:1, lanes] * buf[pl.ds(_al(start + 8 * (taps - 1)), rc), lanes]
    for k in range(1, taps):
        acc = acc + w_ref[k:k + 1, lanes] * buf[pl.ds(_al(start + 8 * (taps - 1 - k)), rc), lanes]
    return acc


def _mix_dims(d):
    dp = d // 4
    dc = 3 * d // 8
    ds = d - dp - dc
    oa, og = dp, dp + dc
    ob = dp + 2 * dc
    oc, ox = ob + ds, ob + 2 * ds
    return dp, dc, ds, oa, og, ob, oc, ox, ox + ds


def _pool_consts(dp):
    win = jnp.repeat(jnp.asarray(POOL_WINDOWS, F32), dp // len(POOL_WINDOWS))[None, :]
    mask = (jnp.arange(MAX_WINDOW, dtype=F32)[:, None] < win).astype(F32)
    return mask, win


def _pool_count(row0, rc, dp, seg, wl):
    r = lax.broadcasted_iota(jnp.int32, (rc, dp), 0) + row0
    return jnp.minimum(((r & 7) * seg + (r >> 3) + 1).astype(F32), wl)


def _mix_fwd(z, mbd, pscale3, wdw, bdw3, lng3, lnb3, wsc, l, name):
    t, din = z.shape
    dp, dc, ds, oa, og, ob, oc, ox, din2 = _mix_dims(din * 8 // 17)
    assert din2 == din
    d = ob
    tt = min(TILE_MIX, t)
    hp, hc, hs, rc = POOL_HALO, CONF_HALO, SHORT_HALO, ROW_CHUNK
    assert tt % hc == 0 and t % tt == 0
    seg = t // 8
    pmask, wlane = _pool_consts(dp)

    def body(z_ref, zpa_ref, zpb_ref, mbd_ref, ps_ref, pmask_ref, wl_ref, wdw_ref, bdw_ref, lng_ref, lnb_ref, wsc_ref,
             cat_ref, c_ref, pbuf, vbuf, sbuf):
        i = pl.program_id(0)
        pbuf[0:hp, :] = zpa_ref[hc - hp:hc, 0:dp]

        def prev(r0):
            rows = pl.ds(r0, rc)
            vbuf[rows, :] = zpa_ref[rows, oa:oa + dc] * _sig(zpa_ref[rows, og:og + dc])
        _chunks(hc, prev)
        sbuf[0:hs, :] = zpb_ref[:, oc:oc + ds] * zpb_ref[:, ox:ox + ds]

        @pl.when(i == 0)
        def _():
            pbuf[0:hp, :] = _shift_down(pbuf[0:hp, :])
            vbuf[0:hc, :] = _shift_down(vbuf[0:hc, :])
            sbuf[0:hs, :] = _shift_down(sbuf[0:hs, :])

        mbd_v, ps, wl = mbd_ref[...], ps_ref[...], wl_ref[...]
        bdw, lng, lnb = bdw_ref[...], lng_ref[...], lnb_ref[...]

        def step(r0):
            rows = pl.ds(r0, rc)
            zp = z_ref[rows, 0:dp]
            pbuf[pl.ds(_al(hp + r0), rc), :] = zp
            vbuf[pl.ds(_al(hc + r0), rc), :] = z_ref[rows, oa:oa + dc] * _sig(z_ref[rows, og:og + dc])
            sbuf[pl.ds(_al(hs + r0), rc), :] = z_ref[rows, oc:oc + ds] * z_ref[rows, ox:ox + ds]
            pooled = _taps_rev(pmask_ref, pbuf, r0 + hp - 8 * (MAX_WINDOW - 1), MAX_WINDOW, rc)
            pooled = pooled / _pool_count(i * tt + r0, rc, dp, seg, wl) - zp
            pm = jnp.dot(pooled.astype(BF16), mbd_v, preferred_element_type=F32)
            cat_ref[rows, 0:dp] = (pm * ps).astype(BF16)
            c = _taps(wdw_ref, vbuf, r0 + hc - 8 * (CONF_TAPS - 1), CONF_TAPS, rc) + bdw
            c_ref[rows, :] = c
            xc = c - jnp.mean(c, axis=-1, keepdims=True)
            nrm = xc * lax.rsqrt(jnp.mean(xc * xc, axis=-1, keepdims=True) + EPS)
            yln = nrm * lng + lnb
            cat_ref[rows, dp:dp + dc] = (yln * _sig(yln)).astype(BF16)
            cv = _taps(wsc_ref, sbuf, r0 + hs - 8 * (SHORT_TAPS - 1), SHORT_TAPS, rc)
            cat_ref[rows, dp + dc:d] = (z_ref[rows, ob:ob + ds] * cv).astype(BF16)
        _chunks(tt, step, unroll=2)

    return pl.pallas_call(
        body, name=name, grid=(t // tt,),
        in_specs=[_rows(tt, din), _prev_block(hc, d, tt, t), _prev_block(hs, din, tt, t),
                  _layer((dp, dp), l), _layer((1, dp), l), _whole((MAX_WINDOW, dp)), _whole((1, dp)),
                  _layer((CONF_TAPS, dc), l), _layer((1, dc), l), _layer((1, dc), l), _layer((1, dc), l),
                  _layer((SHORT_TAPS, ds), l)],
        out_specs=[_rows(tt, d), _rows(tt, dc)],
        out_shape=[jax.ShapeDtypeStruct((t, d), BF16), jax.ShapeDtypeStruct((t, dc), F32)],
        scratch_shapes=[pltpu.VMEM((hp + tt, dp), F32), pltpu.VMEM((hc + tt, dc), F32), pltpu.VMEM((hs + tt, ds), F32)],
        compiler_params=_params("parallel"),
    )(z, z, z, mbd, pscale3, pmask, wlane, wdw, bdw3, lng3, lnb3, wsc)


def _mix_bwd(dcat, z, c, mbd, pscale3, wdw, lng3, lnb3, wsc, l, name):
    t, din = z.shape
    dp, dc, ds, oa, og, ob, oc, ox, _ = _mix_dims(din * 8 // 17)
    d = ob
    tt = min(TILE_MIX, t)
    nt = t // tt
    hp, hc, hs, rc = POOL_HALO, CONF_HALO, SHORT_HALO, ROW_CHUNK
    assert tt % hc == 0 and t % tt == 0 and tt >= 8 * MAX_WINDOW
    seg = t // 8
    pmask, wlane = _pool_consts(dp)

    def body(dcat_ref, dcn_ref, z_ref, zpa_ref, zpb_ref, znb_ref, c_ref, cn_ref, mbd_ref, ps_ref, pmask_ref, wl_ref,
             wdw_ref, lng_ref, lnb_ref, wsc_ref,
             dz_ref, dmbd_ref, dps_ref, dwdw_ref, dbdw_ref, dlng_ref, dlnb_ref, dwsc_ref,
             pbuf, qbuf, dpbuf, pbf, vbuf, dcbuf, sbuf, dsbuf, dw8, ds8, ln8, ps8):
        i = pl.program_id(0)
        first, last = i == 0, i == nt - 1

        @pl.when(first)
        def _():
            for ref in (dmbd_ref, dw8, ds8, ln8, ps8):
                ref[...] = jnp.zeros_like(ref)

        mbd_v, ps, wl = mbd_ref[...], ps_ref[...], wl_ref[...]
        lng, lnb = lng_ref[...], lnb_ref[...]

        def ln_silu_bwd(cc, dyb):
            xc = cc - jnp.mean(cc, axis=-1, keepdims=True)
            rstd = lax.rsqrt(jnp.mean(xc * xc, axis=-1, keepdims=True) + EPS)
            nrm = xc * rstd
            yln = nrm * lng + lnb
            s = _sig(yln)
            dyln = dyb * (s * (1.0 + yln * (1.0 - s)))
            dn = dyln * lng
            dcc = rstd * (dn - jnp.mean(dn, axis=-1, keepdims=True) - nrm * jnp.mean(dn * nrm, axis=-1, keepdims=True))
            return dcc, dyln, nrm

        pbuf[0:hp, :] = zpa_ref[hc - hp:hc, 0:dp]

        def prev(r0):
            rows = pl.ds(r0, rc)
            vbuf[rows, :] = zpa_ref[rows, oa:oa + dc] * _sig(zpa_ref[rows, og:og + dc])
        _chunks(hc, prev)
        sbuf[0:hs, :] = zpb_ref[:, oc:oc + ds] * zpb_ref[:, ox:ox + ds]

        @pl.when(first)
        def _():
            pbuf[0:hp, :] = _shift_down(pbuf[0:hp, :])
            vbuf[0:hc, :] = _shift_down(vbuf[0:hc, :])
            sbuf[0:hs, :] = _shift_down(sbuf[0:hs, :])

        def nxt(r0):
            rows = pl.ds(r0, rc)
            dcc, _, _ = ln_silu_bwd(cn_ref[rows, :], dcn_ref[rows, dp:dp + dc])
            dcbuf[pl.ds(_al(tt + r0), rc), :] = dcc
        _chunks(hc, nxt, unroll=2)
        dpm_n = (dcn_ref[0:hp, 0:dp] * ps).astype(BF16)
        qbuf[tt:tt + hp, :] = lax.dot_general(dpm_n, mbd_v, _DIMS["nt"], preferred_element_type=F32) / wl
        dsbuf[tt:tt + hs, :] = dcn_ref[0:hs, dp + dc:d] * znb_ref[:, ob:ob + ds]

        @pl.when(last)
        def _():
            dcbuf[tt:tt + hc, :] = _shift_up(dcbuf[tt:tt + hc, :])
            qbuf[tt:tt + hp, :] = _shift_up(qbuf[tt:tt + hp, :])
            dsbuf[tt:tt + hs, :] = _shift_up(dsbuf[tt:tt + hs, :])

        def fill(r0):
            rows = pl.ds(r0, rc)
            zp = z_ref[rows, 0:dp]
            pbuf[pl.ds(_al(hp + r0), rc), :] = zp
            vbuf[pl.ds(_al(hc + r0), rc), :] = z_ref[rows, oa:oa + dc] * _sig(z_ref[rows, og:og + dc])
            sbuf[pl.ds(_al(hs + r0), rc), :] = z_ref[rows, oc:oc + ds] * z_ref[rows, ox:ox + ds]
            pooled = _taps_rev(pmask_ref, pbuf, r0 + hp - 8 * (MAX_WINDOW - 1), MAX_WINDOW, rc)
            pbf[rows, :] = (pooled / _pool_count(i * tt + r0, rc, dp, seg, wl) - zp).astype(BF16)
            dcc, dyln, nrm = ln_silu_bwd(c_ref[rows, :], dcat_ref[rows, dp:dp + dc])
            dcbuf[rows, :] = dcc
            ln8[0] += _fold8(dyln * nrm)
            ln8[1] += _fold8(dyln)
            ln8[2] += _fold8(dcc)
            dsbuf[rows, :] = dcat_ref[rows, dp + dc:d] * z_ref[rows, ob:ob + ds]
        _chunks(tt, fill, unroll=2)

        pb = pbf[...]
        dya = dcat_ref[:, 0:dp]
        ps8[...] += _fold8(dya * jnp.dot(pb, mbd_v, preferred_element_type=F32))
        dpm = (dya * ps).astype(BF16)
        dmbd_ref[...] += lax.dot_general(pb, dpm, _DIMS["tn"], preferred_element_type=F32)
        dpbuf[...] = lax.dot_general(dpm, mbd_v, _DIMS["nt"], preferred_element_type=F32)

        def quot(r0):
            rows = pl.ds(r0, rc)
            qbuf[rows, :] = dpbuf[rows, :] / _pool_count(i * tt + r0, rc, dp, seg, wl)
        _chunks(tt, quot)

        def back(r0):
            rows = pl.ds(r0, rc)
            dzp = _taps(pmask_ref, qbuf, r0, MAX_WINDOW, rc) - dpbuf[rows, :]
            dz_ref[rows, 0:dp] = dzp.astype(BF16)
            dcc = dcbuf[rows, :]
            for k in range(CONF_TAPS):
                dw8[k] += _fold8(dcc * vbuf[pl.ds(_al(r0 + hc - 8 * (CONF_TAPS - 1 - k)), rc), :])
            dv = _taps_rev(wdw_ref, dcbuf, r0, CONF_TAPS, rc)
            za = z_ref[rows, oa:oa + dc]
            sg = _sig(z_ref[rows, og:og + dc])
            dz_ref[rows, oa:oa + dc] = (dv * sg).astype(BF16)
            dz_ref[rows, og:og + dc] = (dv * za * sg * (1.0 - sg)).astype(BF16)
            cv = _taps(wsc_ref, sbuf, r0 + hs - 8 * (SHORT_TAPS - 1), SHORT_TAPS, rc)
            dz_ref[rows, ob:ob + ds] = (dcat_ref[rows, dp + dc:d] * cv).astype(BF16)
            dcv = dsbuf[rows, :]
            for k in range(SHORT_TAPS):
                ds8[k] += _fold8(dcv * sbuf[pl.ds(_al(r0 + hs - 8 * (SHORT_TAPS - 1 - k)), rc), :])
            dpv = _taps_rev(wsc_ref, dsbuf, r0, SHORT_TAPS, rc)
            dz_ref[rows, oc:oc + ds] = (dpv * z_ref[rows, ox:ox + ds]).astype(BF16)
            dz_ref[rows, ox:ox + ds] = (dpv * z_ref[rows, oc:oc + ds]).astype(BF16)
        _chunks(tt, back)

        @pl.when(last)
        def _():
            dps_ref[...] = jnp.sum(ps8[...], axis=0, keepdims=True)
            dwdw_ref[...] = jnp.sum(dw8[...], axis=1)
            dwsc_ref[...] = jnp.sum(ds8[...], axis=1)
            dlng_ref[...] = jnp.sum(ln8[0], axis=0, keepdims=True)
            dlnb_ref[...] = jnp.sum(ln8[1], axis=0, keepdims=True)
            dbdw_ref[...] = jnp.sum(ln8[2], axis=0, keepdims=True)

    return pl.pallas_call(
        body, name=name, grid=(nt,),
        in_specs=[_rows(tt, d), _next_block(hc, d, tt, t),
                  _rows(tt, din), _prev_block(hc, d, tt, t), _prev_block(hs, din, tt, t), _next_block(hs, din, tt, t),
                  _rows(tt, dc), _next_block(hc, dc, tt, t),
                  _layer((dp, dp), l), _layer((1, dp), l), _whole((MAX_WINDOW, dp)), _whole((1, dp)),
                  _layer((CONF_TAPS, dc), l), _layer((1, dc), l), _layer((1, dc), l), _layer((SHORT_TAPS, ds), l)],
        out_specs=[_rows(tt, din), _whole((dp, dp)), _whole((1, dp)), _whole((CONF_TAPS, dc)), _whole((1, dc)),
                   _whole((1, dc)), _whole((1, dc)), _whole((SHORT_TAPS, ds))],
        out_shape=[jax.ShapeDtypeStruct((t, din), BF16), jax.ShapeDtypeStruct((dp, dp), F32),
                   jax.ShapeDtypeStruct((1, dp), F32), jax.ShapeDtypeStruct((CONF_TAPS, dc), F32),
                   jax.ShapeDtypeStruct((1, dc), F32), jax.ShapeDtypeStruct((1, dc), F32),
                   jax.ShapeDtypeStruct((1, dc), F32), jax.ShapeDtypeStruct((SHORT_TAPS, ds), F32)],
        scratch_shapes=[pltpu.VMEM((hp + tt, dp), F32), pltpu.VMEM((tt + hp, dp), F32), pltpu.VMEM((tt, dp), F32),
                        pltpu.VMEM((tt, dp), BF16), pltpu.VMEM((hc + tt, dc), F32), pltpu.VMEM((tt + hc, dc), F32),
                        pltpu.VMEM((hs + tt, ds), F32), pltpu.VMEM((tt + hs, ds), F32),
                        pltpu.VMEM((CONF_TAPS, 8, dc), F32), pltpu.VMEM((SHORT_TAPS, 8, ds), F32),
                        pltpu.VMEM((3, 8, dc), F32), pltpu.VMEM((8, dp), F32)],
        compiler_params=_params("arbitrary"),
    )(dcat, dcat, z, z, z, z, c, c, mbd, pscale3, pmask, wlane, wdw, lng3, lnb3, wsc)


def _softmax_rows(qh, kh, scale):
    s = lax.dot_general(qh, kh, _DIMS["nt"], preferred_element_type=F32) * scale
    e = jnp.exp(s - jnp.max(s, axis=-1, keepdims=True))
    return e / jnp.sum(e, axis=-1, keepdims=True)


def _attn_fwd(q, k, v, name):
    t, d = q.shape
    m = k.shape[0]
    hd = d // HEADS
    scale = hd ** -0.5
    tt = min(TILE_ATTN, t)

    def body(q_ref, k_ref, v_ref, o_ref):
        for h in range(HEADS):
            sl = slice(h * hd, (h + 1) * hd)
            p = _softmax_rows(q_ref[:, sl], k_ref[:, sl], scale)
            o_ref[:, sl] = jnp.dot(p.astype(BF16), v_ref[:, sl], preferred_element_type=F32).astype(BF16)

    return pl.pallas_call(
        body, name=name, grid=(t // tt,),
        in_specs=[_rows(tt, d), _whole((m, d)), _whole((m, d))], out_specs=_rows(tt, d),
        out_shape=jax.ShapeDtypeStruct((t, d), BF16), compiler_params=_params("parallel"),
    )(q, k, v)


def _attn_bwd(q, k, v, do, name):
    t, d = q.shape
    m = k.shape[0]
    hd = d // HEADS
    scale = hd ** -0.5
    tt = min(TILE_ATTN, t)

    def body(q_ref, k_ref, v_ref, do_ref, dq_ref, dk_ref, dv_ref):
        @pl.when(pl.program_id(0) == 0)
        def _():
            dk_ref[...] = jnp.zeros_like(dk_ref)
            dv_ref[...] = jnp.zeros_like(dv_ref)

        for h in range(HEADS):
            sl = slice(h * hd, (h + 1) * hd)
            qh, kh, vh, doh = q_ref[:, sl], k_ref[:, sl], v_ref[:, sl], do_ref[:, sl]
            p = _softmax_rows(qh, kh, scale)
            dv_ref[:, sl] += lax.dot_general(p.astype(BF16), doh, _DIMS["tn"], preferred_element_type=F32)
            dp = lax.dot_general(doh, vh, _DIMS["nt"], preferred_element_type=F32)
            ds = (p * (dp - jnp.sum(dp * p, axis=-1, keepdims=True)) * scale).astype(BF16)
            dq_ref[:, sl] = jnp.dot(ds, kh, preferred_element_type=F32).astype(BF16)
            dk_ref[:, sl] += lax.dot_general(ds, qh, _DIMS["tn"], preferred_element_type=F32)

    return pl.pallas_call(
        body, name=name, grid=(t // tt,),
        in_specs=[_rows(tt, d), _whole((m, d)), _whole((m, d)), _rows(tt, d)],
        out_specs=[_rows(tt, d), _whole((m, d)), _whole((m, d))],
        out_shape=[jax.ShapeDtypeStruct((t, d), BF16), jax.ShapeDtypeStruct((m, d), F32),
                   jax.ShapeDtypeStruct((m, d), F32)],
        compiler_params=_params("arbitrary"),
    )(q, k, v, do)


def _lane_chunks(f):
    w = 256 if f % 256 == 0 else 128 if f % 128 == 0 else f
    return [(c0, w) for c0 in range(0, f, w)]


def _ffn_act_fwd(u, wc, l, name):
    t, f2 = u.shape
    f = f2 // 2
    tt = min(TILE_FFN, t)
    hs, rc = SHORT_HALO, ROW_CHUNK
    lanes = _lane_chunks(f)

    def body(u_ref, up_ref, wc_ref, a_ref, ubuf):
        ubuf[0:hs, :] = up_ref[...]

        @pl.when(pl.program_id(0) == 0)
        def _():
            ubuf[0:hs, :] = _shift_down(ubuf[0:hs, :])

        def step(r0):
            rows = pl.ds(r0, rc)
            ubuf[pl.ds(_al(hs + r0), rc), :] = u_ref[rows, :]
            start = r0 + hs - 8 * (SHORT_TAPS - 1)
            for c0, cw in lanes:
                g = _taps(wc_ref, ubuf, start, SHORT_TAPS, rc, slice(c0, c0 + cw))
                vv = _taps(wc_ref, ubuf, start, SHORT_TAPS, rc, slice(f + c0, f + c0 + cw))
                a_ref[rows, c0:c0 + cw] = (g * _sig(g) * vv).astype(BF16)
        _chunks(tt, step)

    return pl.pallas_call(
        body, name=name, grid=(t // tt,),
        in_specs=[_rows(tt, f2), _prev_block(hs, f2, tt, t), _layer((SHORT_TAPS, f2), l)],
        out_specs=_rows(tt, f), out_shape=jax.ShapeDtypeStruct((t, f), BF16),
        scratch_shapes=[pltpu.VMEM((hs + tt, f2), F32)], compiler_params=_params("parallel"),
    )(u, u, wc)


def _ffn_act_bwd(u, da, wc, l, name):
    t, f2 = u.shape
    f = f2 // 2
    tt = min(TILE_FFN, t)
    nt = t // tt
    hs, rc = SHORT_HALO, ROW_CHUNK
    lanes = _lane_chunks(f)

    def body(u_ref, up_ref, un_ref, da_ref, dan_ref, wc_ref, du_ref, dwc_ref, ubuf, danbuf, dbuf, dw8):
        i = pl.program_id(0)
        first, last = i == 0, i == nt - 1
        ubuf[0:hs, :] = up_ref[...]
        ubuf[hs + tt:hs + tt + hs, :] = un_ref[...]
        danbuf[...] = dan_ref[...]

        @pl.when(first)
        def _():
            dw8[...] = jnp.zeros_like(dw8)
            ubuf[0:hs, :] = _shift_down(ubuf[0:hs, :])

        @pl.when(last)
        def _():
            ubuf[hs + tt:hs + tt + hs, :] = _shift_up(ubuf[hs + tt:hs + tt + hs, :])
            danbuf[...] = _shift_up(danbuf[...])

        def fill(r0):
            ubuf[pl.ds(_al(hs + r0), rc), :] = u_ref[pl.ds(r0, rc), :]
        _chunks(tt, fill)

        def conv_grads(r0, n, da_rows):
            start = r0 + hs - 8 * (SHORT_TAPS - 1)
            for c0, cw in lanes:
                sl_g, sl_v = slice(c0, c0 + cw), slice(f + c0, f + c0 + cw)
                g = _taps(wc_ref, ubuf, start, SHORT_TAPS, n, sl_g)
                vv = _taps(wc_ref, ubuf, start, SHORT_TAPS, n, sl_v)
                dav = da_rows(c0, cw)
                sg = _sig(g)
                dbuf[pl.ds(_al(r0), n), sl_g] = dav * vv * (sg * (1.0 + g * (1.0 - sg)))
                dbuf[pl.ds(_al(r0), n), sl_v] = dav * (g * sg)

        _chunks(tt, lambda r0: conv_grads(r0, rc, lambda c0, cw: da_ref[pl.ds(r0, rc), c0:c0 + cw]))
        conv_grads(tt, hs, lambda c0, cw: danbuf[:, c0:c0 + cw])

        def back(r0):
            rows = pl.ds(r0, rc)
            for c0, cw in lanes:
                for off in (c0, f + c0):
                    sl = slice(off, off + cw)
                    du_ref[rows, sl] = _taps_rev(wc_ref, dbuf, r0, SHORT_TAPS, rc, sl).astype(BF16)
                    dd = dbuf[rows, sl]
                    for k in range(SHORT_TAPS):
                        dw8[k, :, sl] += _fold8(dd * ubuf[pl.ds(_al(r0 + hs - 8 * (SHORT_TAPS - 1 - k)), rc), sl])
        _chunks(tt, back)

        @pl.when(last)
        def _():
            dwc_ref[...] = jnp.sum(dw8[...], axis=1)

    return pl.pallas_call(
        body, name=name, grid=(nt,),
        in_specs=[_rows(tt, f2), _prev_block(hs, f2, tt, t), _next_block(hs, f2, tt, t),
                  _rows(tt, f), _next_block(hs, f, tt, t), _layer((SHORT_TAPS, f2), l)],
        out_specs=[_rows(tt, f2), _whole((SHORT_TAPS, f2))],
        out_shape=[jax.ShapeDtypeStruct((t, f2), BF16), jax.ShapeDtypeStruct((SHORT_TAPS, f2), F32)],
        scratch_shapes=[pltpu.VMEM((hs + tt + hs, f2), F32), pltpu.VMEM((hs, f), F32), pltpu.VMEM((tt + hs, f2), F32),
                        pltpu.VMEM((SHORT_TAPS, 8, f2), F32)],
        compiler_params=_params("arbitrary"),
    )(u, u, u, da, da, wc)


def _place():
    return lax.axis_index("x"), lax.axis_index("y"), lax.axis_index("c")


def _flip(v, bit):
    return 1 - v if bit else v


def _peers(x, y, c):
    out = []
    for kk in range(1, N_DEV):
        px, py, pc = _flip(x, kk & 4), _flip(y, kk & 2), _flip(c, kk & 1)
        out.append((kk - 1, (px, py, pc), 4 * px + 2 * py + pc))
    return out


def _allgather(shards, name):
    nt = len(shards)

    def body(*refs):
        srcs, outs = refs[:nt], refs[nt:2 * nt]
        send_sems, recv_sems, local_sems = refs[2 * nt:]
        x, y, c = _place()
        me, sibling = (x, y, c), (x, y, 1 - c)
        chips = [(1 - x, y), (x, 1 - y), (1 - x, 1 - y)]

        def rows(ti, px, py, pc):
            r = srcs[ti].shape[1]
            return outs[ti].at[:, pl.ds((4 * px + 2 * py + pc) * r, r), :]

        def copy(ti, kk, block, to, src=None):
            return pltpu.make_async_remote_copy(
                src_ref=rows(ti, *block) if src is None else src, dst_ref=rows(ti, *block),
                send_sem=send_sems.at[ti, kk], recv_sem=recv_sems.at[ti, kk], device_id=to, device_id_type=MESH)

        mine = [pltpu.make_async_copy(srcs[ti], rows(ti, *me), local_sems.at[ti]) for ti in range(nt)]
        for cp in mine:
            cp.start()
        first = []
        for ti in range(nt):
            first.append(copy(ti, 0, me, sibling, src=srcs[ti]))
            first += [copy(ti, 1 + j, me, (*chip, c), src=srcs[ti]) for j, chip in enumerate(chips)]
        for cp in first:
            cp.start()
        passed = []
        for j, chip in enumerate(chips):
            for ti in range(nt):
                copy(ti, 1 + j, (*chip, c), me).wait_recv()
                fwd = copy(ti, 4 + j, (*chip, c), sibling)
                fwd.start()
                passed.append(fwd)
        for ti in range(nt):
            copy(ti, 0, sibling, me).wait_recv()
            for j, chip in enumerate(chips):
                copy(ti, 4 + j, (*chip, 1 - c), me).wait_recv()
        for cp in first + passed:
            cp.wait_send()
        for cp in mine:
            cp.wait()

    return pl.pallas_call(
        body, name=name,
        in_specs=[ANY] * nt, out_specs=[ANY] * nt,
        out_shape=[jax.ShapeDtypeStruct((s.shape[0], N_DEV * s.shape[1], s.shape[2]), s.dtype) for s in shards],
        scratch_shapes=[pltpu.SemaphoreType.DMA((nt, 7)), pltpu.SemaphoreType.DMA((nt, 7)),
                        pltpu.SemaphoreType.DMA((nt,))],
    )(*shards)


def _gather_piece(src, land, me, to):
    r = src.shape[0]
    return src, land.at[pl.ds(me * r, r), :]


def _scatter_piece(l):
    def piece(src, land, me, to):
        r = src.shape[0] // N_DEV
        return src.at[pl.ds(to * r, r), :], land.at[l, me]
    return piece


def _split_start(srcs, lands, piece, after, name):
    nt = len(srcs)

    def body(*refs):
        src_refs, land_refs = refs[:nt], refs[nt:2 * nt]
        send_sems, recv_sems, local_sems, token = refs[2 * nt + 1], refs[2 * nt + 2], refs[2 * nt + 3], refs[4 * nt + 4]
        x, y, c = _place()
        me = 4 * x + 2 * y + c
        for ti in range(nt):
            for slot, peer, flat in _peers(x, y, c):
                src, dst = piece(src_refs[ti], land_refs[ti], me, flat)
                pltpu.make_async_remote_copy(
                    src_ref=src, dst_ref=dst, send_sem=send_sems.at[7 * ti + slot], recv_sem=recv_sems.at[7 * ti + slot],
                    device_id=peer, device_id_type=MESH).start()
        for ti in range(nt):
            pltpu.make_async_copy(*piece(src_refs[ti], land_refs[ti], me, me), local_sems.at[ti]).start()
        token[...] = jnp.zeros_like(token)

    both = list(srcs) + list(lands)
    return pl.pallas_call(
        body, name=name,
        in_specs=[HBM] * (2 * nt) + [ANY],
        out_specs=[SEM, SEM, SEM] + [HBM] * (2 * nt) + [pl.BlockSpec(memory_space=pltpu.VMEM)],
        out_shape=[pltpu.SemaphoreType.DMA((7 * nt,)), pltpu.SemaphoreType.DMA((7 * nt,)), pltpu.SemaphoreType.DMA((nt,))]
        + [pltpu.HBM(a.shape, a.dtype) for a in both] + [jax.ShapeDtypeStruct((8, 128), F32)],
        input_output_aliases={i: i + 3 for i in range(2 * nt)},
        compiler_params=pltpu.CompilerParams(has_side_effects=EFFECT),
    )(*[pltpu.with_memory_space_constraint(a, pltpu.HBM) for a in both], after)


def _split_wait(started, after, piece, name):
    send_sems, recv_sems, local_sems, *both = started[:-1]
    nt = len(both) // 2

    def body(*refs):
        src_refs, land_refs = refs[:nt], refs[nt:2 * nt]
        send_ref, recv_ref, local_ref = refs[2 * nt], refs[2 * nt + 1], refs[2 * nt + 2]
        x, y, c = _place()
        me = 4 * x + 2 * y + c
        for ti in range(nt):
            src, dst = piece(src_refs[ti], land_refs[ti], me, me)
            for slot in range(N_DEV - 1):
                cp = pltpu.make_async_remote_copy(
                    src_ref=src, dst_ref=dst, send_sem=send_ref.at[7 * ti + slot], recv_sem=recv_ref.at[7 * ti + slot],
                    device_id=(x, y, c), device_id_type=MESH)
                cp.wait_send()
                cp.wait_recv()
            pltpu.make_async_copy(src, dst, local_ref.at[ti]).wait()

    outs = pl.pallas_call(
        body, name=name,
        in_specs=[HBM] * (2 * nt) + [SEM, SEM, SEM, ANY], out_specs=[HBM] * (2 * nt),
        out_shape=[pltpu.HBM(a.shape, a.dtype) for a in both],
        input_output_aliases={i: i for i in range(2 * nt)},
        compiler_params=pltpu.CompilerParams(has_side_effects=EFFECT),
    )(*both, send_sems, recv_sems, local_sems, after)
    return outs[nt:]


def _adam(w, g, m, v):
    m2 = ADAM_B1 * m + (1.0 - ADAM_B1) * g
    v2 = ADAM_B2 * v + (1.0 - ADAM_B2) * (g * g)
    m_hat = m2 / (1.0 - ADAM_B1 ** ADAM_STEP)
    v_hat = v2 / (1.0 - ADAM_B2 ** ADAM_STEP)
    return -ADAM_LR * (m_hat / (jnp.sqrt(v_hat) + ADAM_EPS) + ADAM_WD * w), m2, v2


def _adam_sharded(recv, recv_first, w, m, v, lo, hi, name, prev=None):
    nl, r, c = w.shape
    tr = max([rows for rows in range(16, min(r, TILE_ADAM) + 1, 16) if r % rows == 0] or [r])

    def body(recv_ref, w_ref, m_ref, v_ref, *rest):
        g_ref, d_ref, m2_ref, v2_ref = rest[-4:]
        g = recv_ref[0].astype(F32)
        for s in range(1, N_DEV):
            g = g + recv_ref[s].astype(F32)
        g_ref[...] = g
        d_ref[...], m2_ref[...], v2_ref[...] = _adam(w_ref[...], g, m_ref[...], v_ref[...])

    blk = pl.BlockSpec((None, tr, c), lambda li, i: (li + lo, i, 0))
    extra = [] if prev is None else list(prev)
    return pl.pallas_call(
        body, name=name, grid=(hi - lo, r // tr),
        in_specs=[pl.BlockSpec((None, N_DEV, tr, c), lambda li, i: (li + lo - recv_first, 0, i, 0)), blk, blk, blk]
        + [ANY] * len(extra),
        out_specs=[blk] * 4, out_shape=[jax.ShapeDtypeStruct((nl, r, c), F32)] * 4,
        input_output_aliases={4 + i: i for i in range(len(extra))},
        compiler_params=_params("parallel", "parallel"),
    )(recv, w, m, v, *extra)


def _sum_sources(parts, name):
    _, r, c = parts.shape

    def body(p_ref, o_ref):
        g = p_ref[0]
        for s in range(1, N_DEV):
            g = g + p_ref[s]
        o_ref[...] = g

    return pl.pallas_call(
        body, name=name, grid=(1,), in_specs=[_whole((N_DEV, r, c))], out_specs=_whole((r, c)),
        out_shape=jax.ShapeDtypeStruct((r, c), F32), compiler_params=_params("arbitrary"),
    )(parts)


def _adam_flat(w, g, m, v, name):
    r, c = w.shape

    def body(w_ref, g_ref, m_ref, v_ref, d_ref, m2_ref, v2_ref):
        d_ref[...], m2_ref[...], v2_ref[...] = _adam(w_ref[...], g_ref[...], m_ref[...], v_ref[...])

    return pl.pallas_call(
        body, name=name, grid=(1,), in_specs=[_whole((r, c))] * 4, out_specs=[_whole((r, c))] * 3,
        out_shape=[jax.ShapeDtypeStruct((r, c), F32)] * 3, compiler_params=_params("arbitrary"),
    )(w, g, m, v)


def _pack(arrays):
    flat = jnp.concatenate([a.reshape(-1).astype(F32) for a in arrays])
    rows = -(-flat.shape[0] // 1024) * 8
    return jnp.pad(flat, (0, rows * 128 - flat.shape[0])).reshape(rows, 128)


def _unpack(slab, like):
    flat = slab.reshape(-1)
    out, at = [], 0
    for a in like:
        out.append(flat[at:at + a.size].reshape(a.shape))
        at += a.size
    return out


def kernel(x, mem, mem_norm, mix_pre_norm, mix_post_norm, w_in, pool_maps, pool_scale, conf_dw_w, conf_dw_b, conf_ln_g, conf_ln_b, sconv_w, w_out, xattn_pre_norm, xattn_post_norm, xattn_wq, xattn_wk, xattn_wv, xattn_wo, ffn_pre_norm, ffn_post_norm, ffn_w_up, ffn_conv_w, ffn_w_down, loss_target, m_mem_norm, m_mix_pre_norm, m_mix_post_norm, m_w_in, m_pool_maps, m_pool_scale, m_conf_dw_w, m_conf_dw_b, m_conf_ln_g, m_conf_ln_b, m_sconv_w, m_w_out, m_xattn_pre_norm, m_xattn_post_norm, m_xattn_wq, m_xattn_wk, m_xattn_wv, m_xattn_wo, m_ffn_pre_norm, m_ffn_post_norm, m_ffn_w_up, m_ffn_conv_w, m_ffn_w_down, v_mem_norm, v_mix_pre_norm, v_mix_post_norm, v_w_in, v_pool_maps, v_pool_scale, v_conf_dw_w, v_conf_dw_b, v_conf_ln_g, v_conf_ln_b, v_sconv_w, v_w_out, v_xattn_pre_norm, v_xattn_post_norm, v_xattn_wq, v_xattn_wk, v_xattn_wv, v_xattn_wo, v_ffn_pre_norm, v_ffn_post_norm, v_ffn_w_up, v_ffn_conv_w, v_ffn_w_down):
    weights = dict(mem_norm=mem_norm, mix_pre_norm=mix_pre_norm, mix_post_norm=mix_post_norm, w_in=w_in, pool_maps=pool_maps, pool_scale=pool_scale, conf_dw_w=conf_dw_w, conf_dw_b=conf_dw_b, conf_ln_g=conf_ln_g, conf_ln_b=conf_ln_b, sconv_w=sconv_w, w_out=w_out, xattn_pre_norm=xattn_pre_norm, xattn_post_norm=xattn_post_norm, xattn_wq=xattn_wq, xattn_wk=xattn_wk, xattn_wv=xattn_wv, xattn_wo=xattn_wo, ffn_pre_norm=ffn_pre_norm, ffn_post_norm=ffn_post_norm, ffn_w_up=ffn_w_up, ffn_conv_w=ffn_conv_w, ffn_w_down=ffn_w_down)
    mom1 = dict(mem_norm=m_mem_norm, mix_pre_norm=m_mix_pre_norm, mix_post_norm=m_mix_post_norm, w_in=m_w_in, pool_maps=m_pool_maps, pool_scale=m_pool_scale, conf_dw_w=m_conf_dw_w, conf_dw_b=m_conf_dw_b, conf_ln_g=m_conf_ln_g, conf_ln_b=m_conf_ln_b, sconv_w=m_sconv_w, w_out=m_w_out, xattn_pre_norm=m_xattn_pre_norm, xattn_post_norm=m_xattn_post_norm, xattn_wq=m_xattn_wq, xattn_wk=m_xattn_wk, xattn_wv=m_xattn_wv, xattn_wo=m_xattn_wo, ffn_pre_norm=m_ffn_pre_norm, ffn_post_norm=m_ffn_post_norm, ffn_w_up=m_ffn_w_up, ffn_conv_w=m_ffn_conv_w, ffn_w_down=m_ffn_w_down)
    mom2 = dict(mem_norm=v_mem_norm, mix_pre_norm=v_mix_pre_norm, mix_post_norm=v_mix_post_norm, w_in=v_w_in, pool_maps=v_pool_maps, pool_scale=v_pool_scale, conf_dw_w=v_conf_dw_w, conf_dw_b=v_conf_dw_b, conf_ln_g=v_conf_ln_g, conf_ln_b=v_conf_ln_b, sconv_w=v_sconv_w, w_out=v_w_out, xattn_pre_norm=v_xattn_pre_norm, xattn_post_norm=v_xattn_post_norm, xattn_wq=v_xattn_wq, xattn_wk=v_xattn_wk, xattn_wv=v_xattn_wv, xattn_wo=v_xattn_wo, ffn_pre_norm=v_ffn_pre_norm, ffn_post_norm=v_ffn_post_norm, ffn_w_up=v_ffn_w_up, ffn_conv_w=v_ffn_conv_w, ffn_w_down=v_ffn_w_down)
    names = list(weights)

    nl, d = mix_pre_norm.shape
    x0, mem0, target = _to_steps(x[0]), mem[0], _to_steps(loss_target[0])
    dp, dc, ds, *_ = _mix_dims(d)
    pg = dp // len(POOL_WINDOWS)
    me = 4 * lax.axis_index("x") + 2 * lax.axis_index("y") + lax.axis_index("c")

    big = ["w_in", "w_out", "xattn_wq", "xattn_wk", "xattn_wv", "xattn_wo", "ffn_w_up", "ffn_w_down"]
    transposed = ("w_in", "ffn_w_up")

    def row_shard(n, a):
        return a.transpose(0, 2, 1) if n in transposed else a

    shards = [row_shard(n, weights[n]).astype(BF16) for n in big]
    taps = ["conf_dw_w", "sconv_w", "ffn_conv_w"]
    tap_slab = _pack([weights[n] for n in taps])
    *first_w, tap_all = _allgather([s[0:1] for s in shards] + [tap_slab[None]], "gather_weights0")
    layer_w = [dict(zip(big, [a[0] for a in first_w]))]
    tap_all = tap_all[0].reshape(N_DEV, *tap_slab.shape)
    tap_parts = [_unpack(tap_all[p], [weights[n] for n in taps]) for p in range(N_DEV)]
    wdw, wsc, wcf = (jnp.concatenate([tap_parts[p][i] for p in range(N_DEV)], axis=-1) for i in range(3))

    def g3(a):
        return a.reshape(a.shape[0], 1, a.shape[-1])

    mbd = jnp.zeros((nl, dp, dp), F32)
    for gi in range(len(POOL_WINDOWS)):
        mbd = mbd.at[:, gi * pg:(gi + 1) * pg, gi * pg:(gi + 1) * pg].set(pool_maps[:, gi])
    mbd = mbd.astype(BF16)
    pre1, post1, pre2, post2, pre3, post3 = (g3(weights[n]) for n in (
        "mix_pre_norm", "mix_post_norm", "xattn_pre_norm", "xattn_post_norm", "ffn_pre_norm", "ffn_post_norm"))
    pscale3, bdw3, lng3, lnb3 = g3(pool_scale), g3(conf_dw_b), g3(conf_ln_g), g3(conf_ln_b)
    memg3 = mem_norm.reshape(1, 1, d)

    def mm(a, b, mode, dt, name, tm=2048, tn=1024, tk=1024, a_outer=True):
        return _matmul(a, b, mode, dt, name, tm=tm, tn=tn, tk=tk, a_outer=a_outer)

    mem_n = _prenorm(mem0, memg3, 0, "mem_norm")
    xs = x0
    h = _prenorm(xs, pre1, 0, "pre_norm0")
    saved = []
    for l in range(nl):
        ps_l = pscale3
        if l + 1 < nl:
            lands = [lax.empty((N_DEV * s.shape[1], s.shape[2]), BF16) for s in shards]
            flying = _split_start([s[l + 1] for s in shards], lands, _gather_piece,
                                  xs if l else layer_w[0]["w_in"], f"gather_start{l + 1}")
            ps_l = pscale3 + flying[-1][0, 0]
        w = layer_w[l]
        s = {"x": xs, "h": h}
        s["z"] = mm(h, w["w_in"], "nt", F32, f"z{l}", tm=1024, tn=4096)
        s["cat"], s["c"] = _mix_fwd(s["z"], mbd, ps_l, wdw, bdw3, lng3, lnb3, wsc, l, f"mix_fwd{l}")
        s["y1"], s["x1"], s["h1"] = _matmul_resnorm(s["cat"], w["w_out"], xs, post1, l, pre2, l, f"y1_{l}")
        s["q"] = mm(s["h1"], w["xattn_wq"], "nn", BF16, f"q{l}")
        s["k"] = mm(mem_n, w["xattn_wk"], "nn", BF16, f"k{l}")
        s["v"] = mm(mem_n, w["xattn_wv"], "nn", BF16, f"v{l}")
        s["o"] = _attn_fwd(s["q"], s["k"], s["v"], f"attn_fwd{l}")
        s["y2"], s["x2"], s["h2"] = _matmul_resnorm(s["o"], w["xattn_wo"], s["x1"], post2, l, pre3, l, f"y2_{l}")
        s["u"] = mm(s["h2"], w["ffn_w_up"], "nt", F32, f"u{l}", tn=1408, a_outer=False)
        s["a"] = _ffn_act_fwd(s["u"], wcf, l, f"ffn_act{l}")
        if l + 1 == nl:
            s["y3"] = mm(s["a"], w["ffn_w_down"], "nn", F32, f"y3_{l}", tm=1024, tk=4096)
        else:
            s["y3"], xs, h = _matmul_resnorm(s["a"], w["ffn_w_down"], s["x2"], post3, l, pre1, l + 1, f"y3_{l}")
            layer_w.append(dict(zip(big, _split_wait(flying, xs, _gather_piece, f"gather_wait{l + 1}"))))
        saved.append(s)

    last = saved[-1]
    dxn, dy3, dg_post3, loss_lanes = _loss_head(last["x2"], last["y3"], post3, nl - 1, target, "loss_head")
    loss = lax.psum(loss_lanes[0, 0], ("x", "y", "c"))

    recvs = [lax.empty((max(nl - 1, 1), N_DEV, s.shape[1], d), BF16) for s in shards]
    recv0 = [lax.empty((1, N_DEV, s.shape[1], d), BF16) for s in shards]
    small = {n: [None] * nl for n in names if n not in big and n != "mem_norm"}
    small["ffn_post_norm"][nl - 1] = dg_post3
    dmem_n = jnp.zeros(mem0.shape, F32)
    flying = None
    for l in reversed(range(nl)):
        s, w = saved[l], layer_w[l]
        wc_l = wcf if flying is None else wcf + flying[-1][0, 0]
        gw = {}
        da = mm(dy3, w["ffn_w_down"], "nt", F32, f"da{l}", tn=1408, a_outer=False)
        gw["ffn_w_down"] = mm(s["a"], dy3, "tn", BF16, f"dw_down{l}", tm=1408, tk=2048)
        du, small["ffn_conv_w"][l] = _ffn_act_bwd(s["u"], da, wc_l, l, f"ffn_act_bwd{l}")
        gw["ffn_w_up"] = mm(du, s["h2"], "tn", BF16, f"dw_up{l}", tm=1408, tk=2048)
        dx2, small["ffn_pre_norm"][l], dy2, small["xattn_post_norm"][l] = _matmul_norm_bwd(
            du, w["ffn_w_up"], "nn", dxn, s["x2"], pre3, l, f"dh2_{l}", s["y2"], post2, l, tm=256)
        do = mm(dy2, w["xattn_wo"], "nt", BF16, f"do{l}")
        gw["xattn_wo"] = mm(s["o"], dy2, "tn", BF16, f"dw_o{l}", tk=4096)
        dq, dk, dv = _attn_bwd(s["q"], s["k"], s["v"], do, f"attn_bwd{l}")
        dkb, dvb = dk.astype(BF16), dv.astype(BF16)
        gw["xattn_wq"] = mm(s["h1"], dq, "tn", BF16, f"dw_q{l}", tk=4096)
        gw["xattn_wk"] = mm(mem_n, dkb, "tn", BF16, f"dw_k{l}")
        gw["xattn_wv"] = mm(mem_n, dvb, "tn", BF16, f"dw_v{l}")
        dmem_n = dmem_n + mm(dkb, w["xattn_wk"], "nt", F32, f"dmem_k{l}") \
            + mm(dvb, w["xattn_wv"], "nt", F32, f"dmem_v{l}")
        pre2_l = pre2
        if l == 0:
            flying0 = _split_start([gw[n] for n in big[2:]], recv0[2:], _scatter_piece(0), dmem_n, "scatter_start0a")
            pre2_l = pre2 + flying0[-1][0, 0]
        dx1, small["xattn_pre_norm"][l], dy1, small["mix_post_norm"][l] = _matmul_norm_bwd(
            dq, w["xattn_wq"], "nt", dx2, s["x1"], pre2_l, l, f"dh1_{l}", s["y1"], post1, l)
        dcat = mm(dy1, w["w_out"], "nt", F32, f"dcat{l}")
        gw["w_out"] = mm(s["cat"], dy1, "tn", BF16, f"dw_out{l}", tk=4096)
        dz, dmbd, dps, dwdw, dbdw, dlng, dlnb, dwsc = _mix_bwd(
            dcat, s["z"], s["c"], mbd, pscale3, wdw, lng3, lnb3, wsc, l, f"mix_bwd{l}")
        small["pool_maps"][l] = jnp.stack([dmbd[gi * pg:(gi + 1) * pg, gi * pg:(gi + 1) * pg]
                                           for gi in range(len(POOL_WINDOWS))])
        small["pool_scale"][l], small["conf_dw_w"][l], small["conf_dw_b"][l] = dps, dwdw, dbdw
        small["conf_ln_g"][l], small["conf_ln_b"][l], small["sconv_w"][l] = dlng, dlnb, dwsc
        gw["w_in"] = mm(dz, s["h"], "tn", BF16, f"dw_in{l}", tm=4096, tk=2048)
        if l > 0:
            dxn, small["mix_pre_norm"][l], dy3, small["ffn_post_norm"][l - 1] = _matmul_norm_bwd(
                dz, w["w_in"], "nn", dx1, s["x"], pre1, l, f"dh{l}", saved[l - 1]["y3"], post3, l - 1)
        else:
            dxn, small["mix_pre_norm"][l] = _matmul_norm_bwd(dz, w["w_in"], "nn", dx1, s["x"], pre1, l, "dh0")
        if flying is not None:
            recvs = _split_wait(flying, dxn, _scatter_piece(l), f"scatter_wait{l + 1}")
        if l:
            flying = _split_start([gw[n] for n in big], recvs, _scatter_piece(l - 1), dxn, f"scatter_start{l}")
        else:
            flying = _split_start([gw[n] for n in big[:2]], recv0[:2], _scatter_piece(0), dxn, "scatter_start0b")
    grad_x = _from_steps(dxn)[None]
    _, dg_mem = _norm_bwd(jnp.zeros(mem0.shape, F32), dmem_n, mem0, memg3, 0, "norm_bwd_mem")

    small_names = [n for n in names if n not in big]
    partial = {n: (dg_mem.reshape(d) if n == "mem_norm" else
                   jnp.stack([g.reshape(g.shape[-1]) if g.shape[0] == 1 and weights[n].ndim == 2 else g
                              for g in small[n]])) for n in small_names}
    slab = _pack([partial[n] for n in small_names])
    gathered = _allgather([slab[None]], "gather_small_grads")[0][0].reshape(N_DEV, *slab.shape)
    summed = dict(zip(small_names, _unpack(_sum_sources(gathered, "sum_small_grads"), [partial[n] for n in small_names])))
    grad = {}
    for n in small_names:
        g = summed[n]
        if n in taps:
            width = weights[n].shape[-1]
            g = lax.dynamic_slice_in_dim(g, me * width, width, axis=g.ndim - 1)
        grad[n] = g

    delta, new_m, new_v = {}, {}, {}
    wmv = {n: [row_shard(n, a[n]) for a in (weights, mom1, mom2)] for n in big}
    upper = {n: _adam_sharded(recv, 1, *wmv[n], 1, nl, f"adam_{n}") for n, recv in zip(big, recvs)} if nl > 1 else {}
    upd = _adam_flat(_pack([weights[n] for n in small_names]), _pack([grad[n] for n in small_names]),
                     _pack([mom1[n] for n in small_names]), _pack([mom2[n] for n in small_names]), "adam_small")
    for out, slab_o in zip((delta, new_m, new_v), upd):
        out.update(zip(small_names, _unpack(slab_o, [weights[n] for n in small_names])))
    done = sum(r[3][0, 0, :1] for r in upper.values()) + upd[0][0, :1]
    recv0 = _split_wait(flying, done, _scatter_piece(0), "scatter_wait0b") \
        + _split_wait(flying0, done, _scatter_piece(0), "scatter_wait0a")
    for n, recv in zip(big, recv0):
        res = _adam_sharded(recv, 0, *wmv[n], 0, 1, f"adam0_{n}", prev=upper.get(n))
        grad[n], delta[n], new_m[n], new_v[n] = (row_shard(n, r) for r in res)

    return (loss, grad_x, *[grad[n] for n in names], *[delta[n] for n in names],
            *[new_m[n] for n in names], *[new_v[n] for n in names])
```

```python
import jax
import jax.numpy as jnp
from jax import lax
from jax.experimental import pallas as pl
from jax.experimental.pallas import tpu as pltpu

F32, BF16 = jnp.float32, jnp.bfloat16
EPS = 1e-6
POOL_WINDOWS = (2, 4, 8, 16)
MAX_WINDOW = 16
CONF_TAPS, SHORT_TAPS = 31, 3
CONF_HALO, POOL_HALO, SHORT_HALO = 256, 128, 16
ROW_CHUNK = 64
HEADS = 4
N_DEV = 8
ADAM_LR, ADAM_B1, ADAM_B2, ADAM_EPS, ADAM_WD, ADAM_STEP = 0.001, 0.9, 0.999, 1e-08, 0.01, 10
VMEM_LIMIT_V7X = 56 * 2**20
MESH = pl.DeviceIdType.MESH
ANY = pl.BlockSpec(memory_space=pl.ANY)
HBM = pl.BlockSpec(memory_space=pltpu.HBM)
SEM = pl.BlockSpec(memory_space=pltpu.SEMAPHORE)
EFFECT = pltpu.SideEffectType.DATAFLOW_SIDE_EFFECTING

TILE_NORM, TILE_MIX, TILE_FFN, TILE_ATTN, TILE_ADAM = 256, 512, 256, 512, 352


def _params(*sem):
    return pltpu.CompilerParams(dimension_semantics=sem, vmem_limit_bytes=VMEM_LIMIT_V7X)


def _sig(x):
    return 1.0 / (1.0 + jnp.exp(-x))


def _rms(x):
    r = lax.rsqrt(jnp.mean(x * x, axis=-1, keepdims=True) + EPS)
    return x * r, r


def _rms_bwd(dout, g, n, r):
    dn = dout * g
    return r * (dn - n * jnp.mean(dn * n, axis=-1, keepdims=True))


def _rows(tt, c):
    return pl.BlockSpec((tt, c), lambda i: (i, 0))


def _whole(shape):
    return pl.BlockSpec(shape, lambda i: (0,) * len(shape))


def _layer(shape, l):
    return pl.BlockSpec((None,) + shape, lambda i: (l,) + (0,) * len(shape))


def _colsum(x):
    return jnp.sum(x, axis=0, keepdims=True)


_DIMS = {"nn": (((1,), (0,)), ((), ())), "nt": (((1,), (1,)), ((), ())), "tn": (((0,), (0,)), ((), ()))}


def _matmul(a, b, mode, out_dtype, name, *, tm, tn, tk, a_outer=True):
    if mode == "nn":
        (m, k), (k2, n) = a.shape, b.shape
    elif mode == "nt":
        (m, k), (n, k2) = a.shape, b.shape
    else:
        (k, m), (k2, n) = a.shape, b.shape
    assert k == k2, (name, a.shape, b.shape)
    tm, tn, tk = min(tm, m), min(tn, n), min(tk, k)
    assert m % tm == 0 and n % tn == 0 and k % tk == 0, (name, m, n, k, tm, tn, tk)
    gm, gn, gk = m // tm, n // tn, k // tk

    def ij(g0, g1):
        return (g0, g1) if a_outer else (g1, g0)

    def a_map(g0, g1, kk):
        i, _ = ij(g0, g1)
        return (kk, i) if mode == "tn" else (i, kk)

    def b_map(g0, g1, kk):
        _, j = ij(g0, g1)
        return (j, kk) if mode == "nt" else (kk, j)

    def o_map(g0, g1, kk):
        return ij(g0, g1)

    a_block = (tk, tm) if mode == "tn" else (tm, tk)
    b_block = (tn, tk) if mode == "nt" else (tk, tn)
    dims = _DIMS[mode]

    def body(a_ref, b_ref, o_ref, *acc):
        p = lax.dot_general(a_ref[...].astype(BF16), b_ref[...].astype(BF16), dims, preferred_element_type=F32)
        if gk == 1:
            o_ref[...] = p.astype(o_ref.dtype)
        else:
            kk = pl.program_id(2)

            @pl.when(kk == 0)
            def _():
                acc[0][...] = p

            @pl.when(kk > 0)
            def _():
                acc[0][...] += p

            @pl.when(kk == gk - 1)
            def _():
                o_ref[...] = acc[0][...].astype(o_ref.dtype)

    return pl.pallas_call(
        body, name=name, grid=(gm, gn, gk) if a_outer else (gn, gm, gk),
        in_specs=[pl.BlockSpec(a_block, a_map), pl.BlockSpec(b_block, b_map)],
        out_specs=pl.BlockSpec((tm, tn), o_map),
        out_shape=jax.ShapeDtypeStruct((m, n), out_dtype),
        scratch_shapes=[pltpu.VMEM((tm, tn), F32)] if gk > 1 else [],
        compiler_params=_params("parallel", "parallel", "arbitrary"),
    )(a, b)


def _prenorm(x, g3, l, name):
    t, d = x.shape
    tt = min(TILE_NORM, t)

    def body(x_ref, g_ref, h_ref):
        n, _ = _rms(x_ref[...])
        h_ref[...] = (n * g_ref[...]).astype(BF16)

    return pl.pallas_call(
        body, name=name, grid=(t // tt,),
        in_specs=[_rows(tt, d), _layer((1, d), l)], out_specs=_rows(tt, d),
        out_shape=jax.ShapeDtypeStruct((t, d), BF16), compiler_params=_params("parallel"),
    )(x, g3)


def _resnorm(x, y, gpost3, l, gnext3, l2, name):
    t, d = x.shape
    tt = min(TILE_NORM, t)

    def body(x_ref, y_ref, gp_ref, gn_ref, xo_ref, h_ref):
        n, _ = _rms(y_ref[...])
        xn = x_ref[...] + n * gp_ref[...]
        xo_ref[...] = xn
        n2, _ = _rms(xn)
        h_ref[...] = (n2 * gn_ref[...]).astype(BF16)

    return pl.pallas_call(
        body, name=name, grid=(t // tt,),
        in_specs=[_rows(tt, d), _rows(tt, d), _layer((1, d), l), _layer((1, d), l2)],
        out_specs=[_rows(tt, d), _rows(tt, d)],
        out_shape=[jax.ShapeDtypeStruct((t, d), F32), jax.ShapeDtypeStruct((t, d), BF16)],
        compiler_params=_params("parallel"),
    )(x, y, gpost3, gnext3)


def _loss_head(x, y, gpost3, l, target, name):
    t, d = x.shape
    tt = min(TILE_NORM, t)

    def body(x_ref, y_ref, g_ref, t_ref, dxn_ref, dy_ref, dg_ref, loss_ref):
        @pl.when(pl.program_id(0) == 0)
        def _():
            dg_ref[...] = jnp.zeros_like(dg_ref)
            loss_ref[...] = jnp.zeros_like(loss_ref)

        g = g_ref[...]
        n, r = _rms(y_ref[...])
        diff = x_ref[...] + n * g - t_ref[...]
        loss_ref[...] += 0.5 * jnp.sum(jnp.mean(diff * diff, axis=-1, keepdims=True))
        dxn = diff * (1.0 / d)
        dxn_ref[...] = dxn
        dy_ref[...] = _rms_bwd(dxn, g, n, r).astype(BF16)
        dg_ref[...] += _colsum(dxn * n)

    return pl.pallas_call(
        body, name=name, grid=(t // tt,),
        in_specs=[_rows(tt, d), _rows(tt, d), _layer((1, d), l), _rows(tt, d)],
        out_specs=[_rows(tt, d), _rows(tt, d), _whole((1, d)), _whole((1, 128))],
        out_shape=[jax.ShapeDtypeStruct((t, d), F32), jax.ShapeDtypeStruct((t, d), BF16),
                   jax.ShapeDtypeStruct((1, d), F32), jax.ShapeDtypeStruct((1, 128), F32)],
        compiler_params=_params("arbitrary"),
    )(x, y, gpost3, target)


def _norm_bwd(dxn, dh, x_in, gpre3, l, name, y_prev=None, gpost3=None, l_prev=None):
    t, d = x_in.shape
    tt = min(TILE_NORM, t)
    has_prev = y_prev is not None

    def body(*refs):
        if has_prev:
            dxn_ref, dh_ref, x_ref, g_ref, y_ref, g2_ref, dx_ref, dg_ref, dy_ref, dg2_ref = refs
        else:
            dxn_ref, dh_ref, x_ref, g_ref, dx_ref, dg_ref = refs

        @pl.when(pl.program_id(0) == 0)
        def _():
            dg_ref[...] = jnp.zeros_like(dg_ref)
            if has_prev:
                dg2_ref[...] = jnp.zeros_like(dg2_ref)

        dh_v = dh_ref[...]
        n, r = _rms(x_ref[...])
        dx = dxn_ref[...] + _rms_bwd(dh_v, g_ref[...], n, r)
        dx_ref[...] = dx
        dg_ref[...] += _colsum(dh_v * n)
        if has_prev:
            n2, r2 = _rms(y_ref[...])
            dy_ref[...] = _rms_bwd(dx, g2_ref[...], n2, r2).astype(BF16)
            dg2_ref[...] += _colsum(dx * n2)

    in_specs = [_rows(tt, d), _rows(tt, d), _rows(tt, d), _layer((1, d), l)]
    out_specs = [_rows(tt, d), _whole((1, d))]
    out_shape = [jax.ShapeDtypeStruct((t, d), F32), jax.ShapeDtypeStruct((1, d), F32)]
    args = [dxn, dh, x_in, gpre3]
    if has_prev:
        in_specs += [_rows(tt, d), _layer((1, d), l_prev)]
        out_specs += [_rows(tt, d), _whole((1, d))]
        out_shape += [jax.ShapeDtypeStruct((t, d), BF16), jax.ShapeDtypeStruct((1, d), F32)]
        args += [y_prev, gpost3]
    return pl.pallas_call(
        body, name=name, grid=(t // tt,), in_specs=in_specs, out_specs=out_specs, out_shape=out_shape,
        compiler_params=_params("arbitrary"),
    )(*args)


def _halves(tt):
    return [slice(0, tt // 2), slice(tt // 2, tt)] if tt % 32 == 0 else [slice(0, tt)]


def _matmul_resnorm(a, w, x, gpost3, l, gnext3, l2, name, tm=512):
    t, k = a.shape
    d = w.shape[1]
    tm = min(tm, t)

    def body(a_ref, w_ref, x_ref, gp_ref, gn_ref, y_ref, xo_ref, h_ref):
        wv, gp, gn = w_ref[...], gp_ref[...], gn_ref[...]
        for rows in _halves(tm):
            y = jnp.dot(a_ref[rows, :], wv, preferred_element_type=F32)
            y_ref[rows, :] = y
            n, _ = _rms(y)
            xn = x_ref[rows, :] + n * gp
            xo_ref[rows, :] = xn
            n2, _ = _rms(xn)
            h_ref[rows, :] = (n2 * gn).astype(BF16)

    return pl.pallas_call(
        body, name=name, grid=(t // tm,),
        in_specs=[_rows(tm, k), _whole((k, d)), _rows(tm, d), _layer((1, d), l), _layer((1, d), l2)],
        out_specs=[_rows(tm, d)] * 3,
        out_shape=[jax.ShapeDtypeStruct((t, d), F32), jax.ShapeDtypeStruct((t, d), F32),
                   jax.ShapeDtypeStruct((t, d), BF16)],
        compiler_params=_params("parallel"),
    )(a, w, x, gpost3, gnext3)


def _matmul_norm_bwd(a, w, mode, dxn, x_in, gpre3, l, name, y_prev=None, gpost3=None, l_prev=None, tm=512):
    t, k = a.shape
    d = x_in.shape[1]
    tm = min(tm, t)
    has_prev = y_prev is not None
    dims = _DIMS[mode]

    def body(*refs):
        if has_prev:
            a_ref, w_ref, dxn_ref, x_ref, g_ref, y_ref, g2_ref, dx_ref, dg_ref, dy_ref, dg2_ref = refs
        else:
            a_ref, w_ref, dxn_ref, x_ref, g_ref, dx_ref, dg_ref = refs

        @pl.when(pl.program_id(0) == 0)
        def _():
            dg_ref[...] = jnp.zeros_like(dg_ref)
            if has_prev:
                dg2_ref[...] = jnp.zeros_like(dg2_ref)

        wv, g = w_ref[...], g_ref[...]
        for rows in _halves(tm):
            dh = lax.dot_general(a_ref[rows, :], wv, dims, preferred_element_type=F32)
            n, r = _rms(x_ref[rows, :])
            dx = dxn_ref[rows, :] + _rms_bwd(dh, g, n, r)
            dx_ref[rows, :] = dx
            dg_ref[...] += _colsum(dh * n)
            if has_prev:
                n2, r2 = _rms(y_ref[rows, :])
                dy_ref[rows, :] = _rms_bwd(dx, g2_ref[...], n2, r2).astype(BF16)
                dg2_ref[...] += _colsum(dx * n2)

    in_specs = [_rows(tm, k), _whole(w.shape), _rows(tm, d), _rows(tm, d), _layer((1, d), l)]
    out_specs = [_rows(tm, d), _whole((1, d))]
    out_shape = [jax.ShapeDtypeStruct((t, d), F32), jax.ShapeDtypeStruct((1, d), F32)]
    args = [a, w, dxn, x_in, gpre3]
    if has_prev:
        in_specs += [_rows(tm, d), _layer((1, d), l_prev)]
        out_specs += [_rows(tm, d), _whole((1, d))]
        out_shape += [jax.ShapeDtypeStruct((t, d), BF16), jax.ShapeDtypeStruct((1, d), F32)]
        args += [y_prev, gpost3]
    return pl.pallas_call(
        body, name=name, grid=(t // tm,), in_specs=in_specs, out_specs=out_specs, out_shape=out_shape,
        compiler_params=_params("arbitrary"),
    )(*args)


def _to_steps(a):
    t = a.shape[0]
    return a.reshape(8, t // 8, -1).transpose(1, 0, 2).reshape(a.shape)


def _from_steps(a):
    t = a.shape[0]
    return a.reshape(t // 8, 8, -1).transpose(1, 0, 2).reshape(a.shape)


def _al(v):
    return v if isinstance(v, int) else pl.multiple_of(v, 8)


def _chunks(n_rows, fn, unroll=1):
    def step(r, carry):
        fn(pl.multiple_of(r * ROW_CHUNK, ROW_CHUNK))
        return carry
    lax.fori_loop(0, n_rows // ROW_CHUNK, step, 0, unroll=unroll)


def _fold8(a):
    return a.reshape(a.shape[0] // 8, 8, a.shape[1]).sum(axis=0)


def _shift_down(a):
    row = lax.broadcasted_iota(jnp.int32, a.shape, 0)
    return jnp.where(row % 8 == 0, 0.0, pltpu.roll(a, 1, 0))


def _shift_up(a):
    row = lax.broadcasted_iota(jnp.int32, a.shape, 0)
    return jnp.where(row % 8 == 7, 0.0, pltpu.roll(a, a.shape[0] - 1, 0))


def _prev_block(h, c, tt, t):
    return pl.BlockSpec((h, c), lambda i: (jnp.where(i == 0, t // h - 1, i * (tt // h) - 1), 0))


def _next_block(h, c, tt, t):
    return pl.BlockSpec((h, c), lambda i: (jnp.where(i == t // tt - 1, 0, (i + 1) * (tt // h)), 0))


def _taps(w_ref, buf, start, taps, rc, lanes=slice(None)):
    acc = w_ref[0:1, lanes] * buf[pl.ds(_al(start), rc), lanes]
    for k in range(1, taps):
        acc = acc + w_ref[k:k + 1, lanes] * buf[pl.ds(_al(start + 8 * k), rc), lanes]
    return acc


def _taps_rev(w_ref, buf, start, taps, rc, lanes=slice(None)):
    acc = w_ref[0:1, lanes] * buf[pl.ds(_al(start + 8 * (taps - 1)), rc), lanes]
    for k in range(1, taps):
        acc = acc + w_ref[k:k + 1, lanes] * buf[pl.ds(_al(start + 8 * (taps - 1 - k)), rc), lanes]
    return acc


def _mix_dims(d):
    dp = d // 4
    dc = 3 * d // 8
    ds = d - dp - dc
    oa, og = dp, dp + dc
    ob = dp + 2 * dc
    oc, ox = ob + ds, ob + 2 * ds
    return dp, dc, ds, oa, og, ob, oc, ox, ox + ds


def _pool_consts(dp):
    win = jnp.repeat(jnp.asarray(POOL_WINDOWS, F32), dp // len(POOL_WINDOWS))[None, :]
    mask = (jnp.arange(MAX_WINDOW, dtype=F32)[:, None] < win).astype(F32)
    return mask, win


def _pool_count(row0, rc, dp, seg, wl):
    r = lax.broadcasted_iota(jnp.int32, (rc, dp), 0) + row0
    return jnp.minimum(((r & 7) * seg + (r >> 3) + 1).astype(F32), wl)


def _mix_fwd(z, mbd, pscale3, wdw, bdw3, lng3, lnb3, wsc, l, name):
    t, din = z.shape
    dp, dc, ds, oa, og, ob, oc, ox, din2 = _mix_dims(din * 8 // 17)
    assert din2 == din
    d = ob
    tt = min(TILE_MIX, t)
    hp, hc, hs, rc = POOL_HALO, CONF_HALO, SHORT_HALO, ROW_CHUNK
    assert tt % hc == 0 and t % tt == 0
    seg = t // 8
    pmask, wlane = _pool_consts(dp)

    def body(z_ref, zpa_ref, zpb_ref, mbd_ref, ps_ref, pmask_ref, wl_ref, wdw_ref, bdw_ref, lng_ref, lnb_ref, wsc_ref,
             cat_ref, c_ref, pbuf, vbuf, sbuf):
        i = pl.program_id(0)
        pbuf[0:hp, :] = zpa_ref[hc - hp:hc, 0:dp]

        def prev(r0):
            rows = pl.ds(r0, rc)
            vbuf[rows, :] = zpa_ref[rows, oa:oa + dc] * _sig(zpa_ref[rows, og:og + dc])
        _chunks(hc, prev)
        sbuf[0:hs, :] = zpb_ref[:, oc:oc + ds] * zpb_ref[:, ox:ox + ds]

        @pl.when(i == 0)
        def _():
            pbuf[0:hp, :] = _shift_down(pbuf[0:hp, :])
            vbuf[0:hc, :] = _shift_down(vbuf[0:hc, :])
            sbuf[0:hs, :] = _shift_down(sbuf[0:hs, :])

        mbd_v, ps, wl = mbd_ref[...], ps_ref[...], wl_ref[...]
        bdw, lng, lnb = bdw_ref[...], lng_ref[...], lnb_ref[...]

        def step(r0):
            rows = pl.ds(r0, rc)
            zp = z_ref[rows, 0:dp]
            pbuf[pl.ds(_al(hp + r0), rc), :] = zp
            vbuf[pl.ds(_al(hc + r0), rc), :] = z_ref[rows, oa:oa + dc] * _sig(z_ref[rows, og:og + dc])
            sbuf[pl.ds(_al(hs + r0), rc), :] = z_ref[rows, oc:oc + ds] * z_ref[rows, ox:ox + ds]
            pooled = _taps_rev(pmask_ref, pbuf, r0 + hp - 8 * (MAX_WINDOW - 1), MAX_WINDOW, rc)
            pooled = pooled / _pool_count(i * tt + r0, rc, dp, seg, wl) - zp
            pm = jnp.dot(pooled.astype(BF16), mbd_v, preferred_element_type=F32)
            cat_ref[rows, 0:dp] = (pm * ps).astype(BF16)
            c = _taps(wdw_ref, vbuf, r0 + hc - 8 * (CONF_TAPS - 1), CONF_TAPS, rc) + bdw
            c_ref[rows, :] = c
            xc = c - jnp.mean(c, axis=-1, keepdims=True)
            nrm = xc * lax.rsqrt(jnp.mean(xc * xc, axis=-1, keepdims=True) + EPS)
            yln = nrm * lng + lnb
            cat_ref[rows, dp:dp + dc] = (yln * _sig(yln)).astype(BF16)
            cv = _taps(wsc_ref, sbuf, r0 + hs - 8 * (SHORT_TAPS - 1), SHORT_TAPS, rc)
            cat_ref[rows, dp + dc:d] = (z_ref[rows, ob:ob + ds] * cv).astype(BF16)
        _chunks(tt, step, unroll=2)

    return pl.pallas_call(
        body, name=name, grid=(t // tt,),
        in_specs=[_rows(tt, din), _prev_block(hc, d, tt, t), _prev_block(hs, din, tt, t),
                  _layer((dp, dp), l), _layer((1, dp), l), _whole((MAX_WINDOW, dp)), _whole((1, dp)),
                  _layer((CONF_TAPS, dc), l), _layer((1, dc), l), _layer((1, dc), l), _layer((1, dc), l),
                  _layer((SHORT_TAPS, ds), l)],
        out_specs=[_rows(tt, d), _rows(tt, dc)],
        out_shape=[jax.ShapeDtypeStruct((t, d), BF16), jax.ShapeDtypeStruct((t, dc), F32)],
        scratch_shapes=[pltpu.VMEM((hp + tt, dp), F32), pltpu.VMEM((hc + tt, dc), F32), pltpu.VMEM((hs + tt, ds), F32)],
        compiler_params=_params("parallel"),
    )(z, z, z, mbd, pscale3, pmask, wlane, wdw, bdw3, lng3, lnb3, wsc)


def _mix_bwd(dcat, z, c, mbd, pscale3, wdw, lng3, lnb3, wsc, l, name):
    t, din = z.shape
    dp, dc, ds, oa, og, ob, oc, ox, _ = _mix_dims(din * 8 // 17)
    d = ob
    tt = min(TILE_MIX, t)
    nt = t // tt
    hp, hc, hs, rc = POOL_HALO, CONF_HALO, SHORT_HALO, ROW_CHUNK
    assert tt % hc == 0 and t % tt == 0 and tt >= 8 * MAX_WINDOW
    seg = t // 8
    pmask, wlane = _pool_consts(dp)

    def body(dcat_ref, dcn_ref, z_ref, zpa_ref, zpb_ref, znb_ref, c_ref, cn_ref, mbd_ref, ps_ref, pmask_ref, wl_ref,
             wdw_ref, lng_ref, lnb_ref, wsc_ref,
             dz_ref, dmbd_ref, dps_ref, dwdw_ref, dbdw_ref, dlng_ref, dlnb_ref, dwsc_ref,
             pbuf, qbuf, dpbuf, pbf, vbuf, dcbuf, sbuf, dsbuf, dw8, ds8, ln8, ps8):
        i = pl.program_id(0)
        first, last = i == 0, i == nt - 1

        @pl.when(first)
        def _():
            for ref in (dmbd_ref, dw8, ds8, ln8, ps8):
                ref[...] = jnp.zeros_like(ref)

        mbd_v, ps, wl = mbd_ref[...], ps_ref[...], wl_ref[...]
        lng, lnb = lng_ref[...], lnb_ref[...]

        def ln_silu_bwd(cc, dyb):
            xc = cc - jnp.mean(cc, axis=-1, keepdims=True)
            rstd = lax.rsqrt(jnp.mean(xc * xc, axis=-1, keepdims=True) + EPS)
            nrm = xc * rstd
            yln = nrm * lng + lnb
            s = _sig(yln)
            dyln = dyb * (s * (1.0 + yln * (1.0 - s)))
            dn = dyln * lng
            dcc = rstd * (dn - jnp.mean(dn, axis=-1, keepdims=True) - nrm * jnp.mean(dn * nrm, axis=-1, keepdims=True))
            return dcc, dyln, nrm

        pbuf[0:hp, :] = zpa_ref[hc - hp:hc, 0:dp]

        def prev(r0):
            rows = pl.ds(r0, rc)
            vbuf[rows, :] = zpa_ref[rows, oa:oa + dc] * _sig(zpa_ref[rows, og:og + dc])
        _chunks(hc, prev)
        sbuf[0:hs, :] = zpb_ref[:, oc:oc + ds] * zpb_ref[:, ox:ox + ds]

        @pl.when(first)
        def _():
            pbuf[0:hp, :] = _shift_down(pbuf[0:hp, :])
            vbuf[0:hc, :] = _shift_down(vbuf[0:hc, :])
            sbuf[0:hs, :] = _shift_down(sbuf[0:hs, :])

        def nxt(r0):
            rows = pl.ds(r0, rc)
            dcc, _, _ = ln_silu_bwd(cn_ref[rows, :], dcn_ref[rows, dp:dp + dc])
            dcbuf[pl.ds(_al(tt + r0), rc), :] = dcc
        _chunks(hc, nxt, unroll=2)
        dpm_n = (dcn_ref[0:hp, 0:dp] * ps).astype(BF16)
        qbuf[tt:tt + hp, :] = lax.dot_general(dpm_n, mbd_v, _DIMS["nt"], preferred_element_type=F32) / wl
        dsbuf[tt:tt + hs, :] = dcn_ref[0:hs, dp + dc:d] * znb_ref[:, ob:ob + ds]

        @pl.when(last)
        def _():
            dcbuf[tt:tt + hc, :] = _shift_up(dcbuf[tt:tt + hc, :])
            qbuf[tt:tt + hp, :] = _shift_up(qbuf[tt:tt + hp, :])
            dsbuf[tt:tt + hs, :] = _shift_up(dsbuf[tt:tt + hs, :])

        def fill(r0):
            rows = pl.ds(r0, rc)
            zp = z_ref[rows, 0:dp]
            pbuf[pl.ds(_al(hp + r0), rc), :] = zp
            vbuf[pl.ds(_al(hc + r0), rc), :] = z_ref[rows, oa:oa + dc] * _sig(z_ref[rows, og:og + dc])
            sbuf[pl.ds(_al(hs + r0), rc), :] = z_ref[rows, oc:oc + ds] * z_ref[rows, ox:ox + ds]
            pooled = _taps_rev(pmask_ref, pbuf, r0 + hp - 8 * (MAX_WINDOW - 1), MAX_WINDOW, rc)
            pbf[rows, :] = (pooled / _pool_count(i * tt + r0, rc, dp, seg, wl) - zp).astype(BF16)
            dcc, dyln, nrm = ln_silu_bwd(c_ref[rows, :], dcat_ref[rows, dp:dp + dc])
            dcbuf[rows, :] = dcc
            ln8[0] += _fold8(dyln * nrm)
            ln8[1] += _fold8(dyln)
            ln8[2] += _fold8(dcc)
            dsbuf[rows, :] = dcat_ref[rows, dp + dc:d] * z_ref[rows, ob:ob + ds]
        _chunks(tt, fill, unroll=2)

        pb = pbf[...]
        dya = dcat_ref[:, 0:dp]
        ps8[...] += _fold8(dya * jnp.dot(pb, mbd_v, preferred_element_type=F32))
        dpm = (dya * ps).astype(BF16)
        dmbd_ref[...] += lax.dot_general(pb, dpm, _DIMS["tn"], preferred_element_type=F32)
        dpbuf[...] = lax.dot_general(dpm, mbd_v, _DIMS["nt"], preferred_element_type=F32)

        def quot(r0):
            rows = pl.ds(r0, rc)
            qbuf[rows, :] = dpbuf[rows, :] / _pool_count(i * tt + r0, rc, dp, seg, wl)
        _chunks(tt, quot)

        def back(r0):
            rows = pl.ds(r0, rc)
            dzp = _taps(pmask_ref, qbuf, r0, MAX_WINDOW, rc) - dpbuf[rows, :]
            dz_ref[rows, 0:dp] = dzp.astype(BF16)
            dcc = dcbuf[rows, :]
            for k in range(CONF_TAPS):
                dw8[k] += _fold8(dcc * vbuf[pl.ds(_al(r0 + hc - 8 * (CONF_TAPS - 1 - k)), rc), :])
            dv = _taps_rev(wdw_ref, dcbuf, r0, CONF_TAPS, rc)
            za = z_ref[rows, oa:oa + dc]
            sg = _sig(z_ref[rows, og:og + dc])
            dz_ref[rows, oa:oa + dc] = (dv * sg).astype(BF16)
            dz_ref[rows, og:og + dc] = (dv * za * sg * (1.0 - sg)).astype(BF16)
            cv = _taps(wsc_ref, sbuf, r0 + hs - 8 * (SHORT_TAPS - 1), SHORT_TAPS, rc)
            dz_ref[rows, ob:ob + ds] = (dcat_ref[rows, dp + dc:d] * cv).astype(BF16)
            dcv = dsbuf[rows, :]
            for k in range(SHORT_TAPS):
                ds8[k] += _fold8(dcv * sbuf[pl.ds(_al(r0 + hs - 8 * (SHORT_TAPS - 1 - k)), rc), :])
            dpv = _taps_rev(wsc_ref, dsbuf, r0, SHORT_TAPS, rc)
            dz_ref[rows, oc:oc + ds] = (dpv * z_ref[rows, ox:ox + ds]).astype(BF16)
            dz_ref[rows, ox:ox + ds] = (dpv * z_ref[rows, oc:oc + ds]).astype(BF16)
        _chunks(tt, back)

        @pl.when(last)
        def _():
            dps_ref[...] = jnp.sum(ps8[...], axis=0, keepdims=True)
            dwdw_ref[...] = jnp.sum(dw8[...], axis=1)
            dwsc_ref[...] = jnp.sum(ds8[...], axis=1)
            dlng_ref[...] = jnp.sum(ln8[0], axis=0, keepdims=True)
            dlnb_ref[...] = jnp.sum(ln8[1], axis=0, keepdims=True)
            dbdw_ref[...] = jnp.sum(ln8[2], axis=0, keepdims=True)

    return pl.pallas_call(
        body, name=name, grid=(nt,),
        in_specs=[_rows(tt, d), _next_block(hc, d, tt, t),
                  _rows(tt, din), _prev_block(hc, d, tt, t), _prev_block(hs, din, tt, t), _next_block(hs, din, tt, t),
                  _rows(tt, dc), _next_block(hc, dc, tt, t),
                  _layer((dp, dp), l), _layer((1, dp), l), _whole((MAX_WINDOW, dp)), _whole((1, dp)),
                  _layer((CONF_TAPS, dc), l), _layer((1, dc), l), _layer((1, dc), l), _layer((SHORT_TAPS, ds), l)],
        out_specs=[_rows(tt, din), _whole((dp, dp)), _whole((1, dp)), _whole((CONF_TAPS, dc)), _whole((1, dc)),
                   _whole((1, dc)), _whole((1, dc)), _whole((SHORT_TAPS, ds))],
        out_shape=[jax.ShapeDtypeStruct((t, din), BF16), jax.ShapeDtypeStruct((dp, dp), F32),
                   jax.ShapeDtypeStruct((1, dp), F32), jax.ShapeDtypeStruct((CONF_TAPS, dc), F32),
                   jax.ShapeDtypeStruct((1, dc), F32), jax.ShapeDtypeStruct((1, dc), F32),
                   jax.ShapeDtypeStruct((1, dc), F32), jax.ShapeDtypeStruct((SHORT_TAPS, ds), F32)],
        scratch_shapes=[pltpu.VMEM((hp + tt, dp), F32), pltpu.VMEM((tt + hp, dp), F32), pltpu.VMEM((tt, dp), F32),
                        pltpu.VMEM((tt, dp), BF16), pltpu.VMEM((hc + tt, dc), F32), pltpu.VMEM((tt + hc, dc), F32),
                        pltpu.VMEM((hs + tt, ds), F32), pltpu.VMEM((tt + hs, ds), F32),
                        pltpu.VMEM((CONF_TAPS, 8, dc), F32), pltpu.VMEM((SHORT_TAPS, 8, ds), F32),
                        pltpu.VMEM((3, 8, dc), F32), pltpu.VMEM((8, dp), F32)],
        compiler_params=_params("arbitrary"),
    )(dcat, dcat, z, z, z, z, c, c, mbd, pscale3, pmask, wlane, wdw, lng3, lnb3, wsc)


def _softmax_rows(qh, kh, scale):
    s = lax.dot_general(qh, kh, _DIMS["nt"], preferred_element_type=F32) * scale
    e = jnp.exp(s - jnp.max(s, axis=-1, keepdims=True))
    return e / jnp.sum(e, axis=-1, keepdims=True)


def _attn_fwd(q, k, v, name):
    t, d = q.shape
    m = k.shape[0]
    hd = d // HEADS
    scale = hd ** -0.5
    tt = min(TILE_ATTN, t)

    def body(q_ref, k_ref, v_ref, o_ref):
        for h in range(HEADS):
            sl = slice(h * hd, (h + 1) * hd)
            p = _softmax_rows(q_ref[:, sl], k_ref[:, sl], scale)
            o_ref[:, sl] = jnp.dot(p.astype(BF16), v_ref[:, sl], preferred_element_type=F32).astype(BF16)

    return pl.pallas_call(
        body, name=name, grid=(t // tt,),
        in_specs=[_rows(tt, d), _whole((m, d)), _whole((m, d))], out_specs=_rows(tt, d),
        out_shape=jax.ShapeDtypeStruct((t, d), BF16), compiler_params=_params("parallel"),
    )(q, k, v)


def _attn_bwd(q, k, v, do, name):
    t, d = q.shape
    m = k.shape[0]
    hd = d // HEADS
    scale = hd ** -0.5
    tt = min(TILE_ATTN, t)

    def body(q_ref, k_ref, v_ref, do_ref, dq_ref, dk_ref, dv_ref):
        @pl.when(pl.program_id(0) == 0)
        def _():
            dk_ref[...] = jnp.zeros_like(dk_ref)
            dv_ref[...] = jnp.zeros_like(dv_ref)

        for h in range(HEADS):
            sl = slice(h * hd, (h + 1) * hd)
            qh, kh, vh, doh = q_ref[:, sl], k_ref[:, sl], v_ref[:, sl], do_ref[:, sl]
            p = _softmax_rows(qh, kh, scale)
            dv_ref[:, sl] += lax.dot_general(p.astype(BF16), doh, _DIMS["tn"], preferred_element_type=F32)
            dp = lax.dot_general(doh, vh, _DIMS["nt"], preferred_element_type=F32)
            ds = (p * (dp - jnp.sum(dp * p, axis=-1, keepdims=True)) * scale).astype(BF16)
            dq_ref[:, sl] = jnp.dot(ds, kh, preferred_element_type=F32).astype(BF16)
            dk_ref[:, sl] += lax.dot_general(ds, qh, _DIMS["tn"], preferred_element_type=F32)

    return pl.pallas_call(
        body, name=name, grid=(t // tt,),
        in_specs=[_rows(tt, d), _whole((m, d)), _whole((m, d)), _rows(tt, d)],
        out_specs=[_rows(tt, d), _whole((m, d)), _whole((m, d))],
        out_shape=[jax.ShapeDtypeStruct((t, d), BF16), jax.ShapeDtypeStruct((m, d), F32),
                   jax.ShapeDtypeStruct((m, d), F32)],
        compiler_params=_params("arbitrary"),
    )(q, k, v, do)


def _lane_chunks(f):
    w = 256 if f % 256 == 0 else 128 if f % 128 == 0 else f
    return [(c0, w) for c0 in range(0, f, w)]


def _ffn_act_fwd(u, wc, l, name):
    t, f2 = u.shape
    f = f2 // 2
    tt = min(TILE_FFN, t)
    hs, rc = SHORT_HALO, ROW_CHUNK
    lanes = _lane_chunks(f)

    def body(u_ref, up_ref, wc_ref, a_ref, ubuf):
        ubuf[0:hs, :] = up_ref[...]

        @pl.when(pl.program_id(0) == 0)
        def _():
            ubuf[0:hs, :] = _shift_down(ubuf[0:hs, :])

        def step(r0):
            rows = pl.ds(r0, rc)
            ubuf[pl.ds(_al(hs + r0), rc), :] = u_ref[rows, :]
            start = r0 + hs - 8 * (SHORT_TAPS - 1)
            for c0, cw in lanes:
                g = _taps(wc_ref, ubuf, start, SHORT_TAPS, rc, slice(c0, c0 + cw))
                vv = _taps(wc_ref, ubuf, start, SHORT_TAPS, rc, slice(f + c0, f + c0 + cw))
                a_ref[rows, c0:c0 + cw] = (g * _sig(g) * vv).astype(BF16)
        _chunks(tt, step)

    return pl.pallas_call(
        body, name=name, grid=(t // tt,),
        in_specs=[_rows(tt, f2), _prev_block(hs, f2, tt, t), _layer((SHORT_TAPS, f2), l)],
        out_specs=_rows(tt, f), out_shape=jax.ShapeDtypeStruct((t, f), BF16),
        scratch_shapes=[pltpu.VMEM((hs + tt, f2), F32)], compiler_params=_params("parallel"),
    )(u, u, wc)


def _ffn_act_bwd(u, da, wc, l, name):
    t, f2 = u.shape
    f = f2 // 2
    tt = min(TILE_FFN, t)
    nt = t // tt
    hs, rc = SHORT_HALO, ROW_CHUNK
    lanes = _lane_chunks(f)

    def body(u_ref, up_ref, un_ref, da_ref, dan_ref, wc_ref, du_ref, dwc_ref, ubuf, danbuf, dbuf, dw8):
        i = pl.program_id(0)
        first, last = i == 0, i == nt - 1
        ubuf[0:hs, :] = up_ref[...]
        ubuf[hs + tt:hs + tt + hs, :] = un_ref[...]
        danbuf[...] = dan_ref[...]

        @pl.when(first)
        def _():
            dw8[...] = jnp.zeros_like(dw8)
            ubuf[0:hs, :] = _shift_down(ubuf[0:hs, :])

        @pl.when(last)
        def _():
            ubuf[hs + tt:hs + tt + hs, :] = _shift_up(ubuf[hs + tt:hs + tt + hs, :])
            danbuf[...] = _shift_up(danbuf[...])

        def fill(r0):
            ubuf[pl.ds(_al(hs + r0), rc), :] = u_ref[pl.ds(r0, rc), :]
        _chunks(tt, fill)

        def conv_grads(r0, n, da_rows):
            start = r0 + hs - 8 * (SHORT_TAPS - 1)
            for c0, cw in lanes:
                sl_g, sl_v = slice(c0, c0 + cw), slice(f + c0, f + c0 + cw)
                g = _taps(wc_ref, ubuf, start, SHORT_TAPS, n, sl_g)
                vv = _taps(wc_ref, ubuf, start, SHORT_TAPS, n, sl_v)
                dav = da_rows(c0, cw)
                sg = _sig(g)
                dbuf[pl.ds(_al(r0), n), sl_g] = dav * vv * (sg * (1.0 + g * (1.0 - sg)))
                dbuf[pl.ds(_al(r0), n), sl_v] = dav * (g * sg)

        _chunks(tt, lambda r0: conv_grads(r0, rc, lambda c0, cw: da_ref[pl.ds(r0, rc), c0:c0 + cw]))
        conv_grads(tt, hs, lambda c0, cw: danbuf[:, c0:c0 + cw])

        def back(r0):
            rows = pl.ds(r0, rc)
            for c0, cw in lanes:
                for off in (c0, f + c0):
                    sl = slice(off, off + cw)
                    du_ref[rows, sl] = _taps_rev(wc_ref, dbuf, r0, SHORT_TAPS, rc, sl).astype(BF16)
                    dd = dbuf[rows, sl]
                    for k in range(SHORT_TAPS):
                        dw8[k, :, sl] += _fold8(dd * ubuf[pl.ds(_al(r0 + hs - 8 * (SHORT_TAPS - 1 - k)), rc), sl])
        _chunks(tt, back)

        @pl.when(last)
        def _():
            dwc_ref[...] = jnp.sum(dw8[...], axis=1)

    return pl.pallas_call(
        body, name=name, grid=(nt,),
        in_specs=[_rows(tt, f2), _prev_block(hs, f2, tt, t), _next_block(hs, f2, tt, t),
                  _rows(tt, f), _next_block(hs, f, tt, t), _layer((SHORT_TAPS, f2), l)],
        out_specs=[_rows(tt, f2), _whole((SHORT_TAPS, f2))],
        out_shape=[jax.ShapeDtypeStruct((t, f2), BF16), jax.ShapeDtypeStruct((SHORT_TAPS, f2), F32)],
        scratch_shapes=[pltpu.VMEM((hs + tt + hs, f2), F32), pltpu.VMEM((hs, f), F32), pltpu.VMEM((tt + hs, f2), F32),
                        pltpu.VMEM((SHORT_TAPS, 8, f2), F32)],
        compiler_params=_params("arbitrary"),
    )(u, u, u, da, da, wc)


def _place():
    return lax.axis_index("x"), lax.axis_index("y"), lax.axis_index("c")


def _flip(v, bit):
    return 1 - v if bit else v


def _peers(x, y, c):
    out = []
    for kk in range(1, N_DEV):
        px, py, pc = _flip(x, kk & 4), _flip(y, kk & 2), _flip(c, kk & 1)
        out.append((kk - 1, (px, py, pc), 4 * px + 2 * py + pc))
    return out


def _allgather(shards, name):
    nt = len(shards)

    def body(*refs):
        srcs, outs = refs[:nt], refs[nt:2 * nt]
        send_sems, recv_sems, local_sems = refs[2 * nt:]
        x, y, c = _place()
        me, sibling = (x, y, c), (x, y, 1 - c)
        chips = [(1 - x, y), (x, 1 - y), (1 - x, 1 - y)]

        def rows(ti, px, py, pc):
            r = srcs[ti].shape[1]
            return outs[ti].at[:, pl.ds((4 * px + 2 * py + pc) * r, r), :]

        def copy(ti, kk, block, to, src=None):
            return pltpu.make_async_remote_copy(
                src_ref=rows(ti, *block) if src is None else src, dst_ref=rows(ti, *block),
                send_sem=send_sems.at[ti, kk], recv_sem=recv_sems.at[ti, kk], device_id=to, device_id_type=MESH)

        mine = [pltpu.make_async_copy(srcs[ti], rows(ti, *me), local_sems.at[ti]) for ti in range(nt)]
        for cp in mine:
            cp.start()
        first = []
        for ti in range(nt):
            first.append(copy(ti, 0, me, sibling, src=srcs[ti]))
            first += [copy(ti, 1 + j, me, (*chip, c), src=srcs[ti]) for j, chip in enumerate(chips)]
        for cp in first:
            cp.start()
        passed = []
        for j, chip in enumerate(chips):
            for ti in range(nt):
                copy(ti, 1 + j, (*chip, c), me).wait_recv()
                fwd = copy(ti, 4 + j, (*chip, c), sibling)
                fwd.start()
                passed.append(fwd)
        for ti in range(nt):
            copy(ti, 0, sibling, me).wait_recv()
            for j, chip in enumerate(chips):
                copy(ti, 4 + j, (*chip, 1 - c), me).wait_recv()
        for cp in first + passed:
            cp.wait_send()
        for cp in mine:
            cp.wait()

    return pl.pallas_call(
        body, name=name,
        in_specs=[ANY] * nt, out_specs=[ANY] * nt,
        out_shape=[jax.ShapeDtypeStruct((s.shape[0], N_DEV * s.shape[1], s.shape[2]), s.dtype) for s in shards],
        scratch_shapes=[pltpu.SemaphoreType.DMA((nt, 7)), pltpu.SemaphoreType.DMA((nt, 7)),
                        pltpu.SemaphoreType.DMA((nt,))],
    )(*shards)


def _gather_piece(src, land, me, to):
    r = src.shape[0]
    return src, land.at[pl.ds(me * r, r), :]


def _scatter_piece(l):
    def piece(src, land, me, to):
        r = src.shape[0] // N_DEV
        return src.at[pl.ds(to * r, r), :], land.at[l, me]
    return piece


def _split_start(srcs, lands, piece, after, name):
    nt = len(srcs)

    def body(*refs):
        src_refs, land_refs = refs[:nt], refs[nt:2 * nt]
        send_sems, recv_sems, local_sems, token = refs[2 * nt + 1], refs[2 * nt + 2], refs[2 * nt + 3], refs[4 * nt + 4]
        x, y, c = _place()
        me = 4 * x + 2 * y + c
        for ti in range(nt):
            for slot, peer, flat in _peers(x, y, c):
                src, dst = piece(src_refs[ti], land_refs[ti], me, flat)
                pltpu.make_async_remote_copy(
                    src_ref=src, dst_ref=dst, send_sem=send_sems.at[7 * ti + slot], recv_sem=recv_sems.at[7 * ti + slot],
                    device_id=peer, device_id_type=MESH).start()
        for ti in range(nt):
            pltpu.make_async_copy(*piece(src_refs[ti], land_refs[ti], me, me), local_sems.at[ti]).start()
        token[...] = jnp.zeros_like(token)

    both = list(srcs) + list(lands)
    return pl.pallas_call(
        body, name=name,
        in_specs=[HBM] * (2 * nt) + [ANY],
        out_specs=[SEM, SEM, SEM] + [HBM] * (2 * nt) + [pl.BlockSpec(memory_space=pltpu.VMEM)],
        out_shape=[pltpu.SemaphoreType.DMA((7 * nt,)), pltpu.SemaphoreType.DMA((7 * nt,)), pltpu.SemaphoreType.DMA((nt,))]
        + [pltpu.HBM(a.shape, a.dtype) for a in both] + [jax.ShapeDtypeStruct((8, 128), F32)],
        input_output_aliases={i: i + 3 for i in range(2 * nt)},
        compiler_params=pltpu.CompilerParams(has_side_effects=EFFECT),
    )(*[pltpu.with_memory_space_constraint(a, pltpu.HBM) for a in both], after)


def _split_wait(started, after, piece, name):
    send_sems, recv_sems, local_sems, *both = started[:-1]
    nt = len(both) // 2

    def body(*refs):
        src_refs, land_refs = refs[:nt], refs[nt:2 * nt]
        send_ref, recv_ref, local_ref = refs[2 * nt], refs[2 * nt + 1], refs[2 * nt + 2]
        x, y, c = _place()
        me = 4 * x + 2 * y + c
        for ti in range(nt):
            src, dst = piece(src_refs[ti], land_refs[ti], me, me)
            for slot in range(N_DEV - 1):
                cp = pltpu.make_async_remote_copy(
                    src_ref=src, dst_ref=dst, send_sem=send_ref.at[7 * ti + slot], recv_sem=recv_ref.at[7 * ti + slot],
                    device_id=(x, y, c), device_id_type=MESH)
                cp.wait_send()
                cp.wait_recv()
            pltpu.make_async_copy(src, dst, local_ref.at[ti]).wait()

    outs = pl.pallas_call(
        body, name=name,
        in_specs=[HBM] * (2 * nt) + [SEM, SEM, SEM, ANY], out_specs=[HBM] * (2 * nt),
        out_shape=[pltpu.HBM(a.shape, a.dtype) for a in both],
        input_output_aliases={i: i for i in range(2 * nt)},
        compiler_params=pltpu.CompilerParams(has_side_effects=EFFECT),
    )(*both, send_sems, recv_sems, local_sems, after)
    return outs[nt:]


def _adam(w, g, m, v):
    m2 = ADAM_B1 * m + (1.0 - ADAM_B1) * g
    v2 = ADAM_B2 * v + (1.0 - ADAM_B2) * (g * g)
    m_hat = m2 / (1.0 - ADAM_B1 ** ADAM_STEP)
    v_hat = v2 / (1.0 - ADAM_B2 ** ADAM_STEP)
    return -ADAM_LR * (m_hat / (jnp.sqrt(v_hat) + ADAM_EPS) + ADAM_WD * w), m2, v2


def _adam_sharded(recv, recv_first, w, m, v, lo, hi, name, prev=None):
    nl, r, c = w.shape
    tr = max([rows for rows in range(16, min(r, TILE_ADAM) + 1, 16) if r % rows == 0] or [r])

    def body(recv_ref, w_ref, m_ref, v_ref, *rest):
        g_ref, d_ref, m2_ref, v2_ref = rest[-4:]
        g = recv_ref[0].astype(F32)
        for s in range(1, N_DEV):
            g = g + recv_ref[s].astype(F32)
        g_ref[...] = g
        d_ref[...], m2_ref[...], v2_ref[...] = _adam(w_ref[...], g, m_ref[...], v_ref[...])

    blk = pl.BlockSpec((None, tr, c), lambda li, i: (li + lo, i, 0))
    extra = [] if prev is None else list(prev)
    return pl.pallas_call(
        body, name=name, grid=(hi - lo, r // tr),
        in_specs=[pl.BlockSpec((None, N_DEV, tr, c), lambda li, i: (li + lo - recv_first, 0, i, 0)), blk, blk, blk]
        + [ANY] * len(extra),
        out_specs=[blk] * 4, out_shape=[jax.ShapeDtypeStruct((nl, r, c), F32)] * 4,
        input_output_aliases={4 + i: i for i in range(len(extra))},
        compiler_params=_params("parallel", "parallel"),
    )(recv, w, m, v, *extra)


def _sum_sources(parts, name):
    _, r, c = parts.shape

    def body(p_ref, o_ref):
        g = p_ref[0]
        for s in range(1, N_DEV):
            g = g + p_ref[s]
        o_ref[...] = g

    return pl.pallas_call(
        body, name=name, grid=(1,), in_specs=[_whole((N_DEV, r, c))], out_specs=_whole((r, c)),
        out_shape=jax.ShapeDtypeStruct((r, c), F32), compiler_params=_params("arbitrary"),
    )(parts)


def _adam_flat(w, g, m, v, name):
    r, c = w.shape

    def body(w_ref, g_ref, m_ref, v_ref, d_ref, m2_ref, v2_ref):
        d_ref[...], m2_ref[...], v2_ref[...] = _adam(w_ref[...], g_ref[...], m_ref[...], v_ref[...])

    return pl.pallas_call(
        body, name=name, grid=(1,), in_specs=[_whole((r, c))] * 4, out_specs=[_whole((r, c))] * 3,
        out_shape=[jax.ShapeDtypeStruct((r, c), F32)] * 3, compiler_params=_params("arbitrary"),
    )(w, g, m, v)


def _pack(arrays):
    flat = jnp.concatenate([a.reshape(-1).astype(F32) for a in arrays])
    rows = -(-flat.shape[0] // 1024) * 8
    return jnp.pad(flat, (0, rows * 128 - flat.shape[0])).reshape(rows, 128)


def _unpack(slab, like):
    flat = slab.reshape(-1)
    out, at = [], 0
    for a in like:
        out.append(flat[at:at + a.size].reshape(a.shape))
        at += a.size
    return out


def kernel(x, mem, mem_norm, mix_pre_norm, mix_post_norm, w_in, pool_maps, pool_scale, conf_dw_w, conf_dw_b, conf_ln_g, conf_ln_b, sconv_w, w_out, xattn_pre_norm, xattn_post_norm, xattn_wq, xattn_wk, xattn_wv, xattn_wo, ffn_pre_norm, ffn_post_norm, ffn_w_up, ffn_conv_w, ffn_w_down, loss_target, m_mem_norm, m_mix_pre_norm, m_mix_post_norm, m_w_in, m_pool_maps, m_pool_scale, m_conf_dw_w, m_conf_dw_b, m_conf_ln_g, m_conf_ln_b, m_sconv_w, m_w_out, m_xattn_pre_norm, m_xattn_post_norm, m_xattn_wq, m_xattn_wk, m_xattn_wv, m_xattn_wo, m_ffn_pre_norm, m_ffn_post_norm, m_ffn_w_up, m_ffn_conv_w, m_ffn_w_down, v_mem_norm, v_mix_pre_norm, v_mix_post_norm, v_w_in, v_pool_maps, v_pool_scale, v_conf_dw_w, v_conf_dw_b, v_conf_ln_g, v_conf_ln_b, v_sconv_w, v_w_out, v_xattn_pre_norm, v_xattn_post_norm, v_xattn_wq, v_xattn_wk, v_xattn_wv, v_xattn_wo, v_ffn_pre_norm, v_ffn_post_norm, v_ffn_w_up, v_ffn_conv_w, v_ffn_w_down):
    weights = dict(mem_norm=mem_norm, mix_pre_norm=mix_pre_norm, mix_post_norm=mix_post_norm, w_in=w_in, pool_maps=pool_maps, pool_scale=pool_scale, conf_dw_w=conf_dw_w, conf_dw_b=conf_dw_b, conf_ln_g=conf_ln_g, conf_ln_b=conf_ln_b, sconv_w=sconv_w, w_out=w_out, xattn_pre_norm=xattn_pre_norm, xattn_post_norm=xattn_post_norm, xattn_wq=xattn_wq, xattn_wk=xattn_wk, xattn_wv=xattn_wv, xattn_wo=xattn_wo, ffn_pre_norm=ffn_pre_norm, ffn_post_norm=ffn_post_norm, ffn_w_up=ffn_w_up, ffn_conv_w=ffn_conv_w, ffn_w_down=ffn_w_down)
    mom1 = dict(mem_norm=m_mem_norm, mix_pre_norm=m_mix_pre_norm, mix_post_norm=m_mix_post_norm, w_in=m_w_in, pool_maps=m_pool_maps, pool_scale=m_pool_scale, conf_dw_w=m_conf_dw_w, conf_dw_b=m_conf_dw_b, conf_ln_g=m_conf_ln_g, conf_ln_b=m_conf_ln_b, sconv_w=m_sconv_w, w_out=m_w_out, xattn_pre_norm=m_xattn_pre_norm, xattn_post_norm=m_xattn_post_norm, xattn_wq=m_xattn_wq, xattn_wk=m_xattn_wk, xattn_wv=m_xattn_wv, xattn_wo=m_xattn_wo, ffn_pre_norm=m_ffn_pre_norm, ffn_post_norm=m_ffn_post_norm, ffn_w_up=m_ffn_w_up, ffn_conv_w=m_ffn_conv_w, ffn_w_down=m_ffn_w_down)
    mom2 = dict(mem_norm=v_mem_norm, mix_pre_norm=v_mix_pre_norm, mix_post_norm=v_mix_post_norm, w_in=v_w_in, pool_maps=v_pool_maps, pool_scale=v_pool_scale, conf_dw_w=v_conf_dw_w, conf_dw_b=v_conf_dw_b, conf_ln_g=v_conf_ln_g, conf_ln_b=v_conf_ln_b, sconv_w=v_sconv_w, w_out=v_w_out, xattn_pre_norm=v_xattn_pre_norm, xattn_post_norm=v_xattn_post_norm, xattn_wq=v_xattn_wq, xattn_wk=v_xattn_wk, xattn_wv=v_xattn_wv, xattn_wo=v_xattn_wo, ffn_pre_norm=v_ffn_pre_norm, ffn_post_norm=v_ffn_post_norm, ffn_w_up=v_ffn_w_up, ffn_conv_w=v_ffn_conv_w, ffn_w_down=v_ffn_w_down)
    names = list(weights)

    nl, d = mix_pre_norm.shape
    x0, mem0, target = _to_steps(x[0]), mem[0], _to_steps(loss_target[0])
    dp, dc, ds, *_ = _mix_dims(d)
    pg = dp // len(POOL_WINDOWS)
    me = 4 * lax.axis_index("x") + 2 * lax.axis_index("y") + lax.axis_index("c")

    big = ["w_in", "w_out", "xattn_wq", "xattn_wk", "xattn_wv", "xattn_wo", "ffn_w_up", "ffn_w_down"]
    transposed = ("w_in", "ffn_w_up")

    def row_shard(n, a):
        return a.transpose(0, 2, 1) if n in transposed else a

    shards = [row_shard(n, weights[n]).astype(BF16) for n in big]
    taps = ["conf_dw_w", "sconv_w", "ffn_conv_w"]
    tap_slab = _pack([weights[n] for n in taps])
    win0, tap_all = _allgather([shards[0][0:1], tap_slab[None]], "gather_weights0")
    layer_w = [{"w_in": win0[0]}]

    def land(s):
        return lax.empty((N_DEV * s.shape[1], s.shape[2]), BF16)

    first_a = _split_start([s[0] for s in shards[1:6]], [land(s) for s in shards[1:6]], _gather_piece, win0,
                           "gather_start0a")
    first_b = _split_start([s[0] for s in shards[6:]], [land(s) for s in shards[6:]], _gather_piece, first_a[-1],
                           "gather_start0b")
    tap_all = tap_all[0].reshape(N_DEV, *tap_slab.shape)
    tap_parts = [_unpack(tap_all[p], [weights[n] for n in taps]) for p in range(N_DEV)]
    wdw, wsc, wcf = (jnp.concatenate([tap_parts[p][i] for p in range(N_DEV)], axis=-1) for i in range(3))

    def g3(a):
        return a.reshape(a.shape[0], 1, a.shape[-1])

    mbd = jnp.zeros((nl, dp, dp), F32)
    for gi in range(len(POOL_WINDOWS)):
        mbd = mbd.at[:, gi * pg:(gi + 1) * pg, gi * pg:(gi + 1) * pg].set(pool_maps[:, gi])
    mbd = mbd.astype(BF16)
    pre1, post1, pre2, post2, pre3, post3 = (g3(weights[n]) for n in (
        "mix_pre_norm", "mix_post_norm", "xattn_pre_norm", "xattn_post_norm", "ffn_pre_norm", "ffn_post_norm"))
    pscale3, bdw3, lng3, lnb3 = g3(pool_scale), g3(conf_dw_b), g3(conf_ln_g), g3(conf_ln_b)
    memg3 = mem_norm.reshape(1, 1, d)

    def mm(a, b, mode, dt, name, tm=2048, tn=1024, tk=1024, a_outer=True):
        return _matmul(a, b, mode, dt, name, tm=tm, tn=tn, tk=tk, a_outer=a_outer)

    mem_n = _prenorm(mem0, memg3, 0, "mem_norm")
    xs = x0
    h = _prenorm(xs, pre1, 0, "pre_norm0")
    saved = []
    for l in range(nl):
        ps_l = pscale3
        if l + 1 < nl:
            flying = _split_start([s[l + 1] for s in shards], [land(s) for s in shards], _gather_piece,
                                  xs if l else first_b[-1], f"gather_start{l + 1}")
            ps_l = pscale3 + flying[-1][0, 0]
        w = layer_w[l]
        s = {"x": xs, "h": h}
        s["z"] = mm(h, w["w_in"], "nt", F32, f"z{l}", tm=1024, tn=4096)
        s["cat"], s["c"] = _mix_fwd(s["z"], mbd, ps_l, wdw, bdw3, lng3, lnb3, wsc, l, f"mix_fwd{l}")
        if l == 0:
            w.update(zip(big[1:6], _split_wait(first_a, s["cat"], _gather_piece, "gather_wait0a")))
        s["y1"], s["x1"], s["h1"] = _matmul_resnorm(s["cat"], w["w_out"], xs, post1, l, pre2, l, f"y1_{l}")
        s["q"] = mm(s["h1"], w["xattn_wq"], "nn", BF16, f"q{l}")
        s["k"] = mm(mem_n, w["xattn_wk"], "nn", BF16, f"k{l}")
        s["v"] = mm(mem_n, w["xattn_wv"], "nn", BF16, f"v{l}")
        s["o"] = _attn_fwd(s["q"], s["k"], s["v"], f"attn_fwd{l}")
        s["y2"], s["x2"], s["h2"] = _matmul_resnorm(s["o"], w["xattn_wo"], s["x1"], post2, l, pre3, l, f"y2_{l}")
        if l == 0:
            w.update(zip(big[6:], _split_wait(first_b, s["h2"], _gather_piece, "gather_wait0b")))
        s["u"] = mm(s["h2"], w["ffn_w_up"], "nt", F32, f"u{l}", tn=1408, a_outer=False)
        s["a"] = _ffn_act_fwd(s["u"], wcf, l, f"ffn_act{l}")
        if l + 1 == nl:
            s["y3"] = mm(s["a"], w["ffn_w_down"], "nn", F32, f"y3_{l}", tm=1024, tk=4096)
        else:
            s["y3"], xs, h = _matmul_resnorm(s["a"], w["ffn_w_down"], s["x2"], post3, l, pre1, l + 1, f"y3_{l}")
            layer_w.append(dict(zip(big, _split_wait(flying, xs, _gather_piece, f"gather_wait{l + 1}"))))
        saved.append(s)

    last = saved[-1]
    dxn, dy3, dg_post3, loss_lanes = _loss_head(last["x2"], last["y3"], post3, nl - 1, target, "loss_head")
    loss = lax.psum(loss_lanes[0, 0], ("x", "y", "c"))

    recvs = [lax.empty((max(nl - 1, 1), N_DEV, s.shape[1], d), BF16) for s in shards]
    recv0 = [lax.empty((1, N_DEV, s.shape[1], d), BF16) for s in shards]
    small = {n: [None] * nl for n in names if n not in big and n != "mem_norm"}
    small["ffn_post_norm"][nl - 1] = dg_post3
    dmem_n = jnp.zeros(mem0.shape, F32)
    flying = None
    for l in reversed(range(nl)):
        s, w = saved[l], layer_w[l]
        wc_l = wcf if flying is None else wcf + flying[-1][0, 0]
        gw = {}
        da = mm(dy3, w["ffn_w_down"], "nt", F32, f"da{l}", tn=1408, a_outer=False)
        gw["ffn_w_down"] = mm(s["a"], dy3, "tn", BF16, f"dw_down{l}", tm=1408, tk=2048)
        du, small["ffn_conv_w"][l] = _ffn_act_bwd(s["u"], da, wc_l, l, f"ffn_act_bwd{l}")
        gw["ffn_w_up"] = mm(du, s["h2"], "tn", BF16, f"dw_up{l}", tm=1408, tk=2048)
        dx2, small["ffn_pre_norm"][l], dy2, small["xattn_post_norm"][l] = _matmul_norm_bwd(
            du, w["ffn_w_up"], "nn", dxn, s["x2"], pre3, l, f"dh2_{l}", s["y2"], post2, l, tm=256)
        do = mm(dy2, w["xattn_wo"], "nt", BF16, f"do{l}")
        gw["xattn_wo"] = mm(s["o"], dy2, "tn", BF16, f"dw_o{l}", tk=4096)
        dq, dk, dv = _attn_bwd(s["q"], s["k"], s["v"], do, f"attn_bwd{l}")
        dkb, dvb = dk.astype(BF16), dv.astype(BF16)
        gw["xattn_wq"] = mm(s["h1"], dq, "tn", BF16, f"dw_q{l}", tk=4096)
        gw["xattn_wk"] = mm(mem_n, dkb, "tn", BF16, f"dw_k{l}")
        gw["xattn_wv"] = mm(mem_n, dvb, "tn", BF16, f"dw_v{l}")
        dmem_n = dmem_n + mm(dkb, w["xattn_wk"], "nt", F32, f"dmem_k{l}") \
            + mm(dvb, w["xattn_wv"], "nt", F32, f"dmem_v{l}")
        pre2_l = pre2
        if l == 0:
            flying0 = _split_start([gw[n] for n in big[2:]], recv0[2:], _scatter_piece(0), dmem_n, "scatter_start0a")
            pre2_l = pre2 + flying0[-1][0, 0]
        dx1, small["xattn_pre_norm"][l], dy1, small["mix_post_norm"][l] = _matmul_norm_bwd(
            dq, w["xattn_wq"], "nt", dx2, s["x1"], pre2_l, l, f"dh1_{l}", s["y1"], post1, l)
        dcat = mm(dy1, w["w_out"], "nt", F32, f"dcat{l}")
        gw["w_out"] = mm(s["cat"], dy1, "tn", BF16, f"dw_out{l}", tk=4096)
        dz, dmbd, dps, dwdw, dbdw, dlng, dlnb, dwsc = _mix_bwd(
            dcat, s["z"], s["c"], mbd, pscale3, wdw, lng3, lnb3, wsc, l, f"mix_bwd{l}")
        small["pool_maps"][l] = jnp.stack([dmbd[gi * pg:(gi + 1) * pg, gi * pg:(gi + 1) * pg]
                                           for gi in range(len(POOL_WINDOWS))])
        small["pool_scale"][l], small["conf_dw_w"][l], small["conf_dw_b"][l] = dps, dwdw, dbdw
        small["conf_ln_g"][l], small["conf_ln_b"][l], small["sconv_w"][l] = dlng, dlnb, dwsc
        gw["w_in"] = mm(dz, s["h"], "tn", BF16, f"dw_in{l}", tm=4096, tk=2048)
        if l > 0:
            dxn, small["mix_pre_norm"][l], dy3, small["ffn_post_norm"][l - 1] = _matmul_norm_bwd(
                dz, w["w_in"], "nn", dx1, s["x"], pre1, l, f"dh{l}", saved[l - 1]["y3"], post3, l - 1)
            if flying is not None:
                recvs = _split_wait(flying, dxn, _scatter_piece(l), f"scatter_wait{l + 1}")
            flying = _split_start([gw[n] for n in big], recvs, _scatter_piece(l - 1), dxn, f"scatter_start{l}")
        else:
            if flying is not None:
                recvs = _split_wait(flying, dx1, _scatter_piece(0), "scatter_wait1")
            flying = _split_start([gw[n] for n in big[:2]], recv0[:2], _scatter_piece(0), dx1, "scatter_start0b")
            dxn, small["mix_pre_norm"][l] = _matmul_norm_bwd(
                dz, w["w_in"], "nn", dx1, s["x"], pre1 + flying[-1][0, 0], l, "dh0")
    grad_x = _from_steps(dxn)[None]
    _, dg_mem = _norm_bwd(jnp.zeros(mem0.shape, F32), dmem_n, mem0, memg3, 0, "norm_bwd_mem")

    small_names = [n for n in names if n not in big]
    partial = {n: (dg_mem.reshape(d) if n == "mem_norm" else
                   jnp.stack([g.reshape(g.shape[-1]) if g.shape[0] == 1 and weights[n].ndim == 2 else g
                              for g in small[n]])) for n in small_names}
    slab = _pack([partial[n] for n in small_names])
    gathered = _allgather([slab[None]], "gather_small_grads")[0][0].reshape(N_DEV, *slab.shape)
    summed = dict(zip(small_names, _unpack(_sum_sources(gathered, "sum_small_grads"), [partial[n] for n in small_names])))
    grad = {}
    for n in small_names:
        g = summed[n]
        if n in taps:
            width = weights[n].shape[-1]
            g = lax.dynamic_slice_in_dim(g, me * width, width, axis=g.ndim - 1)
        grad[n] = g

    delta, new_m, new_v = {}, {}, {}
    wmv = {n: [row_shard(n, a[n]) for a in (weights, mom1, mom2)] for n in big}
    upper = {n: _adam_sharded(recv, 1, *wmv[n], 1, nl, f"adam_{n}") for n, recv in zip(big, recvs)} if nl > 1 else {}
    upd = _adam_flat(_pack([weights[n] for n in small_names]), _pack([grad[n] for n in small_names]),
                     _pack([mom1[n] for n in small_names]), _pack([mom2[n] for n in small_names]), "adam_small")
    for out, slab_o in zip((delta, new_m, new_v), upd):
        out.update(zip(small_names, _unpack(slab_o, [weights[n] for n in small_names])))
    done = sum(r[3][0, 0, :1] for r in upper.values()) + upd[0][0, :1]
    recv0 = _split_wait(flying, done, _scatter_piece(0), "scatter_wait0b") \
        + _split_wait(flying0, done, _scatter_piece(0), "scatter_wait0a")
    for n, recv in zip(big, recv0):
        res = _adam_sharded(recv, 0, *wmv[n], 0, 1, f"adam0_{n}", prev=upper.get(n))
        grad[n], delta[n], new_m[n], new_v[n] = (row_shard(n, r) for r in res)

    return (loss, grad_x, *[grad[n] for n in names], *[delta[n] for n in names],
            *[new_m[n] for n in names], *[new_v[n] for n in names])
```

```python
import jax
import jax.numpy as jnp
from jax import lax
from jax.experimental import pallas as pl
from jax.experimental.pallas import tpu as pltpu

F32, BF16 = jnp.float32, jnp.bfloat16
EPS = 1e-6
POOL_WINDOWS = (2, 4, 8, 16)
MAX_WINDOW = 16
CONF_TAPS, SHORT_TAPS = 31, 3
CONF_HALO, POOL_HALO, SHORT_HALO = 256, 128, 16
ROW_CHUNK = 64
HEADS = 4
N_DEV = 8
ADAM_LR, ADAM_B1, ADAM_B2, ADAM_EPS, ADAM_WD, ADAM_STEP = 0.001, 0.9, 0.999, 1e-08, 0.01, 10
VMEM_LIMIT_V7X = 56 * 2**20
MESH = pl.DeviceIdType.MESH
ANY = pl.BlockSpec(memory_space=pl.ANY)
HBM = pl.BlockSpec(memory_space=pltpu.HBM)
SEM = pl.BlockSpec(memory_space=pltpu.SEMAPHORE)
EFFECT = pltpu.SideEffectType.DATAFLOW_SIDE_EFFECTING

TILE_NORM, TILE_MIX, TILE_FFN, TILE_ATTN, TILE_ADAM = 256, 512, 256, 512, 352


def _params(*sem):
    return pltpu.CompilerParams(dimension_semantics=sem, vmem_limit_bytes=VMEM_LIMIT_V7X)


def _sig(x):
    return 1.0 / (1.0 + jnp.exp(-x))


def _rms(x):
    r = lax.rsqrt(jnp.mean(x * x, axis=-1, keepdims=True) + EPS)
    return x * r, r


def _rms_bwd(dout, g, n, r):
    dn = dout * g
    return r * (dn - n * jnp.mean(dn * n, axis=-1, keepdims=True))


def _rows(tt, c):
    return pl.BlockSpec((tt, c), lambda i: (i, 0))


def _whole(shape):
    return pl.BlockSpec(shape, lambda i: (0,) * len(shape))


def _layer(shape, l):
    return pl.BlockSpec((None,) + shape, lambda i: (l,) + (0,) * len(shape))


def _colsum(x):
    return jnp.sum(x, axis=0, keepdims=True)


_DIMS = {"nn": (((1,), (0,)), ((), ())), "nt": (((1,), (1,)), ((), ())), "tn": (((0,), (0,)), ((), ()))}


def _matmul(a, b, mode, out_dtype, name, *, tm, tn, tk, a_outer=True):
    if mode == "nn":
        (m, k), (k2, n) = a.shape, b.shape
    elif mode == "nt":
        (m, k), (n, k2) = a.shape, b.shape
    else:
        (k, m), (k2, n) = a.shape, b.shape
    assert k == k2, (name, a.shape, b.shape)
    tm, tn, tk = min(tm, m), min(tn, n), min(tk, k)
    assert m % tm == 0 and n % tn == 0 and k % tk == 0, (name, m, n, k, tm, tn, tk)
    gm, gn, gk = m // tm, n // tn, k // tk

    def ij(g0, g1):
        return (g0, g1) if a_outer else (g1, g0)

    def a_map(g0, g1, kk):
        i, _ = ij(g0, g1)
        return (kk, i) if mode == "tn" else (i, kk)

    def b_map(g0, g1, kk):
        _, j = ij(g0, g1)
        return (j, kk) if mode == "nt" else (kk, j)

    def o_map(g0, g1, kk):
        return ij(g0, g1)

    a_block = (tk, tm) if mode == "tn" else (tm, tk)
    b_block = (tn, tk) if mode == "nt" else (tk, tn)
    dims = _DIMS[mode]

    def body(a_ref, b_ref, o_ref, *acc):
        p = lax.dot_general(a_ref[...].astype(BF16), b_ref[...].astype(BF16), dims, preferred_element_type=F32)
        if gk == 1:
            o_ref[...] = p.astype(o_ref.dtype)
        else:
            kk = pl.program_id(2)

            @pl.when(kk == 0)
            def _():
                acc[0][...] = p

            @pl.when(kk > 0)
            def _():
                acc[0][...] += p

            @pl.when(kk == gk - 1)
            def _():
                o_ref[...] = acc[0][...].astype(o_ref.dtype)

    return pl.pallas_call(
        body, name=name, grid=(gm, gn, gk) if a_outer else (gn, gm, gk),
        in_specs=[pl.BlockSpec(a_block, a_map), pl.BlockSpec(b_block, b_map)],
        out_specs=pl.BlockSpec((tm, tn), o_map),
        out_shape=jax.ShapeDtypeStruct((m, n), out_dtype),
        scratch_shapes=[pltpu.VMEM((tm, tn), F32)] if gk > 1 else [],
        compiler_params=_params("parallel", "parallel", "arbitrary"),
    )(a, b)


def _prenorm(x, g3, l, name):
    t, d = x.shape
    tt = min(TILE_NORM, t)

    def body(x_ref, g_ref, h_ref):
        n, _ = _rms(x_ref[...])
        h_ref[...] = (n * g_ref[...]).astype(BF16)

    return pl.pallas_call(
        body, name=name, grid=(t // tt,),
        in_specs=[_rows(tt, d), _layer((1, d), l)], out_specs=_rows(tt, d),
        out_shape=jax.ShapeDtypeStruct((t, d), BF16), compiler_params=_params("parallel"),
    )(x, g3)


def _norm_bwd(dxn, dh, x_in, gpre3, l, name, y_prev=None, gpost3=None, l_prev=None):
    t, d = x_in.shape
    tt = min(TILE_NORM, t)
    has_prev = y_prev is not None

    def body(*refs):
        if has_prev:
            dxn_ref, dh_ref, x_ref, g_ref, y_ref, g2_ref, dx_ref, dg_ref, dy_ref, dg2_ref = refs
        else:
            dxn_ref, dh_ref, x_ref, g_ref, dx_ref, dg_ref = refs

        @pl.when(pl.program_id(0) == 0)
        def _():
            dg_ref[...] = jnp.zeros_like(dg_ref)
            if has_prev:
                dg2_ref[...] = jnp.zeros_like(dg2_ref)

        dh_v = dh_ref[...]
        n, r = _rms(x_ref[...])
        dx = dxn_ref[...] + _rms_bwd(dh_v, g_ref[...], n, r)
        dx_ref[...] = dx
        dg_ref[...] += _colsum(dh_v * n)
        if has_prev:
            n2, r2 = _rms(y_ref[...])
            dy_ref[...] = _rms_bwd(dx, g2_ref[...], n2, r2).astype(BF16)
            dg2_ref[...] += _colsum(dx * n2)

    in_specs = [_rows(tt, d), _rows(tt, d), _rows(tt, d), _layer((1, d), l)]
    out_specs = [_rows(tt, d), _whole((1, d))]
    out_shape = [jax.ShapeDtypeStruct((t, d), F32), jax.ShapeDtypeStruct((1, d), F32)]
    args = [dxn, dh, x_in, gpre3]
    if has_prev:
        in_specs += [_rows(tt, d), _layer((1, d), l_prev)]
        out_specs += [_rows(tt, d), _whole((1, d))]
        out_shape += [jax.ShapeDtypeStruct((t, d), BF16), jax.ShapeDtypeStruct((1, d), F32)]
        args += [y_prev, gpost3]
    return pl.pallas_call(
        body, name=name, grid=(t // tt,), in_specs=in_specs, out_specs=out_specs, out_shape=out_shape,
        compiler_params=_params("arbitrary"),
    )(*args)


def _halves(tt):
    return [slice(0, tt // 2), slice(tt // 2, tt)] if tt % 32 == 0 else [slice(0, tt)]


def _matmul_resnorm(a, w, x, gpost3, l, gnext3, l2, name, tm=512):
    t, k = a.shape
    d = w.shape[1]
    tm = min(tm, t)

    def body(a_ref, w_ref, x_ref, gp_ref, gn_ref, y_ref, xo_ref, h_ref):
        wv, gp, gn = w_ref[...], gp_ref[...], gn_ref[...]
        for rows in _halves(tm):
            y = jnp.dot(a_ref[rows, :], wv, preferred_element_type=F32)
            y_ref[rows, :] = y
            n, _ = _rms(y)
            xn = x_ref[rows, :] + n * gp
            xo_ref[rows, :] = xn
            n2, _ = _rms(xn)
            h_ref[rows, :] = (n2 * gn).astype(BF16)

    return pl.pallas_call(
        body, name=name, grid=(t // tm,),
        in_specs=[_rows(tm, k), _whole((k, d)), _rows(tm, d), _layer((1, d), l), _layer((1, d), l2)],
        out_specs=[_rows(tm, d)] * 3,
        out_shape=[jax.ShapeDtypeStruct((t, d), F32), jax.ShapeDtypeStruct((t, d), F32),
                   jax.ShapeDtypeStruct((t, d), BF16)],
        compiler_params=_params("parallel"),
    )(a, w, x, gpost3, gnext3)


def _matmul_loss(a, w, x, gpost3, l, target, name, tm=512):
    t, k = a.shape
    d = w.shape[1]
    tm = min(tm, t)

    def body(a_ref, w_ref, x_ref, g_ref, t_ref, dxn_ref, dy_ref, dg_ref, loss_ref):
        @pl.when(pl.program_id(0) == 0)
        def _():
            dg_ref[...] = jnp.zeros_like(dg_ref)
            loss_ref[...] = jnp.zeros_like(loss_ref)

        wv, g = w_ref[...], g_ref[...]
        for rows in _halves(tm):
            n, r = _rms(jnp.dot(a_ref[rows, :], wv, preferred_element_type=F32))
            diff = x_ref[rows, :] + n * g - t_ref[rows, :]
            loss_ref[...] += 0.5 * jnp.sum(jnp.mean(diff * diff, axis=-1, keepdims=True))
            dxn = diff * (1.0 / d)
            dxn_ref[rows, :] = dxn
            dy_ref[rows, :] = _rms_bwd(dxn, g, n, r).astype(BF16)
            dg_ref[...] += _colsum(dxn * n)

    return pl.pallas_call(
        body, name=name, grid=(t // tm,),
        in_specs=[_rows(tm, k), _whole((k, d)), _rows(tm, d), _layer((1, d), l), _rows(tm, d)],
        out_specs=[_rows(tm, d), _rows(tm, d), _whole((1, d)), _whole((1, 128))],
        out_shape=[jax.ShapeDtypeStruct((t, d), F32), jax.ShapeDtypeStruct((t, d), BF16),
                   jax.ShapeDtypeStruct((1, d), F32), jax.ShapeDtypeStruct((1, 128), F32)],
        compiler_params=_params("arbitrary"),
    )(a, w, x, gpost3, target)


def _matmul_norm_bwd(a, w, mode, dxn, x_in, gpre3, l, name, y_prev=None, gpost3=None, l_prev=None, tm=512):
    t, k = a.shape
    d = x_in.shape[1]
    tm = min(tm, t)
    has_prev = y_prev is not None
    dims = _DIMS[mode]

    def body(*refs):
        if has_prev:
            a_ref, w_ref, dxn_ref, x_ref, g_ref, y_ref, g2_ref, dx_ref, dg_ref, dy_ref, dg2_ref = refs
        else:
            a_ref, w_ref, dxn_ref, x_ref, g_ref, dx_ref, dg_ref = refs

        @pl.when(pl.program_id(0) == 0)
        def _():
            dg_ref[...] = jnp.zeros_like(dg_ref)
            if has_prev:
                dg2_ref[...] = jnp.zeros_like(dg2_ref)

        wv, g = w_ref[...], g_ref[...]
        for rows in _halves(tm):
            dh = lax.dot_general(a_ref[rows, :], wv, dims, preferred_element_type=F32)
            n, r = _rms(x_ref[rows, :])
            dx = dxn_ref[rows, :] + _rms_bwd(dh, g, n, r)
            dx_ref[rows, :] = dx
            dg_ref[...] += _colsum(dh * n)
            if has_prev:
                n2, r2 = _rms(y_ref[rows, :])
                dy_ref[rows, :] = _rms_bwd(dx, g2_ref[...], n2, r2).astype(BF16)
                dg2_ref[...] += _colsum(dx * n2)

    in_specs = [_rows(tm, k), _whole(w.shape), _rows(tm, d), _rows(tm, d), _layer((1, d), l)]
    out_specs = [_rows(tm, d), _whole((1, d))]
    out_shape = [jax.ShapeDtypeStruct((t, d), F32), jax.ShapeDtypeStruct((1, d), F32)]
    args = [a, w, dxn, x_in, gpre3]
    if has_prev:
        in_specs += [_rows(tm, d), _layer((1, d), l_prev)]
        out_specs += [_rows(tm, d), _whole((1, d))]
        out_shape += [jax.ShapeDtypeStruct((t, d), BF16), jax.ShapeDtypeStruct((1, d), F32)]
        args += [y_prev, gpost3]
    return pl.pallas_call(
        body, name=name, grid=(t // tm,), in_specs=in_specs, out_specs=out_specs, out_shape=out_shape,
        compiler_params=_params("arbitrary"),
    )(*args)


def _to_steps(a):
    t = a.shape[0]
    return a.reshape(8, t // 8, -1).transpose(1, 0, 2).reshape(a.shape)


def _from_steps(a):
    t = a.shape[0]
    return a.reshape(t // 8, 8, -1).transpose(1, 0, 2).reshape(a.shape)


def _al(v):
    return v if isinstance(v, int) else pl.multiple_of(v, 8)


def _chunks(n_rows, fn, unroll=1):
    def step(r, carry):
        fn(pl.multiple_of(r * ROW_CHUNK, ROW_CHUNK))
        return carry
    lax.fori_loop(0, n_rows // ROW_CHUNK, step, 0, unroll=unroll)


def _fold8(a):
    return a.reshape(a.shape[0] // 8, 8, a.shape[1]).sum(axis=0)


def _shift_down(a):
    row = lax.broadcasted_iota(jnp.int32, a.shape, 0)
    return jnp.where(row % 8 == 0, 0.0, pltpu.roll(a, 1, 0))


def _shift_up(a):
    row = lax.broadcasted_iota(jnp.int32, a.shape, 0)
    return jnp.where(row % 8 == 7, 0.0, pltpu.roll(a, a.shape[0] - 1, 0))


def _prev_block(h, c, tt, t):
    return pl.BlockSpec((h, c), lambda i: (jnp.where(i == 0, t // h - 1, i * (tt // h) - 1), 0))


def _next_block(h, c, tt, t):
    return pl.BlockSpec((h, c), lambda i: (jnp.where(i == t // tt - 1, 0, (i + 1) * (tt // h)), 0))


def _taps(w_ref, buf, start, taps, rc, lanes=slice(None)):
    acc = w_ref[0:1, lanes] * buf[pl.ds(_al(start), rc), lanes]
    for k in range(1, taps):
        acc = acc + w_ref[k:k + 1, lanes] * buf[pl.ds(_al(start + 8 * k), rc), lanes]
    return acc


def _taps_rev(w_ref, buf, start, taps, rc, lanes=slice(None)):
    acc = w_ref[0:1, lanes] * buf[pl.ds(_al(start + 8 * (taps - 1)), rc), lanes]
    for k in range(1, taps):
        acc = acc + w_ref[k:k + 1, lanes] * buf[pl.ds(_al(start + 8 * (taps - 1 - k)), rc), lanes]
    return acc


def _mix_dims(d):
    dp = d // 4
    dc = 3 * d // 8
    ds = d - dp - dc
    oa, og = dp, dp + dc
    ob = dp + 2 * dc
    oc, ox = ob + ds, ob + 2 * ds
    return dp, dc, ds, oa, og, ob, oc, ox, ox + ds


def _pool_consts(dp):
    win = jnp.repeat(jnp.asarray(POOL_WINDOWS, F32), dp // len(POOL_WINDOWS))[None, :]
    mask = (jnp.arange(MAX_WINDOW, dtype=F32)[:, None] < win).astype(F32)
    return mask, win


def _pool_count(row0, rc, dp, seg, wl):
    r = lax.broadcasted_iota(jnp.int32, (rc, dp), 0) + row0
    return jnp.minimum(((r & 7) * seg + (r >> 3) + 1).astype(F32), wl)


def _mix_fwd(z, mbd, pscale3, wdw, bdw3, lng3, lnb3, wsc, l, name):
    t, din = z.shape
    dp, dc, ds, oa, og, ob, oc, ox, din2 = _mix_dims(din * 8 // 17)
    assert din2 == din
    d = ob
    tt = min(TILE_MIX, t)
    hp, hc, hs, rc = POOL_HALO, CONF_HALO, SHORT_HALO, ROW_CHUNK
    assert tt % hc == 0 and t % tt == 0
    seg = t // 8
    pmask, wlane = _pool_consts(dp)

    def body(z_ref, zpa_ref, zpb_ref, mbd_ref, ps_ref, pmask_ref, wl_ref, wdw_ref, bdw_ref, lng_ref, lnb_ref, wsc_ref,
             cat_ref, c_ref, pbuf, vbuf, sbuf):
        i = pl.program_id(0)
        pbuf[0:hp, :] = zpa_ref[hc - hp:hc, 0:dp]

        def prev(r0):
            rows = pl.ds(r0, rc)
            vbuf[rows, :] = zpa_ref[rows, oa:oa + dc] * _sig(zpa_ref[rows, og:og + dc])
        _chunks(hc, prev)
        sbuf[0:hs, :] = zpb_ref[:, oc:oc + ds] * zpb_ref[:, ox:ox + ds]

        @pl.when(i == 0)
        def _():
            pbuf[0:hp, :] = _shift_down(pbuf[0:hp, :])
            vbuf[0:hc, :] = _shift_down(vbuf[0:hc, :])
            sbuf[0:hs, :] = _shift_down(sbuf[0:hs, :])

        mbd_v, ps, wl = mbd_ref[...], ps_ref[...], wl_ref[...]
        bdw, lng, lnb = bdw_ref[...], lng_ref[...], lnb_ref[...]

        def step(r0):
            rows = pl.ds(r0, rc)
            zp = z_ref[rows, 0:dp]
            pbuf[pl.ds(_al(hp + r0), rc), :] = zp
            vbuf[pl.ds(_al(hc + r0), rc), :] = z_ref[rows, oa:oa + dc] * _sig(z_ref[rows, og:og + dc])
            sbuf[pl.ds(_al(hs + r0), rc), :] = z_ref[rows, oc:oc + ds] * z_ref[rows, ox:ox + ds]
            pooled = _taps_rev(pmask_ref, pbuf, r0 + hp - 8 * (MAX_WINDOW - 1), MAX_WINDOW, rc)
            pooled = pooled / _pool_count(i * tt + r0, rc, dp, seg, wl) - zp
            pm = jnp.dot(pooled.astype(BF16), mbd_v, preferred_element_type=F32)
            cat_ref[rows, 0:dp] = (pm * ps).astype(BF16)
            c = _taps(wdw_ref, vbuf, r0 + hc - 8 * (CONF_TAPS - 1), CONF_TAPS, rc) + bdw
            c_ref[rows, :] = c
            xc = c - jnp.mean(c, axis=-1, keepdims=True)
            nrm = xc * lax.rsqrt(jnp.mean(xc * xc, axis=-1, keepdims=True) + EPS)
            yln = nrm * lng + lnb
            cat_ref[rows, dp:dp + dc] = (yln * _sig(yln)).astype(BF16)
            cv = _taps(wsc_ref, sbuf, r0 + hs - 8 * (SHORT_TAPS - 1), SHORT_TAPS, rc)
            cat_ref[rows, dp + dc:d] = (z_ref[rows, ob:ob + ds] * cv).astype(BF16)
        _chunks(tt, step, unroll=2)

    return pl.pallas_call(
        body, name=name, grid=(t // tt,),
        in_specs=[_rows(tt, din), _prev_block(hc, d, tt, t), _prev_block(hs, din, tt, t),
                  _layer((dp, dp), l), _layer((1, dp), l), _whole((MAX_WINDOW, dp)), _whole((1, dp)),
                  _layer((CONF_TAPS, dc), l), _layer((1, dc), l), _layer((1, dc), l), _layer((1, dc), l),
                  _layer((SHORT_TAPS, ds), l)],
        out_specs=[_rows(tt, d), _rows(tt, dc)],
        out_shape=[jax.ShapeDtypeStruct((t, d), BF16), jax.ShapeDtypeStruct((t, dc), F32)],
        scratch_shapes=[pltpu.VMEM((hp + tt, dp), F32), pltpu.VMEM((hc + tt, dc), F32), pltpu.VMEM((hs + tt, ds), F32)],
        compiler_params=_params("parallel"),
    )(z, z, z, mbd, pscale3, pmask, wlane, wdw, bdw3, lng3, lnb3, wsc)


def _mix_bwd(dcat, z, c, mbd, pscale3, wdw, lng3, lnb3, wsc, l, name):
    t, din = z.shape
    dp, dc, ds, oa, og, ob, oc, ox, _ = _mix_dims(din * 8 // 17)
    d = ob
    tt = min(TILE_MIX, t)
    nt = t // tt
    hp, hc, hs, rc = POOL_HALO, CONF_HALO, SHORT_HALO, ROW_CHUNK
    assert tt % hc == 0 and t % tt == 0 and tt >= 8 * MAX_WINDOW
    seg = t // 8
    pmask, wlane = _pool_consts(dp)

    def body(dcat_ref, dcn_ref, z_ref, zpa_ref, zpb_ref, znb_ref, c_ref, cn_ref, mbd_ref, ps_ref, pmask_ref, wl_ref,
             wdw_ref, lng_ref, lnb_ref, wsc_ref,
             dz_ref, dmbd_ref, dps_ref, dwdw_ref, dbdw_ref, dlng_ref, dlnb_ref, dwsc_ref,
             pbuf, qbuf, dpbuf, pbf, vbuf, dcbuf, sbuf, dsbuf, dw8, ds8, ln8, ps8):
        i = pl.program_id(0)
        first, last = i == 0, i == nt - 1

        @pl.when(first)
        def _():
            for ref in (dmbd_ref, dw8, ds8, ln8, ps8):
                ref[...] = jnp.zeros_like(ref)

        mbd_v, ps, wl = mbd_ref[...], ps_ref[...], wl_ref[...]
        lng, lnb = lng_ref[...], lnb_ref[...]

        def ln_silu_bwd(cc, dyb):
            xc = cc - jnp.mean(cc, axis=-1, keepdims=True)
            rstd = lax.rsqrt(jnp.mean(xc * xc, axis=-1, keepdims=True) + EPS)
            nrm = xc * rstd
            yln = nrm * lng + lnb
            s = _sig(yln)
            dyln = dyb * (s * (1.0 + yln * (1.0 - s)))
            dn = dyln * lng
            dcc = rstd * (dn - jnp.mean(dn, axis=-1, keepdims=True) - nrm * jnp.mean(dn * nrm, axis=-1, keepdims=True))
            return dcc, dyln, nrm

        pbuf[0:hp, :] = zpa_ref[hc - hp:hc, 0:dp]

        def prev(r0):
            rows = pl.ds(r0, rc)
            vbuf[rows, :] = zpa_ref[rows, oa:oa + dc] * _sig(zpa_ref[rows, og:og + dc])
        _chunks(hc, prev)
        sbuf[0:hs, :] = zpb_ref[:, oc:oc + ds] * zpb_ref[:, ox:ox + ds]

        @pl.when(first)
        def _():
            pbuf[0:hp, :] = _shift_down(pbuf[0:hp, :])
            vbuf[0:hc, :] = _shift_down(vbuf[0:hc, :])
            sbuf[0:hs, :] = _shift_down(sbuf[0:hs, :])

        def nxt(r0):
            rows = pl.ds(r0, rc)
            dcc, _, _ = ln_silu_bwd(cn_ref[rows, :], dcn_ref[rows, dp:dp + dc])
            dcbuf[pl.ds(_al(tt + r0), rc), :] = dcc
        _chunks(hc, nxt, unroll=2)
        dpm_n = (dcn_ref[0:hp, 0:dp] * ps).astype(BF16)
        qbuf[tt:tt + hp, :] = lax.dot_general(dpm_n, mbd_v, _DIMS["nt"], preferred_element_type=F32) / wl
        dsbuf[tt:tt + hs, :] = dcn_ref[0:hs, dp + dc:d] * znb_ref[:, ob:ob + ds]

        @pl.when(last)
        def _():
            dcbuf[tt:tt + hc, :] = _shift_up(dcbuf[tt:tt + hc, :])
            qbuf[tt:tt + hp, :] = _shift_up(qbuf[tt:tt + hp, :])
            dsbuf[tt:tt + hs, :] = _shift_up(dsbuf[tt:tt + hs, :])

        def fill(r0):
            rows = pl.ds(r0, rc)
            zp = z_ref[rows, 0:dp]
            pbuf[pl.ds(_al(hp + r0), rc), :] = zp
            vbuf[pl.ds(_al(hc + r0), rc), :] = z_ref[rows, oa:oa + dc] * _sig(z_ref[rows, og:og + dc])
            sbuf[pl.ds(_al(hs + r0), rc), :] = z_ref[rows, oc:oc + ds] * z_ref[rows, ox:ox + ds]
            pooled = _taps_rev(pmask_ref, pbuf, r0 + hp - 8 * (MAX_WINDOW - 1), MAX_WINDOW, rc)
            pbf[rows, :] = (pooled / _pool_count(i * tt + r0, rc, dp, seg, wl) - zp).astype(BF16)
            dcc, dyln, nrm = ln_silu_bwd(c_ref[rows, :], dcat_ref[rows, dp:dp + dc])
            dcbuf[rows, :] = dcc
            ln8[0] += _fold8(dyln * nrm)
            ln8[1] += _fold8(dyln)
            ln8[2] += _fold8(dcc)
            dsbuf[rows, :] = dcat_ref[rows, dp + dc:d] * z_ref[rows, ob:ob + ds]
        _chunks(tt, fill, unroll=2)

        pb = pbf[...]
        dya = dcat_ref[:, 0:dp]
        ps8[...] += _fold8(dya * jnp.dot(pb, mbd_v, preferred_element_type=F32))
        dpm = (dya * ps).astype(BF16)
        dmbd_ref[...] += lax.dot_general(pb, dpm, _DIMS["tn"], preferred_element_type=F32)
        dpbuf[...] = lax.dot_general(dpm, mbd_v, _DIMS["nt"], preferred_element_type=F32)

        def quot(r0):
            rows = pl.ds(r0, rc)
            qbuf[rows, :] = dpbuf[rows, :] / _pool_count(i * tt + r0, rc, dp, seg, wl)
        _chunks(tt, quot)

        def back(r0):
            rows = pl.ds(r0, rc)
            dzp = _taps(pmask_ref, qbuf, r0, MAX_WINDOW, rc) - dpbuf[rows, :]
            dz_ref[rows, 0:dp] = dzp.astype(BF16)
            dcc = dcbuf[rows, :]
            for k in range(CONF_TAPS):
                dw8[k] += _fold8(dcc * vbuf[pl.ds(_al(r0 + hc - 8 * (CONF_TAPS - 1 - k)), rc), :])
            dv = _taps_rev(wdw_ref, dcbuf, r0, CONF_TAPS, rc)
            za = z_ref[rows, oa:oa + dc]
            sg = _sig(z_ref[rows, og:og + dc])
            dz_ref[rows, oa:oa + dc] = (dv * sg).astype(BF16)
            dz_ref[rows, og:og + dc] = (dv * za * sg * (1.0 - sg)).astype(BF16)
            cv = _taps(wsc_ref, sbuf, r0 + hs - 8 * (SHORT_TAPS - 1), SHORT_TAPS, rc)
            dz_ref[rows, ob:ob + ds] = (dcat_ref[rows, dp + dc:d] * cv).astype(BF16)
            dcv = dsbuf[rows, :]
            for k in range(SHORT_TAPS):
                ds8[k] += _fold8(dcv * sbuf[pl.ds(_al(r0 + hs - 8 * (SHORT_TAPS - 1 - k)), rc), :])
            dpv = _taps_rev(wsc_ref, dsbuf, r0, SHORT_TAPS, rc)
            dz_ref[rows, oc:oc + ds] = (dpv * z_ref[rows, ox:ox + ds]).astype(BF16)
            dz_ref[rows, ox:ox + ds] = (dpv * z_ref[rows, oc:oc + ds]).astype(BF16)
        _chunks(tt, back)

        @pl.when(last)
        def _():
            dps_ref[...] = jnp.sum(ps8[...], axis=0, keepdims=True)
            dwdw_ref[...] = jnp.sum(dw8[...], axis=1)
            dwsc_ref[...] = jnp.sum(ds8[...], axis=1)
            dlng_ref[...] = jnp.sum(ln8[0], axis=0, keepdims=True)
            dlnb_ref[...] = jnp.sum(ln8[1], axis=0, keepdims=True)
            dbdw_ref[...] = jnp.sum(ln8[2], axis=0, keepdims=True)

    return pl.pallas_call(
        body, name=name, grid=(nt,),
        in_specs=[_rows(tt, d), _next_block(hc, d, tt, t),
                  _rows(tt, din), _prev_block(hc, d, tt, t), _prev_block(hs, din, tt, t), _next_block(hs, din, tt, t),
                  _rows(tt, dc), _next_block(hc, dc, tt, t),
                  _layer((dp, dp), l), _layer((1, dp), l), _whole((MAX_WINDOW, dp)), _whole((1, dp)),
                  _layer((CONF_TAPS, dc), l), _layer((1, dc), l), _layer((1, dc), l), _layer((SHORT_TAPS, ds), l)],
        out_specs=[_rows(tt, din), _whole((dp, dp)), _whole((1, dp)), _whole((CONF_TAPS, dc)), _whole((1, dc)),
                   _whole((1, dc)), _whole((1, dc)), _whole((SHORT_TAPS, ds))],
        out_shape=[jax.ShapeDtypeStruct((t, din), BF16), jax.ShapeDtypeStruct((dp, dp), F32),
                   jax.ShapeDtypeStruct((1, dp), F32), jax.ShapeDtypeStruct((CONF_TAPS, dc), F32),
                   jax.ShapeDtypeStruct((1, dc), F32), jax.ShapeDtypeStruct((1, dc), F32),
                   jax.ShapeDtypeStruct((1, dc), F32), jax.ShapeDtypeStruct((SHORT_TAPS, ds), F32)],
        scratch_shapes=[pltpu.VMEM((hp + tt, dp), F32), pltpu.VMEM((tt + hp, dp), F32), pltpu.VMEM((tt, dp), F32),
                        pltpu.VMEM((tt, dp), BF16), pltpu.VMEM((hc + tt, dc), F32), pltpu.VMEM((tt + hc, dc), F32),
                        pltpu.VMEM((hs + tt, ds), F32), pltpu.VMEM((tt + hs, ds), F32),
                        pltpu.VMEM((CONF_TAPS, 8, dc), F32), pltpu.VMEM((SHORT_TAPS, 8, ds), F32),
                        pltpu.VMEM((3, 8, dc), F32), pltpu.VMEM((8, dp), F32)],
        compiler_params=_params("arbitrary"),
    )(dcat, dcat, z, z, z, z, c, c, mbd, pscale3, pmask, wlane, wdw, lng3, lnb3, wsc)


def _softmax_rows(qh, kh, scale):
    s = lax.dot_general(qh, kh, _DIMS["nt"], preferred_element_type=F32) * scale
    e = jnp.exp(s - jnp.max(s, axis=-1, keepdims=True))
    return e / jnp.sum(e, axis=-1, keepdims=True)


def _attn_fwd(q, k, v, name):
    t, d = q.shape
    m = k.shape[0]
    hd = d // HEADS
    scale = hd ** -0.5
    tt = min(TILE_ATTN, t)

    def body(q_ref, k_ref, v_ref, o_ref):
        for h in range(HEADS):
            sl = slice(h * hd, (h + 1) * hd)
            p = _softmax_rows(q_ref[:, sl], k_ref[:, sl], scale)
            o_ref[:, sl] = jnp.dot(p.astype(BF16), v_ref[:, sl], preferred_element_type=F32).astype(BF16)

    return pl.pallas_call(
        body, name=name, grid=(t // tt,),
        in_specs=[_rows(tt, d), _whole((m, d)), _whole((m, d))], out_specs=_rows(tt, d),
        out_shape=jax.ShapeDtypeStruct((t, d), BF16), compiler_params=_params("parallel"),
    )(q, k, v)


def _attn_bwd(q, k, v, do, name):
    t, d = q.shape
    m = k.shape[0]
    hd = d // HEADS
    scale = hd ** -0.5
    tt = min(TILE_ATTN, t)

    def body(q_ref, k_ref, v_ref, do_ref, dq_ref, dk_ref, dv_ref):
        @pl.when(pl.program_id(0) == 0)
        def _():
            dk_ref[...] = jnp.zeros_like(dk_ref)
            dv_ref[...] = jnp.zeros_like(dv_ref)

        for h in range(HEADS):
            sl = slice(h * hd, (h + 1) * hd)
            qh, kh, vh, doh = q_ref[:, sl], k_ref[:, sl], v_ref[:, sl], do_ref[:, sl]
            p = _softmax_rows(qh, kh, scale)
            dv_ref[:, sl] += lax.dot_general(p.astype(BF16), doh, _DIMS["tn"], preferred_element_type=F32)
            dp = lax.dot_general(doh, vh, _DIMS["nt"], preferred_element_type=F32)
            ds = (p * (dp - jnp.sum(dp * p, axis=-1, keepdims=True)) * scale).astype(BF16)
            dq_ref[:, sl] = jnp.dot(ds, kh, preferred_element_type=F32).astype(BF16)
            dk_ref[:, sl] += lax.dot_general(ds, qh, _DIMS["tn"], preferred_element_type=F32)

    return pl.pallas_call(
        body, name=name, grid=(t // tt,),
        in_specs=[_rows(tt, d), _whole((m, d)), _whole((m, d)), _rows(tt, d)],
        out_specs=[_rows(tt, d), _whole((m, d)), _whole((m, d))],
        out_shape=[jax.ShapeDtypeStruct((t, d), BF16), jax.ShapeDtypeStruct((m, d), F32),
                   jax.ShapeDtypeStruct((m, d), F32)],
        compiler_params=_params("arbitrary"),
    )(q, k, v, do)


def _lane_chunks(f):
    w = 256 if f % 256 == 0 else 128 if f % 128 == 0 else f
    return [(c0, w) for c0 in range(0, f, w)]


def _ffn_act_fwd(u, wc, l, name):
    t, f2 = u.shape
    f = f2 // 2
    tt = min(TILE_FFN, t)
    hs, rc = SHORT_HALO, ROW_CHUNK
    lanes = _lane_chunks(f)

    def body(u_ref, up_ref, wc_ref, a_ref, ubuf):
        ubuf[0:hs, :] = up_ref[...]

        @pl.when(pl.program_id(0) == 0)
        def _():
            ubuf[0:hs, :] = _shift_down(ubuf[0:hs, :])

        def step(r0):
            rows = pl.ds(r0, rc)
            ubuf[pl.ds(_al(hs + r0), rc), :] = u_ref[rows, :]
            start = r0 + hs - 8 * (SHORT_TAPS - 1)
            for c0, cw in lanes:
                g = _taps(wc_ref, ubuf, start, SHORT_TAPS, rc, slice(c0, c0 + cw))
                vv = _taps(wc_ref, ubuf, start, SHORT_TAPS, rc, slice(f + c0, f + c0 + cw))
                a_ref[rows, c0:c0 + cw] = (g * _sig(g) * vv).astype(BF16)
        _chunks(tt, step)

    return pl.pallas_call(
        body, name=name, grid=(t // tt,),
        in_specs=[_rows(tt, f2), _prev_block(hs, f2, tt, t), _layer((SHORT_TAPS, f2), l)],
        out_specs=_rows(tt, f), out_shape=jax.ShapeDtypeStruct((t, f), BF16),
        scratch_shapes=[pltpu.VMEM((hs + tt, f2), F32)], compiler_params=_params("parallel"),
    )(u, u, wc)


def _ffn_act_bwd(u, da, wc, l, name):
    t, f2 = u.shape
    f = f2 // 2
    tt = min(TILE_FFN, t)
    nt = t // tt
    hs, rc = SHORT_HALO, ROW_CHUNK
    lanes = _lane_chunks(f)

    def body(u_ref, up_ref, un_ref, da_ref, dan_ref, wc_ref, du_ref, dwc_ref, ubuf, danbuf, dbuf, dw8):
        i = pl.program_id(0)
        first, last = i == 0, i == nt - 1
        ubuf[0:hs, :] = up_ref[...]
        ubuf[hs + tt:hs + tt + hs, :] = un_ref[...]
        danbuf[...] = dan_ref[...]

        @pl.when(first)
        def _():
            dw8[...] = jnp.zeros_like(dw8)
            ubuf[0:hs, :] = _shift_down(ubuf[0:hs, :])

        @pl.when(last)
        def _():
            ubuf[hs + tt:hs + tt + hs, :] = _shift_up(ubuf[hs + tt:hs + tt + hs, :])
            danbuf[...] = _shift_up(danbuf[...])

        def fill(r0):
            ubuf[pl.ds(_al(hs + r0), rc), :] = u_ref[pl.ds(r0, rc), :]
        _chunks(tt, fill)

        def conv_grads(r0, n, da_rows):
            start = r0 + hs - 8 * (SHORT_TAPS - 1)
            for c0, cw in lanes:
                sl_g, sl_v = slice(c0, c0 + cw), slice(f + c0, f + c0 + cw)
                g = _taps(wc_ref, ubuf, start, SHORT_TAPS, n, sl_g)
                vv = _taps(wc_ref, ubuf, start, SHORT_TAPS, n, sl_v)
                dav = da_rows(c0, cw)
                sg = _sig(g)
                dbuf[pl.ds(_al(r0), n), sl_g] = dav * vv * (sg * (1.0 + g * (1.0 - sg)))
                dbuf[pl.ds(_al(r0), n), sl_v] = dav * (g * sg)

        _chunks(tt, lambda r0: conv_grads(r0, rc, lambda c0, cw: da_ref[pl.ds(r0, rc), c0:c0 + cw]))
        conv_grads(tt, hs, lambda c0, cw: danbuf[:, c0:c0 + cw])

        def back(r0):
            rows = pl.ds(r0, rc)
            for c0, cw in lanes:
                for off in (c0, f + c0):
                    sl = slice(off, off + cw)
                    du_ref[rows, sl] = _taps_rev(wc_ref, dbuf, r0, SHORT_TAPS, rc, sl).astype(BF16)
                    dd = dbuf[rows, sl]
                    for k in range(SHORT_TAPS):
                        dw8[k, :, sl] += _fold8(dd * ubuf[pl.ds(_al(r0 + hs - 8 * (SHORT_TAPS - 1 - k)), rc), sl])
        _chunks(tt, back)

        @pl.when(last)
        def _():
            dwc_ref[...] = jnp.sum(dw8[...], axis=1)

    return pl.pallas_call(
        body, name=name, grid=(nt,),
        in_specs=[_rows(tt, f2), _prev_block(hs, f2, tt, t), _next_block(hs, f2, tt, t),
                  _rows(tt, f), _next_block(hs, f, tt, t), _layer((SHORT_TAPS, f2), l)],
        out_specs=[_rows(tt, f2), _whole((SHORT_TAPS, f2))],
        out_shape=[jax.ShapeDtypeStruct((t, f2), BF16), jax.ShapeDtypeStruct((SHORT_TAPS, f2), F32)],
        scratch_shapes=[pltpu.VMEM((hs + tt + hs, f2), F32), pltpu.VMEM((hs, f), F32), pltpu.VMEM((tt + hs, f2), F32),
                        pltpu.VMEM((SHORT_TAPS, 8, f2), F32)],
        compiler_params=_params("arbitrary"),
    )(u, u, u, da, da, wc)


def _place():
    return lax.axis_index("x"), lax.axis_index("y"), lax.axis_index("c")


def _flip(v, bit):
    return 1 - v if bit else v


def _peers(x, y, c):
    out = []
    for kk in range(1, N_DEV):
        px, py, pc = _flip(x, kk & 4), _flip(y, kk & 2), _flip(c, kk & 1)
        out.append((kk - 1, (px, py, pc), 4 * px + 2 * py + pc))
    return out


def _allgather(shards, name):
    nt = len(shards)

    def body(*refs):
        srcs, outs = refs[:nt], refs[nt:2 * nt]
        send_sems, recv_sems, local_sems = refs[2 * nt:]
        x, y, c = _place()
        me, sibling = (x, y, c), (x, y, 1 - c)
        chips = [(1 - x, y), (x, 1 - y), (1 - x, 1 - y)]

        def rows(ti, px, py, pc):
            r = srcs[ti].shape[1]
            return outs[ti].at[:, pl.ds((4 * px + 2 * py + pc) * r, r), :]

        def copy(ti, kk, block, to, src=None):
            return pltpu.make_async_remote_copy(
                src_ref=rows(ti, *block) if src is None else src, dst_ref=rows(ti, *block),
                send_sem=send_sems.at[ti, kk], recv_sem=recv_sems.at[ti, kk], device_id=to, device_id_type=MESH)

        mine = [pltpu.make_async_copy(srcs[ti], rows(ti, *me), local_sems.at[ti]) for ti in range(nt)]
        for cp in mine:
            cp.start()
        first = []
        for ti in range(nt):
            first.append(copy(ti, 0, me, sibling, src=srcs[ti]))
            first += [copy(ti, 1 + j, me, (*chip, c), src=srcs[ti]) for j, chip in enumerate(chips)]
        for cp in first:
            cp.start()
        passed = []
        for j, chip in enumerate(chips):
            for ti in range(nt):
                copy(ti, 1 + j, (*chip, c), me).wait_recv()
                fwd = copy(ti, 4 + j, (*chip, c), sibling)
                fwd.start()
                passed.append(fwd)
        for ti in range(nt):
            copy(ti, 0, sibling, me).wait_recv()
            for j, chip in enumerate(chips):
                copy(ti, 4 + j, (*chip, 1 - c), me).wait_recv()
        for cp in first + passed:
            cp.wait_send()
        for cp in mine:
            cp.wait()

    return pl.pallas_call(
        body, name=name,
        in_specs=[ANY] * nt, out_specs=[ANY] * nt,
        out_shape=[jax.ShapeDtypeStruct((s.shape[0], N_DEV * s.shape[1], s.shape[2]), s.dtype) for s in shards],
        scratch_shapes=[pltpu.SemaphoreType.DMA((nt, 7)), pltpu.SemaphoreType.DMA((nt, 7)),
                        pltpu.SemaphoreType.DMA((nt,))],
    )(*shards)


def _gather_piece(src, land, me, to):
    r = src.shape[0]
    return src, land.at[pl.ds(me * r, r), :]


def _scatter_piece(l):
    def piece(src, land, me, to):
        r = src.shape[0] // N_DEV
        return src.at[pl.ds(to * r, r), :], land.at[l, me]
    return piece


def _split_start(srcs, lands, piece, after, name):
    nt = len(srcs)

    def body(*refs):
        src_refs, land_refs = refs[:nt], refs[nt:2 * nt]
        send_sems, recv_sems, local_sems, token = refs[2 * nt + 1], refs[2 * nt + 2], refs[2 * nt + 3], refs[4 * nt + 4]
        x, y, c = _place()
        me = 4 * x + 2 * y + c
        for ti in range(nt):
            for slot, peer, flat in _peers(x, y, c):
                src, dst = piece(src_refs[ti], land_refs[ti], me, flat)
                pltpu.make_async_remote_copy(
                    src_ref=src, dst_ref=dst, send_sem=send_sems.at[7 * ti + slot], recv_sem=recv_sems.at[7 * ti + slot],
                    device_id=peer, device_id_type=MESH).start()
        for ti in range(nt):
            pltpu.make_async_copy(*piece(src_refs[ti], land_refs[ti], me, me), local_sems.at[ti]).start()
        token[...] = jnp.zeros_like(token)

    both = list(srcs) + list(lands)
    return pl.pallas_call(
        body, name=name,
        in_specs=[HBM] * (2 * nt) + [ANY],
        out_specs=[SEM, SEM, SEM] + [HBM] * (2 * nt) + [pl.BlockSpec(memory_space=pltpu.VMEM)],
        out_shape=[pltpu.SemaphoreType.DMA((7 * nt,)), pltpu.SemaphoreType.DMA((7 * nt,)), pltpu.SemaphoreType.DMA((nt,))]
        + [pltpu.HBM(a.shape, a.dtype) for a in both] + [jax.ShapeDtypeStruct((8, 128), F32)],
        input_output_aliases={i: i + 3 for i in range(2 * nt)},
        compiler_params=pltpu.CompilerParams(has_side_effects=EFFECT),
    )(*[pltpu.with_memory_space_constraint(a, pltpu.HBM) for a in both], after)


def _split_wait(started, after, piece, name):
    send_sems, recv_sems, local_sems, *both = started[:-1]
    nt = len(both) // 2

    def body(*refs):
        src_refs, land_refs = refs[:nt], refs[nt:2 * nt]
        send_ref, recv_ref, local_ref = refs[2 * nt], refs[2 * nt + 1], refs[2 * nt + 2]
        x, y, c = _place()
        me = 4 * x + 2 * y + c
        for ti in range(nt):
            src, dst = piece(src_refs[ti], land_refs[ti], me, me)
            for slot in range(N_DEV - 1):
                cp = pltpu.make_async_remote_copy(
                    src_ref=src, dst_ref=dst, send_sem=send_ref.at[7 * ti + slot], recv_sem=recv_ref.at[7 * ti + slot],
                    device_id=(x, y, c), device_id_type=MESH)
                cp.wait_send()
                cp.wait_recv()
            pltpu.make_async_copy(src, dst, local_ref.at[ti]).wait()

    outs = pl.pallas_call(
        body, name=name,
        in_specs=[HBM] * (2 * nt) + [SEM, SEM, SEM, ANY], out_specs=[HBM] * (2 * nt),
        out_shape=[pltpu.HBM(a.shape, a.dtype) for a in both],
        input_output_aliases={i: i for i in range(2 * nt)},
        compiler_params=pltpu.CompilerParams(has_side_effects=EFFECT),
    )(*both, send_sems, recv_sems, local_sems, after)
    return outs[nt:]


def _adam(w, g, m, v):
    m2 = ADAM_B1 * m + (1.0 - ADAM_B1) * g
    v2 = ADAM_B2 * v + (1.0 - ADAM_B2) * (g * g)
    m_hat = m2 / (1.0 - ADAM_B1 ** ADAM_STEP)
    v_hat = v2 / (1.0 - ADAM_B2 ** ADAM_STEP)
    return -ADAM_LR * (m_hat / (jnp.sqrt(v_hat) + ADAM_EPS) + ADAM_WD * w), m2, v2


def _adam_sharded(recv, recv_first, w, m, v, lo, hi, name, prev=None):
    nl, r, c = w.shape
    tr = max([rows for rows in range(16, min(r, TILE_ADAM) + 1, 16) if r % rows == 0] or [r])

    def body(recv_ref, w_ref, m_ref, v_ref, *rest):
        g_ref, d_ref, m2_ref, v2_ref = rest[-4:]
        g = recv_ref[0].astype(F32)
        for s in range(1, N_DEV):
            g = g + recv_ref[s].astype(F32)
        g_ref[...] = g
        d_ref[...], m2_ref[...], v2_ref[...] = _adam(w_ref[...], g, m_ref[...], v_ref[...])

    blk = pl.BlockSpec((None, tr, c), lambda li, i: (li + lo, i, 0))
    extra = [] if prev is None else list(prev)
    return pl.pallas_call(
        body, name=name, grid=(hi - lo, r // tr),
        in_specs=[pl.BlockSpec((None, N_DEV, tr, c), lambda li, i: (li + lo - recv_first, 0, i, 0)), blk, blk, blk]
        + [ANY] * len(extra),
        out_specs=[blk] * 4, out_shape=[jax.ShapeDtypeStruct((nl, r, c), F32)] * 4,
        input_output_aliases={4 + i: i for i in range(len(extra))},
        compiler_params=_params("parallel", "parallel"),
    )(recv, w, m, v, *extra)


def _sum_sources(parts, name):
    _, r, c = parts.shape

    def body(p_ref, o_ref):
        g = p_ref[0]
        for s in range(1, N_DEV):
            g = g + p_ref[s]
        o_ref[...] = g

    return pl.pallas_call(
        body, name=name, grid=(1,), in_specs=[_whole((N_DEV, r, c))], out_specs=_whole((r, c)),
        out_shape=jax.ShapeDtypeStruct((r, c), F32), compiler_params=_params("arbitrary"),
    )(parts)


def _adam_flat(w, g, m, v, name):
    r, c = w.shape

    def body(w_ref, g_ref, m_ref, v_ref, d_ref, m2_ref, v2_ref):
        d_ref[...], m2_ref[...], v2_ref[...] = _adam(w_ref[...], g_ref[...], m_ref[...], v_ref[...])

    return pl.pallas_call(
        body, name=name, grid=(1,), in_specs=[_whole((r, c))] * 4, out_specs=[_whole((r, c))] * 3,
        out_shape=[jax.ShapeDtypeStruct((r, c), F32)] * 3, compiler_params=_params("arbitrary"),
    )(w, g, m, v)


def _pack(arrays):
    flat = jnp.concatenate([a.reshape(-1).astype(F32) for a in arrays])
    rows = -(-flat.shape[0] // 1024) * 8
    return jnp.pad(flat, (0, rows * 128 - flat.shape[0])).reshape(rows, 128)


def _unpack(slab, like):
    flat = slab.reshape(-1)
    out, at = [], 0
    for a in like:
        out.append(flat[at:at + a.size].reshape(a.shape))
        at += a.size
    return out


def kernel(x, mem, mem_norm, mix_pre_norm, mix_post_norm, w_in, pool_maps, pool_scale, conf_dw_w, conf_dw_b, conf_ln_g, conf_ln_b, sconv_w, w_out, xattn_pre_norm, xattn_post_norm, xattn_wq, xattn_wk, xattn_wv, xattn_wo, ffn_pre_norm, ffn_post_norm, ffn_w_up, ffn_conv_w, ffn_w_down, loss_target, m_mem_norm, m_mix_pre_norm, m_mix_post_norm, m_w_in, m_pool_maps, m_pool_scale, m_conf_dw_w, m_conf_dw_b, m_conf_ln_g, m_conf_ln_b, m_sconv_w, m_w_out, m_xattn_pre_norm, m_xattn_post_norm, m_xattn_wq, m_xattn_wk, m_xattn_wv, m_xattn_wo, m_ffn_pre_norm, m_ffn_post_norm, m_ffn_w_up, m_ffn_conv_w, m_ffn_w_down, v_mem_norm, v_mix_pre_norm, v_mix_post_norm, v_w_in, v_pool_maps, v_pool_scale, v_conf_dw_w, v_conf_dw_b, v_conf_ln_g, v_conf_ln_b, v_sconv_w, v_w_out, v_xattn_pre_norm, v_xattn_post_norm, v_xattn_wq, v_xattn_wk, v_xattn_wv, v_xattn_wo, v_ffn_pre_norm, v_ffn_post_norm, v_ffn_w_up, v_ffn_conv_w, v_ffn_w_down):
    weights = dict(mem_norm=mem_norm, mix_pre_norm=mix_pre_norm, mix_post_norm=mix_post_norm, w_in=w_in, pool_maps=pool_maps, pool_scale=pool_scale, conf_dw_w=conf_dw_w, conf_dw_b=conf_dw_b, conf_ln_g=conf_ln_g, conf_ln_b=conf_ln_b, sconv_w=sconv_w, w_out=w_out, xattn_pre_norm=xattn_pre_norm, xattn_post_norm=xattn_post_norm, xattn_wq=xattn_wq, xattn_wk=xattn_wk, xattn_wv=xattn_wv, xattn_wo=xattn_wo, ffn_pre_norm=ffn_pre_norm, ffn_post_norm=ffn_post_norm, ffn_w_up=ffn_w_up, ffn_conv_w=ffn_conv_w, ffn_w_down=ffn_w_down)
    mom1 = dict(mem_norm=m_mem_norm, mix_pre_norm=m_mix_pre_norm, mix_post_norm=m_mix_post_norm, w_in=m_w_in, pool_maps=m_pool_maps, pool_scale=m_pool_scale, conf_dw_w=m_conf_dw_w, conf_dw_b=m_conf_dw_b, conf_ln_g=m_conf_ln_g, conf_ln_b=m_conf_ln_b, sconv_w=m_sconv_w, w_out=m_w_out, xattn_pre_norm=m_xattn_pre_norm, xattn_post_norm=m_xattn_post_norm, xattn_wq=m_xattn_wq, xattn_wk=m_xattn_wk, xattn_wv=m_xattn_wv, xattn_wo=m_xattn_wo, ffn_pre_norm=m_ffn_pre_norm, ffn_post_norm=m_ffn_post_norm, ffn_w_up=m_ffn_w_up, ffn_conv_w=m_ffn_conv_w, ffn_w_down=m_ffn_w_down)
    mom2 = dict(mem_norm=v_mem_norm, mix_pre_norm=v_mix_pre_norm, mix_post_norm=v_mix_post_norm, w_in=v_w_in, pool_maps=v_pool_maps, pool_scale=v_pool_scale, conf_dw_w=v_conf_dw_w, conf_dw_b=v_conf_dw_b, conf_ln_g=v_conf_ln_g, conf_ln_b=v_conf_ln_b, sconv_w=v_sconv_w, w_out=v_w_out, xattn_pre_norm=v_xattn_pre_norm, xattn_post_norm=v_xattn_post_norm, xattn_wq=v_xattn_wq, xattn_wk=v_xattn_wk, xattn_wv=v_xattn_wv, xattn_wo=v_xattn_wo, ffn_pre_norm=v_ffn_pre_norm, ffn_post_norm=v_ffn_post_norm, ffn_w_up=v_ffn_w_up, ffn_conv_w=v_ffn_conv_w, ffn_w_down=v_ffn_w_down)
    names = list(weights)

    nl, d = mix_pre_norm.shape
    x0, mem0, target = _to_steps(x[0]), mem[0], _to_steps(loss_target[0])
    dp, dc, ds, *_ = _mix_dims(d)
    pg = dp // len(POOL_WINDOWS)
    me = 4 * lax.axis_index("x") + 2 * lax.axis_index("y") + lax.axis_index("c")

    big = ["w_in", "w_out", "xattn_wq", "xattn_wk", "xattn_wv", "xattn_wo", "ffn_w_up", "ffn_w_down"]
    transposed = ("w_in", "ffn_w_up")

    def row_shard(n, a):
        return a.transpose(0, 2, 1) if n in transposed else a

    shards = [row_shard(n, weights[n]).astype(BF16) for n in big]
    taps = ["conf_dw_w", "sconv_w", "ffn_conv_w"]
    tap_slab = _pack([weights[n] for n in taps])
    win0, tap_all = _allgather([shards[0][0:1], tap_slab[None]], "gather_weights0")
    layer_w = [{"w_in": win0[0]}]

    def land(s):
        return lax.empty((N_DEV * s.shape[1], s.shape[2]), BF16)

    first_a = _split_start([s[0] for s in shards[1:6]], [land(s) for s in shards[1:6]], _gather_piece, win0,
                           "gather_start0a")
    first_b = _split_start([s[0] for s in shards[6:]], [land(s) for s in shards[6:]], _gather_piece, first_a[-1],
                           "gather_start0b")
    tap_all = tap_all[0].reshape(N_DEV, *tap_slab.shape)
    tap_parts = [_unpack(tap_all[p], [weights[n] for n in taps]) for p in range(N_DEV)]
    wdw, wsc, wcf = (jnp.concatenate([tap_parts[p][i] for p in range(N_DEV)], axis=-1) for i in range(3))

    def g3(a):
        return a.reshape(a.shape[0], 1, a.shape[-1])

    mbd = jnp.zeros((nl, dp, dp), F32)
    for gi in range(len(POOL_WINDOWS)):
        mbd = mbd.at[:, gi * pg:(gi + 1) * pg, gi * pg:(gi + 1) * pg].set(pool_maps[:, gi])
    mbd = mbd.astype(BF16)
    pre1, post1, pre2, post2, pre3, post3 = (g3(weights[n]) for n in (
        "mix_pre_norm", "mix_post_norm", "xattn_pre_norm", "xattn_post_norm", "ffn_pre_norm", "ffn_post_norm"))
    pscale3, bdw3, lng3, lnb3 = g3(pool_scale), g3(conf_dw_b), g3(conf_ln_g), g3(conf_ln_b)
    memg3 = mem_norm.reshape(1, 1, d)

    def mm(a, b, mode, dt, name, tm=2048, tn=1024, tk=1024, a_outer=True):
        return _matmul(a, b, mode, dt, name, tm=tm, tn=tn, tk=tk, a_outer=a_outer)

    mem_n = _prenorm(mem0, memg3, 0, "mem_norm")
    xs = x0
    h = _prenorm(xs, pre1, 0, "pre_norm0")
    saved = []
    for l in range(nl):
        ps_l = pscale3
        if l + 1 < nl:
            flying = _split_start([s[l + 1] for s in shards], [land(s) for s in shards], _gather_piece,
                                  xs if l else first_b[-1], f"gather_start{l + 1}")
            ps_l = pscale3 + flying[-1][0, 0]
        w = layer_w[l]
        s = {"x": xs, "h": h}
        s["z"] = mm(h, w["w_in"], "nt", F32, f"z{l}", tm=1024, tn=4096)
        s["cat"], s["c"] = _mix_fwd(s["z"], mbd, ps_l, wdw, bdw3, lng3, lnb3, wsc, l, f"mix_fwd{l}")
        if l == 0:
            w.update(zip(big[1:6], _split_wait(first_a, s["cat"], _gather_piece, "gather_wait0a")))
        s["y1"], s["x1"], s["h1"] = _matmul_resnorm(s["cat"], w["w_out"], xs, post1, l, pre2, l, f"y1_{l}")
        s["q"] = mm(s["h1"], w["xattn_wq"], "nn", BF16, f"q{l}")
        s["k"] = mm(mem_n, w["xattn_wk"], "nn", BF16, f"k{l}")
        s["v"] = mm(mem_n, w["xattn_wv"], "nn", BF16, f"v{l}")
        s["o"] = _attn_fwd(s["q"], s["k"], s["v"], f"attn_fwd{l}")
        s["y2"], s["x2"], s["h2"] = _matmul_resnorm(s["o"], w["xattn_wo"], s["x1"], post2, l, pre3, l, f"y2_{l}")
        if l == 0:
            w.update(zip(big[6:], _split_wait(first_b, s["h2"], _gather_piece, "gather_wait0b")))
        s["u"] = mm(s["h2"], w["ffn_w_up"], "nt", F32, f"u{l}", tn=1408, a_outer=False)
        s["a"] = _ffn_act_fwd(s["u"], wcf, l, f"ffn_act{l}")
        if l + 1 < nl:
            s["y3"], xs, h = _matmul_resnorm(s["a"], w["ffn_w_down"], s["x2"], post3, l, pre1, l + 1, f"y3_{l}")
            layer_w.append(dict(zip(big, _split_wait(flying, xs, _gather_piece, f"gather_wait{l + 1}"))))
        saved.append(s)

    last = saved[-1]
    dxn, dy3, dg_post3, loss_lanes = _matmul_loss(
        last["a"], layer_w[-1]["ffn_w_down"], last["x2"], post3, nl - 1, target, "y3_loss")
    loss = lax.psum(loss_lanes[0, 0], ("x", "y", "c"))

    recvs = [lax.empty((max(nl - 1, 1), N_DEV, s.shape[1], d), BF16) for s in shards]
    recv0 = [lax.empty((1, N_DEV, s.shape[1], d), BF16) for s in shards]
    small = {n: [None] * nl for n in names if n not in big and n != "mem_norm"}
    small["ffn_post_norm"][nl - 1] = dg_post3
    dmem_n = jnp.zeros(mem0.shape, F32)
    flying = None
    for l in reversed(range(nl)):
        s, w = saved[l], layer_w[l]
        wc_l = wcf if flying is None else wcf + flying[-1][0, 0]
        gw = {}
        da = mm(dy3, w["ffn_w_down"], "nt", F32, f"da{l}", tn=1408, a_outer=False)
        gw["ffn_w_down"] = mm(s["a"], dy3, "tn", BF16, f"dw_down{l}", tm=1408, tk=2048)
        du, small["ffn_conv_w"][l] = _ffn_act_bwd(s["u"], da, wc_l, l, f"ffn_act_bwd{l}")
        gw["ffn_w_up"] = mm(du, s["h2"], "tn", BF16, f"dw_up{l}", tm=1408, tk=2048)
        dx2, small["ffn_pre_norm"][l], dy2, small["xattn_post_norm"][l] = _matmul_norm_bwd(
            du, w["ffn_w_up"], "nn", dxn, s["x2"], pre3, l, f"dh2_{l}", s["y2"], post2, l, tm=256)
        do = mm(dy2, w["xattn_wo"], "nt", BF16, f"do{l}")
        gw["xattn_wo"] = mm(s["o"], dy2, "tn", BF16, f"dw_o{l}", tk=4096)
        dq, dk, dv = _attn_bwd(s["q"], s["k"], s["v"], do, f"attn_bwd{l}")
        dkb, dvb = dk.astype(BF16), dv.astype(BF16)
        gw["xattn_wq"] = mm(s["h1"], dq, "tn", BF16, f"dw_q{l}", tk=4096)
        gw["xattn_wk"] = mm(mem_n, dkb, "tn", BF16, f"dw_k{l}")
        gw["xattn_wv"] = mm(mem_n, dvb, "tn", BF16, f"dw_v{l}")
        dmem_n = dmem_n + mm(dkb, w["xattn_wk"], "nt", F32, f"dmem_k{l}") \
            + mm(dvb, w["xattn_wv"], "nt", F32, f"dmem_v{l}")
        pre2_l = pre2
        if l == 0:
            flying0 = _split_start([gw[n] for n in big[2:]], recv0[2:], _scatter_piece(0), dmem_n, "scatter_start0a")
            pre2_l = pre2 + flying0[-1][0, 0]
        dx1, small["xattn_pre_norm"][l], dy1, small["mix_post_norm"][l] = _matmul_norm_bwd(
            dq, w["xattn_wq"], "nt", dx2, s["x1"], pre2_l, l, f"dh1_{l}", s["y1"], post1, l)
        dcat = mm(dy1, w["w_out"], "nt", F32, f"dcat{l}")
        gw["w_out"] = mm(s["cat"], dy1, "tn", BF16, f"dw_out{l}", tk=4096)
        dz, dmbd, dps, dwdw, dbdw, dlng, dlnb, dwsc = _mix_bwd(
            dcat, s["z"], s["c"], mbd, pscale3, wdw, lng3, lnb3, wsc, l, f"mix_bwd{l}")
        small["pool_maps"][l] = jnp.stack([dmbd[gi * pg:(gi + 1) * pg, gi * pg:(gi + 1) * pg]
                                           for gi in range(len(POOL_WINDOWS))])
        small["pool_scale"][l], small["conf_dw_w"][l], small["conf_dw_b"][l] = dps, dwdw, dbdw
        small["conf_ln_g"][l], small["conf_ln_b"][l], small["sconv_w"][l] = dlng, dlnb, dwsc
        gw["w_in"] = mm(dz, s["h"], "tn", BF16, f"dw_in{l}", tm=4096, tk=2048)
        if l > 0:
            dxn, small["mix_pre_norm"][l], dy3, small["ffn_post_norm"][l - 1] = _matmul_norm_bwd(
                dz, w["w_in"], "nn", dx1, s["x"], pre1, l, f"dh{l}", saved[l - 1]["y3"], post3, l - 1)
            if flying is not None:
                recvs = _split_wait(flying, dxn, _scatter_piece(l), f"scatter_wait{l + 1}")
            flying = _split_start([gw[n] for n in big], recvs, _scatter_piece(l - 1), dxn, f"scatter_start{l}")
        else:
            if flying is not None:
                recvs = _split_wait(flying, dx1, _scatter_piece(0), "scatter_wait1")
            flying = _split_start([gw[n] for n in big[:2]], recv0[:2], _scatter_piece(0), dx1, "scatter_start0b")
            dxn, small["mix_pre_norm"][l] = _matmul_norm_bwd(
                dz, w["w_in"], "nn", dx1, s["x"], pre1 + flying[-1][0, 0], l, "dh0")
    grad_x = _from_steps(dxn)[None]
    _, dg_mem = _norm_bwd(jnp.zeros(mem0.shape, F32), dmem_n, mem0, memg3, 0, "norm_bwd_mem")

    small_names = [n for n in names if n not in big]
    partial = {n: (dg_mem.reshape(d) if n == "mem_norm" else
                   jnp.stack([g.reshape(g.shape[-1]) if g.shape[0] == 1 and weights[n].ndim == 2 else g
                              for g in small[n]])) for n in small_names}
    slab = _pack([partial[n] for n in small_names])
    gathered = _allgather([slab[None]], "gather_small_grads")[0][0].reshape(N_DEV, *slab.shape)
    summed = dict(zip(small_names, _unpack(_sum_sources(gathered, "sum_small_grads"), [partial[n] for n in small_names])))
    grad = {}
    for n in small_names:
        g = summed[n]
        if n in taps:
            width = weights[n].shape[-1]
            g = lax.dynamic_slice_in_dim(g, me * width, width, axis=g.ndim - 1)
        grad[n] = g

    delta, new_m, new_v = {}, {}, {}
    wmv = {n: [row_shard(n, a[n]) for a in (weights, mom1, mom2)] for n in big}
    upper = {n: _adam_sharded(recv, 1, *wmv[n], 1, nl, f"adam_{n}") for n, recv in zip(big, recvs)} if nl > 1 else {}
    upd = _adam_flat(_pack([weights[n] for n in small_names]), _pack([grad[n] for n in small_names]),
                     _pack([mom1[n] for n in small_names]), _pack([mom2[n] for n in small_names]), "adam_small")
    for out, slab_o in zip((delta, new_m, new_v), upd):
        out.update(zip(small_names, _unpack(slab_o, [weights[n] for n in small_names])))
    done = sum(r[3][0, 0, :1] for r in upper.values()) + upd[0][0, :1]
    recv0 = _split_wait(flying, done, _scatter_piece(0), "scatter_wait0b") \
        + _split_wait(flying0, done, _scatter_piece(0), "scatter_wait0a")
    for n, recv in zip(big, recv0):
        res = _adam_sharded(recv, 0, *wmv[n], 0, 1, f"adam0_{n}", prev=upper.get(n))
        grad[n], delta[n], new_m[n], new_v[n] = (row_shard(n, r) for r in res)

    return (loss, grad_x, *[grad[n] for n in names], *[delta[n] for n in names],
            *[new_m[n] for n in names], *[new_v[n] for n in names])
```

```python
import jax
import jax.numpy as jnp
from jax import lax
from jax.experimental import pallas as pl
from jax.experimental.pallas import tpu as pltpu

F32, BF16 = jnp.float32, jnp.bfloat16
EPS = 1e-6
POOL_WINDOWS = (2, 4, 8, 16)
MAX_WINDOW = 16
CONF_TAPS, SHORT_TAPS = 31, 3
CONF_HALO, POOL_HALO, SHORT_HALO = 256, 128, 16
ROW_CHUNK = 64
HEADS = 4
N_DEV = 8
ADAM_LR, ADAM_B1, ADAM_B2, ADAM_EPS, ADAM_WD, ADAM_STEP = 0.001, 0.9, 0.999, 1e-08, 0.01, 10
VMEM_LIMIT_V7X = 56 * 2**20
MESH = pl.DeviceIdType.MESH
ANY = pl.BlockSpec(memory_space=pl.ANY)
HBM = pl.BlockSpec(memory_space=pltpu.HBM)
SEM = pl.BlockSpec(memory_space=pltpu.SEMAPHORE)
EFFECT = pltpu.SideEffectType.DATAFLOW_SIDE_EFFECTING

TILE_NORM, TILE_MIX, TILE_FFN, TILE_ATTN, TILE_ADAM = 256, 512, 256, 512, 352


def _params(*sem):
    return pltpu.CompilerParams(dimension_semantics=sem, vmem_limit_bytes=VMEM_LIMIT_V7X)


def _sig(x):
    return 1.0 / (1.0 + jnp.exp(-x))


def _rms(x):
    r = lax.rsqrt(jnp.mean(x * x, axis=-1, keepdims=True) + EPS)
    return x * r, r


def _rms_bwd(dout, g, n, r):
    dn = dout * g
    return r * (dn - n * jnp.mean(dn * n, axis=-1, keepdims=True))


def _rows(tt, c):
    return pl.BlockSpec((tt, c), lambda i: (i, 0))


def _whole(shape):
    return pl.BlockSpec(shape, lambda i: (0,) * len(shape))


def _layer(shape, l):
    return pl.BlockSpec((None,) + shape, lambda i: (l,) + (0,) * len(shape))


def _colsum(x):
    return jnp.sum(x, axis=0, keepdims=True)


_DIMS = {"nn": (((1,), (0,)), ((), ())), "nt": (((1,), (1,)), ((), ())), "tn": (((0,), (0,)), ((), ()))}


def _matmul(a, b, mode, out_dtype, name, *, tm, tn, tk, a_outer=True):
    if mode == "nn":
        (m, k), (k2, n) = a.shape, b.shape
    elif mode == "nt":
        (m, k), (n, k2) = a.shape, b.shape
    else:
        (k, m), (k2, n) = a.shape, b.shape
    assert k == k2, (name, a.shape, b.shape)
    tm, tn, tk = min(tm, m), min(tn, n), min(tk, k)
    assert m % tm == 0 and n % tn == 0 and k % tk == 0, (name, m, n, k, tm, tn, tk)
    gm, gn, gk = m // tm, n // tn, k // tk

    def ij(g0, g1):
        return (g0, g1) if a_outer else (g1, g0)

    def a_map(g0, g1, kk):
        i, _ = ij(g0, g1)
        return (kk, i) if mode == "tn" else (i, kk)

    def b_map(g0, g1, kk):
        _, j = ij(g0, g1)
        return (j, kk) if mode == "nt" else (kk, j)

    def o_map(g0, g1, kk):
        return ij(g0, g1)

    a_block = (tk, tm) if mode == "tn" else (tm, tk)
    b_block = (tn, tk) if mode == "nt" else (tk, tn)
    dims = _DIMS[mode]

    def body(a_ref, b_ref, o_ref, *acc):
        p = lax.dot_general(a_ref[...].astype(BF16), b_ref[...].astype(BF16), dims, preferred_element_type=F32)
        if gk == 1:
            o_ref[...] = p.astype(o_ref.dtype)
        else:
            kk = pl.program_id(2)

            @pl.when(kk == 0)
            def _():
                acc[0][...] = p

            @pl.when(kk > 0)
            def _():
                acc[0][...] += p

            @pl.when(kk == gk - 1)
            def _():
                o_ref[...] = acc[0][...].astype(o_ref.dtype)

    return pl.pallas_call(
        body, name=name, grid=(gm, gn, gk) if a_outer else (gn, gm, gk),
        in_specs=[pl.BlockSpec(a_block, a_map), pl.BlockSpec(b_block, b_map)],
        out_specs=pl.BlockSpec((tm, tn), o_map),
        out_shape=jax.ShapeDtypeStruct((m, n), out_dtype),
        scratch_shapes=[pltpu.VMEM((tm, tn), F32)] if gk > 1 else [],
        compiler_params=_params("parallel", "parallel", "arbitrary"),
    )(a, b)


def _prenorm(x, g3, l, name):
    t, d = x.shape
    tt = min(TILE_NORM, t)

    def body(x_ref, g_ref, h_ref):
        n, _ = _rms(x_ref[...])
        h_ref[...] = (n * g_ref[...]).astype(BF16)

    return pl.pallas_call(
        body, name=name, grid=(t // tt,),
        in_specs=[_rows(tt, d), _layer((1, d), l)], out_specs=_rows(tt, d),
        out_shape=jax.ShapeDtypeStruct((t, d), BF16), compiler_params=_params("parallel"),
    )(x, g3)


def _norm_bwd(dxn, dh, x_in, gpre3, l, name, y_prev=None, gpost3=None, l_prev=None):
    t, d = x_in.shape
    tt = min(TILE_NORM, t)
    has_prev = y_prev is not None

    def body(*refs):
        if has_prev:
            dxn_ref, dh_ref, x_ref, g_ref, y_ref, g2_ref, dx_ref, dg_ref, dy_ref, dg2_ref = refs
        else:
            dxn_ref, dh_ref, x_ref, g_ref, dx_ref, dg_ref = refs

        @pl.when(pl.program_id(0) == 0)
        def _():
            dg_ref[...] = jnp.zeros_like(dg_ref)
            if has_prev:
                dg2_ref[...] = jnp.zeros_like(dg2_ref)

        dh_v = dh_ref[...]
        n, r = _rms(x_ref[...])
        dx = dxn_ref[...] + _rms_bwd(dh_v, g_ref[...], n, r)
        dx_ref[...] = dx
        dg_ref[...] += _colsum(dh_v * n)
        if has_prev:
            n2, r2 = _rms(y_ref[...])
            dy_ref[...] = _rms_bwd(dx, g2_ref[...], n2, r2).astype(BF16)
            dg2_ref[...] += _colsum(dx * n2)

    in_specs = [_rows(tt, d), _rows(tt, d), _rows(tt, d), _layer((1, d), l)]
    out_specs = [_rows(tt, d), _whole((1, d))]
    out_shape = [jax.ShapeDtypeStruct((t, d), F32), jax.ShapeDtypeStruct((1, d), F32)]
    args = [dxn, dh, x_in, gpre3]
    if has_prev:
        in_specs += [_rows(tt, d), _layer((1, d), l_prev)]
        out_specs += [_rows(tt, d), _whole((1, d))]
        out_shape += [jax.ShapeDtypeStruct((t, d), BF16), jax.ShapeDtypeStruct((1, d), F32)]
        args += [y_prev, gpost3]
    return pl.pallas_call(
        body, name=name, grid=(t // tt,), in_specs=in_specs, out_specs=out_specs, out_shape=out_shape,
        compiler_params=_params("arbitrary"),
    )(*args)


def _halves(tt):
    return [slice(0, tt // 2), slice(tt // 2, tt)] if tt % 32 == 0 else [slice(0, tt)]


def _matmul_resnorm(a, w, x, gpost3, l, gnext3, l2, name, tm=512):
    t, k = a.shape
    d = w.shape[1]
    tm = min(tm, t)

    def body(a_ref, w_ref, x_ref, gp_ref, gn_ref, y_ref, xo_ref, h_ref):
        wv, gp, gn = w_ref[...], gp_ref[...], gn_ref[...]
        for rows in _halves(tm):
            y = jnp.dot(a_ref[rows, :], wv, preferred_element_type=F32)
            y_ref[rows, :] = y
            n, _ = _rms(y)
            xn = x_ref[rows, :] + n * gp
            xo_ref[rows, :] = xn
            n2, _ = _rms(xn)
            h_ref[rows, :] = (n2 * gn).astype(BF16)

    return pl.pallas_call(
        body, name=name, grid=(t // tm,),
        in_specs=[_rows(tm, k), _whole((k, d)), _rows(tm, d), _layer((1, d), l), _layer((1, d), l2)],
        out_specs=[_rows(tm, d)] * 3,
        out_shape=[jax.ShapeDtypeStruct((t, d), F32), jax.ShapeDtypeStruct((t, d), F32),
                   jax.ShapeDtypeStruct((t, d), BF16)],
        compiler_params=_params("parallel"),
    )(a, w, x, gpost3, gnext3)


def _matmul_loss(a, w, x, gpost3, l, target, name, tm=512):
    t, k = a.shape
    d = w.shape[1]
    tm = min(tm, t)

    def body(a_ref, w_ref, x_ref, g_ref, t_ref, dxn_ref, dy_ref, dg_ref, loss_ref):
        @pl.when(pl.program_id(0) == 0)
        def _():
            dg_ref[...] = jnp.zeros_like(dg_ref)
            loss_ref[...] = jnp.zeros_like(loss_ref)

        wv, g = w_ref[...], g_ref[...]
        for rows in _halves(tm):
            n, r = _rms(jnp.dot(a_ref[rows, :], wv, preferred_element_type=F32))
            diff = x_ref[rows, :] + n * g - t_ref[rows, :]
            loss_ref[...] += 0.5 * jnp.sum(jnp.mean(diff * diff, axis=-1, keepdims=True))
            dxn = diff * (1.0 / d)
            dxn_ref[rows, :] = dxn
            dy_ref[rows, :] = _rms_bwd(dxn, g, n, r).astype(BF16)
            dg_ref[...] += _colsum(dxn * n)

    return pl.pallas_call(
        body, name=name, grid=(t // tm,),
        in_specs=[_rows(tm, k), _whole((k, d)), _rows(tm, d), _layer((1, d), l), _rows(tm, d)],
        out_specs=[_rows(tm, d), _rows(tm, d), _whole((1, d)), _whole((1, 128))],
        out_shape=[jax.ShapeDtypeStruct((t, d), F32), jax.ShapeDtypeStruct((t, d), BF16),
                   jax.ShapeDtypeStruct((1, d), F32), jax.ShapeDtypeStruct((1, 128), F32)],
        compiler_params=_params("arbitrary"),
    )(a, w, x, gpost3, target)


def _matmul_norm_bwd(a, w, mode, dxn, x_in, gpre3, l, name, y_prev=None, gpost3=None, l_prev=None, tm=512):
    t, k = a.shape
    d = x_in.shape[1]
    tm = min(tm, t)
    has_prev = y_prev is not None
    dims = _DIMS[mode]

    def body(*refs):
        if has_prev:
            a_ref, w_ref, dxn_ref, x_ref, g_ref, y_ref, g2_ref, dx_ref, dg_ref, dy_ref, dg2_ref = refs
        else:
            a_ref, w_ref, dxn_ref, x_ref, g_ref, dx_ref, dg_ref = refs

        @pl.when(pl.program_id(0) == 0)
        def _():
            dg_ref[...] = jnp.zeros_like(dg_ref)
            if has_prev:
                dg2_ref[...] = jnp.zeros_like(dg2_ref)

        wv, g = w_ref[...], g_ref[...]
        for rows in _halves(tm):
            dh = lax.dot_general(a_ref[rows, :], wv, dims, preferred_element_type=F32)
            n, r = _rms(x_ref[rows, :])
            dx = dxn_ref[rows, :] + _rms_bwd(dh, g, n, r)
            dx_ref[rows, :] = dx
            dg_ref[...] += _colsum(dh * n)
            if has_prev:
                n2, r2 = _rms(y_ref[rows, :])
                dy_ref[rows, :] = _rms_bwd(dx, g2_ref[...], n2, r2).astype(BF16)
                dg2_ref[...] += _colsum(dx * n2)

    in_specs = [_rows(tm, k), _whole(w.shape), _rows(tm, d), _rows(tm, d), _layer((1, d), l)]
    out_specs = [_rows(tm, d), _whole((1, d))]
    out_shape = [jax.ShapeDtypeStruct((t, d), F32), jax.ShapeDtypeStruct((1, d), F32)]
    args = [a, w, dxn, x_in, gpre3]
    if has_prev:
        in_specs += [_rows(tm, d), _layer((1, d), l_prev)]
        out_specs += [_rows(tm, d), _whole((1, d))]
        out_shape += [jax.ShapeDtypeStruct((t, d), BF16), jax.ShapeDtypeStruct((1, d), F32)]
        args += [y_prev, gpost3]
    return pl.pallas_call(
        body, name=name, grid=(t // tm,), in_specs=in_specs, out_specs=out_specs, out_shape=out_shape,
        compiler_params=_params("arbitrary"),
    )(*args)


def _to_steps(a):
    t = a.shape[0]
    return a.reshape(8, t // 8, -1).transpose(1, 0, 2).reshape(a.shape)


def _from_steps(a):
    t = a.shape[0]
    return a.reshape(t // 8, 8, -1).transpose(1, 0, 2).reshape(a.shape)


def _al(v):
    return v if isinstance(v, int) else pl.multiple_of(v, 8)


def _chunks(n_rows, fn, unroll=1):
    def step(r, carry):
        fn(pl.multiple_of(r * ROW_CHUNK, ROW_CHUNK))
        return carry
    lax.fori_loop(0, n_rows // ROW_CHUNK, step, 0, unroll=unroll)


def _fold8(a):
    return a.reshape(a.shape[0] // 8, 8, a.shape[1]).sum(axis=0)


def _shift_down(a):
    row = lax.broadcasted_iota(jnp.int32, a.shape, 0)
    return jnp.where(row % 8 == 0, 0.0, pltpu.roll(a, 1, 0))


def _shift_up(a):
    row = lax.broadcasted_iota(jnp.int32, a.shape, 0)
    return jnp.where(row % 8 == 7, 0.0, pltpu.roll(a, a.shape[0] - 1, 0))


def _prev_block(h, c, tt, t):
    return pl.BlockSpec((h, c), lambda i: (jnp.where(i == 0, t // h - 1, i * (tt // h) - 1), 0))


def _next_block(h, c, tt, t):
    return pl.BlockSpec((h, c), lambda i: (jnp.where(i == t // tt - 1, 0, (i + 1) * (tt // h)), 0))


def _taps(w_ref, buf, start, taps, rc, lanes=slice(None)):
    acc = w_ref[0:1, lanes] * buf[pl.ds(_al(start), rc), lanes]
    for k in range(1, taps):
        acc = acc + w_ref[k:k + 1, lanes] * buf[pl.ds(_al(start + 8 * k), rc), lanes]
    return acc


def _taps_rev(w_ref, buf, start, taps, rc, lanes=slice(None)):
    acc = w_ref[0:1, lanes] * buf[pl.ds(_al(start + 8 * (taps - 1)), rc), lanes]
    for k in range(1, taps):
        acc = acc + w_ref[k:k + 1, lanes] * buf[pl.ds(_al(start + 8 * (taps - 1 - k)), rc), lanes]
    return acc


def _mix_dims(d):
    dp = d // 4
    dc = 3 * d // 8
    ds = d - dp - dc
    oa, og = dp, dp + dc
    ob = dp + 2 * dc
    oc, ox = ob + ds, ob + 2 * ds
    return dp, dc, ds, oa, og, ob, oc, ox, ox + ds


def _pool_consts(dp):
    win = jnp.repeat(jnp.asarray(POOL_WINDOWS, F32), dp // len(POOL_WINDOWS))[None, :]
    mask = (jnp.arange(MAX_WINDOW, dtype=F32)[:, None] < win).astype(F32)
    return mask, win


def _pool_count(row0, rc, dp, seg, wl):
    r = lax.broadcasted_iota(jnp.int32, (rc, dp), 0) + row0
    return jnp.minimum(((r & 7) * seg + (r >> 3) + 1).astype(F32), wl)


def _mix_fwd(z, mbd, pscale3, wdw, bdw3, lng3, lnb3, wsc, l, name):
    t, din = z.shape
    dp, dc, ds, oa, og, ob, oc, ox, din2 = _mix_dims(din * 8 // 17)
    assert din2 == din
    d = ob
    tt = min(TILE_MIX, t)
    hp, hc, hs, rc = POOL_HALO, CONF_HALO, SHORT_HALO, ROW_CHUNK
    assert tt % hc == 0 and t % tt == 0
    seg = t // 8
    pmask, wlane = _pool_consts(dp)

    def body(z_ref, zpa_ref, zpb_ref, mbd_ref, ps_ref, pmask_ref, wl_ref, wdw_ref, bdw_ref, lng_ref, lnb_ref, wsc_ref,
             cat_ref, c_ref, pbuf, vbuf, sbuf):
        i = pl.program_id(0)
        pbuf[0:hp, :] = zpa_ref[hc - hp:hc, 0:dp]

        def prev(r0):
            rows = pl.ds(r0, rc)
            vbuf[rows, :] = zpa_ref[rows, oa:oa + dc] * _sig(zpa_ref[rows, og:og + dc])
        _chunks(hc, prev)
        sbuf[0:hs, :] = zpb_ref[:, oc:oc + ds] * zpb_ref[:, ox:ox + ds]

        @pl.when(i == 0)
        def _():
            pbuf[0:hp, :] = _shift_down(pbuf[0:hp, :])
            vbuf[0:hc, :] = _shift_down(vbuf[0:hc, :])
            sbuf[0:hs, :] = _shift_down(sbuf[0:hs, :])

        mbd_v, ps, wl = mbd_ref[...], ps_ref[...], wl_ref[...]
        bdw, lng, lnb = bdw_ref[...], lng_ref[...], lnb_ref[...]

        def step(r0):
            rows = pl.ds(r0, rc)
            zp = z_ref[rows, 0:dp]
            pbuf[pl.ds(_al(hp + r0), rc), :] = zp
            vbuf[pl.ds(_al(hc + r0), rc), :] = z_ref[rows, oa:oa + dc] * _sig(z_ref[rows, og:og + dc])
            sbuf[pl.ds(_al(hs + r0), rc), :] = z_ref[rows, oc:oc + ds] * z_ref[rows, ox:ox + ds]
            pooled = _taps_rev(pmask_ref, pbuf, r0 + hp - 8 * (MAX_WINDOW - 1), MAX_WINDOW, rc)
            pooled = pooled / _pool_count(i * tt + r0, rc, dp, seg, wl) - zp
            pm = jnp.dot(pooled.astype(BF16), mbd_v, preferred_element_type=F32)
            cat_ref[rows, 0:dp] = (pm * ps).astype(BF16)
            c = _taps(wdw_ref, vbuf, r0 + hc - 8 * (CONF_TAPS - 1), CONF_TAPS, rc) + bdw
            c_ref[rows, :] = c
            xc = c - jnp.mean(c, axis=-1, keepdims=True)
            nrm = xc * lax.rsqrt(jnp.mean(xc * xc, axis=-1, keepdims=True) + EPS)
            yln = nrm * lng + lnb
            cat_ref[rows, dp:dp + dc] = (yln * _sig(yln)).astype(BF16)
            cv = _taps(wsc_ref, sbuf, r0 + hs - 8 * (SHORT_TAPS - 1), SHORT_TAPS, rc)
            cat_ref[rows, dp + dc:d] = (z_ref[rows, ob:ob + ds] * cv).astype(BF16)
        _chunks(tt, step, unroll=2)

    return pl.pallas_call(
        body, name=name, grid=(t // tt,),
        in_specs=[_rows(tt, din), _prev_block(hc, d, tt, t), _prev_block(hs, din, tt, t),
                  _layer((dp, dp), l), _layer((1, dp), l), _whole((MAX_WINDOW, dp)), _whole((1, dp)),
                  _layer((CONF_TAPS, dc), l), _layer((1, dc), l), _layer((1, dc), l), _layer((1, dc), l),
                  _layer((SHORT_TAPS, ds), l)],
        out_specs=[_rows(tt, d), _rows(tt, dc)],
        out_shape=[jax.ShapeDtypeStruct((t, d), BF16), jax.ShapeDtypeStruct((t, dc), F32)],
        scratch_shapes=[pltpu.VMEM((hp + tt, dp), F32), pltpu.VMEM((hc + tt, dc), F32), pltpu.VMEM((hs + tt, ds), F32)],
        compiler_params=_params("parallel"),
    )(z, z, z, mbd, pscale3, pmask, wlane, wdw, bdw3, lng3, lnb3, wsc)


def _mix_bwd(dcat, z, c, mbd, pscale3, wdw, lng3, lnb3, wsc, l, name):
    t, din = z.shape
    dp, dc, ds, oa, og, ob, oc, ox, _ = _mix_dims(din * 8 // 17)
    d = ob
    tt = min(TILE_MIX, t)
    nt = t // tt
    hp, hc, hs, rc = POOL_HALO, CONF_HALO, SHORT_HALO, ROW_CHUNK
    assert tt % hc == 0 and t % tt == 0 and tt >= 8 * MAX_WINDOW
    seg = t // 8
    pmask, wlane = _pool_consts(dp)

    def body(dcat_ref, dcn_ref, z_ref, zpa_ref, zpb_ref, znb_ref, c_ref, cn_ref, mbd_ref, ps_ref, pmask_ref, wl_ref,
             wdw_ref, lng_ref, lnb_ref, wsc_ref,
             dz_ref, dmbd_ref, dps_ref, dwdw_ref, dbdw_ref, dlng_ref, dlnb_ref, dwsc_ref,
             pbuf, qbuf, dpbuf, pbf, vbuf, dcbuf, sbuf, dsbuf, dw8, ds8, ln8, ps8):
        i = pl.program_id(0)
        first, last = i == 0, i == nt - 1

        @pl.when(first)
        def _():
            for ref in (dmbd_ref, dw8, ds8, ln8, ps8):
                ref[...] = jnp.zeros_like(ref)

        mbd_v, ps, wl = mbd_ref[...], ps_ref[...], wl_ref[...]
        lng, lnb = lng_ref[...], lnb_ref[...]

        def ln_silu_bwd(cc, dyb):
            xc = cc - jnp.mean(cc, axis=-1, keepdims=True)
            rstd = lax.rsqrt(jnp.mean(xc * xc, axis=-1, keepdims=True) + EPS)
            nrm = xc * rstd
            yln = nrm * lng + lnb
            s = _sig(yln)
            dyln = dyb * (s * (1.0 + yln * (1.0 - s)))
            dn = dyln * lng
            dcc = rstd * (dn - jnp.mean(dn, axis=-1, keepdims=True) - nrm * jnp.mean(dn * nrm, axis=-1, keepdims=True))
            return dcc, dyln, nrm

        pbuf[0:hp, :] = zpa_ref[hc - hp:hc, 0:dp]

        def prev(r0):
            rows = pl.ds(r0, rc)
            vbuf[rows, :] = zpa_ref[rows, oa:oa + dc] * _sig(zpa_ref[rows, og:og + dc])
        _chunks(hc, prev)
        sbuf[0:hs, :] = zpb_ref[:, oc:oc + ds] * zpb_ref[:, ox:ox + ds]

        @pl.when(first)
        def _():
            pbuf[0:hp, :] = _shift_down(pbuf[0:hp, :])
            vbuf[0:hc, :] = _shift_down(vbuf[0:hc, :])
            sbuf[0:hs, :] = _shift_down(sbuf[0:hs, :])

        def nxt(r0):
            rows = pl.ds(r0, rc)
            dcc, _, _ = ln_silu_bwd(cn_ref[rows, :], dcn_ref[rows, dp:dp + dc])
            dcbuf[pl.ds(_al(tt + r0), rc), :] = dcc
        _chunks(hc, nxt, unroll=4)
        dpm_n = (dcn_ref[0:hp, 0:dp] * ps).astype(BF16)
        qbuf[tt:tt + hp, :] = lax.dot_general(dpm_n, mbd_v, _DIMS["nt"], preferred_element_type=F32) / wl
        dsbuf[tt:tt + hs, :] = dcn_ref[0:hs, dp + dc:d] * znb_ref[:, ob:ob + ds]

        @pl.when(last)
        def _():
            dcbuf[tt:tt + hc, :] = _shift_up(dcbuf[tt:tt + hc, :])
            qbuf[tt:tt + hp, :] = _shift_up(qbuf[tt:tt + hp, :])
            dsbuf[tt:tt + hs, :] = _shift_up(dsbuf[tt:tt + hs, :])

        def fill(r0):
            rows = pl.ds(r0, rc)
            zp = z_ref[rows, 0:dp]
            pbuf[pl.ds(_al(hp + r0), rc), :] = zp
            vbuf[pl.ds(_al(hc + r0), rc), :] = z_ref[rows, oa:oa + dc] * _sig(z_ref[rows, og:og + dc])
            sbuf[pl.ds(_al(hs + r0), rc), :] = z_ref[rows, oc:oc + ds] * z_ref[rows, ox:ox + ds]
            pooled = _taps_rev(pmask_ref, pbuf, r0 + hp - 8 * (MAX_WINDOW - 1), MAX_WINDOW, rc)
            pbf[rows, :] = (pooled / _pool_count(i * tt + r0, rc, dp, seg, wl) - zp).astype(BF16)
            dcc, dyln, nrm = ln_silu_bwd(c_ref[rows, :], dcat_ref[rows, dp:dp + dc])
            dcbuf[rows, :] = dcc
            ln8[0] += _fold8(dyln * nrm)
            ln8[1] += _fold8(dyln)
            ln8[2] += _fold8(dcc)
            dsbuf[rows, :] = dcat_ref[rows, dp + dc:d] * z_ref[rows, ob:ob + ds]
        _chunks(tt, fill, unroll=4)

        pb = pbf[...]
        dya = dcat_ref[:, 0:dp]
        ps8[...] += _fold8(dya * jnp.dot(pb, mbd_v, preferred_element_type=F32))
        dpm = (dya * ps).astype(BF16)
        dmbd_ref[...] += lax.dot_general(pb, dpm, _DIMS["tn"], preferred_element_type=F32)
        dpbuf[...] = lax.dot_general(dpm, mbd_v, _DIMS["nt"], preferred_element_type=F32)

        def quot(r0):
            rows = pl.ds(r0, rc)
            qbuf[rows, :] = dpbuf[rows, :] / _pool_count(i * tt + r0, rc, dp, seg, wl)
        _chunks(tt, quot)

        def back(r0):
            rows = pl.ds(r0, rc)
            dzp = _taps(pmask_ref, qbuf, r0, MAX_WINDOW, rc) - dpbuf[rows, :]
            dz_ref[rows, 0:dp] = dzp.astype(BF16)
            dcc = dcbuf[rows, :]
            for k in range(CONF_TAPS):
                dw8[k] += _fold8(dcc * vbuf[pl.ds(_al(r0 + hc - 8 * (CONF_TAPS - 1 - k)), rc), :])
            dv = _taps_rev(wdw_ref, dcbuf, r0, CONF_TAPS, rc)
            za = z_ref[rows, oa:oa + dc]
            sg = _sig(z_ref[rows, og:og + dc])
            dz_ref[rows, oa:oa + dc] = (dv * sg).astype(BF16)
            dz_ref[rows, og:og + dc] = (dv * za * sg * (1.0 - sg)).astype(BF16)
            cv = _taps(wsc_ref, sbuf, r0 + hs - 8 * (SHORT_TAPS - 1), SHORT_TAPS, rc)
            dz_ref[rows, ob:ob + ds] = (dcat_ref[rows, dp + dc:d] * cv).astype(BF16)
            dcv = dsbuf[rows, :]
            for k in range(SHORT_TAPS):
                ds8[k] += _fold8(dcv * sbuf[pl.ds(_al(r0 + hs - 8 * (SHORT_TAPS - 1 - k)), rc), :])
            dpv = _taps_rev(wsc_ref, dsbuf, r0, SHORT_TAPS, rc)
            dz_ref[rows, oc:oc + ds] = (dpv * z_ref[rows, ox:ox + ds]).astype(BF16)
            dz_ref[rows, ox:ox + ds] = (dpv * z_ref[rows, oc:oc + ds]).astype(BF16)
        _chunks(tt, back, unroll=2)

        @pl.when(last)
        def _():
            dps_ref[...] = jnp.sum(ps8[...], axis=0, keepdims=True)
            dwdw_ref[...] = jnp.sum(dw8[...], axis=1)
            dwsc_ref[...] = jnp.sum(ds8[...], axis=1)
            dlng_ref[...] = jnp.sum(ln8[0], axis=0, keepdims=True)
            dlnb_ref[...] = jnp.sum(ln8[1], axis=0, keepdims=True)
            dbdw_ref[...] = jnp.sum(ln8[2], axis=0, keepdims=True)

    return pl.pallas_call(
        body, name=name, grid=(nt,),
        in_specs=[_rows(tt, d), _next_block(hc, d, tt, t),
                  _rows(tt, din), _prev_block(hc, d, tt, t), _prev_block(hs, din, tt, t), _next_block(hs, din, tt, t),
                  _rows(tt, dc), _next_block(hc, dc, tt, t),
                  _layer((dp, dp), l), _layer((1, dp), l), _whole((MAX_WINDOW, dp)), _whole((1, dp)),
                  _layer((CONF_TAPS, dc), l), _layer((1, dc), l), _layer((1, dc), l), _layer((SHORT_TAPS, ds), l)],
        out_specs=[_rows(tt, din), _whole((dp, dp)), _whole((1, dp)), _whole((CONF_TAPS, dc)), _whole((1, dc)),
                   _whole((1, dc)), _whole((1, dc)), _whole((SHORT_TAPS, ds))],
        out_shape=[jax.ShapeDtypeStruct((t, din), BF16), jax.ShapeDtypeStruct((dp, dp), F32),
                   jax.ShapeDtypeStruct((1, dp), F32), jax.ShapeDtypeStruct((CONF_TAPS, dc), F32),
                   jax.ShapeDtypeStruct((1, dc), F32), jax.ShapeDtypeStruct((1, dc), F32),
                   jax.ShapeDtypeStruct((1, dc), F32), jax.ShapeDtypeStruct((SHORT_TAPS, ds), F32)],
        scratch_shapes=[pltpu.VMEM((hp + tt, dp), F32), pltpu.VMEM((tt + hp, dp), F32), pltpu.VMEM((tt, dp), F32),
                        pltpu.VMEM((tt, dp), BF16), pltpu.VMEM((hc + tt, dc), F32), pltpu.VMEM((tt + hc, dc), F32),
                        pltpu.VMEM((hs + tt, ds), F32), pltpu.VMEM((tt + hs, ds), F32),
                        pltpu.VMEM((CONF_TAPS, 8, dc), F32), pltpu.VMEM((SHORT_TAPS, 8, ds), F32),
                        pltpu.VMEM((3, 8, dc), F32), pltpu.VMEM((8, dp), F32)],
        compiler_params=_params("arbitrary"),
    )(dcat, dcat, z, z, z, z, c, c, mbd, pscale3, pmask, wlane, wdw, lng3, lnb3, wsc)


def _softmax_rows(qh, kh, scale):
    s = lax.dot_general(qh, kh, _DIMS["nt"], preferred_element_type=F32) * scale
    e = jnp.exp(s - jnp.max(s, axis=-1, keepdims=True))
    return e / jnp.sum(e, axis=-1, keepdims=True)


def _attn_fwd(q, k, v, name):
    t, d = q.shape
    m = k.shape[0]
    hd = d // HEADS
    scale = hd ** -0.5
    tt = min(TILE_ATTN, t)

    def body(q_ref, k_ref, v_ref, o_ref):
        for h in range(HEADS):
            sl = slice(h * hd, (h + 1) * hd)
            p = _softmax_rows(q_ref[:, sl], k_ref[:, sl], scale)
            o_ref[:, sl] = jnp.dot(p.astype(BF16), v_ref[:, sl], preferred_element_type=F32).astype(BF16)

    return pl.pallas_call(
        body, name=name, grid=(t // tt,),
        in_specs=[_rows(tt, d), _whole((m, d)), _whole((m, d))], out_specs=_rows(tt, d),
        out_shape=jax.ShapeDtypeStruct((t, d), BF16), compiler_params=_params("parallel"),
    )(q, k, v)


def _attn_bwd(q, k, v, do, name):
    t, d = q.shape
    m = k.shape[0]
    hd = d // HEADS
    scale = hd ** -0.5
    tt = min(TILE_ATTN, t)

    def body(q_ref, k_ref, v_ref, do_ref, dq_ref, dk_ref, dv_ref):
        @pl.when(pl.program_id(0) == 0)
        def _():
            dk_ref[...] = jnp.zeros_like(dk_ref)
            dv_ref[...] = jnp.zeros_like(dv_ref)

        for h in range(HEADS):
            sl = slice(h * hd, (h + 1) * hd)
            qh, kh, vh, doh = q_ref[:, sl], k_ref[:, sl], v_ref[:, sl], do_ref[:, sl]
            p = _softmax_rows(qh, kh, scale)
            dv_ref[:, sl] += lax.dot_general(p.astype(BF16), doh, _DIMS["tn"], preferred_element_type=F32)
            dp = lax.dot_general(doh, vh, _DIMS["nt"], preferred_element_type=F32)
            ds = (p * (dp - jnp.sum(dp * p, axis=-1, keepdims=True)) * scale).astype(BF16)
            dq_ref[:, sl] = jnp.dot(ds, kh, preferred_element_type=F32).astype(BF16)
            dk_ref[:, sl] += lax.dot_general(ds, qh, _DIMS["tn"], preferred_element_type=F32)

    return pl.pallas_call(
        body, name=name, grid=(t // tt,),
        in_specs=[_rows(tt, d), _whole((m, d)), _whole((m, d)), _rows(tt, d)],
        out_specs=[_rows(tt, d), _whole((m, d)), _whole((m, d))],
        out_shape=[jax.ShapeDtypeStruct((t, d), BF16), jax.ShapeDtypeStruct((m, d), F32),
                   jax.ShapeDtypeStruct((m, d), F32)],
        compiler_params=_params("arbitrary"),
    )(q, k, v, do)


def _lane_chunks(f):
    w = 256 if f % 256 == 0 else 128 if f % 128 == 0 else f
    return [(c0, w) for c0 in range(0, f, w)]


def _ffn_act_fwd(u, wc, l, name):
    t, f2 = u.shape
    f = f2 // 2
    tt = min(TILE_FFN, t)
    hs, rc = SHORT_HALO, ROW_CHUNK
    lanes = _lane_chunks(f)

    def body(u_ref, up_ref, wc_ref, a_ref, ubuf):
        ubuf[0:hs, :] = up_ref[...]

        @pl.when(pl.program_id(0) == 0)
        def _():
            ubuf[0:hs, :] = _shift_down(ubuf[0:hs, :])

        def step(r0):
            rows = pl.ds(r0, rc)
            ubuf[pl.ds(_al(hs + r0), rc), :] = u_ref[rows, :]
            start = r0 + hs - 8 * (SHORT_TAPS - 1)
            for c0, cw in lanes:
                g = _taps(wc_ref, ubuf, start, SHORT_TAPS, rc, slice(c0, c0 + cw))
                vv = _taps(wc_ref, ubuf, start, SHORT_TAPS, rc, slice(f + c0, f + c0 + cw))
                a_ref[rows, c0:c0 + cw] = (g * _sig(g) * vv).astype(BF16)
        _chunks(tt, step, unroll=2)

    return pl.pallas_call(
        body, name=name, grid=(t // tt,),
        in_specs=[_rows(tt, f2), _prev_block(hs, f2, tt, t), _layer((SHORT_TAPS, f2), l)],
        out_specs=_rows(tt, f), out_shape=jax.ShapeDtypeStruct((t, f), BF16),
        scratch_shapes=[pltpu.VMEM((hs + tt, f2), F32)], compiler_params=_params("parallel"),
    )(u, u, wc)


def _ffn_act_bwd(u, da, wc, l, name):
    t, f2 = u.shape
    f = f2 // 2
    tt = min(TILE_FFN, t)
    nt = t // tt
    hs, rc = SHORT_HALO, ROW_CHUNK
    lanes = _lane_chunks(f)

    def body(u_ref, up_ref, un_ref, da_ref, dan_ref, wc_ref, du_ref, dwc_ref, ubuf, danbuf, dbuf, dw8):
        i = pl.program_id(0)
        first, last = i == 0, i == nt - 1
        ubuf[0:hs, :] = up_ref[...]
        ubuf[hs + tt:hs + tt + hs, :] = un_ref[...]
        danbuf[...] = dan_ref[...]

        @pl.when(first)
        def _():
            dw8[...] = jnp.zeros_like(dw8)
            ubuf[0:hs, :] = _shift_down(ubuf[0:hs, :])

        @pl.when(last)
        def _():
            ubuf[hs + tt:hs + tt + hs, :] = _shift_up(ubuf[hs + tt:hs + tt + hs, :])
            danbuf[...] = _shift_up(danbuf[...])

        def fill(r0):
            ubuf[pl.ds(_al(hs + r0), rc), :] = u_ref[pl.ds(r0, rc), :]
        _chunks(tt, fill)

        def conv_grads(r0, n, da_rows):
            start = r0 + hs - 8 * (SHORT_TAPS - 1)
            for c0, cw in lanes:
                sl_g, sl_v = slice(c0, c0 + cw), slice(f + c0, f + c0 + cw)
                g = _taps(wc_ref, ubuf, start, SHORT_TAPS, n, sl_g)
                vv = _taps(wc_ref, ubuf, start, SHORT_TAPS, n, sl_v)
                dav = da_rows(c0, cw)
                sg = _sig(g)
                dbuf[pl.ds(_al(r0), n), sl_g] = dav * vv * (sg * (1.0 + g * (1.0 - sg)))
                dbuf[pl.ds(_al(r0), n), sl_v] = dav * (g * sg)

        _chunks(tt, lambda r0: conv_grads(r0, rc, lambda c0, cw: da_ref[pl.ds(r0, rc), c0:c0 + cw]), unroll=2)
        conv_grads(tt, hs, lambda c0, cw: danbuf[:, c0:c0 + cw])

        def back(r0):
            rows = pl.ds(r0, rc)
            for c0, cw in lanes:
                for off in (c0, f + c0):
                    sl = slice(off, off + cw)
                    du_ref[rows, sl] = _taps_rev(wc_ref, dbuf, r0, SHORT_TAPS, rc, sl).astype(BF16)
                    dd = dbuf[rows, sl]
                    for k in range(SHORT_TAPS):
                        dw8[k, :, sl] += _fold8(dd * ubuf[pl.ds(_al(r0 + hs - 8 * (SHORT_TAPS - 1 - k)), rc), sl])
        _chunks(tt, back, unroll=2)

        @pl.when(last)
        def _():
            dwc_ref[...] = jnp.sum(dw8[...], axis=1)

    return pl.pallas_call(
        body, name=name, grid=(nt,),
        in_specs=[_rows(tt, f2), _prev_block(hs, f2, tt, t), _next_block(hs, f2, tt, t),
                  _rows(tt, f), _next_block(hs, f, tt, t), _layer((SHORT_TAPS, f2), l)],
        out_specs=[_rows(tt, f2), _whole((SHORT_TAPS, f2))],
        out_shape=[jax.ShapeDtypeStruct((t, f2), BF16), jax.ShapeDtypeStruct((SHORT_TAPS, f2), F32)],
        scratch_shapes=[pltpu.VMEM((hs + tt + hs, f2), F32), pltpu.VMEM((hs, f), F32), pltpu.VMEM((tt + hs, f2), F32),
                        pltpu.VMEM((SHORT_TAPS, 8, f2), F32)],
        compiler_params=_params("arbitrary"),
    )(u, u, u, da, da, wc)


def _place():
    return lax.axis_index("x"), lax.axis_index("y"), lax.axis_index("c")


def _flip(v, bit):
    return 1 - v if bit else v


def _peers(x, y, c):
    out = []
    for kk in range(1, N_DEV):
        px, py, pc = _flip(x, kk & 4), _flip(y, kk & 2), _flip(c, kk & 1)
        out.append((kk - 1, (px, py, pc), 4 * px + 2 * py + pc))
    return out


def _allgather(shards, name):
    nt = len(shards)

    def body(*refs):
        srcs, outs = refs[:nt], refs[nt:2 * nt]
        send_sems, recv_sems, local_sems = refs[2 * nt:]
        x, y, c = _place()
        me, sibling = (x, y, c), (x, y, 1 - c)
        chips = [(1 - x, y), (x, 1 - y), (1 - x, 1 - y)]

        def rows(ti, px, py, pc):
            r = srcs[ti].shape[1]
            return outs[ti].at[:, pl.ds((4 * px + 2 * py + pc) * r, r), :]

        def copy(ti, kk, block, to, src=None):
            return pltpu.make_async_remote_copy(
                src_ref=rows(ti, *block) if src is None else src, dst_ref=rows(ti, *block),
                send_sem=send_sems.at[ti, kk], recv_sem=recv_sems.at[ti, kk], device_id=to, device_id_type=MESH)

        mine = [pltpu.make_async_copy(srcs[ti], rows(ti, *me), local_sems.at[ti]) for ti in range(nt)]
        for cp in mine:
            cp.start()
        first = []
        for ti in range(nt):
            first.append(copy(ti, 0, me, sibling, src=srcs[ti]))
            first += [copy(ti, 1 + j, me, (*chip, c), src=srcs[ti]) for j, chip in enumerate(chips)]
        for cp in first:
            cp.start()
        passed = []
        for j, chip in enumerate(chips):
            for ti in range(nt):
                copy(ti, 1 + j, (*chip, c), me).wait_recv()
                fwd = copy(ti, 4 + j, (*chip, c), sibling)
                fwd.start()
                passed.append(fwd)
        for ti in range(nt):
            copy(ti, 0, sibling, me).wait_recv()
            for j, chip in enumerate(chips):
                copy(ti, 4 + j, (*chip, 1 - c), me).wait_recv()
        for cp in first + passed:
            cp.wait_send()
        for cp in mine:
            cp.wait()

    return pl.pallas_call(
        body, name=name,
        in_specs=[ANY] * nt, out_specs=[ANY] * nt,
        out_shape=[jax.ShapeDtypeStruct((s.shape[0], N_DEV * s.shape[1], s.shape[2]), s.dtype) for s in shards],
        scratch_shapes=[pltpu.SemaphoreType.DMA((nt, 7)), pltpu.SemaphoreType.DMA((nt, 7)),
                        pltpu.SemaphoreType.DMA((nt,))],
    )(*shards)


def _gather_piece(src, land, me, to):
    r = src.shape[0]
    return src, land.at[pl.ds(me * r, r), :]


def _scatter_piece(l):
    def piece(src, land, me, to):
        r = src.shape[0] // N_DEV
        return src.at[pl.ds(to * r, r), :], land.at[l, me]
    return piece


def _split_start(srcs, lands, piece, after, name):
    nt = len(srcs)

    def body(*refs):
        src_refs, land_refs = refs[:nt], refs[nt:2 * nt]
        send_sems, recv_sems, local_sems, token = refs[2 * nt + 1], refs[2 * nt + 2], refs[2 * nt + 3], refs[4 * nt + 4]
        x, y, c = _place()
        me = 4 * x + 2 * y + c
        for ti in range(nt):
            for slot, peer, flat in _peers(x, y, c):
                src, dst = piece(src_refs[ti], land_refs[ti], me, flat)
                pltpu.make_async_remote_copy(
                    src_ref=src, dst_ref=dst, send_sem=send_sems.at[7 * ti + slot], recv_sem=recv_sems.at[7 * ti + slot],
                    device_id=peer, device_id_type=MESH).start()
        for ti in range(nt):
            pltpu.make_async_copy(*piece(src_refs[ti], land_refs[ti], me, me), local_sems.at[ti]).start()
        token[...] = jnp.zeros_like(token)

    both = list(srcs) + list(lands)
    return pl.pallas_call(
        body, name=name,
        in_specs=[HBM] * (2 * nt) + [ANY],
        out_specs=[SEM, SEM, SEM] + [HBM] * (2 * nt) + [pl.BlockSpec(memory_space=pltpu.VMEM)],
        out_shape=[pltpu.SemaphoreType.DMA((7 * nt,)), pltpu.SemaphoreType.DMA((7 * nt,)), pltpu.SemaphoreType.DMA((nt,))]
        + [pltpu.HBM(a.shape, a.dtype) for a in both] + [jax.ShapeDtypeStruct((8, 128), F32)],
        input_output_aliases={i: i + 3 for i in range(2 * nt)},
        compiler_params=pltpu.CompilerParams(has_side_effects=EFFECT),
    )(*[pltpu.with_memory_space_constraint(a, pltpu.HBM) for a in both], after)


def _split_wait(started, after, piece, name):
    send_sems, recv_sems, local_sems, *both = started[:-1]
    nt = len(both) // 2

    def body(*refs):
        src_refs, land_refs = refs[:nt], refs[nt:2 * nt]
        send_ref, recv_ref, local_ref = refs[2 * nt], refs[2 * nt + 1], refs[2 * nt + 2]
        x, y, c = _place()
        me = 4 * x + 2 * y + c
        for ti in range(nt):
            src, dst = piece(src_refs[ti], land_refs[ti], me, me)
            for slot in range(N_DEV - 1):
                cp = pltpu.make_async_remote_copy(
                    src_ref=src, dst_ref=dst, send_sem=send_ref.at[7 * ti + slot], recv_sem=recv_ref.at[7 * ti + slot],
                    device_id=(x, y, c), device_id_type=MESH)
                cp.wait_send()
                cp.wait_recv()
            pltpu.make_async_copy(src, dst, local_ref.at[ti]).wait()

    outs = pl.pallas_call(
        body, name=name,
        in_specs=[HBM] * (2 * nt) + [SEM, SEM, SEM, ANY], out_specs=[HBM] * (2 * nt),
        out_shape=[pltpu.HBM(a.shape, a.dtype) for a in both],
        input_output_aliases={i: i for i in range(2 * nt)},
        compiler_params=pltpu.CompilerParams(has_side_effects=EFFECT),
    )(*both, send_sems, recv_sems, local_sems, after)
    return outs[nt:]


def _adam(w, g, m, v):
    m2 = ADAM_B1 * m + (1.0 - ADAM_B1) * g
    v2 = ADAM_B2 * v + (1.0 - ADAM_B2) * (g * g)
    m_hat = m2 / (1.0 - ADAM_B1 ** ADAM_STEP)
    v_hat = v2 / (1.0 - ADAM_B2 ** ADAM_STEP)
    return -ADAM_LR * (m_hat / (jnp.sqrt(v_hat) + ADAM_EPS) + ADAM_WD * w), m2, v2


def _adam_sharded(recv, recv_first, w, m, v, lo, hi, name, prev=None):
    nl, r, c = w.shape
    tr = max([rows for rows in range(16, min(r, TILE_ADAM) + 1, 16) if r % rows == 0] or [r])

    def body(recv_ref, w_ref, m_ref, v_ref, *rest):
        g_ref, d_ref, m2_ref, v2_ref = rest[-4:]
        g = recv_ref[0].astype(F32)
        for s in range(1, N_DEV):
            g = g + recv_ref[s].astype(F32)
        g_ref[...] = g
        d_ref[...], m2_ref[...], v2_ref[...] = _adam(w_ref[...], g, m_ref[...], v_ref[...])

    blk = pl.BlockSpec((None, tr, c), lambda li, i: (li + lo, i, 0))
    extra = [] if prev is None else list(prev)
    return pl.pallas_call(
        body, name=name, grid=(hi - lo, r // tr),
        in_specs=[pl.BlockSpec((None, N_DEV, tr, c), lambda li, i: (li + lo - recv_first, 0, i, 0)), blk, blk, blk]
        + [ANY] * len(extra),
        out_specs=[blk] * 4, out_shape=[jax.ShapeDtypeStruct((nl, r, c), F32)] * 4,
        input_output_aliases={4 + i: i for i in range(len(extra))},
        compiler_params=_params("parallel", "parallel"),
    )(recv, w, m, v, *extra)


def _sum_sources(parts, name):
    _, r, c = parts.shape

    def body(p_ref, o_ref):
        g = p_ref[0]
        for s in range(1, N_DEV):
            g = g + p_ref[s]
        o_ref[...] = g

    return pl.pallas_call(
        body, name=name, grid=(1,), in_specs=[_whole((N_DEV, r, c))], out_specs=_whole((r, c)),
        out_shape=jax.ShapeDtypeStruct((r, c), F32), compiler_params=_params("arbitrary"),
    )(parts)


def _adam_flat(w, g, m, v, name):
    r, c = w.shape

    def body(w_ref, g_ref, m_ref, v_ref, d_ref, m2_ref, v2_ref):
        d_ref[...], m2_ref[...], v2_ref[...] = _adam(w_ref[...], g_ref[...], m_ref[...], v_ref[...])

    return pl.pallas_call(
        body, name=name, grid=(1,), in_specs=[_whole((r, c))] * 4, out_specs=[_whole((r, c))] * 3,
        out_shape=[jax.ShapeDtypeStruct((r, c), F32)] * 3, compiler_params=_params("arbitrary"),
    )(w, g, m, v)


def _pack(arrays):
    flat = jnp.concatenate([a.reshape(-1).astype(F32) for a in arrays])
    rows = -(-flat.shape[0] // 1024) * 8
    return jnp.pad(flat, (0, rows * 128 - flat.shape[0])).reshape(rows, 128)


def _unpack(slab, like):
    flat = slab.reshape(-1)
    out, at = [], 0
    for a in like:
        out.append(flat[at:at + a.size].reshape(a.shape))
        at += a.size
    return out


def kernel(x, mem, mem_norm, mix_pre_norm, mix_post_norm, w_in, pool_maps, pool_scale, conf_dw_w, conf_dw_b, conf_ln_g, conf_ln_b, sconv_w, w_out, xattn_pre_norm, xattn_post_norm, xattn_wq, xattn_wk, xattn_wv, xattn_wo, ffn_pre_norm, ffn_post_norm, ffn_w_up, ffn_conv_w, ffn_w_down, loss_target, m_mem_norm, m_mix_pre_norm, m_mix_post_norm, m_w_in, m_pool_maps, m_pool_scale, m_conf_dw_w, m_conf_dw_b, m_conf_ln_g, m_conf_ln_b, m_sconv_w, m_w_out, m_xattn_pre_norm, m_xattn_post_norm, m_xattn_wq, m_xattn_wk, m_xattn_wv, m_xattn_wo, m_ffn_pre_norm, m_ffn_post_norm, m_ffn_w_up, m_ffn_conv_w, m_ffn_w_down, v_mem_norm, v_mix_pre_norm, v_mix_post_norm, v_w_in, v_pool_maps, v_pool_scale, v_conf_dw_w, v_conf_dw_b, v_conf_ln_g, v_conf_ln_b, v_sconv_w, v_w_out, v_xattn_pre_norm, v_xattn_post_norm, v_xattn_wq, v_xattn_wk, v_xattn_wv, v_xattn_wo, v_ffn_pre_norm, v_ffn_post_norm, v_ffn_w_up, v_ffn_conv_w, v_ffn_w_down):
    weights = dict(mem_norm=mem_norm, mix_pre_norm=mix_pre_norm, mix_post_norm=mix_post_norm, w_in=w_in, pool_maps=pool_maps, pool_scale=pool_scale, conf_dw_w=conf_dw_w, conf_dw_b=conf_dw_b, conf_ln_g=conf_ln_g, conf_ln_b=conf_ln_b, sconv_w=sconv_w, w_out=w_out, xattn_pre_norm=xattn_pre_norm, xattn_post_norm=xattn_post_norm, xattn_wq=xattn_wq, xattn_wk=xattn_wk, xattn_wv=xattn_wv, xattn_wo=xattn_wo, ffn_pre_norm=ffn_pre_norm, ffn_post_norm=ffn_post_norm, ffn_w_up=ffn_w_up, ffn_conv_w=ffn_conv_w, ffn_w_down=ffn_w_down)
    mom1 = dict(mem_norm=m_mem_norm, mix_pre_norm=m_mix_pre_norm, mix_post_norm=m_mix_post_norm, w_in=m_w_in, pool_maps=m_pool_maps, pool_scale=m_pool_scale, conf_dw_w=m_conf_dw_w, conf_dw_b=m_conf_dw_b, conf_ln_g=m_conf_ln_g, conf_ln_b=m_conf_ln_b, sconv_w=m_sconv_w, w_out=m_w_out, xattn_pre_norm=m_xattn_pre_norm, xattn_post_norm=m_xattn_post_norm, xattn_wq=m_xattn_wq, xattn_wk=m_xattn_wk, xattn_wv=m_xattn_wv, xattn_wo=m_xattn_wo, ffn_pre_norm=m_ffn_pre_norm, ffn_post_norm=m_ffn_post_norm, ffn_w_up=m_ffn_w_up, ffn_conv_w=m_ffn_conv_w, ffn_w_down=m_ffn_w_down)
    mom2 = dict(mem_norm=v_mem_norm, mix_pre_norm=v_mix_pre_norm, mix_post_norm=v_mix_post_norm, w_in=v_w_in, pool_maps=v_pool_maps, pool_scale=v_pool_scale, conf_dw_w=v_conf_dw_w, conf_dw_b=v_conf_dw_b, conf_ln_g=v_conf_ln_g, conf_ln_b=v_conf_ln_b, sconv_w=v_sconv_w, w_out=v_w_out, xattn_pre_norm=v_xattn_pre_norm, xattn_post_norm=v_xattn_post_norm, xattn_wq=v_xattn_wq, xattn_wk=v_xattn_wk, xattn_wv=v_xattn_wv, xattn_wo=v_xattn_wo, ffn_pre_norm=v_ffn_pre_norm, ffn_post_norm=v_ffn_post_norm, ffn_w_up=v_ffn_w_up, ffn_conv_w=v_ffn_conv_w, ffn_w_down=v_ffn_w_down)
    names = list(weights)

    nl, d = mix_pre_norm.shape
    x0, mem0, target = _to_steps(x[0]), mem[0], _to_steps(loss_target[0])
    dp, dc, ds, *_ = _mix_dims(d)
    pg = dp // len(POOL_WINDOWS)
    me = 4 * lax.axis_index("x") + 2 * lax.axis_index("y") + lax.axis_index("c")

    big = ["w_in", "w_out", "xattn_wq", "xattn_wk", "xattn_wv", "xattn_wo", "ffn_w_up", "ffn_w_down"]
    transposed = ("w_in", "ffn_w_up")

    def row_shard(n, a):
        return a.transpose(0, 2, 1) if n in transposed else a

    shards = [row_shard(n, weights[n]).astype(BF16) for n in big]
    taps = ["conf_dw_w", "sconv_w", "ffn_conv_w"]
    tap_slab = _pack([weights[n] for n in taps])
    win0, tap_all = _allgather([shards[0][0:1], tap_slab[None]], "gather_weights0")
    layer_w = [{"w_in": win0[0]}]

    def land(s):
        return lax.empty((N_DEV * s.shape[1], s.shape[2]), BF16)

    first_a = _split_start([s[0] for s in shards[1:6]], [land(s) for s in shards[1:6]], _gather_piece, win0,
                           "gather_start0a")
    first_b = _split_start([s[0] for s in shards[6:]], [land(s) for s in shards[6:]], _gather_piece, first_a[-1],
                           "gather_start0b")
    tap_all = tap_all[0].reshape(N_DEV, *tap_slab.shape)
    tap_parts = [_unpack(tap_all[p], [weights[n] for n in taps]) for p in range(N_DEV)]
    wdw, wsc, wcf = (jnp.concatenate([tap_parts[p][i] for p in range(N_DEV)], axis=-1) for i in range(3))

    def g3(a):
        return a.reshape(a.shape[0], 1, a.shape[-1])

    mbd = jnp.zeros((nl, dp, dp), F32)
    for gi in range(len(POOL_WINDOWS)):
        mbd = mbd.at[:, gi * pg:(gi + 1) * pg, gi * pg:(gi + 1) * pg].set(pool_maps[:, gi])
    mbd = mbd.astype(BF16)
    pre1, post1, pre2, post2, pre3, post3 = (g3(weights[n]) for n in (
        "mix_pre_norm", "mix_post_norm", "xattn_pre_norm", "xattn_post_norm", "ffn_pre_norm", "ffn_post_norm"))
    pscale3, bdw3, lng3, lnb3 = g3(pool_scale), g3(conf_dw_b), g3(conf_ln_g), g3(conf_ln_b)
    memg3 = mem_norm.reshape(1, 1, d)

    def mm(a, b, mode, dt, name, tm=2048, tn=1024, tk=1024, a_outer=True):
        return _matmul(a, b, mode, dt, name, tm=tm, tn=tn, tk=tk, a_outer=a_outer)

    mem_n = _prenorm(mem0, memg3, 0, "mem_norm")
    xs = x0
    h = _prenorm(xs, pre1, 0, "pre_norm0")
    saved = []
    for l in range(nl):
        ps_l = pscale3
        if l + 1 < nl:
            flying = _split_start([s[l + 1] for s in shards], [land(s) for s in shards], _gather_piece,
                                  xs if l else first_b[-1], f"gather_start{l + 1}")
            ps_l = pscale3 + flying[-1][0, 0]
        w = layer_w[l]
        s = {"x": xs, "h": h}
        s["z"] = mm(h, w["w_in"], "nt", F32, f"z{l}", tm=1024, tn=4096)
        s["cat"], s["c"] = _mix_fwd(s["z"], mbd, ps_l, wdw, bdw3, lng3, lnb3, wsc, l, f"mix_fwd{l}")
        if l == 0:
            w.update(zip(big[1:6], _split_wait(first_a, s["cat"], _gather_piece, "gather_wait0a")))
        s["y1"], s["x1"], s["h1"] = _matmul_resnorm(s["cat"], w["w_out"], xs, post1, l, pre2, l, f"y1_{l}", tm=1024)
        s["q"] = mm(s["h1"], w["xattn_wq"], "nn", BF16, f"q{l}")
        s["k"] = mm(mem_n, w["xattn_wk"], "nn", BF16, f"k{l}")
        s["v"] = mm(mem_n, w["xattn_wv"], "nn", BF16, f"v{l}")
        s["o"] = _attn_fwd(s["q"], s["k"], s["v"], f"attn_fwd{l}")
        s["y2"], s["x2"], s["h2"] = _matmul_resnorm(s["o"], w["xattn_wo"], s["x1"], post2, l, pre3, l, f"y2_{l}", tm=1024)
        if l == 0:
            w.update(zip(big[6:], _split_wait(first_b, s["h2"], _gather_piece, "gather_wait0b")))
        s["u"] = mm(s["h2"], w["ffn_w_up"], "nt", F32, f"u{l}", tn=1408, a_outer=False)
        s["a"] = _ffn_act_fwd(s["u"], wcf, l, f"ffn_act{l}")
        if l + 1 < nl:
            s["y3"], xs, h = _matmul_resnorm(s["a"], w["ffn_w_down"], s["x2"], post3, l, pre1, l + 1, f"y3_{l}")
            layer_w.append(dict(zip(big, _split_wait(flying, xs, _gather_piece, f"gather_wait{l + 1}"))))
        saved.append(s)

    last = saved[-1]
    dxn, dy3, dg_post3, loss_lanes = _matmul_loss(
        last["a"], layer_w[-1]["ffn_w_down"], last["x2"], post3, nl - 1, target, "y3_loss")
    loss = lax.psum(loss_lanes[0, 0], ("x", "y", "c"))

    recvs = [lax.empty((max(nl - 1, 1), N_DEV, s.shape[1], d), BF16) for s in shards]
    recv0 = [lax.empty((1, N_DEV, s.shape[1], d), BF16) for s in shards]
    small = {n: [None] * nl for n in names if n not in big and n != "mem_norm"}
    small["ffn_post_norm"][nl - 1] = dg_post3
    dmem_n = jnp.zeros(mem0.shape, F32)
    flying = None
    for l in reversed(range(nl)):
        s, w = saved[l], layer_w[l]
        wc_l = wcf if flying is None else wcf + flying[-1][0, 0]
        gw = {}
        da = mm(dy3, w["ffn_w_down"], "nt", F32, f"da{l}", tn=1408, a_outer=False)
        gw["ffn_w_down"] = mm(s["a"], dy3, "tn", BF16, f"dw_down{l}", tm=1408, tk=2048)
        du, small["ffn_conv_w"][l] = _ffn_act_bwd(s["u"], da, wc_l, l, f"ffn_act_bwd{l}")
        gw["ffn_w_up"] = mm(du, s["h2"], "tn", BF16, f"dw_up{l}", tm=1408, tk=2048)
        dx2, small["ffn_pre_norm"][l], dy2, small["xattn_post_norm"][l] = _matmul_norm_bwd(
            du, w["ffn_w_up"], "nn", dxn, s["x2"], pre3, l, f"dh2_{l}", s["y2"], post2, l)
        do = mm(dy2, w["xattn_wo"], "nt", BF16, f"do{l}")
        gw["xattn_wo"] = mm(s["o"], dy2, "tn", BF16, f"dw_o{l}", tk=4096)
        dq, dk, dv = _attn_bwd(s["q"], s["k"], s["v"], do, f"attn_bwd{l}")
        dkb, dvb = dk.astype(BF16), dv.astype(BF16)
        gw["xattn_wq"] = mm(s["h1"], dq, "tn", BF16, f"dw_q{l}", tk=4096)
        gw["xattn_wk"] = mm(mem_n, dkb, "tn", BF16, f"dw_k{l}")
        gw["xattn_wv"] = mm(mem_n, dvb, "tn", BF16, f"dw_v{l}")
        dmem_n = dmem_n + mm(dkb, w["xattn_wk"], "nt", F32, f"dmem_k{l}") \
            + mm(dvb, w["xattn_wv"], "nt", F32, f"dmem_v{l}")
        pre2_l = pre2
        if l == 0:
            flying0 = _split_start([gw[n] for n in big[2:]], recv0[2:], _scatter_piece(0), dmem_n, "scatter_start0a")
            pre2_l = pre2 + flying0[-1][0, 0]
        dx1, small["xattn_pre_norm"][l], dy1, small["mix_post_norm"][l] = _matmul_norm_bwd(
            dq, w["xattn_wq"], "nt", dx2, s["x1"], pre2_l, l, f"dh1_{l}", s["y1"], post1, l, tm=1024)
        dcat = mm(dy1, w["w_out"], "nt", F32, f"dcat{l}")
        gw["w_out"] = mm(s["cat"], dy1, "tn", BF16, f"dw_out{l}", tk=4096)
        dz, dmbd, dps, dwdw, dbdw, dlng, dlnb, dwsc = _mix_bwd(
            dcat, s["z"], s["c"], mbd, pscale3, wdw, lng3, lnb3, wsc, l, f"mix_bwd{l}")
        small["pool_maps"][l] = jnp.stack([dmbd[gi * pg:(gi + 1) * pg, gi * pg:(gi + 1) * pg]
                                           for gi in range(len(POOL_WINDOWS))])
        small["pool_scale"][l], small["conf_dw_w"][l], small["conf_dw_b"][l] = dps, dwdw, dbdw
        small["conf_ln_g"][l], small["conf_ln_b"][l], small["sconv_w"][l] = dlng, dlnb, dwsc
        gw["w_in"] = mm(dz, s["h"], "tn", BF16, f"dw_in{l}", tm=4096, tk=2048)
        if l > 0:
            dxn, small["mix_pre_norm"][l], dy3, small["ffn_post_norm"][l - 1] = _matmul_norm_bwd(
                dz, w["w_in"], "nn", dx1, s["x"], pre1, l, f"dh{l}", saved[l - 1]["y3"], post3, l - 1)
            if flying is not None:
                recvs = _split_wait(flying, dxn, _scatter_piece(l), f"scatter_wait{l + 1}")
            flying = _split_start([gw[n] for n in big], recvs, _scatter_piece(l - 1), dxn, f"scatter_start{l}")
        else:
            if flying is not None:
                recvs = _split_wait(flying, dx1, _scatter_piece(0), "scatter_wait1")
            flying = _split_start([gw[n] for n in big[:2]], recv0[:2], _scatter_piece(0), dx1, "scatter_start0b")
            dxn, small["mix_pre_norm"][l] = _matmul_norm_bwd(
                dz, w["w_in"], "nn", dx1, s["x"], pre1 + flying[-1][0, 0], l, "dh0")
    grad_x = _from_steps(dxn)[None]
    _, dg_mem = _norm_bwd(jnp.zeros(mem0.shape, F32), dmem_n, mem0, memg3, 0, "norm_bwd_mem")

    small_names = [n for n in names if n not in big]
    partial = {n: (dg_mem.reshape(d) if n == "mem_norm" else
                   jnp.stack([g.reshape(g.shape[-1]) if g.shape[0] == 1 and weights[n].ndim == 2 else g
                              for g in small[n]])) for n in small_names}
    slab = _pack([partial[n] for n in small_names])
    gathered = _allgather([slab[None]], "gather_small_grads")[0][0].reshape(N_DEV, *slab.shape)
    summed = dict(zip(small_names, _unpack(_sum_sources(gathered, "sum_small_grads"), [partial[n] for n in small_names])))
    grad = {}
    for n in small_names:
        g = summed[n]
        if n in taps:
            width = weights[n].shape[-1]
            g = lax.dynamic_slice_in_dim(g, me * width, width, axis=g.ndim - 1)
        grad[n] = g

    delta, new_m, new_v = {}, {}, {}
    wmv = {n: [row_shard(n, a[n]) for a in (weights, mom1, mom2)] for n in big}
    upper = {n: _adam_sharded(recv, 1, *wmv[n], 1, nl, f"adam_{n}") for n, recv in zip(big, recvs)} if nl > 1 else {}
    upd = _adam_flat(_pack([weights[n] for n in small_names]), _pack([grad[n] for n in small_names]),
                     _pack([mom1[n] for n in small_names]), _pack([mom2[n] for n in small_names]), "adam_small")
    for out, slab_o in zip((delta, new_m, new_v), upd):
        out.update(zip(small_names, _unpack(slab_o, [weights[n] for n in small_names])))
    done = sum(r[3][0, 0, :1] for r in upper.values()) + upd[0][0, :1]
    recv0 = _split_wait(flying, done, _scatter_piece(0), "scatter_wait0b") \
        + _split_wait(flying0, done, _scatter_piece(0), "scatter_wait0a")
    for n, recv in zip(big, recv0):
        res = _adam_sharded(recv, 0, *wmv[n], 0, 1, f"adam0_{n}", prev=upper.get(n))
        grad[n], delta[n], new_m[n], new_v[n] = (row_shard(n, r) for r in res)

    return (loss, grad_x, *[grad[n] for n in names], *[delta[n] for n in names],
            *[new_m[n] for n in names], *[new_v[n] for n in names])
```

```python
import jax
import jax.numpy as jnp
from jax import lax
from jax.experimental import pallas as pl
from jax.experimental.pallas import tpu as pltpu

F32, BF16 = jnp.float32, jnp.bfloat16
EPS = 1e-6
POOL_WINDOWS = (2, 4, 8, 16)
MAX_WINDOW = 16
CONF_TAPS, SHORT_TAPS = 31, 3
CONF_HALO, POOL_HALO, SHORT_HALO = 256, 128, 16
ROW_CHUNK = 64
HEADS = 4
N_DEV = 8
ADAM_LR, ADAM_B1, ADAM_B2, ADAM_EPS, ADAM_WD, ADAM_STEP = 0.001, 0.9, 0.999, 1e-08, 0.01, 10
VMEM_LIMIT_V7X = 56 * 2**20
MESH = pl.DeviceIdType.MESH
ANY = pl.BlockSpec(memory_space=pl.ANY)
HBM = pl.BlockSpec(memory_space=pltpu.HBM)
SEM = pl.BlockSpec(memory_space=pltpu.SEMAPHORE)
EFFECT = pltpu.SideEffectType.DATAFLOW_SIDE_EFFECTING

TILE_NORM, TILE_MIX, TILE_FFN, TILE_ATTN, TILE_ADAM = 256, 512, 256, 512, 352


def _params(*sem):
    return pltpu.CompilerParams(dimension_semantics=sem, vmem_limit_bytes=VMEM_LIMIT_V7X)


def _sig(x):
    return 1.0 / (1.0 + jnp.exp(-x))


def _rms(x):
    r = lax.rsqrt(jnp.mean(x * x, axis=-1, keepdims=True) + EPS)
    return x * r, r


def _rms_bwd(dout, g, n, r):
    dn = dout * g
    return r * (dn - n * jnp.mean(dn * n, axis=-1, keepdims=True))


def _rows(tt, c):
    return pl.BlockSpec((tt, c), lambda i: (i, 0))


def _whole(shape):
    return pl.BlockSpec(shape, lambda i: (0,) * len(shape))


def _layer(shape, l):
    return pl.BlockSpec((None,) + shape, lambda i: (l,) + (0,) * len(shape))


def _colsum(x):
    return jnp.sum(x, axis=0, keepdims=True)


_DIMS = {"nn": (((1,), (0,)), ((), ())), "nt": (((1,), (1,)), ((), ())), "tn": (((0,), (0,)), ((), ()))}


def _matmul(a, b, mode, out_dtype, name, *, tm, tn, tk, a_outer=True):
    if mode == "nn":
        (m, k), (k2, n) = a.shape, b.shape
    elif mode == "nt":
        (m, k), (n, k2) = a.shape, b.shape
    else:
        (k, m), (k2, n) = a.shape, b.shape
    assert k == k2, (name, a.shape, b.shape)
    tm, tn, tk = min(tm, m), min(tn, n), min(tk, k)
    assert m % tm == 0 and n % tn == 0 and k % tk == 0, (name, m, n, k, tm, tn, tk)
    gm, gn, gk = m // tm, n // tn, k // tk

    def ij(g0, g1):
        return (g0, g1) if a_outer else (g1, g0)

    def a_map(g0, g1, kk):
        i, _ = ij(g0, g1)
        return (kk, i) if mode == "tn" else (i, kk)

    def b_map(g0, g1, kk):
        _, j = ij(g0, g1)
        return (j, kk) if mode == "nt" else (kk, j)

    def o_map(g0, g1, kk):
        return ij(g0, g1)

    a_block = (tk, tm) if mode == "tn" else (tm, tk)
    b_block = (tn, tk) if mode == "nt" else (tk, tn)
    dims = _DIMS[mode]

    def body(a_ref, b_ref, o_ref, *acc):
        p = lax.dot_general(a_ref[...].astype(BF16), b_ref[...].astype(BF16), dims, preferred_element_type=F32)
        if gk == 1:
            o_ref[...] = p.astype(o_ref.dtype)
        else:
            kk = pl.program_id(2)

            @pl.when(kk == 0)
            def _():
                acc[0][...] = p

            @pl.when(kk > 0)
            def _():
                acc[0][...] += p

            @pl.when(kk == gk - 1)
            def _():
                o_ref[...] = acc[0][...].astype(o_ref.dtype)

    return pl.pallas_call(
        body, name=name, grid=(gm, gn, gk) if a_outer else (gn, gm, gk),
        in_specs=[pl.BlockSpec(a_block, a_map), pl.BlockSpec(b_block, b_map)],
        out_specs=pl.BlockSpec((tm, tn), o_map),
        out_shape=jax.ShapeDtypeStruct((m, n), out_dtype),
        scratch_shapes=[pltpu.VMEM((tm, tn), F32)] if gk > 1 else [],
        compiler_params=_params("parallel", "parallel", "arbitrary"),
    )(a, b)


def _prenorm(x, g3, l, name):
    t, d = x.shape
    tt = min(TILE_NORM, t)

    def body(x_ref, g_ref, h_ref):
        n, _ = _rms(x_ref[...])
        h_ref[...] = (n * g_ref[...]).astype(BF16)

    return pl.pallas_call(
        body, name=name, grid=(t // tt,),
        in_specs=[_rows(tt, d), _layer((1, d), l)], out_specs=_rows(tt, d),
        out_shape=jax.ShapeDtypeStruct((t, d), BF16), compiler_params=_params("parallel"),
    )(x, g3)


def _norm_bwd(dxn, dh, x_in, gpre3, l, name, y_prev=None, gpost3=None, l_prev=None):
    t, d = x_in.shape
    tt = min(TILE_NORM, t)
    has_prev = y_prev is not None

    def body(*refs):
        if has_prev:
            dxn_ref, dh_ref, x_ref, g_ref, y_ref, g2_ref, dx_ref, dg_ref, dy_ref, dg2_ref = refs
        else:
            dxn_ref, dh_ref, x_ref, g_ref, dx_ref, dg_ref = refs

        @pl.when(pl.program_id(0) == 0)
        def _():
            dg_ref[...] = jnp.zeros_like(dg_ref)
            if has_prev:
                dg2_ref[...] = jnp.zeros_like(dg2_ref)

        dh_v = dh_ref[...]
        n, r = _rms(x_ref[...])
        dx = dxn_ref[...] + _rms_bwd(dh_v, g_ref[...], n, r)
        dx_ref[...] = dx
        dg_ref[...] += _colsum(dh_v * n)
        if has_prev:
            n2, r2 = _rms(y_ref[...])
            dy_ref[...] = _rms_bwd(dx, g2_ref[...], n2, r2).astype(BF16)
            dg2_ref[...] += _colsum(dx * n2)

    in_specs = [_rows(tt, d), _rows(tt, d), _rows(tt, d), _layer((1, d), l)]
    out_specs = [_rows(tt, d), _whole((1, d))]
    out_shape = [jax.ShapeDtypeStruct((t, d), F32), jax.ShapeDtypeStruct((1, d), F32)]
    args = [dxn, dh, x_in, gpre3]
    if has_prev:
        in_specs += [_rows(tt, d), _layer((1, d), l_prev)]
        out_specs += [_rows(tt, d), _whole((1, d))]
        out_shape += [jax.ShapeDtypeStruct((t, d), BF16), jax.ShapeDtypeStruct((1, d), F32)]
        args += [y_prev, gpost3]
    return pl.pallas_call(
        body, name=name, grid=(t // tt,), in_specs=in_specs, out_specs=out_specs, out_shape=out_shape,
        compiler_params=_params("arbitrary"),
    )(*args)


def _halves(tt):
    return [slice(0, tt // 2), slice(tt // 2, tt)] if tt % 32 == 0 else [slice(0, tt)]


def _matmul_resnorm(a, w, x, gpost3, l, gnext3, l2, name, tm=512):
    t, k = a.shape
    d = w.shape[1]
    tm = min(tm, t)

    def body(a_ref, w_ref, x_ref, gp_ref, gn_ref, y_ref, xo_ref, h_ref):
        wv, gp, gn = w_ref[...], gp_ref[...], gn_ref[...]
        for rows in _halves(tm):
            y = jnp.dot(a_ref[rows, :], wv, preferred_element_type=F32)
            y_ref[rows, :] = y
            n, _ = _rms(y)
            xn = x_ref[rows, :] + n * gp
            xo_ref[rows, :] = xn
            n2, _ = _rms(xn)
            h_ref[rows, :] = (n2 * gn).astype(BF16)

    return pl.pallas_call(
        body, name=name, grid=(t // tm,),
        in_specs=[_rows(tm, k), _whole((k, d)), _rows(tm, d), _layer((1, d), l), _layer((1, d), l2)],
        out_specs=[_rows(tm, d)] * 3,
        out_shape=[jax.ShapeDtypeStruct((t, d), F32), jax.ShapeDtypeStruct((t, d), F32),
                   jax.ShapeDtypeStruct((t, d), BF16)],
        compiler_params=_params("parallel"),
    )(a, w, x, gpost3, gnext3)


def _matmul_loss(a, w, x, gpost3, l, target, name, tm=512):
    t, k = a.shape
    d = w.shape[1]
    tm = min(tm, t)

    def body(a_ref, w_ref, x_ref, g_ref, t_ref, dxn_ref, dy_ref, dg_ref, loss_ref):
        @pl.when(pl.program_id(0) == 0)
        def _():
            dg_ref[...] = jnp.zeros_like(dg_ref)
            loss_ref[...] = jnp.zeros_like(loss_ref)

        wv, g = w_ref[...], g_ref[...]
        for rows in _halves(tm):
            n, r = _rms(jnp.dot(a_ref[rows, :], wv, preferred_element_type=F32))
            diff = x_ref[rows, :] + n * g - t_ref[rows, :]
            loss_ref[...] += 0.5 * jnp.sum(jnp.mean(diff * diff, axis=-1, keepdims=True))
            dxn = diff * (1.0 / d)
            dxn_ref[rows, :] = dxn
            dy_ref[rows, :] = _rms_bwd(dxn, g, n, r).astype(BF16)
            dg_ref[...] += _colsum(dxn * n)

    return pl.pallas_call(
        body, name=name, grid=(t // tm,),
        in_specs=[_rows(tm, k), _whole((k, d)), _rows(tm, d), _layer((1, d), l), _rows(tm, d)],
        out_specs=[_rows(tm, d), _rows(tm, d), _whole((1, d)), _whole((1, 128))],
        out_shape=[jax.ShapeDtypeStruct((t, d), F32), jax.ShapeDtypeStruct((t, d), BF16),
                   jax.ShapeDtypeStruct((1, d), F32), jax.ShapeDtypeStruct((1, 128), F32)],
        compiler_params=_params("arbitrary"),
    )(a, w, x, gpost3, target)


def _matmul_norm_bwd(a, w, mode, dxn, x_in, gpre3, l, name, y_prev=None, gpost3=None, l_prev=None, tm=512):
    t, k = a.shape
    d = x_in.shape[1]
    tm = min(tm, t)
    has_prev = y_prev is not None
    dims = _DIMS[mode]

    def body(*refs):
        if has_prev:
            a_ref, w_ref, dxn_ref, x_ref, g_ref, y_ref, g2_ref, dx_ref, dg_ref, dy_ref, dg2_ref = refs
        else:
            a_ref, w_ref, dxn_ref, x_ref, g_ref, dx_ref, dg_ref = refs

        @pl.when(pl.program_id(0) == 0)
        def _():
            dg_ref[...] = jnp.zeros_like(dg_ref)
            if has_prev:
                dg2_ref[...] = jnp.zeros_like(dg2_ref)

        wv, g = w_ref[...], g_ref[...]
        for rows in _halves(tm):
            dh = lax.dot_general(a_ref[rows, :], wv, dims, preferred_element_type=F32)
            n, r = _rms(x_ref[rows, :])
            dx = dxn_ref[rows, :] + _rms_bwd(dh, g, n, r)
            dx_ref[rows, :] = dx
            dg_ref[...] += _colsum(dh * n)
            if has_prev:
                n2, r2 = _rms(y_ref[rows, :])
                dy_ref[rows, :] = _rms_bwd(dx, g2_ref[...], n2, r2).astype(BF16)
                dg2_ref[...] += _colsum(dx * n2)

    in_specs = [_rows(tm, k), _whole(w.shape), _rows(tm, d), _rows(tm, d), _layer((1, d), l)]
    out_specs = [_rows(tm, d), _whole((1, d))]
    out_shape = [jax.ShapeDtypeStruct((t, d), F32), jax.ShapeDtypeStruct((1, d), F32)]
    args = [a, w, dxn, x_in, gpre3]
    if has_prev:
        in_specs += [_rows(tm, d), _layer((1, d), l_prev)]
        out_specs += [_rows(tm, d), _whole((1, d))]
        out_shape += [jax.ShapeDtypeStruct((t, d), BF16), jax.ShapeDtypeStruct((1, d), F32)]
        args += [y_prev, gpost3]
    return pl.pallas_call(
        body, name=name, grid=(t // tm,), in_specs=in_specs, out_specs=out_specs, out_shape=out_shape,
        compiler_params=_params("arbitrary"),
    )(*args)


def _to_steps(a):
    t = a.shape[0]
    return a.reshape(8, t // 8, -1).transpose(1, 0, 2).reshape(a.shape)


def _from_steps(a):
    t = a.shape[0]
    return a.reshape(t // 8, 8, -1).transpose(1, 0, 2).reshape(a.shape)


def _al(v):
    return v if isinstance(v, int) else pl.multiple_of(v, 8)


def _chunks(n_rows, fn, unroll=1, rc=ROW_CHUNK):
    def step(r, carry):
        fn(pl.multiple_of(r * rc, rc))
        return carry
    lax.fori_loop(0, n_rows // rc, step, 0, unroll=unroll)


def _fold8(a):
    return a.reshape(a.shape[0] // 8, 8, a.shape[1]).sum(axis=0)


def _shift_down(a):
    row = lax.broadcasted_iota(jnp.int32, a.shape, 0)
    return jnp.where(row % 8 == 0, 0.0, pltpu.roll(a, 1, 0))


def _shift_up(a):
    row = lax.broadcasted_iota(jnp.int32, a.shape, 0)
    return jnp.where(row % 8 == 7, 0.0, pltpu.roll(a, a.shape[0] - 1, 0))


def _prev_block(h, c, tt, t):
    return pl.BlockSpec((h, c), lambda i: (jnp.where(i == 0, t // h - 1, i * (tt // h) - 1), 0))


def _next_block(h, c, tt, t):
    return pl.BlockSpec((h, c), lambda i: (jnp.where(i == t // tt - 1, 0, (i + 1) * (tt // h)), 0))


def _taps(w_ref, buf, start, taps, rc, lanes=slice(None)):
    acc = w_ref[0:1, lanes] * buf[pl.ds(_al(start), rc), lanes]
    for k in range(1, taps):
        acc = acc + w_ref[k:k + 1, lanes] * buf[pl.ds(_al(start + 8 * k), rc), lanes]
    return acc


def _taps_rev(w_ref, buf, start, taps, rc, lanes=slice(None)):
    acc = w_ref[0:1, lanes] * buf[pl.ds(_al(start + 8 * (taps - 1)), rc), lanes]
    for k in range(1, taps):
        acc = acc + w_ref[k:k + 1, lanes] * buf[pl.ds(_al(start + 8 * (taps - 1 - k)), rc), lanes]
    return acc


def _mix_dims(d):
    dp = d // 4
    dc = 3 * d // 8
    ds = d - dp - dc
    oa, og = dp, dp + dc
    ob = dp + 2 * dc
    oc, ox = ob + ds, ob + 2 * ds
    return dp, dc, ds, oa, og, ob, oc, ox, ox + ds


def _pool_consts(dp):
    win = jnp.repeat(jnp.asarray(POOL_WINDOWS, F32), dp // len(POOL_WINDOWS))[None, :]
    mask = (jnp.arange(MAX_WINDOW, dtype=F32)[:, None] < win).astype(F32)
    return mask, win


def _pool_count(row0, rc, dp, seg, wl):
    r = lax.broadcasted_iota(jnp.int32, (rc, dp), 0) + row0
    return jnp.minimum(((r & 7) * seg + (r >> 3) + 1).astype(F32), wl)


def _mix_fwd(z, mbd, pscale3, wdw, bdw3, lng3, lnb3, wsc, l, name):
    t, din = z.shape
    dp, dc, ds, oa, og, ob, oc, ox, din2 = _mix_dims(din * 8 // 17)
    assert din2 == din
    d = ob
    tt = min(TILE_MIX, t)
    hp, hc, hs, rc = POOL_HALO, CONF_HALO, SHORT_HALO, ROW_CHUNK
    assert tt % hc == 0 and t % tt == 0
    seg = t // 8
    pmask, wlane = _pool_consts(dp)

    def body(z_ref, zpa_ref, zpb_ref, mbd_ref, ps_ref, pmask_ref, wl_ref, wdw_ref, bdw_ref, lng_ref, lnb_ref, wsc_ref,
             cat_ref, c_ref, pbuf, vbuf, sbuf):
        i = pl.program_id(0)
        pbuf[0:hp, :] = zpa_ref[hc - hp:hc, 0:dp]

        def prev(r0):
            rows = pl.ds(r0, rc)
            vbuf[rows, :] = zpa_ref[rows, oa:oa + dc] * _sig(zpa_ref[rows, og:og + dc])
        _chunks(hc, prev, rc=rc)
        sbuf[0:hs, :] = zpb_ref[:, oc:oc + ds] * zpb_ref[:, ox:ox + ds]

        @pl.when(i == 0)
        def _():
            pbuf[0:hp, :] = _shift_down(pbuf[0:hp, :])
            vbuf[0:hc, :] = _shift_down(vbuf[0:hc, :])
            sbuf[0:hs, :] = _shift_down(sbuf[0:hs, :])

        mbd_v, ps, wl = mbd_ref[...], ps_ref[...], wl_ref[...]
        bdw, lng, lnb = bdw_ref[...], lng_ref[...], lnb_ref[...]

        def step(r0):
            rows = pl.ds(r0, rc)
            zp = z_ref[rows, 0:dp]
            pbuf[pl.ds(_al(hp + r0), rc), :] = zp
            vbuf[pl.ds(_al(hc + r0), rc), :] = z_ref[rows, oa:oa + dc] * _sig(z_ref[rows, og:og + dc])
            sbuf[pl.ds(_al(hs + r0), rc), :] = z_ref[rows, oc:oc + ds] * z_ref[rows, ox:ox + ds]
            pooled = _taps_rev(pmask_ref, pbuf, r0 + hp - 8 * (MAX_WINDOW - 1), MAX_WINDOW, rc)
            pooled = pooled / _pool_count(i * tt + r0, rc, dp, seg, wl) - zp
            pm = jnp.dot(pooled.astype(BF16), mbd_v, preferred_element_type=F32)
            cat_ref[rows, 0:dp] = (pm * ps).astype(BF16)
            c = _taps(wdw_ref, vbuf, r0 + hc - 8 * (CONF_TAPS - 1), CONF_TAPS, rc) + bdw
            c_ref[rows, :] = c
            xc = c - jnp.mean(c, axis=-1, keepdims=True)
            nrm = xc * lax.rsqrt(jnp.mean(xc * xc, axis=-1, keepdims=True) + EPS)
            yln = nrm * lng + lnb
            cat_ref[rows, dp:dp + dc] = (yln * _sig(yln)).astype(BF16)
            cv = _taps(wsc_ref, sbuf, r0 + hs - 8 * (SHORT_TAPS - 1), SHORT_TAPS, rc)
            cat_ref[rows, dp + dc:d] = (z_ref[rows, ob:ob + ds] * cv).astype(BF16)
        _chunks(tt, step, unroll=2, rc=rc)

    return pl.pallas_call(
        body, name=name, grid=(t // tt,),
        in_specs=[_rows(tt, din), _prev_block(hc, d, tt, t), _prev_block(hs, din, tt, t),
                  _layer((dp, dp), l), _layer((1, dp), l), _whole((MAX_WINDOW, dp)), _whole((1, dp)),
                  _layer((CONF_TAPS, dc), l), _layer((1, dc), l), _layer((1, dc), l), _layer((1, dc), l),
                  _layer((SHORT_TAPS, ds), l)],
        out_specs=[_rows(tt, d), _rows(tt, dc)],
        out_shape=[jax.ShapeDtypeStruct((t, d), BF16), jax.ShapeDtypeStruct((t, dc), F32)],
        scratch_shapes=[pltpu.VMEM((hp + tt, dp), F32), pltpu.VMEM((hc + tt, dc), F32), pltpu.VMEM((hs + tt, ds), F32)],
        compiler_params=_params("parallel"),
    )(z, z, z, mbd, pscale3, pmask, wlane, wdw, bdw3, lng3, lnb3, wsc)


def _mix_bwd(dcat, z, c, mbd, pscale3, wdw, lng3, lnb3, wsc, l, name):
    t, din = z.shape
    dp, dc, ds, oa, og, ob, oc, ox, _ = _mix_dims(din * 8 // 17)
    d = ob
    tt = min(TILE_MIX, t)
    nt = t // tt
    hp, hc, hs, rc = POOL_HALO, CONF_HALO, SHORT_HALO, ROW_CHUNK
    assert tt % hc == 0 and t % tt == 0 and tt >= 8 * MAX_WINDOW
    seg = t // 8
    pmask, wlane = _pool_consts(dp)

    def body(dcat_ref, dcn_ref, z_ref, zpa_ref, zpb_ref, znb_ref, c_ref, cn_ref, mbd_ref, ps_ref, pmask_ref, wl_ref,
             wdw_ref, lng_ref, lnb_ref, wsc_ref,
             dz_ref, dmbd_ref, dps_ref, dwdw_ref, dbdw_ref, dlng_ref, dlnb_ref, dwsc_ref,
             pbuf, qbuf, dpbuf, pbf, vbuf, dcbuf, sbuf, dsbuf, dw8, ds8, ln8, ps8):
        i = pl.program_id(0)
        first, last = i == 0, i == nt - 1

        @pl.when(first)
        def _():
            for ref in (dmbd_ref, dw8, ds8, ln8, ps8):
                ref[...] = jnp.zeros_like(ref)

        mbd_v, ps, wl = mbd_ref[...], ps_ref[...], wl_ref[...]
        lng, lnb = lng_ref[...], lnb_ref[...]

        def ln_silu_bwd(cc, dyb):
            xc = cc - jnp.mean(cc, axis=-1, keepdims=True)
            rstd = lax.rsqrt(jnp.mean(xc * xc, axis=-1, keepdims=True) + EPS)
            nrm = xc * rstd
            yln = nrm * lng + lnb
            s = _sig(yln)
            dyln = dyb * (s * (1.0 + yln * (1.0 - s)))
            dn = dyln * lng
            dcc = rstd * (dn - jnp.mean(dn, axis=-1, keepdims=True) - nrm * jnp.mean(dn * nrm, axis=-1, keepdims=True))
            return dcc, dyln, nrm

        pbuf[0:hp, :] = zpa_ref[hc - hp:hc, 0:dp]

        def prev(r0):
            rows = pl.ds(r0, rc)
            vbuf[rows, :] = zpa_ref[rows, oa:oa + dc] * _sig(zpa_ref[rows, og:og + dc])
        _chunks(hc, prev)
        sbuf[0:hs, :] = zpb_ref[:, oc:oc + ds] * zpb_ref[:, ox:ox + ds]

        @pl.when(first)
        def _():
            pbuf[0:hp, :] = _shift_down(pbuf[0:hp, :])
            vbuf[0:hc, :] = _shift_down(vbuf[0:hc, :])
            sbuf[0:hs, :] = _shift_down(sbuf[0:hs, :])

        def nxt(r0):
            rows = pl.ds(r0, rc)
            dcc, _, _ = ln_silu_bwd(cn_ref[rows, :], dcn_ref[rows, dp:dp + dc])
            dcbuf[pl.ds(_al(tt + r0), rc), :] = dcc
        _chunks(hc, nxt, unroll=4)
        dpm_n = (dcn_ref[0:hp, 0:dp] * ps).astype(BF16)
        qbuf[tt:tt + hp, :] = lax.dot_general(dpm_n, mbd_v, _DIMS["nt"], preferred_element_type=F32) / wl
        dsbuf[tt:tt + hs, :] = dcn_ref[0:hs, dp + dc:d] * znb_ref[:, ob:ob + ds]

        @pl.when(last)
        def _():
            dcbuf[tt:tt + hc, :] = _shift_up(dcbuf[tt:tt + hc, :])
            qbuf[tt:tt + hp, :] = _shift_up(qbuf[tt:tt + hp, :])
            dsbuf[tt:tt + hs, :] = _shift_up(dsbuf[tt:tt + hs, :])

        def fill_pool(r0):
            rows = pl.ds(r0, rc)
            zp = z_ref[rows, 0:dp]
            pbuf[pl.ds(_al(hp + r0), rc), :] = zp
            pooled = _taps_rev(pmask_ref, pbuf, r0 + hp - 8 * (MAX_WINDOW - 1), MAX_WINDOW, rc)
            pbf[rows, :] = (pooled / _pool_count(i * tt + r0, rc, dp, seg, wl) - zp).astype(BF16)
        _chunks(tt, fill_pool, unroll=2)

        def fill_conformer(r0):
            rows = pl.ds(r0, rc)
            vbuf[pl.ds(_al(hc + r0), rc), :] = z_ref[rows, oa:oa + dc] * _sig(z_ref[rows, og:og + dc])
            dcc, dyln, nrm = ln_silu_bwd(c_ref[rows, :], dcat_ref[rows, dp:dp + dc])
            dcbuf[rows, :] = dcc
            ln8[0] += _fold8(dyln * nrm)
            ln8[1] += _fold8(dyln)
            ln8[2] += _fold8(dcc)
        _chunks(tt, fill_conformer, unroll=4)

        def fill_short(r0):
            rows = pl.ds(r0, rc)
            sbuf[pl.ds(_al(hs + r0), rc), :] = z_ref[rows, oc:oc + ds] * z_ref[rows, ox:ox + ds]
            dsbuf[rows, :] = dcat_ref[rows, dp + dc:d] * z_ref[rows, ob:ob + ds]
        _chunks(tt, fill_short, unroll=2)

        pb = pbf[...]
        dya = dcat_ref[:, 0:dp]
        ps8[...] += _fold8(dya * jnp.dot(pb, mbd_v, preferred_element_type=F32))
        dpm = (dya * ps).astype(BF16)
        dmbd_ref[...] += lax.dot_general(pb, dpm, _DIMS["tn"], preferred_element_type=F32)
        dpbuf[...] = lax.dot_general(dpm, mbd_v, _DIMS["nt"], preferred_element_type=F32)

        def quot(r0):
            rows = pl.ds(r0, rc)
            qbuf[rows, :] = dpbuf[rows, :] / _pool_count(i * tt + r0, rc, dp, seg, wl)
        _chunks(tt, quot)

        def back(r0):
            rows = pl.ds(r0, rc)
            dzp = _taps(pmask_ref, qbuf, r0, MAX_WINDOW, rc) - dpbuf[rows, :]
            dz_ref[rows, 0:dp] = dzp.astype(BF16)
            dcc = dcbuf[rows, :]
            for k in range(CONF_TAPS):
                dw8[k] += _fold8(dcc * vbuf[pl.ds(_al(r0 + hc - 8 * (CONF_TAPS - 1 - k)), rc), :])
            dv = _taps_rev(wdw_ref, dcbuf, r0, CONF_TAPS, rc)
            za = z_ref[rows, oa:oa + dc]
            sg = _sig(z_ref[rows, og:og + dc])
            dz_ref[rows, oa:oa + dc] = (dv * sg).astype(BF16)
            dz_ref[rows, og:og + dc] = (dv * za * sg * (1.0 - sg)).astype(BF16)
            cv = _taps(wsc_ref, sbuf, r0 + hs - 8 * (SHORT_TAPS - 1), SHORT_TAPS, rc)
            dz_ref[rows, ob:ob + ds] = (dcat_ref[rows, dp + dc:d] * cv).astype(BF16)
            dcv = dsbuf[rows, :]
            for k in range(SHORT_TAPS):
                ds8[k] += _fold8(dcv * sbuf[pl.ds(_al(r0 + hs - 8 * (SHORT_TAPS - 1 - k)), rc), :])
            dpv = _taps_rev(wsc_ref, dsbuf, r0, SHORT_TAPS, rc)
            dz_ref[rows, oc:oc + ds] = (dpv * z_ref[rows, ox:ox + ds]).astype(BF16)
            dz_ref[rows, ox:ox + ds] = (dpv * z_ref[rows, oc:oc + ds]).astype(BF16)
        _chunks(tt, back, unroll=2)

        @pl.when(last)
        def _():
            dps_ref[...] = jnp.sum(ps8[...], axis=0, keepdims=True)
            dwdw_ref[...] = jnp.sum(dw8[...], axis=1)
            dwsc_ref[...] = jnp.sum(ds8[...], axis=1)
            dlng_ref[...] = jnp.sum(ln8[0], axis=0, keepdims=True)
            dlnb_ref[...] = jnp.sum(ln8[1], axis=0, keepdims=True)
            dbdw_ref[...] = jnp.sum(ln8[2], axis=0, keepdims=True)

    return pl.pallas_call(
        body, name=name, grid=(nt,),
        in_specs=[_rows(tt, d), _next_block(hc, d, tt, t),
                  _rows(tt, din), _prev_block(hc, d, tt, t), _prev_block(hs, din, tt, t), _next_block(hs, din, tt, t),
                  _rows(tt, dc), _next_block(hc, dc, tt, t),
                  _layer((dp, dp), l), _layer((1, dp), l), _whole((MAX_WINDOW, dp)), _whole((1, dp)),
                  _layer((CONF_TAPS, dc), l), _layer((1, dc), l), _layer((1, dc), l), _layer((SHORT_TAPS, ds), l)],
        out_specs=[_rows(tt, din), _whole((dp, dp)), _whole((1, dp)), _whole((CONF_TAPS, dc)), _whole((1, dc)),
                   _whole((1, dc)), _whole((1, dc)), _whole((SHORT_TAPS, ds))],
        out_shape=[jax.ShapeDtypeStruct((t, din), BF16), jax.ShapeDtypeStruct((dp, dp), F32),
                   jax.ShapeDtypeStruct((1, dp), F32), jax.ShapeDtypeStruct((CONF_TAPS, dc), F32),
                   jax.ShapeDtypeStruct((1, dc), F32), jax.ShapeDtypeStruct((1, dc), F32),
                   jax.ShapeDtypeStruct((1, dc), F32), jax.ShapeDtypeStruct((SHORT_TAPS, ds), F32)],
        scratch_shapes=[pltpu.VMEM((hp + tt, dp), F32), pltpu.VMEM((tt + hp, dp), F32), pltpu.VMEM((tt, dp), F32),
                        pltpu.VMEM((tt, dp), BF16), pltpu.VMEM((hc + tt, dc), F32), pltpu.VMEM((tt + hc, dc), F32),
                        pltpu.VMEM((hs + tt, ds), F32), pltpu.VMEM((tt + hs, ds), F32),
                        pltpu.VMEM((CONF_TAPS, 8, dc), F32), pltpu.VMEM((SHORT_TAPS, 8, ds), F32),
                        pltpu.VMEM((3, 8, dc), F32), pltpu.VMEM((8, dp), F32)],
        compiler_params=_params("arbitrary"),
    )(dcat, dcat, z, z, z, z, c, c, mbd, pscale3, pmask, wlane, wdw, lng3, lnb3, wsc)


def _softmax_rows(qh, kh, scale):
    s = lax.dot_general(qh, kh, _DIMS["nt"], preferred_element_type=F32) * scale
    e = jnp.exp(s - jnp.max(s, axis=-1, keepdims=True))
    return e / jnp.sum(e, axis=-1, keepdims=True)


def _attn_fwd(q, k, v, name):
    t, d = q.shape
    m = k.shape[0]
    hd = d // HEADS
    scale = hd ** -0.5
    tt = min(TILE_ATTN, t)

    def body(q_ref, k_ref, v_ref, o_ref):
        for h in range(HEADS):
            sl = slice(h * hd, (h + 1) * hd)
            p = _softmax_rows(q_ref[:, sl], k_ref[:, sl], scale)
            o_ref[:, sl] = jnp.dot(p.astype(BF16), v_ref[:, sl], preferred_element_type=F32).astype(BF16)

    return pl.pallas_call(
        body, name=name, grid=(t // tt,),
        in_specs=[_rows(tt, d), _whole((m, d)), _whole((m, d))], out_specs=_rows(tt, d),
        out_shape=jax.ShapeDtypeStruct((t, d), BF16), compiler_params=_params("parallel"),
    )(q, k, v)


def _attn_bwd(q, k, v, do, name):
    t, d = q.shape
    m = k.shape[0]
    hd = d // HEADS
    scale = hd ** -0.5
    tt = min(TILE_ATTN, t)

    def body(q_ref, k_ref, v_ref, do_ref, dq_ref, dk_ref, dv_ref):
        @pl.when(pl.program_id(0) == 0)
        def _():
            dk_ref[...] = jnp.zeros_like(dk_ref)
            dv_ref[...] = jnp.zeros_like(dv_ref)

        for h in range(HEADS):
            sl = slice(h * hd, (h + 1) * hd)
            qh, kh, vh, doh = q_ref[:, sl], k_ref[:, sl], v_ref[:, sl], do_ref[:, sl]
            p = _softmax_rows(qh, kh, scale)
            dv_ref[:, sl] += lax.dot_general(p.astype(BF16), doh, _DIMS["tn"], preferred_element_type=F32)
            dp = lax.dot_general(doh, vh, _DIMS["nt"], preferred_element_type=F32)
            ds = (p * (dp - jnp.sum(dp * p, axis=-1, keepdims=True)) * scale).astype(BF16)
            dq_ref[:, sl] = jnp.dot(ds, kh, preferred_element_type=F32).astype(BF16)
            dk_ref[:, sl] += lax.dot_general(ds, qh, _DIMS["tn"], preferred_element_type=F32)

    return pl.pallas_call(
        body, name=name, grid=(t // tt,),
        in_specs=[_rows(tt, d), _whole((m, d)), _whole((m, d)), _rows(tt, d)],
        out_specs=[_rows(tt, d), _whole((m, d)), _whole((m, d))],
        out_shape=[jax.ShapeDtypeStruct((t, d), BF16), jax.ShapeDtypeStruct((m, d), F32),
                   jax.ShapeDtypeStruct((m, d), F32)],
        compiler_params=_params("arbitrary"),
    )(q, k, v, do)


def _lane_chunks(f):
    w = 256 if f % 256 == 0 else 128 if f % 128 == 0 else f
    return [(c0, w) for c0 in range(0, f, w)]


def _ffn_act_fwd(u, wc, l, name):
    t, f2 = u.shape
    f = f2 // 2
    tt = min(TILE_FFN, t)
    hs, rc = SHORT_HALO, ROW_CHUNK
    lanes = _lane_chunks(f)

    def body(u_ref, up_ref, wc_ref, a_ref, ubuf):
        ubuf[0:hs, :] = up_ref[...]

        @pl.when(pl.program_id(0) == 0)
        def _():
            ubuf[0:hs, :] = _shift_down(ubuf[0:hs, :])

        def step(r0):
            rows = pl.ds(r0, rc)
            ubuf[pl.ds(_al(hs + r0), rc), :] = u_ref[rows, :]
            start = r0 + hs - 8 * (SHORT_TAPS - 1)
            for c0, cw in lanes:
                g = _taps(wc_ref, ubuf, start, SHORT_TAPS, rc, slice(c0, c0 + cw))
                vv = _taps(wc_ref, ubuf, start, SHORT_TAPS, rc, slice(f + c0, f + c0 + cw))
                a_ref[rows, c0:c0 + cw] = (g * _sig(g) * vv).astype(BF16)
        _chunks(tt, step, unroll=2)

    return pl.pallas_call(
        body, name=name, grid=(t // tt,),
        in_specs=[_rows(tt, f2), _prev_block(hs, f2, tt, t), _layer((SHORT_TAPS, f2), l)],
        out_specs=_rows(tt, f), out_shape=jax.ShapeDtypeStruct((t, f), BF16),
        scratch_shapes=[pltpu.VMEM((hs + tt, f2), F32)], compiler_params=_params("parallel"),
    )(u, u, wc)


def _ffn_act_bwd(u, da, wc, l, name):
    t, f2 = u.shape
    f = f2 // 2
    tt = min(TILE_FFN, t)
    nt = t // tt
    hs, rc = SHORT_HALO, ROW_CHUNK
    lanes = _lane_chunks(f)

    def body(u_ref, up_ref, un_ref, da_ref, dan_ref, wc_ref, du_ref, dwc_ref, ubuf, danbuf, dbuf, dw8):
        i = pl.program_id(0)
        first, last = i == 0, i == nt - 1
        ubuf[0:hs, :] = up_ref[...]
        ubuf[hs + tt:hs + tt + hs, :] = un_ref[...]
        danbuf[...] = dan_ref[...]

        @pl.when(first)
        def _():
            dw8[...] = jnp.zeros_like(dw8)
            ubuf[0:hs, :] = _shift_down(ubuf[0:hs, :])

        @pl.when(last)
        def _():
            ubuf[hs + tt:hs + tt + hs, :] = _shift_up(ubuf[hs + tt:hs + tt + hs, :])
            danbuf[...] = _shift_up(danbuf[...])

        def fill(r0):
            ubuf[pl.ds(_al(hs + r0), rc), :] = u_ref[pl.ds(r0, rc), :]
        _chunks(tt, fill)

        def conv_grads(r0, n, da_rows):
            start = r0 + hs - 8 * (SHORT_TAPS - 1)
            for c0, cw in lanes:
                sl_g, sl_v = slice(c0, c0 + cw), slice(f + c0, f + c0 + cw)
                g = _taps(wc_ref, ubuf, start, SHORT_TAPS, n, sl_g)
                vv = _taps(wc_ref, ubuf, start, SHORT_TAPS, n, sl_v)
                dav = da_rows(c0, cw)
                sg = _sig(g)
                dbuf[pl.ds(_al(r0), n), sl_g] = dav * vv * (sg * (1.0 + g * (1.0 - sg)))
                dbuf[pl.ds(_al(r0), n), sl_v] = dav * (g * sg)

        _chunks(tt, lambda r0: conv_grads(r0, rc, lambda c0, cw: da_ref[pl.ds(r0, rc), c0:c0 + cw]), unroll=2)
        conv_grads(tt, hs, lambda c0, cw: danbuf[:, c0:c0 + cw])

        def back(r0):
            rows = pl.ds(r0, rc)
            for c0, cw in lanes:
                for off in (c0, f + c0):
                    sl = slice(off, off + cw)
                    du_ref[rows, sl] = _taps_rev(wc_ref, dbuf, r0, SHORT_TAPS, rc, sl).astype(BF16)
                    dd = dbuf[rows, sl]
                    for k in range(SHORT_TAPS):
                        dw8[k, :, sl] += _fold8(dd * ubuf[pl.ds(_al(r0 + hs - 8 * (SHORT_TAPS - 1 - k)), rc), sl])
        _chunks(tt, back, unroll=2)

        @pl.when(last)
        def _():
            dwc_ref[...] = jnp.sum(dw8[...], axis=1)

    return pl.pallas_call(
        body, name=name, grid=(nt,),
        in_specs=[_rows(tt, f2), _prev_block(hs, f2, tt, t), _next_block(hs, f2, tt, t),
                  _rows(tt, f), _next_block(hs, f, tt, t), _layer((SHORT_TAPS, f2), l)],
        out_specs=[_rows(tt, f2), _whole((SHORT_TAPS, f2))],
        out_shape=[jax.ShapeDtypeStruct((t, f2), BF16), jax.ShapeDtypeStruct((SHORT_TAPS, f2), F32)],
        scratch_shapes=[pltpu.VMEM((hs + tt + hs, f2), F32), pltpu.VMEM((hs, f), F32), pltpu.VMEM((tt + hs, f2), F32),
                        pltpu.VMEM((SHORT_TAPS, 8, f2), F32)],
        compiler_params=_params("arbitrary"),
    )(u, u, u, da, da, wc)


def _place():
    return lax.axis_index("x"), lax.axis_index("y"), lax.axis_index("c")


def _flip(v, bit):
    return 1 - v if bit else v


def _peers(x, y, c):
    out = []
    for kk in range(1, N_DEV):
        px, py, pc = _flip(x, kk & 4), _flip(y, kk & 2), _flip(c, kk & 1)
        out.append((kk - 1, (px, py, pc), 4 * px + 2 * py + pc))
    return out


def _allgather(shards, name):
    nt = len(shards)

    def body(*refs):
        srcs, outs = refs[:nt], refs[nt:2 * nt]
        send_sems, recv_sems, local_sems = refs[2 * nt:]
        x, y, c = _place()
        me, sibling = (x, y, c), (x, y, 1 - c)
        chips = [(1 - x, y), (x, 1 - y), (1 - x, 1 - y)]

        def rows(ti, px, py, pc):
            r = srcs[ti].shape[1]
            return outs[ti].at[:, pl.ds((4 * px + 2 * py + pc) * r, r), :]

        def copy(ti, kk, block, to, src=None):
            return pltpu.make_async_remote_copy(
                src_ref=rows(ti, *block) if src is None else src, dst_ref=rows(ti, *block),
                send_sem=send_sems.at[ti, kk], recv_sem=recv_sems.at[ti, kk], device_id=to, device_id_type=MESH)

        mine = [pltpu.make_async_copy(srcs[ti], rows(ti, *me), local_sems.at[ti]) for ti in range(nt)]
        for cp in mine:
            cp.start()
        first = []
        for ti in range(nt):
            first.append(copy(ti, 0, me, sibling, src=srcs[ti]))
            first += [copy(ti, 1 + j, me, (*chip, c), src=srcs[ti]) for j, chip in enumerate(chips)]
        for cp in first:
            cp.start()
        passed = []
        for j, chip in enumerate(chips):
            for ti in range(nt):
                copy(ti, 1 + j, (*chip, c), me).wait_recv()
                fwd = copy(ti, 4 + j, (*chip, c), sibling)
                fwd.start()
                passed.append(fwd)
        for ti in range(nt):
            copy(ti, 0, sibling, me).wait_recv()
            for j, chip in enumerate(chips):
                copy(ti, 4 + j, (*chip, 1 - c), me).wait_recv()
        for cp in first + passed:
            cp.wait_send()
        for cp in mine:
            cp.wait()

    return pl.pallas_call(
        body, name=name,
        in_specs=[ANY] * nt, out_specs=[ANY] * nt,
        out_shape=[jax.ShapeDtypeStruct((s.shape[0], N_DEV * s.shape[1], s.shape[2]), s.dtype) for s in shards],
        scratch_shapes=[pltpu.SemaphoreType.DMA((nt, 7)), pltpu.SemaphoreType.DMA((nt, 7)),
                        pltpu.SemaphoreType.DMA((nt,))],
    )(*shards)


def _gather_piece(src, land, me, to):
    r = src.shape[0]
    return src, land.at[pl.ds(me * r, r), :]


def _scatter_piece(l):
    def piece(src, land, me, to):
        r = src.shape[0] // N_DEV
        return src.at[pl.ds(to * r, r), :], land.at[l, me]
    return piece


def _split_start(srcs, lands, piece, after, name):
    nt = len(srcs)

    def body(*refs):
        src_refs, land_refs = refs[:nt], refs[nt:2 * nt]
        send_sems, recv_sems, local_sems, token = refs[2 * nt + 1], refs[2 * nt + 2], refs[2 * nt + 3], refs[4 * nt + 4]
        x, y, c = _place()
        me = 4 * x + 2 * y + c
        for ti in range(nt):
            for slot, peer, flat in _peers(x, y, c):
                src, dst = piece(src_refs[ti], land_refs[ti], me, flat)
                pltpu.make_async_remote_copy(
                    src_ref=src, dst_ref=dst, send_sem=send_sems.at[7 * ti + slot], recv_sem=recv_sems.at[7 * ti + slot],
                    device_id=peer, device_id_type=MESH).start()
        for ti in range(nt):
            pltpu.make_async_copy(*piece(src_refs[ti], land_refs[ti], me, me), local_sems.at[ti]).start()
        token[...] = jnp.zeros_like(token)

    both = list(srcs) + list(lands)
    return pl.pallas_call(
        body, name=name,
        in_specs=[HBM] * (2 * nt) + [ANY],
        out_specs=[SEM, SEM, SEM] + [HBM] * (2 * nt) + [pl.BlockSpec(memory_space=pltpu.VMEM)],
        out_shape=[pltpu.SemaphoreType.DMA((7 * nt,)), pltpu.SemaphoreType.DMA((7 * nt,)), pltpu.SemaphoreType.DMA((nt,))]
        + [pltpu.HBM(a.shape, a.dtype) for a in both] + [jax.ShapeDtypeStruct((8, 128), F32)],
        input_output_aliases={i: i + 3 for i in range(2 * nt)},
        compiler_params=pltpu.CompilerParams(has_side_effects=EFFECT),
    )(*[pltpu.with_memory_space_constraint(a, pltpu.HBM) for a in both], after)


def _split_wait(started, after, piece, name):
    send_sems, recv_sems, local_sems, *both = started[:-1]
    nt = len(both) // 2

    def body(*refs):
        src_refs, land_refs = refs[:nt], refs[nt:2 * nt]
        send_ref, recv_ref, local_ref = refs[2 * nt], refs[2 * nt + 1], refs[2 * nt + 2]
        x, y, c = _place()
        me = 4 * x + 2 * y + c
        for ti in range(nt):
            src, dst = piece(src_refs[ti], land_refs[ti], me, me)
            for slot in range(N_DEV - 1):
                cp = pltpu.make_async_remote_copy(
                    src_ref=src, dst_ref=dst, send_sem=send_ref.at[7 * ti + slot], recv_sem=recv_ref.at[7 * ti + slot],
                    device_id=(x, y, c), device_id_type=MESH)
                cp.wait_send()
                cp.wait_recv()
            pltpu.make_async_copy(src, dst, local_ref.at[ti]).wait()

    outs = pl.pallas_call(
        body, name=name,
        in_specs=[HBM] * (2 * nt) + [SEM, SEM, SEM, ANY], out_specs=[HBM] * (2 * nt),
        out_shape=[pltpu.HBM(a.shape, a.dtype) for a in both],
        input_output_aliases={i: i for i in range(2 * nt)},
        compiler_params=pltpu.CompilerParams(has_side_effects=EFFECT),
    )(*both, send_sems, recv_sems, local_sems, after)
    return outs[nt:]


def _adam(w, g, m, v):
    m2 = ADAM_B1 * m + (1.0 - ADAM_B1) * g
    v2 = ADAM_B2 * v + (1.0 - ADAM_B2) * (g * g)
    m_hat = m2 / (1.0 - ADAM_B1 ** ADAM_STEP)
    v_hat = v2 / (1.0 - ADAM_B2 ** ADAM_STEP)
    return -ADAM_LR * (m_hat / (jnp.sqrt(v_hat) + ADAM_EPS) + ADAM_WD * w), m2, v2


def _adam_sharded(recv, recv_first, w, m, v, lo, hi, name, prev=None):
    nl, r, c = w.shape
    tr = max([rows for rows in range(16, min(r, TILE_ADAM) + 1, 16) if r % rows == 0] or [r])

    def body(recv_ref, w_ref, m_ref, v_ref, *rest):
        g_ref, d_ref, m2_ref, v2_ref = rest[-4:]
        g = recv_ref[0].astype(F32)
        for s in range(1, N_DEV):
            g = g + recv_ref[s].astype(F32)
        g_ref[...] = g
        d_ref[...], m2_ref[...], v2_ref[...] = _adam(w_ref[...], g, m_ref[...], v_ref[...])

    blk = pl.BlockSpec((None, tr, c), lambda li, i: (li + lo, i, 0))
    extra = [] if prev is None else list(prev)
    return pl.pallas_call(
        body, name=name, grid=(hi - lo, r // tr),
        in_specs=[pl.BlockSpec((None, N_DEV, tr, c), lambda li, i: (li + lo - recv_first, 0, i, 0)), blk, blk, blk]
        + [ANY] * len(extra),
        out_specs=[blk] * 4, out_shape=[jax.ShapeDtypeStruct((nl, r, c), F32)] * 4,
        input_output_aliases={4 + i: i for i in range(len(extra))},
        compiler_params=_params("parallel", "parallel"),
    )(recv, w, m, v, *extra)


def _sum_sources(parts, name):
    _, r, c = parts.shape

    def body(p_ref, o_ref):
        g = p_ref[0]
        for s in range(1, N_DEV):
            g = g + p_ref[s]
        o_ref[...] = g

    return pl.pallas_call(
        body, name=name, grid=(1,), in_specs=[_whole((N_DEV, r, c))], out_specs=_whole((r, c)),
        out_shape=jax.ShapeDtypeStruct((r, c), F32), compiler_params=_params("arbitrary"),
    )(parts)


def _adam_flat(w, g, m, v, name):
    r, c = w.shape

    def body(w_ref, g_ref, m_ref, v_ref, d_ref, m2_ref, v2_ref):
        d_ref[...], m2_ref[...], v2_ref[...] = _adam(w_ref[...], g_ref[...], m_ref[...], v_ref[...])

    return pl.pallas_call(
        body, name=name, grid=(1,), in_specs=[_whole((r, c))] * 4, out_specs=[_whole((r, c))] * 3,
        out_shape=[jax.ShapeDtypeStruct((r, c), F32)] * 3, compiler_params=_params("arbitrary"),
    )(w, g, m, v)


def _pack(arrays):
    flat = jnp.concatenate([a.reshape(-1).astype(F32) for a in arrays])
    rows = -(-flat.shape[0] // 1024) * 8
    return jnp.pad(flat, (0, rows * 128 - flat.shape[0])).reshape(rows, 128)


def _unpack(slab, like):
    flat = slab.reshape(-1)
    out, at = [], 0
    for a in like:
        out.append(flat[at:at + a.size].reshape(a.shape))
        at += a.size
    return out


def kernel(x, mem, mem_norm, mix_pre_norm, mix_post_norm, w_in, pool_maps, pool_scale, conf_dw_w, conf_dw_b, conf_ln_g, conf_ln_b, sconv_w, w_out, xattn_pre_norm, xattn_post_norm, xattn_wq, xattn_wk, xattn_wv, xattn_wo, ffn_pre_norm, ffn_post_norm, ffn_w_up, ffn_conv_w, ffn_w_down, loss_target, m_mem_norm, m_mix_pre_norm, m_mix_post_norm, m_w_in, m_pool_maps, m_pool_scale, m_conf_dw_w, m_conf_dw_b, m_conf_ln_g, m_conf_ln_b, m_sconv_w, m_w_out, m_xattn_pre_norm, m_xattn_post_norm, m_xattn_wq, m_xattn_wk, m_xattn_wv, m_xattn_wo, m_ffn_pre_norm, m_ffn_post_norm, m_ffn_w_up, m_ffn_conv_w, m_ffn_w_down, v_mem_norm, v_mix_pre_norm, v_mix_post_norm, v_w_in, v_pool_maps, v_pool_scale, v_conf_dw_w, v_conf_dw_b, v_conf_ln_g, v_conf_ln_b, v_sconv_w, v_w_out, v_xattn_pre_norm, v_xattn_post_norm, v_xattn_wq, v_xattn_wk, v_xattn_wv, v_xattn_wo, v_ffn_pre_norm, v_ffn_post_norm, v_ffn_w_up, v_ffn_conv_w, v_ffn_w_down):
    weights = dict(mem_norm=mem_norm, mix_pre_norm=mix_pre_norm, mix_post_norm=mix_post_norm, w_in=w_in, pool_maps=pool_maps, pool_scale=pool_scale, conf_dw_w=conf_dw_w, conf_dw_b=conf_dw_b, conf_ln_g=conf_ln_g, conf_ln_b=conf_ln_b, sconv_w=sconv_w, w_out=w_out, xattn_pre_norm=xattn_pre_norm, xattn_post_norm=xattn_post_norm, xattn_wq=xattn_wq, xattn_wk=xattn_wk, xattn_wv=xattn_wv, xattn_wo=xattn_wo, ffn_pre_norm=ffn_pre_norm, ffn_post_norm=ffn_post_norm, ffn_w_up=ffn_w_up, ffn_conv_w=ffn_conv_w, ffn_w_down=ffn_w_down)
    mom1 = dict(mem_norm=m_mem_norm, mix_pre_norm=m_mix_pre_norm, mix_post_norm=m_mix_post_norm, w_in=m_w_in, pool_maps=m_pool_maps, pool_scale=m_pool_scale, conf_dw_w=m_conf_dw_w, conf_dw_b=m_conf_dw_b, conf_ln_g=m_conf_ln_g, conf_ln_b=m_conf_ln_b, sconv_w=m_sconv_w, w_out=m_w_out, xattn_pre_norm=m_xattn_pre_norm, xattn_post_norm=m_xattn_post_norm, xattn_wq=m_xattn_wq, xattn_wk=m_xattn_wk, xattn_wv=m_xattn_wv, xattn_wo=m_xattn_wo, ffn_pre_norm=m_ffn_pre_norm, ffn_post_norm=m_ffn_post_norm, ffn_w_up=m_ffn_w_up, ffn_conv_w=m_ffn_conv_w, ffn_w_down=m_ffn_w_down)
    mom2 = dict(mem_norm=v_mem_norm, mix_pre_norm=v_mix_pre_norm, mix_post_norm=v_mix_post_norm, w_in=v_w_in, pool_maps=v_pool_maps, pool_scale=v_pool_scale, conf_dw_w=v_conf_dw_w, conf_dw_b=v_conf_dw_b, conf_ln_g=v_conf_ln_g, conf_ln_b=v_conf_ln_b, sconv_w=v_sconv_w, w_out=v_w_out, xattn_pre_norm=v_xattn_pre_norm, xattn_post_norm=v_xattn_post_norm, xattn_wq=v_xattn_wq, xattn_wk=v_xattn_wk, xattn_wv=v_xattn_wv, xattn_wo=v_xattn_wo, ffn_pre_norm=v_ffn_pre_norm, ffn_post_norm=v_ffn_post_norm, ffn_w_up=v_ffn_w_up, ffn_conv_w=v_ffn_conv_w, ffn_w_down=v_ffn_w_down)
    names = list(weights)

    nl, d = mix_pre_norm.shape
    x0, mem0, target = _to_steps(x[0]), mem[0], _to_steps(loss_target[0])
    dp, dc, ds, *_ = _mix_dims(d)
    pg = dp // len(POOL_WINDOWS)
    me = 4 * lax.axis_index("x") + 2 * lax.axis_index("y") + lax.axis_index("c")

    big = ["w_in", "w_out", "xattn_wq", "xattn_wk", "xattn_wv", "xattn_wo", "ffn_w_up", "ffn_w_down"]
    transposed = ("w_in", "ffn_w_up")

    def row_shard(n, a):
        return a.transpose(0, 2, 1) if n in transposed else a

    shards = [row_shard(n, weights[n]).astype(BF16) for n in big]
    taps = ["conf_dw_w", "sconv_w", "ffn_conv_w"]
    tap_slab = _pack([weights[n] for n in taps])
    win0, tap_all = _allgather([shards[0][0:1], tap_slab[None]], "gather_weights0")
    layer_w = [{"w_in": win0[0]}]

    def land(s):
        return lax.empty((N_DEV * s.shape[1], s.shape[2]), BF16)

    first_a = _split_start([s[0] for s in shards[1:6]], [land(s) for s in shards[1:6]], _gather_piece, win0,
                           "gather_start0a")
    first_b = _split_start([s[0] for s in shards[6:]], [land(s) for s in shards[6:]], _gather_piece, first_a[-1],
                           "gather_start0b")
    tap_all = tap_all[0].reshape(N_DEV, *tap_slab.shape)
    tap_parts = [_unpack(tap_all[p], [weights[n] for n in taps]) for p in range(N_DEV)]
    wdw, wsc, wcf = (jnp.concatenate([tap_parts[p][i] for p in range(N_DEV)], axis=-1) for i in range(3))

    def g3(a):
        return a.reshape(a.shape[0], 1, a.shape[-1])

    mbd = jnp.zeros((nl, dp, dp), F32)
    for gi in range(len(POOL_WINDOWS)):
        mbd = mbd.at[:, gi * pg:(gi + 1) * pg, gi * pg:(gi + 1) * pg].set(pool_maps[:, gi])
    mbd = mbd.astype(BF16)
    pre1, post1, pre2, post2, pre3, post3 = (g3(weights[n]) for n in (
        "mix_pre_norm", "mix_post_norm", "xattn_pre_norm", "xattn_post_norm", "ffn_pre_norm", "ffn_post_norm"))
    pscale3, bdw3, lng3, lnb3 = g3(pool_scale), g3(conf_dw_b), g3(conf_ln_g), g3(conf_ln_b)
    memg3 = mem_norm.reshape(1, 1, d)

    def mm(a, b, mode, dt, name, tm=2048, tn=1024, tk=1024, a_outer=True):
        return _matmul(a, b, mode, dt, name, tm=tm, tn=tn, tk=tk, a_outer=a_outer)

    mem_n = _prenorm(mem0, memg3, 0, "mem_norm")
    xs = x0
    h = _prenorm(xs, pre1, 0, "pre_norm0")
    saved = []
    for l in range(nl):
        ps_l = pscale3
        if l + 1 < nl:
            flying = _split_start([s[l + 1] for s in shards], [land(s) for s in shards], _gather_piece,
                                  xs if l else first_b[-1], f"gather_start{l + 1}")
            ps_l = pscale3 + flying[-1][0, 0]
        w = layer_w[l]
        s = {"x": xs, "h": h}
        s["z"] = mm(h, w["w_in"], "nt", F32, f"z{l}", tm=1024, tn=4096)
        s["cat"], s["c"] = _mix_fwd(s["z"], mbd, ps_l, wdw, bdw3, lng3, lnb3, wsc, l, f"mix_fwd{l}")
        if l == 0:
            w.update(zip(big[1:6], _split_wait(first_a, s["cat"], _gather_piece, "gather_wait0a")))
        s["y1"], s["x1"], s["h1"] = _matmul_resnorm(s["cat"], w["w_out"], xs, post1, l, pre2, l, f"y1_{l}", tm=1024)
        s["q"] = mm(s["h1"], w["xattn_wq"], "nn", BF16, f"q{l}")
        s["k"] = mm(mem_n, w["xattn_wk"], "nn", BF16, f"k{l}")
        s["v"] = mm(mem_n, w["xattn_wv"], "nn", BF16, f"v{l}")
        s["o"] = _attn_fwd(s["q"], s["k"], s["v"], f"attn_fwd{l}")
        s["y2"], s["x2"], s["h2"] = _matmul_resnorm(s["o"], w["xattn_wo"], s["x1"], post2, l, pre3, l, f"y2_{l}", tm=1024)
        if l == 0:
            w.update(zip(big[6:], _split_wait(first_b, s["h2"], _gather_piece, "gather_wait0b")))
        s["u"] = mm(s["h2"], w["ffn_w_up"], "nt", F32, f"u{l}", tn=1408, a_outer=False)
        s["a"] = _ffn_act_fwd(s["u"], wcf, l, f"ffn_act{l}")
        if l + 1 < nl:
            s["y3"], xs, h = _matmul_resnorm(s["a"], w["ffn_w_down"], s["x2"], post3, l, pre1, l + 1, f"y3_{l}")
            layer_w.append(dict(zip(big, _split_wait(flying, xs, _gather_piece, f"gather_wait{l + 1}"))))
        saved.append(s)

    last = saved[-1]
    dxn, dy3, dg_post3, loss_lanes = _matmul_loss(
        last["a"], layer_w[-1]["ffn_w_down"], last["x2"], post3, nl - 1, target, "y3_loss")
    loss = lax.psum(loss_lanes[0, 0], ("x", "y", "c"))

    recvs = [lax.empty((max(nl - 1, 1), N_DEV, s.shape[1], d), BF16) for s in shards]
    recv0 = [lax.empty((1, N_DEV, s.shape[1], d), BF16) for s in shards]
    small = {n: [None] * nl for n in names if n not in big and n != "mem_norm"}
    small["ffn_post_norm"][nl - 1] = dg_post3
    dmem_n = jnp.zeros(mem0.shape, F32)
    flying = None
    for l in reversed(range(nl)):
        s, w = saved[l], layer_w[l]
        wc_l = wcf if flying is None else wcf + flying[-1][0, 0]
        gw = {}
        da = mm(dy3, w["ffn_w_down"], "nt", F32, f"da{l}", tn=1408, a_outer=False)
        gw["ffn_w_down"] = mm(s["a"], dy3, "tn", BF16, f"dw_down{l}", tm=1408, tk=2048)
        du, small["ffn_conv_w"][l] = _ffn_act_bwd(s["u"], da, wc_l, l, f"ffn_act_bwd{l}")
        gw["ffn_w_up"] = mm(du, s["h2"], "tn", BF16, f"dw_up{l}", tm=512, tk=4096)
        dx2, small["ffn_pre_norm"][l], dy2, small["xattn_post_norm"][l] = _matmul_norm_bwd(
            du, w["ffn_w_up"], "nn", dxn, s["x2"], pre3, l, f"dh2_{l}", s["y2"], post2, l)
        do = mm(dy2, w["xattn_wo"], "nt", BF16, f"do{l}")
        gw["xattn_wo"] = mm(s["o"], dy2, "tn", BF16, f"dw_o{l}", tk=4096)
        dq, dk, dv = _attn_bwd(s["q"], s["k"], s["v"], do, f"attn_bwd{l}")
        dkb, dvb = dk.astype(BF16), dv.astype(BF16)
        gw["xattn_wq"] = mm(s["h1"], dq, "tn", BF16, f"dw_q{l}", tk=4096)
        gw["xattn_wk"] = mm(mem_n, dkb, "tn", BF16, f"dw_k{l}")
        gw["xattn_wv"] = mm(mem_n, dvb, "tn", BF16, f"dw_v{l}")
        dmem_n = dmem_n + mm(dkb, w["xattn_wk"], "nt", F32, f"dmem_k{l}") \
            + mm(dvb, w["xattn_wv"], "nt", F32, f"dmem_v{l}")
        pre2_l = pre2
        if l == 0:
            flying0 = _split_start([gw[n] for n in big[2:]], recv0[2:], _scatter_piece(0), dmem_n, "scatter_start0a")
            pre2_l = pre2 + flying0[-1][0, 0]
        dx1, small["xattn_pre_norm"][l], dy1, small["mix_post_norm"][l] = _matmul_norm_bwd(
            dq, w["xattn_wq"], "nt", dx2, s["x1"], pre2_l, l, f"dh1_{l}", s["y1"], post1, l, tm=1024)
        dcat = mm(dy1, w["w_out"], "nt", F32, f"dcat{l}")
        gw["w_out"] = mm(s["cat"], dy1, "tn", BF16, f"dw_out{l}", tk=4096)
        dz, dmbd, dps, dwdw, dbdw, dlng, dlnb, dwsc = _mix_bwd(
            dcat, s["z"], s["c"], mbd, pscale3, wdw, lng3, lnb3, wsc, l, f"mix_bwd{l}")
        small["pool_maps"][l] = jnp.stack([dmbd[gi * pg:(gi + 1) * pg, gi * pg:(gi + 1) * pg]
                                           for gi in range(len(POOL_WINDOWS))])
        small["pool_scale"][l], small["conf_dw_w"][l], small["conf_dw_b"][l] = dps, dwdw, dbdw
        small["conf_ln_g"][l], small["conf_ln_b"][l], small["sconv_w"][l] = dlng, dlnb, dwsc
        gw["w_in"] = mm(dz, s["h"], "tn", BF16, f"dw_in{l}", tm=4096, tk=2048)
        if l > 0:
            dxn, small["mix_pre_norm"][l], dy3, small["ffn_post_norm"][l - 1] = _matmul_norm_bwd(
                dz, w["w_in"], "nn", dx1, s["x"], pre1, l, f"dh{l}", saved[l - 1]["y3"], post3, l - 1)
            if flying is not None:
                recvs = _split_wait(flying, dxn, _scatter_piece(l), f"scatter_wait{l + 1}")
            flying = _split_start([gw[n] for n in big], recvs, _scatter_piece(l - 1), dxn, f"scatter_start{l}")
        else:
            if flying is not None:
                recvs = _split_wait(flying, dx1, _scatter_piece(0), "scatter_wait1")
            flying = _split_start([gw[n] for n in big[:2]], recv0[:2], _scatter_piece(0), dx1, "scatter_start0b")
            dxn, small["mix_pre_norm"][l] = _matmul_norm_bwd(
                dz, w["w_in"], "nn", dx1, s["x"], pre1 + flying[-1][0, 0], l, "dh0")
    grad_x = _from_steps(dxn)[None]
    _, dg_mem = _norm_bwd(jnp.zeros(mem0.shape, F32), dmem_n, mem0, memg3, 0, "norm_bwd_mem")

    small_names = [n for n in names if n not in big]
    partial = {n: (dg_mem.reshape(d) if n == "mem_norm" else
                   jnp.stack([g.reshape(g.shape[-1]) if g.shape[0] == 1 and weights[n].ndim == 2 else g
                              for g in small[n]])) for n in small_names}
    slab = _pack([partial[n] for n in small_names])
    gathered = _allgather([slab[None]], "gather_small_grads")[0][0].reshape(N_DEV, *slab.shape)
    summed = dict(zip(small_names, _unpack(_sum_sources(gathered, "sum_small_grads"), [partial[n] for n in small_names])))
    grad = {}
    for n in small_names:
        g = summed[n]
        if n in taps:
            width = weights[n].shape[-1]
            g = lax.dynamic_slice_in_dim(g, me * width, width, axis=g.ndim - 1)
        grad[n] = g

    delta, new_m, new_v = {}, {}, {}
    wmv = {n: [row_shard(n, a[n]) for a in (weights, mom1, mom2)] for n in big}
    upper = {n: _adam_sharded(recv, 1, *wmv[n], 1, nl, f"adam_{n}") for n, recv in zip(big, recvs)} if nl > 1 else {}
    upd = _adam_flat(_pack([weights[n] for n in small_names]), _pack([grad[n] for n in small_names]),
                     _pack([mom1[n] for n in small_names]), _pack([mom2[n] for n in small_names]), "adam_small")
    for out, slab_o in zip((delta, new_m, new_v), upd):
        out.update(zip(small_names, _unpack(slab_o, [weights[n] for n in small_names])))
    done = sum(r[3][0, 0, :1] for r in upper.values()) + upd[0][0, :1]
    recv0 = _split_wait(flying, done, _scatter_piece(0), "scatter_wait0b") \
        + _split_wait(flying0, done, _scatter_piece(0), "scatter_wait0a")
    for n, recv in zip(big, recv0):
        res = _adam_sharded(recv, 0, *wmv[n], 0, 1, f"adam0_{n}", prev=upper.get(n))
        grad[n], delta[n], new_m[n], new_v[n] = (row_shard(n, r) for r in res)

    return (loss, grad_x, *[grad[n] for n in names], *[delta[n] for n in names],
            *[new_m[n] for n in names], *[new_v[n] for n in names])
```

```python
import jax
import jax.numpy as jnp
from jax import lax
from jax.experimental import pallas as pl
from jax.experimental.pallas import tpu as pltpu

F32, BF16 = jnp.float32, jnp.bfloat16
EPS = 1e-6
POOL_WINDOWS = (2, 4, 8, 16)
MAX_WINDOW = 16
CONF_TAPS, SHORT_TAPS = 31, 3
CONF_HALO, POOL_HALO, SHORT_HALO = 256, 128, 16
ROW_CHUNK = 64
HEADS = 4
N_DEV = 8
ADAM_LR, ADAM_B1, ADAM_B2, ADAM_EPS, ADAM_WD, ADAM_STEP = 0.001, 0.9, 0.999, 1e-08, 0.01, 10
VMEM_LIMIT_V7X = 56 * 2**20
MESH = pl.DeviceIdType.MESH
ANY = pl.BlockSpec(memory_space=pl.ANY)
HBM = pl.BlockSpec(memory_space=pltpu.HBM)
SEM = pl.BlockSpec(memory_space=pltpu.SEMAPHORE)
EFFECT = pltpu.SideEffectType.DATAFLOW_SIDE_EFFECTING

TILE_NORM, TILE_MIX, TILE_FFN, TILE_ATTN, TILE_ADAM = 256, 512, 256, 512, 352


def _params(*sem):
    return pltpu.CompilerParams(dimension_semantics=sem, vmem_limit_bytes=VMEM_LIMIT_V7X)


def _sig(x):
    return 1.0 / (1.0 + jnp.exp(-x))


def _rms(x):
    r = lax.rsqrt(jnp.mean(x * x, axis=-1, keepdims=True) + EPS)
    return x * r, r


def _rms_bwd(dout, g, n, r):
    dn = dout * g
    return r * (dn - n * jnp.mean(dn * n, axis=-1, keepdims=True))


def _rows(tt, c):
    return pl.BlockSpec((tt, c), lambda i: (i, 0))


def _whole(shape):
    return pl.BlockSpec(shape, lambda i: (0,) * len(shape))


def _layer(shape, l):
    return pl.BlockSpec((None,) + shape, lambda i: (l,) + (0,) * len(shape))


def _colsum(x):
    return jnp.sum(x, axis=0, keepdims=True)


_DIMS = {"nn": (((1,), (0,)), ((), ())), "nt": (((1,), (1,)), ((), ())), "tn": (((0,), (0,)), ((), ()))}


def _matmul(a, b, mode, out_dtype, name, *, tm, tn, tk, a_outer=True):
    if mode == "nn":
        (m, k), (k2, n) = a.shape, b.shape
    elif mode == "nt":
        (m, k), (n, k2) = a.shape, b.shape
    else:
        (k, m), (k2, n) = a.shape, b.shape
    assert k == k2, (name, a.shape, b.shape)
    tm, tn, tk = min(tm, m), min(tn, n), min(tk, k)
    assert m % tm == 0 and n % tn == 0 and k % tk == 0, (name, m, n, k, tm, tn, tk)
    gm, gn, gk = m // tm, n // tn, k // tk

    def ij(g0, g1):
        return (g0, g1) if a_outer else (g1, g0)

    def a_map(g0, g1, kk):
        i, _ = ij(g0, g1)
        return (kk, i) if mode == "tn" else (i, kk)

    def b_map(g0, g1, kk):
        _, j = ij(g0, g1)
        return (j, kk) if mode == "nt" else (kk, j)

    def o_map(g0, g1, kk):
        return ij(g0, g1)

    a_block = (tk, tm) if mode == "tn" else (tm, tk)
    b_block = (tn, tk) if mode == "nt" else (tk, tn)
    dims = _DIMS[mode]

    def body(a_ref, b_ref, o_ref, *acc):
        p = lax.dot_general(a_ref[...].astype(BF16), b_ref[...].astype(BF16), dims, preferred_element_type=F32)
        if gk == 1:
            o_ref[...] = p.astype(o_ref.dtype)
        else:
            kk = pl.program_id(2)

            @pl.when(kk == 0)
            def _():
                acc[0][...] = p

            @pl.when(kk > 0)
            def _():
                acc[0][...] += p

            @pl.when(kk == gk - 1)
            def _():
                o_ref[...] = acc[0][...].astype(o_ref.dtype)

    return pl.pallas_call(
        body, name=name, grid=(gm, gn, gk) if a_outer else (gn, gm, gk),
        in_specs=[pl.BlockSpec(a_block, a_map), pl.BlockSpec(b_block, b_map)],
        out_specs=pl.BlockSpec((tm, tn), o_map),
        out_shape=jax.ShapeDtypeStruct((m, n), out_dtype),
        scratch_shapes=[pltpu.VMEM((tm, tn), F32)] if gk > 1 else [],
        compiler_params=_params("parallel", "parallel", "arbitrary"),
    )(a, b)


def _prenorm(x, g3, l, name):
    t, d = x.shape
    tt = min(TILE_NORM, t)

    def body(x_ref, g_ref, h_ref):
        n, _ = _rms(x_ref[...])
        h_ref[...] = (n * g_ref[...]).astype(BF16)

    return pl.pallas_call(
        body, name=name, grid=(t // tt,),
        in_specs=[_rows(tt, d), _layer((1, d), l)], out_specs=_rows(tt, d),
        out_shape=jax.ShapeDtypeStruct((t, d), BF16), compiler_params=_params("parallel"),
    )(x, g3)


def _norm_bwd(dxn, dh, x_in, gpre3, l, name, y_prev=None, gpost3=None, l_prev=None):
    t, d = x_in.shape
    tt = min(TILE_NORM, t)
    has_prev = y_prev is not None

    def body(*refs):
        if has_prev:
            dxn_ref, dh_ref, x_ref, g_ref, y_ref, g2_ref, dx_ref, dg_ref, dy_ref, dg2_ref = refs
        else:
            dxn_ref, dh_ref, x_ref, g_ref, dx_ref, dg_ref = refs

        @pl.when(pl.program_id(0) == 0)
        def _():
            dg_ref[...] = jnp.zeros_like(dg_ref)
            if has_prev:
                dg2_ref[...] = jnp.zeros_like(dg2_ref)

        dh_v = dh_ref[...]
        n, r = _rms(x_ref[...])
        dx = dxn_ref[...] + _rms_bwd(dh_v, g_ref[...], n, r)
        dx_ref[...] = dx
        dg_ref[...] += _colsum(dh_v * n)
        if has_prev:
            n2, r2 = _rms(y_ref[...])
            dy_ref[...] = _rms_bwd(dx, g2_ref[...], n2, r2).astype(BF16)
            dg2_ref[...] += _colsum(dx * n2)

    in_specs = [_rows(tt, d), _rows(tt, d), _rows(tt, d), _layer((1, d), l)]
    out_specs = [_rows(tt, d), _whole((1, d))]
    out_shape = [jax.ShapeDtypeStruct((t, d), F32), jax.ShapeDtypeStruct((1, d), F32)]
    args = [dxn, dh, x_in, gpre3]
    if has_prev:
        in_specs += [_rows(tt, d), _layer((1, d), l_prev)]
        out_specs += [_rows(tt, d), _whole((1, d))]
        out_shape += [jax.ShapeDtypeStruct((t, d), BF16), jax.ShapeDtypeStruct((1, d), F32)]
        args += [y_prev, gpost3]
    return pl.pallas_call(
        body, name=name, grid=(t // tt,), in_specs=in_specs, out_specs=out_specs, out_shape=out_shape,
        compiler_params=_params("arbitrary"),
    )(*args)


def _halves(tt):
    return [slice(0, tt // 2), slice(tt // 2, tt)] if tt % 32 == 0 else [slice(0, tt)]


def _matmul_resnorm(a, w, x, gpost3, l, gnext3, l2, name, tm=512):
    t, k = a.shape
    d = w.shape[1]
    tm = min(tm, t)

    def body(a_ref, w_ref, x_ref, gp_ref, gn_ref, y_ref, xo_ref, h_ref):
        wv, gp, gn = w_ref[...], gp_ref[...], gn_ref[...]
        for rows in _halves(tm):
            y = jnp.dot(a_ref[rows, :], wv, preferred_element_type=F32)
            y_ref[rows, :] = y
            n, _ = _rms(y)
            xn = x_ref[rows, :] + n * gp
            xo_ref[rows, :] = xn
            n2, _ = _rms(xn)
            h_ref[rows, :] = (n2 * gn).astype(BF16)

    return pl.pallas_call(
        body, name=name, grid=(t // tm,),
        in_specs=[_rows(tm, k), _whole((k, d)), _rows(tm, d), _layer((1, d), l), _layer((1, d), l2)],
        out_specs=[_rows(tm, d)] * 3,
        out_shape=[jax.ShapeDtypeStruct((t, d), F32), jax.ShapeDtypeStruct((t, d), F32),
                   jax.ShapeDtypeStruct((t, d), BF16)],
        compiler_params=_params("parallel"),
    )(a, w, x, gpost3, gnext3)


def _matmul_loss(a, w, x, gpost3, l, target, name, tm=512):
    t, k = a.shape
    d = w.shape[1]
    tm = min(tm, t)

    def body(a_ref, w_ref, x_ref, g_ref, t_ref, dxn_ref, dy_ref, dg_ref, loss_ref):
        @pl.when(pl.program_id(0) == 0)
        def _():
            dg_ref[...] = jnp.zeros_like(dg_ref)
            loss_ref[...] = jnp.zeros_like(loss_ref)

        wv, g = w_ref[...], g_ref[...]
        for rows in _halves(tm):
            n, r = _rms(jnp.dot(a_ref[rows, :], wv, preferred_element_type=F32))
            diff = x_ref[rows, :] + n * g - t_ref[rows, :]
            loss_ref[...] += 0.5 * jnp.sum(jnp.mean(diff * diff, axis=-1, keepdims=True))
            dxn = diff * (1.0 / d)
            dxn_ref[rows, :] = dxn
            dy_ref[rows, :] = _rms_bwd(dxn, g, n, r).astype(BF16)
            dg_ref[...] += _colsum(dxn * n)

    return pl.pallas_call(
        body, name=name, grid=(t // tm,),
        in_specs=[_rows(tm, k), _whole((k, d)), _rows(tm, d), _layer((1, d), l), _rows(tm, d)],
        out_specs=[_rows(tm, d), _rows(tm, d), _whole((1, d)), _whole((1, 128))],
        out_shape=[jax.ShapeDtypeStruct((t, d), F32), jax.ShapeDtypeStruct((t, d), BF16),
                   jax.ShapeDtypeStruct((1, d), F32), jax.ShapeDtypeStruct((1, 128), F32)],
        compiler_params=_params("arbitrary"),
    )(a, w, x, gpost3, target)


def _matmul_norm_bwd(a, w, mode, dxn, x_in, gpre3, l, name, y_prev=None, gpost3=None, l_prev=None, tm=512):
    t, k = a.shape
    d = x_in.shape[1]
    tm = min(tm, t)
    has_prev = y_prev is not None
    dims = _DIMS[mode]

    def body(*refs):
        if has_prev:
            a_ref, w_ref, dxn_ref, x_ref, g_ref, y_ref, g2_ref, dx_ref, dg_ref, dy_ref, dg2_ref = refs
        else:
            a_ref, w_ref, dxn_ref, x_ref, g_ref, dx_ref, dg_ref = refs

        @pl.when(pl.program_id(0) == 0)
        def _():
            dg_ref[...] = jnp.zeros_like(dg_ref)
            if has_prev:
                dg2_ref[...] = jnp.zeros_like(dg2_ref)

        wv, g = w_ref[...], g_ref[...]
        for rows in _halves(tm):
            dh = lax.dot_general(a_ref[rows, :], wv, dims, preferred_element_type=F32)
            n, r = _rms(x_ref[rows, :])
            dx = dxn_ref[rows, :] + _rms_bwd(dh, g, n, r)
            dx_ref[rows, :] = dx
            dg_ref[...] += _colsum(dh * n)
            if has_prev:
                n2, r2 = _rms(y_ref[rows, :])
                dy_ref[rows, :] = _rms_bwd(dx, g2_ref[...], n2, r2).astype(BF16)
                dg2_ref[...] += _colsum(dx * n2)

    in_specs = [_rows(tm, k), _whole(w.shape), _rows(tm, d), _rows(tm, d), _layer((1, d), l)]
    out_specs = [_rows(tm, d), _whole((1, d))]
    out_shape = [jax.ShapeDtypeStruct((t, d), F32), jax.ShapeDtypeStruct((1, d), F32)]
    args = [a, w, dxn, x_in, gpre3]
    if has_prev:
        in_specs += [_rows(tm, d), _layer((1, d), l_prev)]
        out_specs += [_rows(tm, d), _whole((1, d))]
        out_shape += [jax.ShapeDtypeStruct((t, d), BF16), jax.ShapeDtypeStruct((1, d), F32)]
        args += [y_prev, gpost3]
    return pl.pallas_call(
        body, name=name, grid=(t // tm,), in_specs=in_specs, out_specs=out_specs, out_shape=out_shape,
        compiler_params=_params("arbitrary"),
    )(*args)


def _to_steps(a):
    t = a.shape[0]
    return a.reshape(8, t // 8, -1).transpose(1, 0, 2).reshape(a.shape)


def _from_steps(a):
    t = a.shape[0]
    return a.reshape(t // 8, 8, -1).transpose(1, 0, 2).reshape(a.shape)


def _al(v):
    return v if isinstance(v, int) else pl.multiple_of(v, 8)


def _chunks(n_rows, fn, unroll=1):
    def step(r, carry):
        fn(pl.multiple_of(r * ROW_CHUNK, ROW_CHUNK))
        return carry
    lax.fori_loop(0, n_rows // ROW_CHUNK, step, 0, unroll=unroll)


def _fold8(a):
    return a.reshape(a.shape[0] // 8, 8, a.shape[1]).sum(axis=0)


def _shift_down(a):
    row = lax.broadcasted_iota(jnp.int32, a.shape, 0)
    return jnp.where(row % 8 == 0, 0.0, pltpu.roll(a, 1, 0))


def _shift_up(a):
    row = lax.broadcasted_iota(jnp.int32, a.shape, 0)
    return jnp.where(row % 8 == 7, 0.0, pltpu.roll(a, a.shape[0] - 1, 0))


def _prev_block(h, c, tt, t):
    return pl.BlockSpec((h, c), lambda i: (jnp.where(i == 0, t // h - 1, i * (tt // h) - 1), 0))


def _next_block(h, c, tt, t):
    return pl.BlockSpec((h, c), lambda i: (jnp.where(i == t // tt - 1, 0, (i + 1) * (tt // h)), 0))


def _taps(w_ref, buf, start, taps, rc, lanes=slice(None)):
    acc = w_ref[0:1, lanes] * buf[pl.ds(_al(start), rc), lanes]
    for k in range(1, taps):
        acc = acc + w_ref[k:k + 1, lanes] * buf[pl.ds(_al(start + 8 * k), rc), lanes]
    return acc


def _taps_rev(w_ref, buf, start, taps, rc, lanes=slice(None)):
    acc = w_ref[0:1, lanes] * buf[pl.ds(_al(start + 8 * (taps - 1)), rc), lanes]
    for k in range(1, taps):
        acc = acc + w_ref[k:k + 1, lanes] * buf[pl.ds(_al(start + 8 * (taps - 1 - k)), rc), lanes]
    return acc


def _mix_dims(d):
    dp = d // 4
    dc = 3 * d // 8
    ds = d - dp - dc
    oa, og = dp, dp + dc
    ob = dp + 2 * dc
    oc, ox = ob + ds, ob + 2 * ds
    return dp, dc, ds, oa, og, ob, oc, ox, ox + ds


def _pool_consts(dp):
    win = jnp.repeat(jnp.asarray(POOL_WINDOWS, F32), dp // len(POOL_WINDOWS))[None, :]
    mask = (jnp.arange(MAX_WINDOW, dtype=F32)[:, None] < win).astype(F32)
    return mask, win


def _pool_count(row0, rc, dp, seg, wl):
    r = lax.broadcasted_iota(jnp.int32, (rc, dp), 0) + row0
    return jnp.minimum(((r & 7) * seg + (r >> 3) + 1).astype(F32), wl)


def _mix_fwd(z, mbd, pscale3, wdw, bdw3, lng3, lnb3, wsc, l, name):
    t, din = z.shape
    dp, dc, ds, oa, og, ob, oc, ox, din2 = _mix_dims(din * 8 // 17)
    assert din2 == din
    d = ob
    tt = min(TILE_MIX, t)
    hp, hc, hs, rc = POOL_HALO, CONF_HALO, SHORT_HALO, ROW_CHUNK
    assert tt % hc == 0 and t % tt == 0
    seg = t // 8
    pmask, wlane = _pool_consts(dp)

    def body(z_ref, zpa_ref, zpb_ref, mbd_ref, ps_ref, pmask_ref, wl_ref, wdw_ref, bdw_ref, lng_ref, lnb_ref, wsc_ref,
             cat_ref, c_ref, pbuf, vbuf, sbuf):
        i = pl.program_id(0)
        pbuf[0:hp, :] = zpa_ref[hc - hp:hc, 0:dp]

        def prev(r0):
            rows = pl.ds(r0, rc)
            vbuf[rows, :] = zpa_ref[rows, oa:oa + dc] * _sig(zpa_ref[rows, og:og + dc])
        _chunks(hc, prev)
        sbuf[0:hs, :] = zpb_ref[:, oc:oc + ds] * zpb_ref[:, ox:ox + ds]

        @pl.when(i == 0)
        def _():
            pbuf[0:hp, :] = _shift_down(pbuf[0:hp, :])
            vbuf[0:hc, :] = _shift_down(vbuf[0:hc, :])
            sbuf[0:hs, :] = _shift_down(sbuf[0:hs, :])

        mbd_v, ps, wl = mbd_ref[...], ps_ref[...], wl_ref[...]
        bdw, lng, lnb = bdw_ref[...], lng_ref[...], lnb_ref[...]

        def step(r0):
            rows = pl.ds(r0, rc)
            zp = z_ref[rows, 0:dp]
            pbuf[pl.ds(_al(hp + r0), rc), :] = zp
            vbuf[pl.ds(_al(hc + r0), rc), :] = z_ref[rows, oa:oa + dc] * _sig(z_ref[rows, og:og + dc])
            sbuf[pl.ds(_al(hs + r0), rc), :] = z_ref[rows, oc:oc + ds] * z_ref[rows, ox:ox + ds]
            pooled = _taps_rev(pmask_ref, pbuf, r0 + hp - 8 * (MAX_WINDOW - 1), MAX_WINDOW, rc)
            pooled = pooled / _pool_count(i * tt + r0, rc, dp, seg, wl) - zp
            pm = jnp.dot(pooled.astype(BF16), mbd_v, preferred_element_type=F32)
            cat_ref[rows, 0:dp] = (pm * ps).astype(BF16)
            c = _taps(wdw_ref, vbuf, r0 + hc - 8 * (CONF_TAPS - 1), CONF_TAPS, rc) + bdw
            c_ref[rows, :] = c
            xc = c - jnp.mean(c, axis=-1, keepdims=True)
            nrm = xc * lax.rsqrt(jnp.mean(xc * xc, axis=-1, keepdims=True) + EPS)
            yln = nrm * lng + lnb
            cat_ref[rows, dp:dp + dc] = (yln * _sig(yln)).astype(BF16)
            cv = _taps(wsc_ref, sbuf, r0 + hs - 8 * (SHORT_TAPS - 1), SHORT_TAPS, rc)
            cat_ref[rows, dp + dc:d] = (z_ref[rows, ob:ob + ds] * cv).astype(BF16)
        _chunks(tt, step, unroll=2)

    return pl.pallas_call(
        body, name=name, grid=(t // tt,),
        in_specs=[_rows(tt, din), _prev_block(hc, d, tt, t), _prev_block(hs, din, tt, t),
                  _layer((dp, dp), l), _layer((1, dp), l), _whole((MAX_WINDOW, dp)), _whole((1, dp)),
                  _layer((CONF_TAPS, dc), l), _layer((1, dc), l), _layer((1, dc), l), _layer((1, dc), l),
                  _layer((SHORT_TAPS, ds), l)],
        out_specs=[_rows(tt, d), _rows(tt, dc)],
        out_shape=[jax.ShapeDtypeStruct((t, d), BF16), jax.ShapeDtypeStruct((t, dc), F32)],
        scratch_shapes=[pltpu.VMEM((hp + tt, dp), F32), pltpu.VMEM((hc + tt, dc), F32), pltpu.VMEM((hs + tt, ds), F32)],
        compiler_params=_params("parallel"),
    )(z, z, z, mbd, pscale3, pmask, wlane, wdw, bdw3, lng3, lnb3, wsc)


def _mix_bwd(dcat, z, c, mbd, pscale3, wdw, lng3, lnb3, wsc, l, name):
    t, din = z.shape
    dp, dc, ds, oa, og, ob, oc, ox, _ = _mix_dims(din * 8 // 17)
    d = ob
    tt = min(TILE_MIX, t)
    nt = t // tt
    hp, hc, hs, rc = POOL_HALO, CONF_HALO, SHORT_HALO, ROW_CHUNK
    assert tt % hc == 0 and t % tt == 0 and tt >= 8 * MAX_WINDOW
    seg = t // 8
    pmask, wlane = _pool_consts(dp)

    def body(dcat_ref, dcn_ref, z_ref, zpa_ref, zpb_ref, znb_ref, c_ref, cn_ref, mbd_ref, ps_ref, pmask_ref, wl_ref,
             wdw_ref, lng_ref, lnb_ref, wsc_ref,
             dz_ref, dmbd_ref, dps_ref, dwdw_ref, dbdw_ref, dlng_ref, dlnb_ref, dwsc_ref,
             pbuf, qbuf, dpbuf, pbf, vbuf, dcbuf, sbuf, dsbuf, dw8, ds8, ln8, ps8):
        i = pl.program_id(0)
        first, last = i == 0, i == nt - 1

        @pl.when(first)
        def _():
            for ref in (dmbd_ref, dw8, ds8, ln8, ps8):
                ref[...] = jnp.zeros_like(ref)

        mbd_v, ps, wl = mbd_ref[...], ps_ref[...], wl_ref[...]
        lng, lnb = lng_ref[...], lnb_ref[...]

        def ln_silu_bwd(cc, dyb):
            xc = cc - jnp.mean(cc, axis=-1, keepdims=True)
            rstd = lax.rsqrt(jnp.mean(xc * xc, axis=-1, keepdims=True) + EPS)
            nrm = xc * rstd
            yln = nrm * lng + lnb
            s = _sig(yln)
            dyln = dyb * (s * (1.0 + yln * (1.0 - s)))
            dn = dyln * lng
            dcc = rstd * (dn - jnp.mean(dn, axis=-1, keepdims=True) - nrm * jnp.mean(dn * nrm, axis=-1, keepdims=True))
            return dcc, dyln, nrm

        pbuf[0:hp, :] = zpa_ref[hc - hp:hc, 0:dp]

        def prev(r0):
            rows = pl.ds(r0, rc)
            vbuf[rows, :] = zpa_ref[rows, oa:oa + dc] * _sig(zpa_ref[rows, og:og + dc])
        _chunks(hc, prev)
        sbuf[0:hs, :] = zpb_ref[:, oc:oc + ds] * zpb_ref[:, ox:ox + ds]

        @pl.when(first)
        def _():
            pbuf[0:hp, :] = _shift_down(pbuf[0:hp, :])
            vbuf[0:hc, :] = _shift_down(vbuf[0:hc, :])
            sbuf[0:hs, :] = _shift_down(sbuf[0:hs, :])

        def nxt(r0):
            rows = pl.ds(r0, rc)
            dcc, _, _ = ln_silu_bwd(cn_ref[rows, :], dcn_ref[rows, dp:dp + dc])
            dcbuf[pl.ds(_al(tt + r0), rc), :] = dcc
        _chunks(hc, nxt, unroll=4)
        dpm_n = (dcn_ref[0:hp, 0:dp] * ps).astype(BF16)
        qbuf[tt:tt + hp, :] = lax.dot_general(dpm_n, mbd_v, _DIMS["nt"], preferred_element_type=F32) / wl
        dsbuf[tt:tt + hs, :] = dcn_ref[0:hs, dp + dc:d] * znb_ref[:, ob:ob + ds]

        @pl.when(last)
        def _():
            dcbuf[tt:tt + hc, :] = _shift_up(dcbuf[tt:tt + hc, :])
            qbuf[tt:tt + hp, :] = _shift_up(qbuf[tt:tt + hp, :])
            dsbuf[tt:tt + hs, :] = _shift_up(dsbuf[tt:tt + hs, :])

        def fill(r0):
            rows = pl.ds(r0, rc)
            zp = z_ref[rows, 0:dp]
            pbuf[pl.ds(_al(hp + r0), rc), :] = zp
            vbuf[pl.ds(_al(hc + r0), rc), :] = z_ref[rows, oa:oa + dc] * _sig(z_ref[rows, og:og + dc])
            sbuf[pl.ds(_al(hs + r0), rc), :] = z_ref[rows, oc:oc + ds] * z_ref[rows, ox:ox + ds]
            pooled = _taps_rev(pmask_ref, pbuf, r0 + hp - 8 * (MAX_WINDOW - 1), MAX_WINDOW, rc)
            pbf[rows, :] = (pooled / _pool_count(i * tt + r0, rc, dp, seg, wl) - zp).astype(BF16)
            dcc, dyln, nrm = ln_silu_bwd(c_ref[rows, :], dcat_ref[rows, dp:dp + dc])
            dcbuf[rows, :] = dcc
            ln8[0] += _fold8(dyln * nrm)
            ln8[1] += _fold8(dyln)
            ln8[2] += _fold8(dcc)
            dsbuf[rows, :] = dcat_ref[rows, dp + dc:d] * z_ref[rows, ob:ob + ds]
        _chunks(tt, fill, unroll=4)

        pb = pbf[...]
        dya = dcat_ref[:, 0:dp]
        ps8[...] += _fold8(dya * jnp.dot(pb, mbd_v, preferred_element_type=F32))
        dpm = (dya * ps).astype(BF16)
        dmbd_ref[...] += lax.dot_general(pb, dpm, _DIMS["tn"], preferred_element_type=F32)
        dpbuf[...] = lax.dot_general(dpm, mbd_v, _DIMS["nt"], preferred_element_type=F32)

        def quot(r0):
            rows = pl.ds(r0, rc)
            qbuf[rows, :] = dpbuf[rows, :] / _pool_count(i * tt + r0, rc, dp, seg, wl)
        _chunks(tt, quot)

        def back(r0):
            rows = pl.ds(r0, rc)
            dzp = _taps(pmask_ref, qbuf, r0, MAX_WINDOW, rc) - dpbuf[rows, :]
            dz_ref[rows, 0:dp] = dzp.astype(BF16)
            dcc = dcbuf[rows, :]
            for k in range(CONF_TAPS):
                dw8[k] += _fold8(dcc * vbuf[pl.ds(_al(r0 + hc - 8 * (CONF_TAPS - 1 - k)), rc), :])
            dv = _taps_rev(wdw_ref, dcbuf, r0, CONF_TAPS, rc)
            za = z_ref[rows, oa:oa + dc]
            sg = _sig(z_ref[rows, og:og + dc])
            dz_ref[rows, oa:oa + dc] = (dv * sg).astype(BF16)
            dz_ref[rows, og:og + dc] = (dv * za * sg * (1.0 - sg)).astype(BF16)
            cv = _taps(wsc_ref, sbuf, r0 + hs - 8 * (SHORT_TAPS - 1), SHORT_TAPS, rc)
            dz_ref[rows, ob:ob + ds] = (dcat_ref[rows, dp + dc:d] * cv).astype(BF16)
            dcv = dsbuf[rows, :]
            for k in range(SHORT_TAPS):
                ds8[k] += _fold8(dcv * sbuf[pl.ds(_al(r0 + hs - 8 * (SHORT_TAPS - 1 - k)), rc), :])
            dpv = _taps_rev(wsc_ref, dsbuf, r0, SHORT_TAPS, rc)
            dz_ref[rows, oc:oc + ds] = (dpv * z_ref[rows, ox:ox + ds]).astype(BF16)
            dz_ref[rows, ox:ox + ds] = (dpv * z_ref[rows, oc:oc + ds]).astype(BF16)
        _chunks(tt, back, unroll=2)

        @pl.when(last)
        def _():
            dps_ref[...] = jnp.sum(ps8[...], axis=0, keepdims=True)
            dwdw_ref[...] = jnp.sum(dw8[...], axis=1)
            dwsc_ref[...] = jnp.sum(ds8[...], axis=1)
            dlng_ref[...] = jnp.sum(ln8[0], axis=0, keepdims=True)
            dlnb_ref[...] = jnp.sum(ln8[1], axis=0, keepdims=True)
            dbdw_ref[...] = jnp.sum(ln8[2], axis=0, keepdims=True)

    return pl.pallas_call(
        body, name=name, grid=(nt,),
        in_specs=[_rows(tt, d), _next_block(hc, d, tt, t),
                  _rows(tt, din), _prev_block(hc, d, tt, t), _prev_block(hs, din, tt, t), _next_block(hs, din, tt, t),
                  _rows(tt, dc), _next_block(hc, dc, tt, t),
                  _layer((dp, dp), l), _layer((1, dp), l), _whole((MAX_WINDOW, dp)), _whole((1, dp)),
                  _layer((CONF_TAPS, dc), l), _layer((1, dc), l), _layer((1, dc), l), _layer((SHORT_TAPS, ds), l)],
        out_specs=[_rows(tt, din), _whole((dp, dp)), _whole((1, dp)), _whole((CONF_TAPS, dc)), _whole((1, dc)),
                   _whole((1, dc)), _whole((1, dc)), _whole((SHORT_TAPS, ds))],
        out_shape=[jax.ShapeDtypeStruct((t, din), BF16), jax.ShapeDtypeStruct((dp, dp), F32),
                   jax.ShapeDtypeStruct((1, dp), F32), jax.ShapeDtypeStruct((CONF_TAPS, dc), F32),
                   jax.ShapeDtypeStruct((1, dc), F32), jax.ShapeDtypeStruct((1, dc), F32),
                   jax.ShapeDtypeStruct((1, dc), F32), jax.ShapeDtypeStruct((SHORT_TAPS, ds), F32)],
        scratch_shapes=[pltpu.VMEM((hp + tt, dp), F32), pltpu.VMEM((tt + hp, dp), F32), pltpu.VMEM((tt, dp), F32),
                        pltpu.VMEM((tt, dp), BF16), pltpu.VMEM((hc + tt, dc), F32), pltpu.VMEM((tt + hc, dc), F32),
                        pltpu.VMEM((hs + tt, ds), F32), pltpu.VMEM((tt + hs, ds), F32),
                        pltpu.VMEM((CONF_TAPS, 8, dc), F32), pltpu.VMEM((SHORT_TAPS, 8, ds), F32),
                        pltpu.VMEM((3, 8, dc), F32), pltpu.VMEM((8, dp), F32)],
        compiler_params=_params("arbitrary"),
    )(dcat, dcat, z, z, z, z, c, c, mbd, pscale3, pmask, wlane, wdw, lng3, lnb3, wsc)


def _softmax_rows(qh, kh, scale):
    s = lax.dot_general(qh, kh, _DIMS["nt"], preferred_element_type=F32) * scale
    e = jnp.exp(s - jnp.max(s, axis=-1, keepdims=True))
    return e / jnp.sum(e, axis=-1, keepdims=True)


def _attn_fwd(q, k, v, name):
    t, d = q.shape
    m = k.shape[0]
    hd = d // HEADS
    scale = hd ** -0.5
    tt = min(TILE_ATTN, t)

    def body(q_ref, k_ref, v_ref, o_ref):
        for h in range(HEADS):
            sl = slice(h * hd, (h + 1) * hd)
            p = _softmax_rows(q_ref[:, sl], k_ref[:, sl], scale)
            o_ref[:, sl] = jnp.dot(p.astype(BF16), v_ref[:, sl], preferred_element_type=F32).astype(BF16)

    return pl.pallas_call(
        body, name=name, grid=(t // tt,),
        in_specs=[_rows(tt, d), _whole((m, d)), _whole((m, d))], out_specs=_rows(tt, d),
        out_shape=jax.ShapeDtypeStruct((t, d), BF16), compiler_params=_params("parallel"),
    )(q, k, v)


def _attn_bwd(q, k, v, do, name):
    t, d = q.shape
    m = k.shape[0]
    hd = d // HEADS
    scale = hd ** -0.5
    tt = min(TILE_ATTN, t)

    def body(q_ref, k_ref, v_ref, do_ref, dq_ref, dk_ref, dv_ref):
        @pl.when(pl.program_id(0) == 0)
        def _():
            dk_ref[...] = jnp.zeros_like(dk_ref)
            dv_ref[...] = jnp.zeros_like(dv_ref)

        for h in range(HEADS):
            sl = slice(h * hd, (h + 1) * hd)
            qh, kh, vh, doh = q_ref[:, sl], k_ref[:, sl], v_ref[:, sl], do_ref[:, sl]
            p = _softmax_rows(qh, kh, scale)
            dv_ref[:, sl] += lax.dot_general(p.astype(BF16), doh, _DIMS["tn"], preferred_element_type=F32)
            dp = lax.dot_general(doh, vh, _DIMS["nt"], preferred_element_type=F32)
            ds = (p * (dp - jnp.sum(dp * p, axis=-1, keepdims=True)) * scale).astype(BF16)
            dq_ref[:, sl] = jnp.dot(ds, kh, preferred_element_type=F32).astype(BF16)
            dk_ref[:, sl] += lax.dot_general(ds, qh, _DIMS["tn"], preferred_element_type=F32)

    return pl.pallas_call(
        body, name=name, grid=(t // tt,),
        in_specs=[_rows(tt, d), _whole((m, d)), _whole((m, d)), _rows(tt, d)],
        out_specs=[_rows(tt, d), _whole((m, d)), _whole((m, d))],
        out_shape=[jax.ShapeDtypeStruct((t, d), BF16), jax.ShapeDtypeStruct((m, d), F32),
                   jax.ShapeDtypeStruct((m, d), F32)],
        compiler_params=_params("arbitrary"),
    )(q, k, v, do)


def _lane_chunks(f):
    w = 256 if f % 256 == 0 else 128 if f % 128 == 0 else f
    return [(c0, w) for c0 in range(0, f, w)]


def _ffn_act_fwd(u, wc, l, name):
    t, f2 = u.shape
    f = f2 // 2
    tt = min(TILE_FFN, t)
    hs, rc = SHORT_HALO, ROW_CHUNK
    lanes = _lane_chunks(f)

    def body(u_ref, up_ref, wc_ref, a_ref, ubuf):
        ubuf[0:hs, :] = up_ref[...]

        @pl.when(pl.program_id(0) == 0)
        def _():
            ubuf[0:hs, :] = _shift_down(ubuf[0:hs, :])

        def step(r0):
            rows = pl.ds(r0, rc)
            ubuf[pl.ds(_al(hs + r0), rc), :] = u_ref[rows, :]
            start = r0 + hs - 8 * (SHORT_TAPS - 1)
            for c0, cw in lanes:
                g = _taps(wc_ref, ubuf, start, SHORT_TAPS, rc, slice(c0, c0 + cw))
                vv = _taps(wc_ref, ubuf, start, SHORT_TAPS, rc, slice(f + c0, f + c0 + cw))
                a_ref[rows, c0:c0 + cw] = (g * _sig(g) * vv).astype(BF16)
        _chunks(tt, step, unroll=2)

    return pl.pallas_call(
        body, name=name, grid=(t // tt,),
        in_specs=[_rows(tt, f2), _prev_block(hs, f2, tt, t), _layer((SHORT_TAPS, f2), l)],
        out_specs=_rows(tt, f), out_shape=jax.ShapeDtypeStruct((t, f), BF16),
        scratch_shapes=[pltpu.VMEM((hs + tt, f2), F32)], compiler_params=_params("parallel"),
    )(u, u, wc)


def _ffn_act_bwd(u, da, wc, l, name):
    t, f2 = u.shape
    f = f2 // 2
    tt = min(TILE_FFN, t)
    nt = t // tt
    hs, rc = SHORT_HALO, ROW_CHUNK
    lanes = _lane_chunks(f)

    def body(u_ref, up_ref, un_ref, da_ref, dan_ref, wc_ref, du_ref, dwc_ref, ubuf, danbuf, dbuf, dw8):
        i = pl.program_id(0)
        first, last = i == 0, i == nt - 1
        ubuf[0:hs, :] = up_ref[...]
        ubuf[hs + tt:hs + tt + hs, :] = un_ref[...]
        danbuf[...] = dan_ref[...]

        @pl.when(first)
        def _():
            dw8[...] = jnp.zeros_like(dw8)
            ubuf[0:hs, :] = _shift_down(ubuf[0:hs, :])

        @pl.when(last)
        def _():
            ubuf[hs + tt:hs + tt + hs, :] = _shift_up(ubuf[hs + tt:hs + tt + hs, :])
            danbuf[...] = _shift_up(danbuf[...])

        def fill(r0):
            ubuf[pl.ds(_al(hs + r0), rc), :] = u_ref[pl.ds(r0, rc), :]
        _chunks(tt, fill)

        def conv_grads(r0, n, da_rows):
            start = r0 + hs - 8 * (SHORT_TAPS - 1)
            for c0, cw in lanes:
                sl_g, sl_v = slice(c0, c0 + cw), slice(f + c0, f + c0 + cw)
                g = _taps(wc_ref, ubuf, start, SHORT_TAPS, n, sl_g)
                vv = _taps(wc_ref, ubuf, start, SHORT_TAPS, n, sl_v)
                dav = da_rows(c0, cw)
                sg = _sig(g)
                dbuf[pl.ds(_al(r0), n), sl_g] = dav * vv * (sg * (1.0 + g * (1.0 - sg)))
                dbuf[pl.ds(_al(r0), n), sl_v] = dav * (g * sg)

        _chunks(tt, lambda r0: conv_grads(r0, rc, lambda c0, cw: da_ref[pl.ds(r0, rc), c0:c0 + cw]), unroll=2)
        conv_grads(tt, hs, lambda c0, cw: danbuf[:, c0:c0 + cw])

        def back(r0):
            rows = pl.ds(r0, rc)
            for c0, cw in lanes:
                for off in (c0, f + c0):
                    sl = slice(off, off + cw)
                    du_ref[rows, sl] = _taps_rev(wc_ref, dbuf, r0, SHORT_TAPS, rc, sl).astype(BF16)
                    dd = dbuf[rows, sl]
                    for k in range(SHORT_TAPS):
                        dw8[k, :, sl] += _fold8(dd * ubuf[pl.ds(_al(r0 + hs - 8 * (SHORT_TAPS - 1 - k)), rc), sl])
        _chunks(tt, back, unroll=2)

        @pl.when(last)
        def _():
            dwc_ref[...] = jnp.sum(dw8[...], axis=1)

    return pl.pallas_call(
        body, name=name, grid=(nt,),
        in_specs=[_rows(tt, f2), _prev_block(hs, f2, tt, t), _next_block(hs, f2, tt, t),
                  _rows(tt, f), _next_block(hs, f, tt, t), _layer((SHORT_TAPS, f2), l)],
        out_specs=[_rows(tt, f2), _whole((SHORT_TAPS, f2))],
        out_shape=[jax.ShapeDtypeStruct((t, f2), BF16), jax.ShapeDtypeStruct((SHORT_TAPS, f2), F32)],
        scratch_shapes=[pltpu.VMEM((hs + tt + hs, f2), F32), pltpu.VMEM((hs, f), F32), pltpu.VMEM((tt + hs, f2), F32),
                        pltpu.VMEM((SHORT_TAPS, 8, f2), F32)],
        compiler_params=_params("arbitrary"),
    )(u, u, u, da, da, wc)


def _place():
    return lax.axis_index("x"), lax.axis_index("y"), lax.axis_index("c")


def _flip(v, bit):
    return 1 - v if bit else v


def _peers(x, y, c):
    out = []
    for kk in range(1, N_DEV):
        px, py, pc = _flip(x, kk & 4), _flip(y, kk & 2), _flip(c, kk & 1)
        out.append((kk - 1, (px, py, pc), 4 * px + 2 * py + pc))
    return out


def _allgather(shards, name):
    nt = len(shards)

    def body(*refs):
        srcs, outs = refs[:nt], refs[nt:2 * nt]
        send_sems, recv_sems, local_sems = refs[2 * nt:]
        x, y, c = _place()
        me, sibling = (x, y, c), (x, y, 1 - c)
        chips = [(1 - x, y), (x, 1 - y), (1 - x, 1 - y)]

        def rows(ti, px, py, pc):
            r = srcs[ti].shape[1]
            return outs[ti].at[:, pl.ds((4 * px + 2 * py + pc) * r, r), :]

        def copy(ti, kk, block, to, src=None):
            return pltpu.make_async_remote_copy(
                src_ref=rows(ti, *block) if src is None else src, dst_ref=rows(ti, *block),
                send_sem=send_sems.at[ti, kk], recv_sem=recv_sems.at[ti, kk], device_id=to, device_id_type=MESH)

        mine = [pltpu.make_async_copy(srcs[ti], rows(ti, *me), local_sems.at[ti]) for ti in range(nt)]
        for cp in mine:
            cp.start()
        first = []
        for ti in range(nt):
            first.append(copy(ti, 0, me, sibling, src=srcs[ti]))
            first += [copy(ti, 1 + j, me, (*chip, c), src=srcs[ti]) for j, chip in enumerate(chips)]
        for cp in first:
            cp.start()
        passed = []
        for j, chip in enumerate(chips):
            for ti in range(nt):
                copy(ti, 1 + j, (*chip, c), me).wait_recv()
                fwd = copy(ti, 4 + j, (*chip, c), sibling)
                fwd.start()
                passed.append(fwd)
        for ti in range(nt):
            copy(ti, 0, sibling, me).wait_recv()
            for j, chip in enumerate(chips):
                copy(ti, 4 + j, (*chip, 1 - c), me).wait_recv()
        for cp in first + passed:
            cp.wait_send()
        for cp in mine:
            cp.wait()

    return pl.pallas_call(
        body, name=name,
        in_specs=[ANY] * nt, out_specs=[ANY] * nt,
        out_shape=[jax.ShapeDtypeStruct((s.shape[0], N_DEV * s.shape[1], s.shape[2]), s.dtype) for s in shards],
        scratch_shapes=[pltpu.SemaphoreType.DMA((nt, 7)), pltpu.SemaphoreType.DMA((nt, 7)),
                        pltpu.SemaphoreType.DMA((nt,))],
    )(*shards)


def _gather_piece(src, land, me, to):
    r = src.shape[0]
    return src, land.at[pl.ds(me * r, r), :]


def _scatter_piece(l):
    def piece(src, land, me, to):
        r = src.shape[0] // N_DEV
        return src.at[pl.ds(to * r, r), :], land.at[l, me]
    return piece


def _split_start(srcs, lands, piece, after, name):
    nt = len(srcs)

    def body(*refs):
        src_refs, land_refs = refs[:nt], refs[nt:2 * nt]
        send_sems, recv_sems, local_sems, token = refs[2 * nt + 1], refs[2 * nt + 2], refs[2 * nt + 3], refs[4 * nt + 4]
        x, y, c = _place()
        me = 4 * x + 2 * y + c
        for ti in range(nt):
            for slot, peer, flat in _peers(x, y, c):
                src, dst = piece(src_refs[ti], land_refs[ti], me, flat)
                pltpu.make_async_remote_copy(
                    src_ref=src, dst_ref=dst, send_sem=send_sems.at[7 * ti + slot], recv_sem=recv_sems.at[7 * ti + slot],
                    device_id=peer, device_id_type=MESH).start()
        for ti in range(nt):
            pltpu.make_async_copy(*piece(src_refs[ti], land_refs[ti], me, me), local_sems.at[ti]).start()
        token[...] = jnp.zeros_like(token)

    both = list(srcs) + list(lands)
    return pl.pallas_call(
        body, name=name,
        in_specs=[HBM] * (2 * nt) + [ANY],
        out_specs=[SEM, SEM, SEM] + [HBM] * (2 * nt) + [pl.BlockSpec(memory_space=pltpu.VMEM)],
        out_shape=[pltpu.SemaphoreType.DMA((7 * nt,)), pltpu.SemaphoreType.DMA((7 * nt,)), pltpu.SemaphoreType.DMA((nt,))]
        + [pltpu.HBM(a.shape, a.dtype) for a in both] + [jax.ShapeDtypeStruct((8, 128), F32)],
        input_output_aliases={i: i + 3 for i in range(2 * nt)},
        compiler_params=pltpu.CompilerParams(has_side_effects=EFFECT),
    )(*[pltpu.with_memory_space_constraint(a, pltpu.HBM) for a in both], after)


def _split_wait(started, after, piece, name):
    send_sems, recv_sems, local_sems, *both = started[:-1]
    nt = len(both) // 2

    def body(*refs):
        src_refs, land_refs = refs[:nt], refs[nt:2 * nt]
        send_ref, recv_ref, local_ref = refs[2 * nt], refs[2 * nt + 1], refs[2 * nt + 2]
        x, y, c = _place()
        me = 4 * x + 2 * y + c
        for ti in range(nt):
            src, dst = piece(src_refs[ti], land_refs[ti], me, me)
            for slot in range(N_DEV - 1):
                cp = pltpu.make_async_remote_copy(
                    src_ref=src, dst_ref=dst, send_sem=send_ref.at[7 * ti + slot], recv_sem=recv_ref.at[7 * ti + slot],
                    device_id=(x, y, c), device_id_type=MESH)
                cp.wait_send()
                cp.wait_recv()
            pltpu.make_async_copy(src, dst, local_ref.at[ti]).wait()

    outs = pl.pallas_call(
        body, name=name,
        in_specs=[HBM] * (2 * nt) + [SEM, SEM, SEM, ANY], out_specs=[HBM] * (2 * nt),
        out_shape=[pltpu.HBM(a.shape, a.dtype) for a in both],
        input_output_aliases={i: i for i in range(2 * nt)},
        compiler_params=pltpu.CompilerParams(has_side_effects=EFFECT),
    )(*both, send_sems, recv_sems, local_sems, after)
    return outs[nt:]


def _adam(w, g, m, v):
    m2 = ADAM_B1 * m + (1.0 - ADAM_B1) * g
    v2 = ADAM_B2 * v + (1.0 - ADAM_B2) * (g * g)
    m_hat = m2 / (1.0 - ADAM_B1 ** ADAM_STEP)
    v_hat = v2 / (1.0 - ADAM_B2 ** ADAM_STEP)
    return -ADAM_LR * (m_hat / (jnp.sqrt(v_hat) + ADAM_EPS) + ADAM_WD * w), m2, v2


def _adam_sharded(recv, recv_first, w, m, v, lo, hi, name, prev=None):
    nl, r, c = w.shape
    tr = max([rows for rows in range(16, min(r, TILE_ADAM) + 1, 16) if r % rows == 0] or [r])

    def body(recv_ref, w_ref, m_ref, v_ref, *rest):
        g_ref, d_ref, m2_ref, v2_ref = rest[-4:]
        g = recv_ref[0].astype(F32)
        for s in range(1, N_DEV):
            g = g + recv_ref[s].astype(F32)
        g_ref[...] = g
        d_ref[...], m2_ref[...], v2_ref[...] = _adam(w_ref[...], g, m_ref[...], v_ref[...])

    blk = pl.BlockSpec((None, tr, c), lambda li, i: (li + lo, i, 0))
    extra = [] if prev is None else list(prev)
    return pl.pallas_call(
        body, name=name, grid=(hi - lo, r // tr),
        in_specs=[pl.BlockSpec((None, N_DEV, tr, c), lambda li, i: (li + lo - recv_first, 0, i, 0)), blk, blk, blk]
        + [ANY] * len(extra),
        out_specs=[blk] * 4, out_shape=[jax.ShapeDtypeStruct((nl, r, c), F32)] * 4,
        input_output_aliases={4 + i: i for i in range(len(extra))},
        compiler_params=_params("parallel", "parallel"),
    )(recv, w, m, v, *extra)


def _sum_sources(parts, name):
    _, r, c = parts.shape

    def body(p_ref, o_ref):
        g = p_ref[0]
        for s in range(1, N_DEV):
            g = g + p_ref[s]
        o_ref[...] = g

    return pl.pallas_call(
        body, name=name, grid=(1,), in_specs=[_whole((N_DEV, r, c))], out_specs=_whole((r, c)),
        out_shape=jax.ShapeDtypeStruct((r, c), F32), compiler_params=_params("arbitrary"),
    )(parts)


def _adam_flat(w, g, m, v, name):
    r, c = w.shape

    def body(w_ref, g_ref, m_ref, v_ref, d_ref, m2_ref, v2_ref):
        d_ref[...], m2_ref[...], v2_ref[...] = _adam(w_ref[...], g_ref[...], m_ref[...], v_ref[...])

    return pl.pallas_call(
        body, name=name, grid=(1,), in_specs=[_whole((r, c))] * 4, out_specs=[_whole((r, c))] * 3,
        out_shape=[jax.ShapeDtypeStruct((r, c), F32)] * 3, compiler_params=_params("arbitrary"),
    )(w, g, m, v)


def _pack(arrays):
    flat = jnp.concatenate([a.reshape(-1).astype(F32) for a in arrays])
    rows = -(-flat.shape[0] // 1024) * 8
    return jnp.pad(flat, (0, rows * 128 - flat.shape[0])).reshape(rows, 128)


def _unpack(slab, like):
    flat = slab.reshape(-1)
    out, at = [], 0
    for a in like:
        out.append(flat[at:at + a.size].reshape(a.shape))
        at += a.size
    return out


def kernel(x, mem, mem_norm, mix_pre_norm, mix_post_norm, w_in, pool_maps, pool_scale, conf_dw_w, conf_dw_b, conf_ln_g, conf_ln_b, sconv_w, w_out, xattn_pre_norm, xattn_post_norm, xattn_wq, xattn_wk, xattn_wv, xattn_wo, ffn_pre_norm, ffn_post_norm, ffn_w_up, ffn_conv_w, ffn_w_down, loss_target, m_mem_norm, m_mix_pre_norm, m_mix_post_norm, m_w_in, m_pool_maps, m_pool_scale, m_conf_dw_w, m_conf_dw_b, m_conf_ln_g, m_conf_ln_b, m_sconv_w, m_w_out, m_xattn_pre_norm, m_xattn_post_norm, m_xattn_wq, m_xattn_wk, m_xattn_wv, m_xattn_wo, m_ffn_pre_norm, m_ffn_post_norm, m_ffn_w_up, m_ffn_conv_w, m_ffn_w_down, v_mem_norm, v_mix_pre_norm, v_mix_post_norm, v_w_in, v_pool_maps, v_pool_scale, v_conf_dw_w, v_conf_dw_b, v_conf_ln_g, v_conf_ln_b, v_sconv_w, v_w_out, v_xattn_pre_norm, v_xattn_post_norm, v_xattn_wq, v_xattn_wk, v_xattn_wv, v_xattn_wo, v_ffn_pre_norm, v_ffn_post_norm, v_ffn_w_up, v_ffn_conv_w, v_ffn_w_down):
    weights = dict(mem_norm=mem_norm, mix_pre_norm=mix_pre_norm, mix_post_norm=mix_post_norm, w_in=w_in, pool_maps=pool_maps, pool_scale=pool_scale, conf_dw_w=conf_dw_w, conf_dw_b=conf_dw_b, conf_ln_g=conf_ln_g, conf_ln_b=conf_ln_b, sconv_w=sconv_w, w_out=w_out, xattn_pre_norm=xattn_pre_norm, xattn_post_norm=xattn_post_norm, xattn_wq=xattn_wq, xattn_wk=xattn_wk, xattn_wv=xattn_wv, xattn_wo=xattn_wo, ffn_pre_norm=ffn_pre_norm, ffn_post_norm=ffn_post_norm, ffn_w_up=ffn_w_up, ffn_conv_w=ffn_conv_w, ffn_w_down=ffn_w_down)
    mom1 = dict(mem_norm=m_mem_norm, mix_pre_norm=m_mix_pre_norm, mix_post_norm=m_mix_post_norm, w_in=m_w_in, pool_maps=m_pool_maps, pool_scale=m_pool_scale, conf_dw_w=m_conf_dw_w, conf_dw_b=m_conf_dw_b, conf_ln_g=m_conf_ln_g, conf_ln_b=m_conf_ln_b, sconv_w=m_sconv_w, w_out=m_w_out, xattn_pre_norm=m_xattn_pre_norm, xattn_post_norm=m_xattn_post_norm, xattn_wq=m_xattn_wq, xattn_wk=m_xattn_wk, xattn_wv=m_xattn_wv, xattn_wo=m_xattn_wo, ffn_pre_norm=m_ffn_pre_norm, ffn_post_norm=m_ffn_post_norm, ffn_w_up=m_ffn_w_up, ffn_conv_w=m_ffn_conv_w, ffn_w_down=m_ffn_w_down)
    mom2 = dict(mem_norm=v_mem_norm, mix_pre_norm=v_mix_pre_norm, mix_post_norm=v_mix_post_norm, w_in=v_w_in, pool_maps=v_pool_maps, pool_scale=v_pool_scale, conf_dw_w=v_conf_dw_w, conf_dw_b=v_conf_dw_b, conf_ln_g=v_conf_ln_g, conf_ln_b=v_conf_ln_b, sconv_w=v_sconv_w, w_out=v_w_out, xattn_pre_norm=v_xattn_pre_norm, xattn_post_norm=v_xattn_post_norm, xattn_wq=v_xattn_wq, xattn_wk=v_xattn_wk, xattn_wv=v_xattn_wv, xattn_wo=v_xattn_wo, ffn_pre_norm=v_ffn_pre_norm, ffn_post_norm=v_ffn_post_norm, ffn_w_up=v_ffn_w_up, ffn_conv_w=v_ffn_conv_w, ffn_w_down=v_ffn_w_down)
    names = list(weights)

    nl, d = mix_pre_norm.shape
    x0, mem0, target = _to_steps(x[0]), mem[0], _to_steps(loss_target[0])
    dp, dc, ds, *_ = _mix_dims(d)
    pg = dp // len(POOL_WINDOWS)
    me = 4 * lax.axis_index("x") + 2 * lax.axis_index("y") + lax.axis_index("c")

    big = ["w_in", "w_out", "xattn_wq", "xattn_wk", "xattn_wv", "xattn_wo", "ffn_w_up", "ffn_w_down"]
    transposed = ("w_in", "ffn_w_up")

    def row_shard(n, a):
        return a.transpose(0, 2, 1) if n in transposed else a

    shards = [row_shard(n, weights[n]).astype(BF16) for n in big]
    taps = ["conf_dw_w", "sconv_w", "ffn_conv_w"]
    tap_slab = _pack([weights[n] for n in taps])
    win0, tap_all = _allgather([shards[0][0:1], tap_slab[None]], "gather_weights0")
    layer_w = [{"w_in": win0[0]}]

    def land(s):
        return lax.empty((N_DEV * s.shape[1], s.shape[2]), BF16)

    first_a = _split_start([s[0] for s in shards[1:6]], [land(s) for s in shards[1:6]], _gather_piece, win0,
                           "gather_start0a")
    first_b = _split_start([s[0] for s in shards[6:]], [land(s) for s in shards[6:]], _gather_piece, first_a[-1],
                           "gather_start0b")
    tap_all = tap_all[0].reshape(N_DEV, *tap_slab.shape)
    tap_parts = [_unpack(tap_all[p], [weights[n] for n in taps]) for p in range(N_DEV)]
    wdw, wsc, wcf = (jnp.concatenate([tap_parts[p][i] for p in range(N_DEV)], axis=-1) for i in range(3))

    def g3(a):
        return a.reshape(a.shape[0], 1, a.shape[-1])

    mbd = jnp.zeros((nl, dp, dp), F32)
    for gi in range(len(POOL_WINDOWS)):
        mbd = mbd.at[:, gi * pg:(gi + 1) * pg, gi * pg:(gi + 1) * pg].set(pool_maps[:, gi])
    mbd = mbd.astype(BF16)
    pre1, post1, pre2, post2, pre3, post3 = (g3(weights[n]) for n in (
        "mix_pre_norm", "mix_post_norm", "xattn_pre_norm", "xattn_post_norm", "ffn_pre_norm", "ffn_post_norm"))
    pscale3, bdw3, lng3, lnb3 = g3(pool_scale), g3(conf_dw_b), g3(conf_ln_g), g3(conf_ln_b)
    memg3 = mem_norm.reshape(1, 1, d)

    def mm(a, b, mode, dt, name, tm=2048, tn=1024, tk=1024, a_outer=True):
        return _matmul(a, b, mode, dt, name, tm=tm, tn=tn, tk=tk, a_outer=a_outer)

    mem_n = _prenorm(mem0, memg3, 0, "mem_norm")
    xs = x0
    h = _prenorm(xs, pre1, 0, "pre_norm0")
    saved = []
    for l in range(nl):
        ps_l = pscale3
        if l + 1 < nl:
            flying = _split_start([s[l + 1] for s in shards], [land(s) for s in shards], _gather_piece,
                                  xs if l else first_b[-1], f"gather_start{l + 1}")
            ps_l = pscale3 + flying[-1][0, 0]
        w = layer_w[l]
        s = {"x": xs, "h": h}
        s["z"] = mm(h, w["w_in"], "nt", F32, f"z{l}", tm=1024, tn=4096)
        s["cat"], s["c"] = _mix_fwd(s["z"], mbd, ps_l, wdw, bdw3, lng3, lnb3, wsc, l, f"mix_fwd{l}")
        if l == 0:
            w.update(zip(big[1:6], _split_wait(first_a, s["cat"], _gather_piece, "gather_wait0a")))
        s["y1"], s["x1"], s["h1"] = _matmul_resnorm(s["cat"], w["w_out"], xs, post1, l, pre2, l, f"y1_{l}", tm=1024)
        s["q"] = mm(s["h1"], w["xattn_wq"], "nn", BF16, f"q{l}")
        s["k"] = mm(mem_n, w["xattn_wk"], "nn", BF16, f"k{l}")
        s["v"] = mm(mem_n, w["xattn_wv"], "nn", BF16, f"v{l}")
        s["o"] = _attn_fwd(s["q"], s["k"], s["v"], f"attn_fwd{l}")
        s["y2"], s["x2"], s["h2"] = _matmul_resnorm(s["o"], w["xattn_wo"], s["x1"], post2, l, pre3, l, f"y2_{l}", tm=1024)
        if l == 0:
            w.update(zip(big[6:], _split_wait(first_b, s["h2"], _gather_piece, "gather_wait0b")))
        s["u"] = mm(s["h2"], w["ffn_w_up"], "nt", F32, f"u{l}", tn=1408, a_outer=False)
        s["a"] = _ffn_act_fwd(s["u"], wcf, l, f"ffn_act{l}")
        if l + 1 < nl:
            s["y3"], xs, h = _matmul_resnorm(s["a"], w["ffn_w_down"], s["x2"], post3, l, pre1, l + 1, f"y3_{l}", tm=1024)
            layer_w.append(dict(zip(big, _split_wait(flying, xs, _gather_piece, f"gather_wait{l + 1}"))))
        saved.append(s)

    last = saved[-1]
    dxn, dy3, dg_post3, loss_lanes = _matmul_loss(
        last["a"], layer_w[-1]["ffn_w_down"], last["x2"], post3, nl - 1, target, "y3_loss", tm=1024)
    loss = lax.psum(loss_lanes[0, 0], ("x", "y", "c"))

    recvs = [lax.empty((max(nl - 1, 1), N_DEV, s.shape[1], d), BF16) for s in shards]
    recv0 = [lax.empty((1, N_DEV, s.shape[1], d), BF16) for s in shards]
    small = {n: [None] * nl for n in names if n not in big and n != "mem_norm"}
    small["ffn_post_norm"][nl - 1] = dg_post3
    dmem_n = jnp.zeros(mem0.shape, F32)
    flying = None
    for l in reversed(range(nl)):
        s, w = saved[l], layer_w[l]
        wc_l = wcf if flying is None else wcf + flying[-1][0, 0]
        gw = {}
        da = mm(dy3, w["ffn_w_down"], "nt", F32, f"da{l}", tn=1408, a_outer=False)
        gw["ffn_w_down"] = mm(s["a"], dy3, "tn", BF16, f"dw_down{l}", tm=1408, tk=2048)
        du, small["ffn_conv_w"][l] = _ffn_act_bwd(s["u"], da, wc_l, l, f"ffn_act_bwd{l}")
        gw["ffn_w_up"] = mm(du, s["h2"], "tn", BF16, f"dw_up{l}", tm=1408, tk=2048)
        dx2, small["ffn_pre_norm"][l], dy2, small["xattn_post_norm"][l] = _matmul_norm_bwd(
            du, w["ffn_w_up"], "nn", dxn, s["x2"], pre3, l, f"dh2_{l}", s["y2"], post2, l)
        do = mm(dy2, w["xattn_wo"], "nt", BF16, f"do{l}")
        gw["xattn_wo"] = mm(s["o"], dy2, "tn", BF16, f"dw_o{l}", tk=4096)
        dq, dk, dv = _attn_bwd(s["q"], s["k"], s["v"], do, f"attn_bwd{l}")
        dkb, dvb = dk.astype(BF16), dv.astype(BF16)
        gw["xattn_wq"] = mm(s["h1"], dq, "tn", BF16, f"dw_q{l}", tk=4096)
        gw["xattn_wk"] = mm(mem_n, dkb, "tn", BF16, f"dw_k{l}")
        gw["xattn_wv"] = mm(mem_n, dvb, "tn", BF16, f"dw_v{l}")
        dmem_n = dmem_n + mm(dkb, w["xattn_wk"], "nt", F32, f"dmem_k{l}") \
            + mm(dvb, w["xattn_wv"], "nt", F32, f"dmem_v{l}")
        pre2_l = pre2
        if l == 0:
            flying0 = _split_start([gw[n] for n in big[2:]], recv0[2:], _scatter_piece(0), dmem_n, "scatter_start0a")
            pre2_l = pre2 + flying0[-1][0, 0]
        dx1, small["xattn_pre_norm"][l], dy1, small["mix_post_norm"][l] = _matmul_norm_bwd(
            dq, w["xattn_wq"], "nt", dx2, s["x1"], pre2_l, l, f"dh1_{l}", s["y1"], post1, l, tm=1024)
        dcat = mm(dy1, w["w_out"], "nt", F32, f"dcat{l}")
        gw["w_out"] = mm(s["cat"], dy1, "tn", BF16, f"dw_out{l}", tk=4096)
        dz, dmbd, dps, dwdw, dbdw, dlng, dlnb, dwsc = _mix_bwd(
            dcat, s["z"], s["c"], mbd, pscale3, wdw, lng3, lnb3, wsc, l, f"mix_bwd{l}")
        small["pool_maps"][l] = jnp.stack([dmbd[gi * pg:(gi + 1) * pg, gi * pg:(gi + 1) * pg]
                                           for gi in range(len(POOL_WINDOWS))])
        small["pool_scale"][l], small["conf_dw_w"][l], small["conf_dw_b"][l] = dps, dwdw, dbdw
        small["conf_ln_g"][l], small["conf_ln_b"][l], small["sconv_w"][l] = dlng, dlnb, dwsc
        gw["w_in"] = mm(dz, s["h"], "tn", BF16, f"dw_in{l}", tm=4096, tk=2048)
        if l > 0:
            dxn, small["mix_pre_norm"][l], dy3, small["ffn_post_norm"][l - 1] = _matmul_norm_bwd(
                dz, w["w_in"], "nn", dx1, s["x"], pre1, l, f"dh{l}", saved[l - 1]["y3"], post3, l - 1)
            if flying is not None:
                recvs = _split_wait(flying, dxn, _scatter_piece(l), f"scatter_wait{l + 1}")
            flying = _split_start([gw[n] for n in big], recvs, _scatter_piece(l - 1), dxn, f"scatter_start{l}")
        else:
            if flying is not None:
                recvs = _split_wait(flying, dx1, _scatter_piece(0), "scatter_wait1")
            flying = _split_start([gw[n] for n in big[:2]], recv0[:2], _scatter_piece(0), dx1, "scatter_start0b")
            dxn, small["mix_pre_norm"][l] = _matmul_norm_bwd(
                dz, w["w_in"], "nn", dx1, s["x"], pre1 + flying[-1][0, 0], l, "dh0")
    grad_x = _from_steps(dxn)[None]
    _, dg_mem = _norm_bwd(jnp.zeros(mem0.shape, F32), dmem_n, mem0, memg3, 0, "norm_bwd_mem")

    small_names = [n for n in names if n not in big]
    partial = {n: (dg_mem.reshape(d) if n == "mem_norm" else
                   jnp.stack([g.reshape(g.shape[-1]) if g.shape[0] == 1 and weights[n].ndim == 2 else g
                              for g in small[n]])) for n in small_names}
    slab = _pack([partial[n] for n in small_names])
    gathered = _allgather([slab[None]], "gather_small_grads")[0][0].reshape(N_DEV, *slab.shape)
    summed = dict(zip(small_names, _unpack(_sum_sources(gathered, "sum_small_grads"), [partial[n] for n in small_names])))
    grad = {}
    for n in small_names:
        g = summed[n]
        if n in taps:
            width = weights[n].shape[-1]
            g = lax.dynamic_slice_in_dim(g, me * width, width, axis=g.ndim - 1)
        grad[n] = g

    delta, new_m, new_v = {}, {}, {}
    wmv = {n: [row_shard(n, a[n]) for a in (weights, mom1, mom2)] for n in big}
    upper = {n: _adam_sharded(recv, 1, *wmv[n], 1, nl, f"adam_{n}") for n, recv in zip(big, recvs)} if nl > 1 else {}
    upd = _adam_flat(_pack([weights[n] for n in small_names]), _pack([grad[n] for n in small_names]),
                     _pack([mom1[n] for n in small_names]), _pack([mom2[n] for n in small_names]), "adam_small")
    for out, slab_o in zip((delta, new_m, new_v), upd):
        out.update(zip(small_names, _unpack(slab_o, [weights[n] for n in small_names])))
    done = sum(r[3][0, 0, :1] for r in upper.values()) + upd[0][0, :1]
    recv0 = _split_wait(flying, done, _scatter_piece(0), "scatter_wait0b") \
        + _split_wait(flying0, done, _scatter_piece(0), "scatter_wait0a")
    for n, recv in zip(big, recv0):
        res = _adam_sharded(recv, 0, *wmv[n], 0, 1, f"adam0_{n}", prev=upper.get(n))
        grad[n], delta[n], new_m[n], new_v[n] = (row_shard(n, r) for r in res)

    return (loss, grad_x, *[grad[n] for n in names], *[delta[n] for n in names],
            *[new_m[n] for n in names], *[new_v[n] for n in names])
```

```python
import jax
import jax.numpy as jnp
from jax import lax
from jax.experimental import pallas as pl
from jax.experimental.pallas import tpu as pltpu

F32, BF16 = jnp.float32, jnp.bfloat16
EPS = 1e-6
POOL_WINDOWS = (2, 4, 8, 16)
MAX_WINDOW = 16
CONF_TAPS, SHORT_TAPS = 31, 3
CONF_HALO, POOL_HALO, SHORT_HALO = 256, 128, 16
ROW_CHUNK = 64
HEADS = 4
N_DEV = 8
ADAM_LR, ADAM_B1, ADAM_B2, ADAM_EPS, ADAM_WD, ADAM_STEP = 0.001, 0.9, 0.999, 1e-08, 0.01, 10
VMEM_LIMIT_V7X = 56 * 2**20
MESH = pl.DeviceIdType.MESH
ANY = pl.BlockSpec(memory_space=pl.ANY)
HBM = pl.BlockSpec(memory_space=pltpu.HBM)
SEM = pl.BlockSpec(memory_space=pltpu.SEMAPHORE)
EFFECT = pltpu.SideEffectType.DATAFLOW_SIDE_EFFECTING

TILE_NORM, TILE_MIX, TILE_FFN, TILE_ATTN, TILE_ADAM = 256, 512, 256, 512, 352


def _params(*sem):
    return pltpu.CompilerParams(dimension_semantics=sem, vmem_limit_bytes=VMEM_LIMIT_V7X)


def _sig(x):
    return 1.0 / (1.0 + jnp.exp(-x))


def _rms(x):
    r = lax.rsqrt(jnp.mean(x * x, axis=-1, keepdims=True) + EPS)
    return x * r, r


def _rms_bwd(dout, g, n, r):
    dn = dout * g
    return r * (dn - n * jnp.mean(dn * n, axis=-1, keepdims=True))


def _rows(tt, c):
    return pl.BlockSpec((tt, c), lambda i: (i, 0))


def _whole(shape):
    return pl.BlockSpec(shape, lambda i: (0,) * len(shape))


def _layer(shape, l):
    return pl.BlockSpec((None,) + shape, lambda i: (l,) + (0,) * len(shape))


def _colsum(x):
    return jnp.sum(x, axis=0, keepdims=True)


_DIMS = {"nn": (((1,), (0,)), ((), ())), "nt": (((1,), (1,)), ((), ())), "tn": (((0,), (0,)), ((), ()))}


def _matmul(a, b, mode, out_dtype, name, *, tm, tn, tk, a_outer=True):
    if mode == "nn":
        (m, k), (k2, n) = a.shape, b.shape
    elif mode == "nt":
        (m, k), (n, k2) = a.shape, b.shape
    else:
        (k, m), (k2, n) = a.shape, b.shape
    assert k == k2, (name, a.shape, b.shape)
    tm, tn, tk = min(tm, m), min(tn, n), min(tk, k)
    assert m % tm == 0 and n % tn == 0 and k % tk == 0, (name, m, n, k, tm, tn, tk)
    gm, gn, gk = m // tm, n // tn, k // tk

    def ij(g0, g1):
        return (g0, g1) if a_outer else (g1, g0)

    def a_map(g0, g1, kk):
        i, _ = ij(g0, g1)
        return (kk, i) if mode == "tn" else (i, kk)

    def b_map(g0, g1, kk):
        _, j = ij(g0, g1)
        return (j, kk) if mode == "nt" else (kk, j)

    def o_map(g0, g1, kk):
        return ij(g0, g1)

    a_block = (tk, tm) if mode == "tn" else (tm, tk)
    b_block = (tn, tk) if mode == "nt" else (tk, tn)
    dims = _DIMS[mode]

    def body(a_ref, b_ref, o_ref, *acc):
        p = lax.dot_general(a_ref[...].astype(BF16), b_ref[...].astype(BF16), dims, preferred_element_type=F32)
        if gk == 1:
            o_ref[...] = p.astype(o_ref.dtype)
        else:
            kk = pl.program_id(2)

            @pl.when(kk == 0)
            def _():
                acc[0][...] = p

            @pl.when(kk > 0)
            def _():
                acc[0][...] += p

            @pl.when(kk == gk - 1)
            def _():
                o_ref[...] = acc[0][...].astype(o_ref.dtype)

    return pl.pallas_call(
        body, name=name, grid=(gm, gn, gk) if a_outer else (gn, gm, gk),
        in_specs=[pl.BlockSpec(a_block, a_map), pl.BlockSpec(b_block, b_map)],
        out_specs=pl.BlockSpec((tm, tn), o_map),
        out_shape=jax.ShapeDtypeStruct((m, n), out_dtype),
        scratch_shapes=[pltpu.VMEM((tm, tn), F32)] if gk > 1 else [],
        compiler_params=_params("parallel", "parallel", "arbitrary"),
    )(a, b)


def _prenorm(x, g3, l, name):
    t, d = x.shape
    tt = min(TILE_NORM, t)

    def body(x_ref, g_ref, h_ref):
        n, _ = _rms(x_ref[...])
        h_ref[...] = (n * g_ref[...]).astype(BF16)

    return pl.pallas_call(
        body, name=name, grid=(t // tt,),
        in_specs=[_rows(tt, d), _layer((1, d), l)], out_specs=_rows(tt, d),
        out_shape=jax.ShapeDtypeStruct((t, d), BF16), compiler_params=_params("parallel"),
    )(x, g3)


def _norm_bwd(dxn, dh, x_in, gpre3, l, name, y_prev=None, gpost3=None, l_prev=None):
    t, d = x_in.shape
    tt = min(TILE_NORM, t)
    has_prev = y_prev is not None

    def body(*refs):
        if has_prev:
            dxn_ref, dh_ref, x_ref, g_ref, y_ref, g2_ref, dx_ref, dg_ref, dy_ref, dg2_ref = refs
        else:
            dxn_ref, dh_ref, x_ref, g_ref, dx_ref, dg_ref = refs

        @pl.when(pl.program_id(0) == 0)
        def _():
            dg_ref[...] = jnp.zeros_like(dg_ref)
            if has_prev:
                dg2_ref[...] = jnp.zeros_like(dg2_ref)

        dh_v = dh_ref[...]
        n, r = _rms(x_ref[...])
        dx = dxn_ref[...] + _rms_bwd(dh_v, g_ref[...], n, r)
        dx_ref[...] = dx
        dg_ref[...] += _colsum(dh_v * n)
        if has_prev:
            n2, r2 = _rms(y_ref[...])
            dy_ref[...] = _rms_bwd(dx, g2_ref[...], n2, r2).astype(BF16)
            dg2_ref[...] += _colsum(dx * n2)

    in_specs = [_rows(tt, d), _rows(tt, d), _rows(tt, d), _layer((1, d), l)]
    out_specs = [_rows(tt, d), _whole((1, d))]
    out_shape = [jax.ShapeDtypeStruct((t, d), F32), jax.ShapeDtypeStruct((1, d), F32)]
    args = [dxn, dh, x_in, gpre3]
    if has_prev:
        in_specs += [_rows(tt, d), _layer((1, d), l_prev)]
        out_specs += [_rows(tt, d), _whole((1, d))]
        out_shape += [jax.ShapeDtypeStruct((t, d), BF16), jax.ShapeDtypeStruct((1, d), F32)]
        args += [y_prev, gpost3]
    return pl.pallas_call(
        body, name=name, grid=(t // tt,), in_specs=in_specs, out_specs=out_specs, out_shape=out_shape,
        compiler_params=_params("arbitrary"),
    )(*args)


def _halves(tt):
    return [slice(0, tt // 2), slice(tt // 2, tt)] if tt % 32 == 0 else [slice(0, tt)]


def _matmul_resnorm(a, w, x, gpost3, l, gnext3, l2, name, tm=512):
    t, k = a.shape
    d = w.shape[1]
    tm = min(tm, t)

    def body(a_ref, w_ref, x_ref, gp_ref, gn_ref, y_ref, xo_ref, h_ref):
        wv, gp, gn = w_ref[...], gp_ref[...], gn_ref[...]
        for rows in _halves(tm):
            y = jnp.dot(a_ref[rows, :], wv, preferred_element_type=F32)
            y_ref[rows, :] = y
            n, _ = _rms(y)
            xn = x_ref[rows, :] + n * gp
            xo_ref[rows, :] = xn
            n2, _ = _rms(xn)
            h_ref[rows, :] = (n2 * gn).astype(BF16)

    return pl.pallas_call(
        body, name=name, grid=(t // tm,),
        in_specs=[_rows(tm, k), _whole((k, d)), _rows(tm, d), _layer((1, d), l), _layer((1, d), l2)],
        out_specs=[_rows(tm, d)] * 3,
        out_shape=[jax.ShapeDtypeStruct((t, d), F32), jax.ShapeDtypeStruct((t, d), F32),
                   jax.ShapeDtypeStruct((t, d), BF16)],
        compiler_params=_params("parallel"),
    )(a, w, x, gpost3, gnext3)


def _matmul_loss(a, w, x, gpost3, l, target, name, tm=512):
    t, k = a.shape
    d = w.shape[1]
    tm = min(tm, t)

    def body(a_ref, w_ref, x_ref, g_ref, t_ref, dxn_ref, dy_ref, dg_ref, loss_ref):
        @pl.when(pl.program_id(0) == 0)
        def _():
            dg_ref[...] = jnp.zeros_like(dg_ref)
            loss_ref[...] = jnp.zeros_like(loss_ref)

        wv, g = w_ref[...], g_ref[...]
        for rows in _halves(tm):
            n, r = _rms(jnp.dot(a_ref[rows, :], wv, preferred_element_type=F32))
            diff = x_ref[rows, :] + n * g - t_ref[rows, :]
            loss_ref[...] += 0.5 * jnp.sum(jnp.mean(diff * diff, axis=-1, keepdims=True))
            dxn = diff * (1.0 / d)
            dxn_ref[rows, :] = dxn
            dy_ref[rows, :] = _rms_bwd(dxn, g, n, r).astype(BF16)
            dg_ref[...] += _colsum(dxn * n)

    return pl.pallas_call(
        body, name=name, grid=(t // tm,),
        in_specs=[_rows(tm, k), _whole((k, d)), _rows(tm, d), _layer((1, d), l), _rows(tm, d)],
        out_specs=[_rows(tm, d), _rows(tm, d), _whole((1, d)), _whole((1, 128))],
        out_shape=[jax.ShapeDtypeStruct((t, d), F32), jax.ShapeDtypeStruct((t, d), BF16),
                   jax.ShapeDtypeStruct((1, d), F32), jax.ShapeDtypeStruct((1, 128), F32)],
        compiler_params=_params("arbitrary"),
    )(a, w, x, gpost3, target)


def _matmul_norm_bwd(a, w, mode, dxn, x_in, gpre3, l, name, y_prev=None, gpost3=None, l_prev=None, tm=512):
    t, k = a.shape
    d = x_in.shape[1]
    tm = min(tm, t)
    has_prev = y_prev is not None
    dims = _DIMS[mode]

    def body(*refs):
        if has_prev:
            a_ref, w_ref, dxn_ref, x_ref, g_ref, y_ref, g2_ref, dx_ref, dg_ref, dy_ref, dg2_ref = refs
        else:
            a_ref, w_ref, dxn_ref, x_ref, g_ref, dx_ref, dg_ref = refs

        @pl.when(pl.program_id(0) == 0)
        def _():
            dg_ref[...] = jnp.zeros_like(dg_ref)
            if has_prev:
                dg2_ref[...] = jnp.zeros_like(dg2_ref)

        wv, g = w_ref[...], g_ref[...]
        for rows in _halves(tm):
            dh = lax.dot_general(a_ref[rows, :], wv, dims, preferred_element_type=F32)
            n, r = _rms(x_ref[rows, :])
            dx = dxn_ref[rows, :] + _rms_bwd(dh, g, n, r)
            dx_ref[rows, :] = dx
            dg_ref[...] += _colsum(dh * n)
            if has_prev:
                n2, r2 = _rms(y_ref[rows, :])
                dy_ref[rows, :] = _rms_bwd(dx, g2_ref[...], n2, r2).astype(BF16)
                dg2_ref[...] += _colsum(dx * n2)

    in_specs = [_rows(tm, k), _whole(w.shape), _rows(tm, d), _rows(tm, d), _layer((1, d), l)]
    out_specs = [_rows(tm, d), _whole((1, d))]
    out_shape = [jax.ShapeDtypeStruct((t, d), F32), jax.ShapeDtypeStruct((1, d), F32)]
    args = [a, w, dxn, x_in, gpre3]
    if has_prev:
        in_specs += [_rows(tm, d), _layer((1, d), l_prev)]
        out_specs += [_rows(tm, d), _whole((1, d))]
        out_shape += [jax.ShapeDtypeStruct((t, d), BF16), jax.ShapeDtypeStruct((1, d), F32)]
        args += [y_prev, gpost3]
    return pl.pallas_call(
        body, name=name, grid=(t // tm,), in_specs=in_specs, out_specs=out_specs, out_shape=out_shape,
        compiler_params=_params("arbitrary"),
    )(*args)


def _to_steps(a):
    t = a.shape[0]
    return a.reshape(8, t // 8, -1).transpose(1, 0, 2).reshape(a.shape)


def _from_steps(a):
    t = a.shape[0]
    return a.reshape(t // 8, 8, -1).transpose(1, 0, 2).reshape(a.shape)


def _al(v):
    return v if isinstance(v, int) else pl.multiple_of(v, 8)


def _chunks(n_rows, fn, unroll=1):
    def step(r, carry):
        fn(pl.multiple_of(r * ROW_CHUNK, ROW_CHUNK))
        return carry
    lax.fori_loop(0, n_rows // ROW_CHUNK, step, 0, unroll=unroll)


def _fold8(a):
    return a.reshape(a.shape[0] // 8, 8, a.shape[1]).sum(axis=0)


def _shift_down(a):
    row = lax.broadcasted_iota(jnp.int32, a.shape, 0)
    return jnp.where(row % 8 == 0, 0.0, pltpu.roll(a, 1, 0))


def _shift_up(a):
    row = lax.broadcasted_iota(jnp.int32, a.shape, 0)
    return jnp.where(row % 8 == 7, 0.0, pltpu.roll(a, a.shape[0] - 1, 0))


def _prev_block(h, c, tt, t):
    return pl.BlockSpec((h, c), lambda i: (jnp.where(i == 0, t // h - 1, i * (tt // h) - 1), 0))


def _next_block(h, c, tt, t):
    return pl.BlockSpec((h, c), lambda i: (jnp.where(i == t // tt - 1, 0, (i + 1) * (tt // h)), 0))


def _taps(w_ref, buf, start, taps, rc, lanes=slice(None)):
    acc = w_ref[0:1, lanes] * buf[pl.ds(_al(start), rc), lanes]
    for k in range(1, taps):
        acc = acc + w_ref[k:k + 1, lanes] * buf[pl.ds(_al(start + 8 * k), rc), lanes]
    return acc


def _taps_rev(w_ref, buf, start, taps, rc, lanes=slice(None)):
    acc = w_ref[0:1, lanes] * buf[pl.ds(_al(start + 8 * (taps - 1)), rc), lanes]
    for k in range(1, taps):
        acc = acc + w_ref[k:k + 1, lanes] * buf[pl.ds(_al(start + 8 * (taps - 1 - k)), rc), lanes]
    return acc


def _mix_dims(d):
    dp = d // 4
    dc = 3 * d // 8
    ds = d - dp - dc
    oa, og = dp, dp + dc
    ob = dp + 2 * dc
    oc, ox = ob + ds, ob + 2 * ds
    return dp, dc, ds, oa, og, ob, oc, ox, ox + ds


def _pool_consts(dp):
    win = jnp.repeat(jnp.asarray(POOL_WINDOWS, F32), dp // len(POOL_WINDOWS))[None, :]
    mask = (jnp.arange(MAX_WINDOW, dtype=F32)[:, None] < win).astype(F32)
    return mask, win


def _pool_count(row0, rc, dp, seg, wl):
    r = lax.broadcasted_iota(jnp.int32, (rc, dp), 0) + row0
    return jnp.minimum(((r & 7) * seg + (r >> 3) + 1).astype(F32), wl)


def _mix_fwd(z, mbd, pscale3, wdw, bdw3, lng3, lnb3, wsc, l, name):
    t, din = z.shape
    dp, dc, ds, oa, og, ob, oc, ox, din2 = _mix_dims(din * 8 // 17)
    assert din2 == din
    d = ob
    tt = min(TILE_MIX, t)
    hp, hc, hs, rc = POOL_HALO, CONF_HALO, SHORT_HALO, ROW_CHUNK
    assert tt % hc == 0 and t % tt == 0
    seg = t // 8
    pmask, wlane = _pool_consts(dp)

    def body(z_ref, zpa_ref, zpb_ref, mbd_ref, ps_ref, pmask_ref, wl_ref, wdw_ref, bdw_ref, lng_ref, lnb_ref, wsc_ref,
             cat_ref, c_ref, pbuf, vbuf, sbuf):
        i = pl.program_id(0)
        pbuf[0:hp, :] = zpa_ref[hc - hp:hc, 0:dp]

        def prev(r0):
            rows = pl.ds(r0, rc)
            vbuf[rows, :] = zpa_ref[rows, oa:oa + dc] * _sig(zpa_ref[rows, og:og + dc])
        _chunks(hc, prev)
        sbuf[0:hs, :] = zpb_ref[:, oc:oc + ds] * zpb_ref[:, ox:ox + ds]

        @pl.when(i == 0)
        def _():
            pbuf[0:hp, :] = _shift_down(pbuf[0:hp, :])
            vbuf[0:hc, :] = _shift_down(vbuf[0:hc, :])
            sbuf[0:hs, :] = _shift_down(sbuf[0:hs, :])

        mbd_v, ps, wl = mbd_ref[...], ps_ref[...], wl_ref[...]
        bdw, lng, lnb = bdw_ref[...], lng_ref[...], lnb_ref[...]

        def step(r0):
            rows = pl.ds(r0, rc)
            zp = z_ref[rows, 0:dp]
            pbuf[pl.ds(_al(hp + r0), rc), :] = zp
            vbuf[pl.ds(_al(hc + r0), rc), :] = z_ref[rows, oa:oa + dc] * _sig(z_ref[rows, og:og + dc])
            sbuf[pl.ds(_al(hs + r0), rc), :] = z_ref[rows, oc:oc + ds] * z_ref[rows, ox:ox + ds]
            pooled = _taps_rev(pmask_ref, pbuf, r0 + hp - 8 * (MAX_WINDOW - 1), MAX_WINDOW, rc)
            pooled = pooled / _pool_count(i * tt + r0, rc, dp, seg, wl) - zp
            pm = jnp.dot(pooled.astype(BF16), mbd_v, preferred_element_type=F32)
            cat_ref[rows, 0:dp] = (pm * ps).astype(BF16)
            c = _taps(wdw_ref, vbuf, r0 + hc - 8 * (CONF_TAPS - 1), CONF_TAPS, rc) + bdw
            c_ref[rows, :] = c
            xc = c - jnp.mean(c, axis=-1, keepdims=True)
            nrm = xc * lax.rsqrt(jnp.mean(xc * xc, axis=-1, keepdims=True) + EPS)
            yln = nrm * lng + lnb
            cat_ref[rows, dp:dp + dc] = (yln * _sig(yln)).astype(BF16)
            cv = _taps(wsc_ref, sbuf, r0 + hs - 8 * (SHORT_TAPS - 1), SHORT_TAPS, rc)
            cat_ref[rows, dp + dc:d] = (z_ref[rows, ob:ob + ds] * cv).astype(BF16)
        _chunks(tt, step, unroll=2)

    return pl.pallas_call(
        body, name=name, grid=(t // tt,),
        in_specs=[_rows(tt, din), _prev_block(hc, d, tt, t), _prev_block(hs, din, tt, t),
                  _layer((dp, dp), l), _layer((1, dp), l), _whole((MAX_WINDOW, dp)), _whole((1, dp)),
                  _layer((CONF_TAPS, dc), l), _layer((1, dc), l), _layer((1, dc), l), _layer((1, dc), l),
                  _layer((SHORT_TAPS, ds), l)],
        out_specs=[_rows(tt, d), _rows(tt, dc)],
        out_shape=[jax.ShapeDtypeStruct((t, d), BF16), jax.ShapeDtypeStruct((t, dc), F32)],
        scratch_shapes=[pltpu.VMEM((hp + tt, dp), F32), pltpu.VMEM((hc + tt, dc), F32), pltpu.VMEM((hs + tt, ds), F32)],
        compiler_params=_params("parallel"),
    )(z, z, z, mbd, pscale3, pmask, wlane, wdw, bdw3, lng3, lnb3, wsc)


def _mix_bwd(dcat, z, c, mbd, pscale3, wdw, lng3, lnb3, wsc, l, name):
    t, din = z.shape
    dp, dc, ds, oa, og, ob, oc, ox, _ = _mix_dims(din * 8 // 17)
    d = ob
    tt = min(TILE_MIX, t)
    nt = t // tt
    hp, hc, hs, rc = POOL_HALO, CONF_HALO, SHORT_HALO, ROW_CHUNK
    assert tt % hc == 0 and t % tt == 0 and tt >= 8 * MAX_WINDOW
    seg = t // 8
    pmask, wlane = _pool_consts(dp)

    def body(dcat_ref, dcn_ref, z_ref, zpa_ref, zpb_ref, znb_ref, c_ref, cn_ref, mbd_ref, ps_ref, pmask_ref, wl_ref,
             wdw_ref, lng_ref, lnb_ref, wsc_ref,
             dz_ref, dmbd_ref, dps_ref, dwdw_ref, dbdw_ref, dlng_ref, dlnb_ref, dwsc_ref,
             pbuf, qbuf, dpbuf, pbf, vbuf, dcbuf, sbuf, dsbuf, dw8, ds8, ln8, ps8):
        i = pl.program_id(0)
        first, last = i == 0, i == nt - 1

        @pl.when(first)
        def _():
            for ref in (dmbd_ref, dw8, ds8, ln8, ps8):
                ref[...] = jnp.zeros_like(ref)

        mbd_v, ps, wl = mbd_ref[...], ps_ref[...], wl_ref[...]
        lng, lnb = lng_ref[...], lnb_ref[...]

        def ln_silu_bwd(cc, dyb):
            xc = cc - jnp.mean(cc, axis=-1, keepdims=True)
            rstd = lax.rsqrt(jnp.mean(xc * xc, axis=-1, keepdims=True) + EPS)
            nrm = xc * rstd
            yln = nrm * lng + lnb
            s = _sig(yln)
            dyln = dyb * (s * (1.0 + yln * (1.0 - s)))
            dn = dyln * lng
            dcc = rstd * (dn - jnp.mean(dn, axis=-1, keepdims=True) - nrm * jnp.mean(dn * nrm, axis=-1, keepdims=True))
            return dcc, dyln, nrm

        pbuf[0:hp, :] = zpa_ref[hc - hp:hc, 0:dp]

        def prev(r0):
            rows = pl.ds(r0, rc)
            vbuf[rows, :] = zpa_ref[rows, oa:oa + dc] * _sig(zpa_ref[rows, og:og + dc])
        _chunks(hc, prev)
        sbuf[0:hs, :] = zpb_ref[:, oc:oc + ds] * zpb_ref[:, ox:ox + ds]

        @pl.when(first)
        def _():
            pbuf[0:hp, :] = _shift_down(pbuf[0:hp, :])
            vbuf[0:hc, :] = _shift_down(vbuf[0:hc, :])
            sbuf[0:hs, :] = _shift_down(sbuf[0:hs, :])

        def nxt(r0):
            rows = pl.ds(r0, rc)
            dcc, _, _ = ln_silu_bwd(cn_ref[rows, :], dcn_ref[rows, dp:dp + dc])
            dcbuf[pl.ds(_al(tt + r0), rc), :] = dcc
        _chunks(hc, nxt, unroll=4)
        dpm_n = (dcn_ref[0:hp, 0:dp] * ps).astype(BF16)
        qbuf[tt:tt + hp, :] = lax.dot_general(dpm_n, mbd_v, _DIMS["nt"], preferred_element_type=F32) / wl
        dsbuf[tt:tt + hs, :] = dcn_ref[0:hs, dp + dc:d] * znb_ref[:, ob:ob + ds]

        @pl.when(last)
        def _():
            dcbuf[tt:tt + hc, :] = _shift_up(dcbuf[tt:tt + hc, :])
            qbuf[tt:tt + hp, :] = _shift_up(qbuf[tt:tt + hp, :])
            dsbuf[tt:tt + hs, :] = _shift_up(dsbuf[tt:tt + hs, :])

        def fill(r0):
            rows = pl.ds(r0, rc)
            zp = z_ref[rows, 0:dp]
            pbuf[pl.ds(_al(hp + r0), rc), :] = zp
            vbuf[pl.ds(_al(hc + r0), rc), :] = z_ref[rows, oa:oa + dc] * _sig(z_ref[rows, og:og + dc])
            sbuf[pl.ds(_al(hs + r0), rc), :] = z_ref[rows, oc:oc + ds] * z_ref[rows, ox:ox + ds]
            pooled = _taps_rev(pmask_ref, pbuf, r0 + hp - 8 * (MAX_WINDOW - 1), MAX_WINDOW, rc)
            pbf[rows, :] = (pooled / _pool_count(i * tt + r0, rc, dp, seg, wl) - zp).astype(BF16)
            dcc, dyln, nrm = ln_silu_bwd(c_ref[rows, :], dcat_ref[rows, dp:dp + dc])
            dcbuf[rows, :] = dcc
            ln8[0] += _fold8(dyln * nrm)
            ln8[1] += _fold8(dyln)
            ln8[2] += _fold8(dcc)
            dsbuf[rows, :] = dcat_ref[rows, dp + dc:d] * z_ref[rows, ob:ob + ds]
        _chunks(tt, fill, unroll=4)

        pb = pbf[...]
        dya = dcat_ref[:, 0:dp]
        ps8[...] += _fold8(dya * jnp.dot(pb, mbd_v, preferred_element_type=F32))
        dpm = (dya * ps).astype(BF16)
        dmbd_ref[...] += lax.dot_general(pb, dpm, _DIMS["tn"], preferred_element_type=F32)
        dpbuf[...] = lax.dot_general(dpm, mbd_v, _DIMS["nt"], preferred_element_type=F32)

        def quot(r0):
            rows = pl.ds(r0, rc)
            qbuf[rows, :] = dpbuf[rows, :] / _pool_count(i * tt + r0, rc, dp, seg, wl)
        _chunks(tt, quot)

        def back(r0):
            rows = pl.ds(r0, rc)
            dzp = _taps(pmask_ref, qbuf, r0, MAX_WINDOW, rc) - dpbuf[rows, :]
            dz_ref[rows, 0:dp] = dzp.astype(BF16)
            dcc = dcbuf[rows, :]
            for k in range(CONF_TAPS):
                dw8[k] += _fold8(dcc * vbuf[pl.ds(_al(r0 + hc - 8 * (CONF_TAPS - 1 - k)), rc), :])
            dv = _taps_rev(wdw_ref, dcbuf, r0, CONF_TAPS, rc)
            za = z_ref[rows, oa:oa + dc]
            sg = _sig(z_ref[rows, og:og + dc])
            dz_ref[rows, oa:oa + dc] = (dv * sg).astype(BF16)
            dz_ref[rows, og:og + dc] = (dv * za * sg * (1.0 - sg)).astype(BF16)
            cv = _taps(wsc_ref, sbuf, r0 + hs - 8 * (SHORT_TAPS - 1), SHORT_TAPS, rc)
            dz_ref[rows, ob:ob + ds] = (dcat_ref[rows, dp + dc:d] * cv).astype(BF16)
            dcv = dsbuf[rows, :]
            for k in range(SHORT_TAPS):
                ds8[k] += _fold8(dcv * sbuf[pl.ds(_al(r0 + hs - 8 * (SHORT_TAPS - 1 - k)), rc), :])
            dpv = _taps_rev(wsc_ref, dsbuf, r0, SHORT_TAPS, rc)
            dz_ref[rows, oc:oc + ds] = (dpv * z_ref[rows, ox:ox + ds]).astype(BF16)
            dz_ref[rows, ox:ox + ds] = (dpv * z_ref[rows, oc:oc + ds]).astype(BF16)
        _chunks(tt, back, unroll=2)

        @pl.when(last)
        def _():
            dps_ref[...] = jnp.sum(ps8[...], axis=0, keepdims=True)
            dwdw_ref[...] = jnp.sum(dw8[...], axis=1)
            dwsc_ref[...] = jnp.sum(ds8[...], axis=1)
            dlng_ref[...] = jnp.sum(ln8[0], axis=0, keepdims=True)
            dlnb_ref[...] = jnp.sum(ln8[1], axis=0, keepdims=True)
            dbdw_ref[...] = jnp.sum(ln8[2], axis=0, keepdims=True)

    return pl.pallas_call(
        body, name=name, grid=(nt,),
        in_specs=[_rows(tt, d), _next_block(hc, d, tt, t),
                  _rows(tt, din), _prev_block(hc, d, tt, t), _prev_block(hs, din, tt, t), _next_block(hs, din, tt, t),
                  _rows(tt, dc), _next_block(hc, dc, tt, t),
                  _layer((dp, dp), l), _layer((1, dp), l), _whole((MAX_WINDOW, dp)), _whole((1, dp)),
                  _layer((CONF_TAPS, dc), l), _layer((1, dc), l), _layer((1, dc), l), _layer((SHORT_TAPS, ds), l)],
        out_specs=[_rows(tt, din), _whole((dp, dp)), _whole((1, dp)), _whole((CONF_TAPS, dc)), _whole((1, dc)),
                   _whole((1, dc)), _whole((1, dc)), _whole((SHORT_TAPS, ds))],
        out_shape=[jax.ShapeDtypeStruct((t, din), BF16), jax.ShapeDtypeStruct((dp, dp), F32),
                   jax.ShapeDtypeStruct((1, dp), F32), jax.ShapeDtypeStruct((CONF_TAPS, dc), F32),
                   jax.ShapeDtypeStruct((1, dc), F32), jax.ShapeDtypeStruct((1, dc), F32),
                   jax.ShapeDtypeStruct((1, dc), F32), jax.ShapeDtypeStruct((SHORT_TAPS, ds), F32)],
        scratch_shapes=[pltpu.VMEM((hp + tt, dp), F32), pltpu.VMEM((tt + hp, dp), F32), pltpu.VMEM((tt, dp), F32),
                        pltpu.VMEM((tt, dp), BF16), pltpu.VMEM((hc + tt, dc), F32), pltpu.VMEM((tt + hc, dc), F32),
                        pltpu.VMEM((hs + tt, ds), F32), pltpu.VMEM((tt + hs, ds), F32),
                        pltpu.VMEM((CONF_TAPS, 8, dc), F32), pltpu.VMEM((SHORT_TAPS, 8, ds), F32),
                        pltpu.VMEM((3, 8, dc), F32), pltpu.VMEM((8, dp), F32)],
        compiler_params=_params("arbitrary"),
    )(dcat, dcat, z, z, z, z, c, c, mbd, pscale3, pmask, wlane, wdw, lng3, lnb3, wsc)


def _softmax_rows(qh, kh, scale):
    s = lax.dot_general(qh, kh, _DIMS["nt"], preferred_element_type=F32) * scale
    e = jnp.exp(s - jnp.max(s, axis=-1, keepdims=True))
    return e / jnp.sum(e, axis=-1, keepdims=True)


def _attn_fwd(q, k, v, name):
    t, d = q.shape
    m = k.shape[0]
    hd = d // HEADS
    scale = hd ** -0.5
    tt = min(TILE_ATTN, t)

    def body(q_ref, k_ref, v_ref, o_ref):
        for h in range(HEADS):
            sl = slice(h * hd, (h + 1) * hd)
            p = _softmax_rows(q_ref[:, sl], k_ref[:, sl], scale)
            o_ref[:, sl] = jnp.dot(p.astype(BF16), v_ref[:, sl], preferred_element_type=F32).astype(BF16)

    return pl.pallas_call(
        body, name=name, grid=(t // tt,),
        in_specs=[_rows(tt, d), _whole((m, d)), _whole((m, d))], out_specs=_rows(tt, d),
        out_shape=jax.ShapeDtypeStruct((t, d), BF16), compiler_params=_params("parallel"),
    )(q, k, v)


def _attn_bwd(q, k, v, do, name):
    t, d = q.shape
    m = k.shape[0]
    hd = d // HEADS
    scale = hd ** -0.5
    tt = min(TILE_ATTN, t)

    def body(q_ref, k_ref, v_ref, do_ref, dq_ref, dk_ref, dv_ref):
        @pl.when(pl.program_id(0) == 0)
        def _():
            dk_ref[...] = jnp.zeros_like(dk_ref)
            dv_ref[...] = jnp.zeros_like(dv_ref)

        for h in range(HEADS):
            sl = slice(h * hd, (h + 1) * hd)
            qh, kh, vh, doh = q_ref[:, sl], k_ref[:, sl], v_ref[:, sl], do_ref[:, sl]
            p = _softmax_rows(qh, kh, scale)
            dv_ref[:, sl] += lax.dot_general(p.astype(BF16), doh, _DIMS["tn"], preferred_element_type=F32)
            dp = lax.dot_general(doh, vh, _DIMS["nt"], preferred_element_type=F32)
            ds = (p * (dp - jnp.sum(dp * p, axis=-1, keepdims=True)) * scale).astype(BF16)
            dq_ref[:, sl] = jnp.dot(ds, kh, preferred_element_type=F32).astype(BF16)
            dk_ref[:, sl] += lax.dot_general(ds, qh, _DIMS["tn"], preferred_element_type=F32)

    return pl.pallas_call(
        body, name=name, grid=(t // tt,),
        in_specs=[_rows(tt, d), _whole((m, d)), _whole((m, d)), _rows(tt, d)],
        out_specs=[_rows(tt, d), _whole((m, d)), _whole((m, d))],
        out_shape=[jax.ShapeDtypeStruct((t, d), BF16), jax.ShapeDtypeStruct((m, d), F32),
                   jax.ShapeDtypeStruct((m, d), F32)],
        compiler_params=_params("arbitrary"),
    )(q, k, v, do)


def _lane_chunks(f):
    w = 256 if f % 256 == 0 else 128 if f % 128 == 0 else f
    return [(c0, w) for c0 in range(0, f, w)]


def _ffn_act_fwd(u, wc, l, name):
    t, f2 = u.shape
    f = f2 // 2
    tt = min(TILE_FFN, t)
    hs, rc = SHORT_HALO, ROW_CHUNK
    lanes = _lane_chunks(f)

    def body(u_ref, up_ref, wc_ref, a_ref, ubuf):
        ubuf[0:hs, :] = up_ref[...]

        @pl.when(pl.program_id(0) == 0)
        def _():
            ubuf[0:hs, :] = _shift_down(ubuf[0:hs, :])

        def step(r0):
            rows = pl.ds(r0, rc)
            ubuf[pl.ds(_al(hs + r0), rc), :] = u_ref[rows, :]
            start = r0 + hs - 8 * (SHORT_TAPS - 1)
            for c0, cw in lanes:
                g = _taps(wc_ref, ubuf, start, SHORT_TAPS, rc, slice(c0, c0 + cw))
                vv = _taps(wc_ref, ubuf, start, SHORT_TAPS, rc, slice(f + c0, f + c0 + cw))
                a_ref[rows, c0:c0 + cw] = (g * _sig(g) * vv).astype(BF16)
        _chunks(tt, step, unroll=2)

    return pl.pallas_call(
        body, name=name, grid=(t // tt,),
        in_specs=[_rows(tt, f2), _prev_block(hs, f2, tt, t), _layer((SHORT_TAPS, f2), l)],
        out_specs=_rows(tt, f), out_shape=jax.ShapeDtypeStruct((t, f), BF16),
        scratch_shapes=[pltpu.VMEM((hs + tt, f2), F32)], compiler_params=_params("parallel"),
    )(u, u, wc)


def _ffn_act_bwd(u, da, wc, l, name):
    t, f2 = u.shape
    f = f2 // 2
    tt = min(TILE_FFN, t)
    nt = t // tt
    hs, rc = SHORT_HALO, ROW_CHUNK
    lanes = _lane_chunks(f)

    def body(u_ref, up_ref, un_ref, da_ref, dan_ref, wc_ref, du_ref, dwc_ref, ubuf, danbuf, dbuf, dw8):
        i = pl.program_id(0)
        first, last = i == 0, i == nt - 1
        ubuf[0:hs, :] = up_ref[...]
        ubuf[hs + tt:hs + tt + hs, :] = un_ref[...]
        danbuf[...] = dan_ref[...]

        @pl.when(first)
        def _():
            dw8[...] = jnp.zeros_like(dw8)
            ubuf[0:hs, :] = _shift_down(ubuf[0:hs, :])

        @pl.when(last)
        def _():
            ubuf[hs + tt:hs + tt + hs, :] = _shift_up(ubuf[hs + tt:hs + tt + hs, :])
            danbuf[...] = _shift_up(danbuf[...])

        def fill(r0):
            ubuf[pl.ds(_al(hs + r0), rc), :] = u_ref[pl.ds(r0, rc), :]
        _chunks(tt, fill)

        def conv_grads(r0, n, da_rows):
            start = r0 + hs - 8 * (SHORT_TAPS - 1)
            for c0, cw in lanes:
                sl_g, sl_v = slice(c0, c0 + cw), slice(f + c0, f + c0 + cw)
                g = _taps(wc_ref, ubuf, start, SHORT_TAPS, n, sl_g)
                vv = _taps(wc_ref, ubuf, start, SHORT_TAPS, n, sl_v)
                dav = da_rows(c0, cw)
                sg = _sig(g)
                dbuf[pl.ds(_al(r0), n), sl_g] = dav * vv * (sg * (1.0 + g * (1.0 - sg)))
                dbuf[pl.ds(_al(r0), n), sl_v] = dav * (g * sg)

        _chunks(tt, lambda r0: conv_grads(r0, rc, lambda c0, cw: da_ref[pl.ds(r0, rc), c0:c0 + cw]), unroll=2)
        conv_grads(tt, hs, lambda c0, cw: danbuf[:, c0:c0 + cw])

        def back(r0):
            rows = pl.ds(r0, rc)
            for c0, cw in lanes:
                for off in (c0, f + c0):
                    sl = slice(off, off + cw)
                    du_ref[rows, sl] = _taps_rev(wc_ref, dbuf, r0, SHORT_TAPS, rc, sl).astype(BF16)
                    dd = dbuf[rows, sl]
                    for k in range(SHORT_TAPS):
                        dw8[k, :, sl] += _fold8(dd * ubuf[pl.ds(_al(r0 + hs - 8 * (SHORT_TAPS - 1 - k)), rc), sl])
        _chunks(tt, back, unroll=2)

        @pl.when(last)
        def _():
            dwc_ref[...] = jnp.sum(dw8[...], axis=1)

    return pl.pallas_call(
        body, name=name, grid=(nt,),
        in_specs=[_rows(tt, f2), _prev_block(hs, f2, tt, t), _next_block(hs, f2, tt, t),
                  _rows(tt, f), _next_block(hs, f, tt, t), _layer((SHORT_TAPS, f2), l)],
        out_specs=[_rows(tt, f2), _whole((SHORT_TAPS, f2))],
        out_shape=[jax.ShapeDtypeStruct((t, f2), BF16), jax.ShapeDtypeStruct((SHORT_TAPS, f2), F32)],
        scratch_shapes=[pltpu.VMEM((hs + tt + hs, f2), F32), pltpu.VMEM((hs, f), F32), pltpu.VMEM((tt + hs, f2), F32),
                        pltpu.VMEM((SHORT_TAPS, 8, f2), F32)],
        compiler_params=_params("arbitrary"),
    )(u, u, u, da, da, wc)


def _place():
    return lax.axis_index("x"), lax.axis_index("y"), lax.axis_index("c")


def _flip(v, bit):
    return 1 - v if bit else v


def _peers(x, y, c):
    out = []
    for kk in range(1, N_DEV):
        px, py, pc = _flip(x, kk & 4), _flip(y, kk & 2), _flip(c, kk & 1)
        out.append((kk - 1, (px, py, pc), 4 * px + 2 * py + pc))
    return out


def _allgather(shards, name):
    nt = len(shards)

    def body(*refs):
        srcs, outs = refs[:nt], refs[nt:2 * nt]
        send_sems, recv_sems, local_sems = refs[2 * nt:]
        x, y, c = _place()
        me, sibling = (x, y, c), (x, y, 1 - c)
        chips = [(1 - x, y), (x, 1 - y), (1 - x, 1 - y)]

        def rows(ti, px, py, pc):
            r = srcs[ti].shape[1]
            return outs[ti].at[:, pl.ds((4 * px + 2 * py + pc) * r, r), :]

        def copy(ti, kk, block, to, src=None):
            return pltpu.make_async_remote_copy(
                src_ref=rows(ti, *block) if src is None else src, dst_ref=rows(ti, *block),
                send_sem=send_sems.at[ti, kk], recv_sem=recv_sems.at[ti, kk], device_id=to, device_id_type=MESH)

        mine = [pltpu.make_async_copy(srcs[ti], rows(ti, *me), local_sems.at[ti]) for ti in range(nt)]
        for cp in mine:
            cp.start()
        first = []
        for ti in range(nt):
            first.append(copy(ti, 0, me, sibling, src=srcs[ti]))
            first += [copy(ti, 1 + j, me, (*chip, c), src=srcs[ti]) for j, chip in enumerate(chips)]
        for cp in first:
            cp.start()
        passed = []
        for j, chip in enumerate(chips):
            for ti in range(nt):
                copy(ti, 1 + j, (*chip, c), me).wait_recv()
                fwd = copy(ti, 4 + j, (*chip, c), sibling)
                fwd.start()
                passed.append(fwd)
        for ti in range(nt):
            copy(ti, 0, sibling, me).wait_recv()
            for j, chip in enumerate(chips):
                copy(ti, 4 + j, (*chip, 1 - c), me).wait_recv()
        for cp in first + passed:
            cp.wait_send()
        for cp in mine:
            cp.wait()

    return pl.pallas_call(
        body, name=name,
        in_specs=[ANY] * nt, out_specs=[ANY] * nt,
        out_shape=[jax.ShapeDtypeStruct((s.shape[0], N_DEV * s.shape[1], s.shape[2]), s.dtype) for s in shards],
        scratch_shapes=[pltpu.SemaphoreType.DMA((nt, 7)), pltpu.SemaphoreType.DMA((nt, 7)),
                        pltpu.SemaphoreType.DMA((nt,))],
    )(*shards)


def _gather_piece(src, land, me, to):
    r = src.shape[0]
    return src, land.at[pl.ds(me * r, r), :]


def _scatter_piece(l):
    def piece(src, land, me, to):
        r = src.shape[0] // N_DEV
        return src.at[pl.ds(to * r, r), :], land.at[l, me]
    return piece


def _split_start(srcs, lands, piece, after, name):
    nt = len(srcs)

    def body(*refs):
        src_refs, land_refs = refs[:nt], refs[nt:2 * nt]
        send_sems, recv_sems, local_sems, token = refs[2 * nt + 1], refs[2 * nt + 2], refs[2 * nt + 3], refs[4 * nt + 4]
        x, y, c = _place()
        me = 4 * x + 2 * y + c
        for ti in range(nt):
            for slot, peer, flat in _peers(x, y, c):
                src, dst = piece(src_refs[ti], land_refs[ti], me, flat)
                pltpu.make_async_remote_copy(
                    src_ref=src, dst_ref=dst, send_sem=send_sems.at[7 * ti + slot], recv_sem=recv_sems.at[7 * ti + slot],
                    device_id=peer, device_id_type=MESH).start()
        for ti in range(nt):
            pltpu.make_async_copy(*piece(src_refs[ti], land_refs[ti], me, me), local_sems.at[ti]).start()
        token[...] = jnp.zeros_like(token)

    both = list(srcs) + list(lands)
    return pl.pallas_call(
        body, name=name,
        in_specs=[HBM] * (2 * nt) + [ANY],
        out_specs=[SEM, SEM, SEM] + [HBM] * (2 * nt) + [pl.BlockSpec(memory_space=pltpu.VMEM)],
        out_shape=[pltpu.SemaphoreType.DMA((7 * nt,)), pltpu.SemaphoreType.DMA((7 * nt,)), pltpu.SemaphoreType.DMA((nt,))]
        + [pltpu.HBM(a.shape, a.dtype) for a in both] + [jax.ShapeDtypeStruct((8, 128), F32)],
        input_output_aliases={i: i + 3 for i in range(2 * nt)},
        compiler_params=pltpu.CompilerParams(has_side_effects=EFFECT),
    )(*[pltpu.with_memory_space_constraint(a, pltpu.HBM) for a in both], after)


def _split_wait(started, after, piece, name):
    send_sems, recv_sems, local_sems, *both = started[:-1]
    nt = len(both) // 2

    def body(*refs):
        src_refs, land_refs = refs[:nt], refs[nt:2 * nt]
        send_ref, recv_ref, local_ref = refs[2 * nt], refs[2 * nt + 1], refs[2 * nt + 2]
        x, y, c = _place()
        me = 4 * x + 2 * y + c
        for ti in range(nt):
            src, dst = piece(src_refs[ti], land_refs[ti], me, me)
            for slot in range(N_DEV - 1):
                cp = pltpu.make_async_remote_copy(
                    src_ref=src, dst_ref=dst, send_sem=send_ref.at[7 * ti + slot], recv_sem=recv_ref.at[7 * ti + slot],
                    device_id=(x, y, c), device_id_type=MESH)
                cp.wait_send()
                cp.wait_recv()
            pltpu.make_async_copy(src, dst, local_ref.at[ti]).wait()

    outs = pl.pallas_call(
        body, name=name,
        in_specs=[HBM] * (2 * nt) + [SEM, SEM, SEM, ANY], out_specs=[HBM] * (2 * nt),
        out_shape=[pltpu.HBM(a.shape, a.dtype) for a in both],
        input_output_aliases={i: i for i in range(2 * nt)},
        compiler_params=pltpu.CompilerParams(has_side_effects=EFFECT),
    )(*both, send_sems, recv_sems, local_sems, after)
    return outs[nt:]


def _adam(w, g, m, v):
    m2 = ADAM_B1 * m + (1.0 - ADAM_B1) * g
    v2 = ADAM_B2 * v + (1.0 - ADAM_B2) * (g * g)
    m_hat = m2 / (1.0 - ADAM_B1 ** ADAM_STEP)
    v_hat = v2 / (1.0 - ADAM_B2 ** ADAM_STEP)
    return -ADAM_LR * (m_hat / (jnp.sqrt(v_hat) + ADAM_EPS) + ADAM_WD * w), m2, v2


def _adam_sharded(recv, recv_first, w, m, v, lo, hi, name, prev=None, after=None):
    nl, r, c = w.shape
    tr = max([rows for rows in range(16, min(r, TILE_ADAM) + 1, 16) if r % rows == 0] or [r])

    def body(recv_ref, w_ref, m_ref, v_ref, *rest):
        g_ref, d_ref, m2_ref, v2_ref = rest[-4:]
        g = recv_ref[0].astype(F32)
        for s in range(1, N_DEV):
            g = g + recv_ref[s].astype(F32)
        g_ref[...] = g
        d_ref[...], m2_ref[...], v2_ref[...] = _adam(w_ref[...], g, m_ref[...], v_ref[...])

    blk = pl.BlockSpec((None, tr, c), lambda li, i: (li + lo, i, 0))
    extra = [] if prev is None else list(prev)
    n_prev = len(extra)
    if after is not None:
        extra.append(after)
    return pl.pallas_call(
        body, name=name, grid=(hi - lo, r // tr),
        in_specs=[pl.BlockSpec((None, N_DEV, tr, c), lambda li, i: (li + lo - recv_first, 0, i, 0)), blk, blk, blk]
        + [ANY] * len(extra),
        out_specs=[blk] * 4, out_shape=[jax.ShapeDtypeStruct((nl, r, c), F32)] * 4,
        input_output_aliases={4 + i: i for i in range(n_prev)},
        compiler_params=_params("parallel", "parallel"),
    )(recv, w, m, v, *extra)


def _sum_sources(parts, name):
    _, r, c = parts.shape

    def body(p_ref, o_ref):
        g = p_ref[0]
        for s in range(1, N_DEV):
            g = g + p_ref[s]
        o_ref[...] = g

    return pl.pallas_call(
        body, name=name, grid=(1,), in_specs=[_whole((N_DEV, r, c))], out_specs=_whole((r, c)),
        out_shape=jax.ShapeDtypeStruct((r, c), F32), compiler_params=_params("arbitrary"),
    )(parts)


def _adam_flat(w, g, m, v, name):
    r, c = w.shape

    def body(w_ref, g_ref, m_ref, v_ref, d_ref, m2_ref, v2_ref):
        d_ref[...], m2_ref[...], v2_ref[...] = _adam(w_ref[...], g_ref[...], m_ref[...], v_ref[...])

    return pl.pallas_call(
        body, name=name, grid=(1,), in_specs=[_whole((r, c))] * 4, out_specs=[_whole((r, c))] * 3,
        out_shape=[jax.ShapeDtypeStruct((r, c), F32)] * 3, compiler_params=_params("arbitrary"),
    )(w, g, m, v)


def _pack(arrays):
    flat = jnp.concatenate([a.reshape(-1).astype(F32) for a in arrays])
    rows = -(-flat.shape[0] // 1024) * 8
    return jnp.pad(flat, (0, rows * 128 - flat.shape[0])).reshape(rows, 128)


def _unpack(slab, like):
    flat = slab.reshape(-1)
    out, at = [], 0
    for a in like:
        out.append(flat[at:at + a.size].reshape(a.shape))
        at += a.size
    return out


def kernel(x, mem, mem_norm, mix_pre_norm, mix_post_norm, w_in, pool_maps, pool_scale, conf_dw_w, conf_dw_b, conf_ln_g, conf_ln_b, sconv_w, w_out, xattn_pre_norm, xattn_post_norm, xattn_wq, xattn_wk, xattn_wv, xattn_wo, ffn_pre_norm, ffn_post_norm, ffn_w_up, ffn_conv_w, ffn_w_down, loss_target, m_mem_norm, m_mix_pre_norm, m_mix_post_norm, m_w_in, m_pool_maps, m_pool_scale, m_conf_dw_w, m_conf_dw_b, m_conf_ln_g, m_conf_ln_b, m_sconv_w, m_w_out, m_xattn_pre_norm, m_xattn_post_norm, m_xattn_wq, m_xattn_wk, m_xattn_wv, m_xattn_wo, m_ffn_pre_norm, m_ffn_post_norm, m_ffn_w_up, m_ffn_conv_w, m_ffn_w_down, v_mem_norm, v_mix_pre_norm, v_mix_post_norm, v_w_in, v_pool_maps, v_pool_scale, v_conf_dw_w, v_conf_dw_b, v_conf_ln_g, v_conf_ln_b, v_sconv_w, v_w_out, v_xattn_pre_norm, v_xattn_post_norm, v_xattn_wq, v_xattn_wk, v_xattn_wv, v_xattn_wo, v_ffn_pre_norm, v_ffn_post_norm, v_ffn_w_up, v_ffn_conv_w, v_ffn_w_down):
    weights = dict(mem_norm=mem_norm, mix_pre_norm=mix_pre_norm, mix_post_norm=mix_post_norm, w_in=w_in, pool_maps=pool_maps, pool_scale=pool_scale, conf_dw_w=conf_dw_w, conf_dw_b=conf_dw_b, conf_ln_g=conf_ln_g, conf_ln_b=conf_ln_b, sconv_w=sconv_w, w_out=w_out, xattn_pre_norm=xattn_pre_norm, xattn_post_norm=xattn_post_norm, xattn_wq=xattn_wq, xattn_wk=xattn_wk, xattn_wv=xattn_wv, xattn_wo=xattn_wo, ffn_pre_norm=ffn_pre_norm, ffn_post_norm=ffn_post_norm, ffn_w_up=ffn_w_up, ffn_conv_w=ffn_conv_w, ffn_w_down=ffn_w_down)
    mom1 = dict(mem_norm=m_mem_norm, mix_pre_norm=m_mix_pre_norm, mix_post_norm=m_mix_post_norm, w_in=m_w_in, pool_maps=m_pool_maps, pool_scale=m_pool_scale, conf_dw_w=m_conf_dw_w, conf_dw_b=m_conf_dw_b, conf_ln_g=m_conf_ln_g, conf_ln_b=m_conf_ln_b, sconv_w=m_sconv_w, w_out=m_w_out, xattn_pre_norm=m_xattn_pre_norm, xattn_post_norm=m_xattn_post_norm, xattn_wq=m_xattn_wq, xattn_wk=m_xattn_wk, xattn_wv=m_xattn_wv, xattn_wo=m_xattn_wo, ffn_pre_norm=m_ffn_pre_norm, ffn_post_norm=m_ffn_post_norm, ffn_w_up=m_ffn_w_up, ffn_conv_w=m_ffn_conv_w, ffn_w_down=m_ffn_w_down)
    mom2 = dict(mem_norm=v_mem_norm, mix_pre_norm=v_mix_pre_norm, mix_post_norm=v_mix_post_norm, w_in=v_w_in, pool_maps=v_pool_maps, pool_scale=v_pool_scale, conf_dw_w=v_conf_dw_w, conf_dw_b=v_conf_dw_b, conf_ln_g=v_conf_ln_g, conf_ln_b=v_conf_ln_b, sconv_w=v_sconv_w, w_out=v_w_out, xattn_pre_norm=v_xattn_pre_norm, xattn_post_norm=v_xattn_post_norm, xattn_wq=v_xattn_wq, xattn_wk=v_xattn_wk, xattn_wv=v_xattn_wv, xattn_wo=v_xattn_wo, ffn_pre_norm=v_ffn_pre_norm, ffn_post_norm=v_ffn_post_norm, ffn_w_up=v_ffn_w_up, ffn_conv_w=v_ffn_conv_w, ffn_w_down=v_ffn_w_down)
    names = list(weights)

    nl, d = mix_pre_norm.shape
    x0, mem0, target = _to_steps(x[0]), mem[0], _to_steps(loss_target[0])
    dp, dc, ds, *_ = _mix_dims(d)
    pg = dp // len(POOL_WINDOWS)
    me = 4 * lax.axis_index("x") + 2 * lax.axis_index("y") + lax.axis_index("c")

    big = ["w_in", "w_out", "xattn_wq", "xattn_wk", "xattn_wv", "xattn_wo", "ffn_w_up", "ffn_w_down"]
    transposed = ("w_in", "ffn_w_up")

    def row_shard(n, a):
        return a.transpose(0, 2, 1) if n in transposed else a

    shards = [row_shard(n, weights[n]).astype(BF16) for n in big]
    taps = ["conf_dw_w", "sconv_w", "ffn_conv_w"]
    tap_slab = _pack([weights[n] for n in taps])
    win0, tap_all = _allgather([shards[0][0:1], tap_slab[None]], "gather_weights0")
    layer_w = [{"w_in": win0[0]}]

    def land(s):
        return lax.empty((N_DEV * s.shape[1], s.shape[2]), BF16)

    first_a = _split_start([s[0] for s in shards[1:6]], [land(s) for s in shards[1:6]], _gather_piece, win0,
                           "gather_start0a")
    first_b = _split_start([s[0] for s in shards[6:]], [land(s) for s in shards[6:]], _gather_piece, first_a[-1],
                           "gather_start0b")
    tap_all = tap_all[0].reshape(N_DEV, *tap_slab.shape)
    tap_parts = [_unpack(tap_all[p], [weights[n] for n in taps]) for p in range(N_DEV)]
    wdw, wsc, wcf = (jnp.concatenate([tap_parts[p][i] for p in range(N_DEV)], axis=-1) for i in range(3))

    def g3(a):
        return a.reshape(a.shape[0], 1, a.shape[-1])

    mbd = jnp.zeros((nl, dp, dp), F32)
    for gi in range(len(POOL_WINDOWS)):
        mbd = mbd.at[:, gi * pg:(gi + 1) * pg, gi * pg:(gi + 1) * pg].set(pool_maps[:, gi])
    mbd = mbd.astype(BF16)
    pre1, post1, pre2, post2, pre3, post3 = (g3(weights[n]) for n in (
        "mix_pre_norm", "mix_post_norm", "xattn_pre_norm", "xattn_post_norm", "ffn_pre_norm", "ffn_post_norm"))
    pscale3, bdw3, lng3, lnb3 = g3(pool_scale), g3(conf_dw_b), g3(conf_ln_g), g3(conf_ln_b)
    memg3 = mem_norm.reshape(1, 1, d)

    def mm(a, b, mode, dt, name, tm=2048, tn=1024, tk=1024, a_outer=True):
        return _matmul(a, b, mode, dt, name, tm=tm, tn=tn, tk=tk, a_outer=a_outer)

    mem_n = _prenorm(mem0, memg3, 0, "mem_norm")
    xs = x0
    h = _prenorm(xs, pre1, 0, "pre_norm0")
    saved = []
    for l in range(nl):
        ps_l = pscale3
        if l + 1 < nl:
            flying = _split_start([s[l + 1] for s in shards], [land(s) for s in shards], _gather_piece,
                                  xs if l else first_b[-1], f"gather_start{l + 1}")
            ps_l = pscale3 + flying[-1][0, 0]
        w = layer_w[l]
        s = {"x": xs, "h": h}
        s["z"] = mm(h, w["w_in"], "nt", F32, f"z{l}", tm=1024, tn=4096)
        s["cat"], s["c"] = _mix_fwd(s["z"], mbd, ps_l, wdw, bdw3, lng3, lnb3, wsc, l, f"mix_fwd{l}")
        if l == 0:
            w.update(zip(big[1:6], _split_wait(first_a, s["cat"], _gather_piece, "gather_wait0a")))
        s["y1"], s["x1"], s["h1"] = _matmul_resnorm(s["cat"], w["w_out"], xs, post1, l, pre2, l, f"y1_{l}", tm=1024)
        s["q"] = mm(s["h1"], w["xattn_wq"], "nn", BF16, f"q{l}")
        s["k"] = mm(mem_n, w["xattn_wk"], "nn", BF16, f"k{l}")
        s["v"] = mm(mem_n, w["xattn_wv"], "nn", BF16, f"v{l}")
        s["o"] = _attn_fwd(s["q"], s["k"], s["v"], f"attn_fwd{l}")
        s["y2"], s["x2"], s["h2"] = _matmul_resnorm(s["o"], w["xattn_wo"], s["x1"], post2, l, pre3, l, f"y2_{l}", tm=1024)
        if l == 0:
            w.update(zip(big[6:], _split_wait(first_b, s["h2"], _gather_piece, "gather_wait0b")))
        s["u"] = mm(s["h2"], w["ffn_w_up"], "nt", F32, f"u{l}", tn=1408, a_outer=False)
        s["a"] = _ffn_act_fwd(s["u"], wcf, l, f"ffn_act{l}")
        if l + 1 < nl:
            s["y3"], xs, h = _matmul_resnorm(s["a"], w["ffn_w_down"], s["x2"], post3, l, pre1, l + 1, f"y3_{l}", tm=1024)
            layer_w.append(dict(zip(big, _split_wait(flying, xs, _gather_piece, f"gather_wait{l + 1}"))))
        saved.append(s)

    last = saved[-1]
    dxn, dy3, dg_post3, loss_lanes = _matmul_loss(
        last["a"], layer_w[-1]["ffn_w_down"], last["x2"], post3, nl - 1, target, "y3_loss", tm=1024)
    loss = lax.psum(loss_lanes[0, 0], ("x", "y", "c"))

    recvs = [lax.empty((max(nl - 1, 1), N_DEV, s.shape[1], d), BF16) for s in shards]
    recv0 = [lax.empty((1, N_DEV, s.shape[1], d), BF16) for s in shards]
    small = {n: [None] * nl for n in names if n not in big and n != "mem_norm"}
    small["ffn_post_norm"][nl - 1] = dg_post3
    dmem_n = jnp.zeros(mem0.shape, F32)
    flying = None
    for l in reversed(range(nl)):
        s, w = saved[l], layer_w[l]
        wc_l = wcf if flying is None else wcf + flying[-1][0, 0]
        gw = {}
        da = mm(dy3, w["ffn_w_down"], "nt", F32, f"da{l}", tn=1408, a_outer=False)
        gw["ffn_w_down"] = mm(s["a"], dy3, "tn", BF16, f"dw_down{l}", tm=1408, tk=2048)
        du, small["ffn_conv_w"][l] = _ffn_act_bwd(s["u"], da, wc_l, l, f"ffn_act_bwd{l}")
        gw["ffn_w_up"] = mm(du, s["h2"], "tn", BF16, f"dw_up{l}", tm=1408, tk=2048)
        dx2, small["ffn_pre_norm"][l], dy2, small["xattn_post_norm"][l] = _matmul_norm_bwd(
            du, w["ffn_w_up"], "nn", dxn, s["x2"], pre3, l, f"dh2_{l}", s["y2"], post2, l)
        do = mm(dy2, w["xattn_wo"], "nt", BF16, f"do{l}")
        gw["xattn_wo"] = mm(s["o"], dy2, "tn", BF16, f"dw_o{l}", tk=4096)
        dq, dk, dv = _attn_bwd(s["q"], s["k"], s["v"], do, f"attn_bwd{l}")
        dkb, dvb = dk.astype(BF16), dv.astype(BF16)
        gw["xattn_wq"] = mm(s["h1"], dq, "tn", BF16, f"dw_q{l}", tk=4096)
        gw["xattn_wk"] = mm(mem_n, dkb, "tn", BF16, f"dw_k{l}")
        gw["xattn_wv"] = mm(mem_n, dvb, "tn", BF16, f"dw_v{l}")
        dmem_n = dmem_n + mm(dkb, w["xattn_wk"], "nt", F32, f"dmem_k{l}") \
            + mm(dvb, w["xattn_wv"], "nt", F32, f"dmem_v{l}")
        pre2_l = pre2
        if l == 0:
            flying0 = _split_start([gw[n] for n in big[2:]], recv0[2:], _scatter_piece(0), dmem_n, "scatter_start0a")
            pre2_l = pre2 + flying0[-1][0, 0]
        dx1, small["xattn_pre_norm"][l], dy1, small["mix_post_norm"][l] = _matmul_norm_bwd(
            dq, w["xattn_wq"], "nt", dx2, s["x1"], pre2_l, l, f"dh1_{l}", s["y1"], post1, l, tm=1024)
        dcat = mm(dy1, w["w_out"], "nt", F32, f"dcat{l}")
        gw["w_out"] = mm(s["cat"], dy1, "tn", BF16, f"dw_out{l}", tk=4096)
        dz, dmbd, dps, dwdw, dbdw, dlng, dlnb, dwsc = _mix_bwd(
            dcat, s["z"], s["c"], mbd, pscale3, wdw, lng3, lnb3, wsc, l, f"mix_bwd{l}")
        small["pool_maps"][l] = jnp.stack([dmbd[gi * pg:(gi + 1) * pg, gi * pg:(gi + 1) * pg]
                                           for gi in range(len(POOL_WINDOWS))])
        small["pool_scale"][l], small["conf_dw_w"][l], small["conf_dw_b"][l] = dps, dwdw, dbdw
        small["conf_ln_g"][l], small["conf_ln_b"][l], small["sconv_w"][l] = dlng, dlnb, dwsc
        gw["w_in"] = mm(dz, s["h"], "tn", BF16, f"dw_in{l}", tm=4096, tk=2048)
        if l > 0:
            dxn, small["mix_pre_norm"][l], dy3, small["ffn_post_norm"][l - 1] = _matmul_norm_bwd(
                dz, w["w_in"], "nn", dx1, s["x"], pre1, l, f"dh{l}", saved[l - 1]["y3"], post3, l - 1)
            if flying is not None:
                recvs = _split_wait(flying, dxn, _scatter_piece(l), f"scatter_wait{l + 1}")
            flying = _split_start([gw[n] for n in big], recvs, _scatter_piece(l - 1), dxn, f"scatter_start{l}")
        else:
            if flying is not None:
                recvs = _split_wait(flying, dx1, _scatter_piece(0), "scatter_wait1")
            flying = _split_start([gw[n] for n in big[:2]], recv0[:2], _scatter_piece(0), dx1, "scatter_start0b")
            dxn, small["mix_pre_norm"][l] = _matmul_norm_bwd(
                dz, w["w_in"], "nn", dx1, s["x"], pre1 + flying[-1][0, 0], l, "dh0")
    grad_x = _from_steps(dxn)[None]
    _, dg_mem = _norm_bwd(jnp.zeros(mem0.shape, F32), dmem_n, mem0, memg3, 0, "norm_bwd_mem")

    small_names = [n for n in names if n not in big]
    partial = {n: (dg_mem.reshape(d) if n == "mem_norm" else
                   jnp.stack([g.reshape(g.shape[-1]) if g.shape[0] == 1 and weights[n].ndim == 2 else g
                              for g in small[n]])) for n in small_names}
    slab = _pack([partial[n] for n in small_names])
    small_fly = _split_start([slab], [lax.empty((N_DEV * slab.shape[0], slab.shape[1]), F32)], _gather_piece, dg_mem,
                             "gather_small_start")
    wmv = {n: [row_shard(n, a[n]) for a in (weights, mom1, mom2)] for n in big}
    upper = {n: _adam_sharded(recv, 1, *wmv[n], 1, nl, f"adam_{n}", after=small_fly[-1])
             for n, recv in zip(big, recvs)} if nl > 1 else {}
    upper_done = sum(r[3][0, 0, :1] for r in upper.values()) + small_fly[-1][0, :1]
    gathered = _split_wait(small_fly, upper_done, _gather_piece, "gather_small_wait")[0].reshape(N_DEV, *slab.shape)
    summed = dict(zip(small_names, _unpack(_sum_sources(gathered, "sum_small_grads"), [partial[n] for n in small_names])))
    grad = {}
    for n in small_names:
        g = summed[n]
        if n in taps:
            width = weights[n].shape[-1]
            g = lax.dynamic_slice_in_dim(g, me * width, width, axis=g.ndim - 1)
        grad[n] = g

    delta, new_m, new_v = {}, {}, {}
    upd = _adam_flat(_pack([weights[n] for n in small_names]), _pack([grad[n] for n in small_names]),
                     _pack([mom1[n] for n in small_names]), _pack([mom2[n] for n in small_names]), "adam_small")
    for out, slab_o in zip((delta, new_m, new_v), upd):
        out.update(zip(small_names, _unpack(slab_o, [weights[n] for n in small_names])))
    done = upper_done + upd[0][0, :1]
    recv0 = _split_wait(flying, done, _scatter_piece(0), "scatter_wait0b") \
        + _split_wait(flying0, done, _scatter_piece(0), "scatter_wait0a")
    for n, recv in zip(big, recv0):
        res = _adam_sharded(recv, 0, *wmv[n], 0, 1, f"adam0_{n}", prev=upper.get(n))
        grad[n], delta[n], new_m[n], new_v[n] = (row_shard(n, r) for r in res)

    return (loss, grad_x, *[grad[n] for n in names], *[delta[n] for n in names],
            *[new_m[n] for n in names], *[new_v[n] for n in names])
```

```python
import jax
import jax.numpy as jnp
from jax import lax
from jax.experimental import pallas as pl
from jax.experimental.pallas import tpu as pltpu

F32, BF16 = jnp.float32, jnp.bfloat16
EPS = 1e-6
POOL_WINDOWS = (2, 4, 8, 16)
MAX_WINDOW = 16
CONF_TAPS, SHORT_TAPS = 31, 3
CONF_HALO, POOL_HALO, SHORT_HALO = 256, 128, 16
ROW_CHUNK = 64
HEADS = 4
N_DEV = 8
ADAM_LR, ADAM_B1, ADAM_B2, ADAM_EPS, ADAM_WD, ADAM_STEP = 0.001, 0.9, 0.999, 1e-08, 0.01, 10
VMEM_LIMIT_V7X = 56 * 2**20
MESH = pl.DeviceIdType.MESH
ANY = pl.BlockSpec(memory_space=pl.ANY)
HBM = pl.BlockSpec(memory_space=pltpu.HBM)
SEM = pl.BlockSpec(memory_space=pltpu.SEMAPHORE)
EFFECT = pltpu.SideEffectType.DATAFLOW_SIDE_EFFECTING

TILE_NORM, TILE_MIX, TILE_FFN, TILE_ATTN, TILE_ADAM = 256, 512, 256, 512, 352


def _params(*sem):
    return pltpu.CompilerParams(dimension_semantics=sem, vmem_limit_bytes=VMEM_LIMIT_V7X)


def _sig(x):
    return 1.0 / (1.0 + jnp.exp(-x))


def _rms(x):
    r = lax.rsqrt(jnp.mean(x * x, axis=-1, keepdims=True) + EPS)
    return x * r, r


def _rms_bwd(dout, g, n, r):
    dn = dout * g
    return r * (dn - n * jnp.mean(dn * n, axis=-1, keepdims=True))


def _rows(tt, c):
    return pl.BlockSpec((tt, c), lambda i: (i, 0))


def _whole(shape):
    return pl.BlockSpec(shape, lambda i: (0,) * len(shape))


def _layer(shape, l):
    return pl.BlockSpec((None,) + shape, lambda i: (l,) + (0,) * len(shape))


def _colsum(x):
    return jnp.sum(x, axis=0, keepdims=True)


_DIMS = {"nn": (((1,), (0,)), ((), ())), "nt": (((1,), (1,)), ((), ())), "tn": (((0,), (0,)), ((), ()))}


def _matmul(a, b, mode, out_dtype, name, *, tm, tn, tk, a_outer=True):
    if mode == "nn":
        (m, k), (k2, n) = a.shape, b.shape
    elif mode == "nt":
        (m, k), (n, k2) = a.shape, b.shape
    else:
        (k, m), (k2, n) = a.shape, b.shape
    assert k == k2, (name, a.shape, b.shape)
    tm, tn, tk = min(tm, m), min(tn, n), min(tk, k)
    assert m % tm == 0 and n % tn == 0 and k % tk == 0, (name, m, n, k, tm, tn, tk)
    gm, gn, gk = m // tm, n // tn, k // tk

    def ij(g0, g1):
        return (g0, g1) if a_outer else (g1, g0)

    def a_map(g0, g1, kk):
        i, _ = ij(g0, g1)
        return (kk, i) if mode == "tn" else (i, kk)

    def b_map(g0, g1, kk):
        _, j = ij(g0, g1)
        return (j, kk) if mode == "nt" else (kk, j)

    def o_map(g0, g1, kk):
        return ij(g0, g1)

    a_block = (tk, tm) if mode == "tn" else (tm, tk)
    b_block = (tn, tk) if mode == "nt" else (tk, tn)
    dims = _DIMS[mode]

    def body(a_ref, b_ref, o_ref, *acc):
        p = lax.dot_general(a_ref[...].astype(BF16), b_ref[...].astype(BF16), dims, preferred_element_type=F32)
        if gk == 1:
            o_ref[...] = p.astype(o_ref.dtype)
        else:
            kk = pl.program_id(2)

            @pl.when(kk == 0)
            def _():
                acc[0][...] = p

            @pl.when(kk > 0)
            def _():
                acc[0][...] += p

            @pl.when(kk == gk - 1)
            def _():
                o_ref[...] = acc[0][...].astype(o_ref.dtype)

    return pl.pallas_call(
        body, name=name, grid=(gm, gn, gk) if a_outer else (gn, gm, gk),
        in_specs=[pl.BlockSpec(a_block, a_map), pl.BlockSpec(b_block, b_map)],
        out_specs=pl.BlockSpec((tm, tn), o_map),
        out_shape=jax.ShapeDtypeStruct((m, n), out_dtype),
        scratch_shapes=[pltpu.VMEM((tm, tn), F32)] if gk > 1 else [],
        compiler_params=_params("parallel", "parallel", "arbitrary"),
    )(a, b)


def _prenorm(x, g3, l, name):
    t, d = x.shape
    tt = min(TILE_NORM, t)

    def body(x_ref, g_ref, h_ref):
        n, _ = _rms(x_ref[...])
        h_ref[...] = (n * g_ref[...]).astype(BF16)

    return pl.pallas_call(
        body, name=name, grid=(t // tt,),
        in_specs=[_rows(tt, d), _layer((1, d), l)], out_specs=_rows(tt, d),
        out_shape=jax.ShapeDtypeStruct((t, d), BF16), compiler_params=_params("parallel"),
    )(x, g3)


def _norm_bwd(dxn, dh, x_in, gpre3, l, name, y_prev=None, gpost3=None, l_prev=None):
    t, d = x_in.shape
    tt = min(TILE_NORM, t)
    has_prev = y_prev is not None

    def body(*refs):
        if has_prev:
            dxn_ref, dh_ref, x_ref, g_ref, y_ref, g2_ref, dx_ref, dg_ref, dy_ref, dg2_ref = refs
        else:
            dxn_ref, dh_ref, x_ref, g_ref, dx_ref, dg_ref = refs

        @pl.when(pl.program_id(0) == 0)
        def _():
            dg_ref[...] = jnp.zeros_like(dg_ref)
            if has_prev:
                dg2_ref[...] = jnp.zeros_like(dg2_ref)

        dh_v = dh_ref[...]
        n, r = _rms(x_ref[...])
        dx = dxn_ref[...] + _rms_bwd(dh_v, g_ref[...], n, r)
        dx_ref[...] = dx
        dg_ref[...] += _colsum(dh_v * n)
        if has_prev:
            n2, r2 = _rms(y_ref[...])
            dy_ref[...] = _rms_bwd(dx, g2_ref[...], n2, r2).astype(BF16)
            dg2_ref[...] += _colsum(dx * n2)

    in_specs = [_rows(tt, d), _rows(tt, d), _rows(tt, d), _layer((1, d), l)]
    out_specs = [_rows(tt, d), _whole((1, d))]
    out_shape = [jax.ShapeDtypeStruct((t, d), F32), jax.ShapeDtypeStruct((1, d), F32)]
    args = [dxn, dh, x_in, gpre3]
    if has_prev:
        in_specs += [_rows(tt, d), _layer((1, d), l_prev)]
        out_specs += [_rows(tt, d), _whole((1, d))]
        out_shape += [jax.ShapeDtypeStruct((t, d), BF16), jax.ShapeDtypeStruct((1, d), F32)]
        args += [y_prev, gpost3]
    return pl.pallas_call(
        body, name=name, grid=(t // tt,), in_specs=in_specs, out_specs=out_specs, out_shape=out_shape,
        compiler_params=_params("arbitrary"),
    )(*args)


def _halves(tt):
    return [slice(0, tt // 2), slice(tt // 2, tt)] if tt % 32 == 0 else [slice(0, tt)]


def _matmul_resnorm(a, w, x, gpost3, l, gnext3, l2, name, tm=512):
    t, k = a.shape
    d = w.shape[1]
    tm = min(tm, t)

    def body(a_ref, w_ref, x_ref, gp_ref, gn_ref, y_ref, xo_ref, h_ref):
        wv, gp, gn = w_ref[...], gp_ref[...], gn_ref[...]
        for rows in _halves(tm):
            y = jnp.dot(a_ref[rows, :], wv, preferred_element_type=F32)
            y_ref[rows, :] = y
            n, _ = _rms(y)
            xn = x_ref[rows, :] + n * gp
            xo_ref[rows, :] = xn
            n2, _ = _rms(xn)
            h_ref[rows, :] = (n2 * gn).astype(BF16)

    return pl.pallas_call(
        body, name=name, grid=(t // tm,),
        in_specs=[_rows(tm, k), _whole((k, d)), _rows(tm, d), _layer((1, d), l), _layer((1, d), l2)],
        out_specs=[_rows(tm, d)] * 3,
        out_shape=[jax.ShapeDtypeStruct((t, d), F32), jax.ShapeDtypeStruct((t, d), F32),
                   jax.ShapeDtypeStruct((t, d), BF16)],
        compiler_params=_params("parallel"),
    )(a, w, x, gpost3, gnext3)


def _matmul_loss(a, w, x, gpost3, l, target, name, tm=512):
    t, k = a.shape
    d = w.shape[1]
    tm = min(tm, t)

    def body(a_ref, w_ref, x_ref, g_ref, t_ref, dxn_ref, dy_ref, dg_ref, loss_ref):
        @pl.when(pl.program_id(0) == 0)
        def _():
            dg_ref[...] = jnp.zeros_like(dg_ref)
            loss_ref[...] = jnp.zeros_like(loss_ref)

        wv, g = w_ref[...], g_ref[...]
        for rows in _halves(tm):
            n, r = _rms(jnp.dot(a_ref[rows, :], wv, preferred_element_type=F32))
            diff = x_ref[rows, :] + n * g - t_ref[rows, :]
            loss_ref[...] += 0.5 * jnp.sum(jnp.mean(diff * diff, axis=-1, keepdims=True))
            dxn = diff * (1.0 / d)
            dxn_ref[rows, :] = dxn
            dy_ref[rows, :] = _rms_bwd(dxn, g, n, r).astype(BF16)
            dg_ref[...] += _colsum(dxn * n)

    return pl.pallas_call(
        body, name=name, grid=(t // tm,),
        in_specs=[_rows(tm, k), _whole((k, d)), _rows(tm, d), _layer((1, d), l), _rows(tm, d)],
        out_specs=[_rows(tm, d), _rows(tm, d), _whole((1, d)), _whole((1, 128))],
        out_shape=[jax.ShapeDtypeStruct((t, d), F32), jax.ShapeDtypeStruct((t, d), BF16),
                   jax.ShapeDtypeStruct((1, d), F32), jax.ShapeDtypeStruct((1, 128), F32)],
        compiler_params=_params("arbitrary"),
    )(a, w, x, gpost3, target)


def _matmul_norm_bwd(a, w, mode, dxn, x_in, gpre3, l, name, y_prev=None, gpost3=None, l_prev=None, tm=512):
    t, k = a.shape
    d = x_in.shape[1]
    tm = min(tm, t)
    has_prev = y_prev is not None
    dims = _DIMS[mode]

    def body(*refs):
        if has_prev:
            a_ref, w_ref, dxn_ref, x_ref, g_ref, y_ref, g2_ref, dx_ref, dg_ref, dy_ref, dg2_ref = refs
        else:
            a_ref, w_ref, dxn_ref, x_ref, g_ref, dx_ref, dg_ref = refs

        @pl.when(pl.program_id(0) == 0)
        def _():
            dg_ref[...] = jnp.zeros_like(dg_ref)
            if has_prev:
                dg2_ref[...] = jnp.zeros_like(dg2_ref)

        wv, g = w_ref[...], g_ref[...]
        for rows in _halves(tm):
            dh = lax.dot_general(a_ref[rows, :], wv, dims, preferred_element_type=F32)
            n, r = _rms(x_ref[rows, :])
            dx = dxn_ref[rows, :] + _rms_bwd(dh, g, n, r)
            dx_ref[rows, :] = dx
            dg_ref[...] += _colsum(dh * n)
            if has_prev:
                n2, r2 = _rms(y_ref[rows, :])
                dy_ref[rows, :] = _rms_bwd(dx, g2_ref[...], n2, r2).astype(BF16)
                dg2_ref[...] += _colsum(dx * n2)

    in_specs = [_rows(tm, k), _whole(w.shape), _rows(tm, d), _rows(tm, d), _layer((1, d), l)]
    out_specs = [_rows(tm, d), _whole((1, d))]
    out_shape = [jax.ShapeDtypeStruct((t, d), F32), jax.ShapeDtypeStruct((1, d), F32)]
    args = [a, w, dxn, x_in, gpre3]
    if has_prev:
        in_specs += [_rows(tm, d), _layer((1, d), l_prev)]
        out_specs += [_rows(tm, d), _whole((1, d))]
        out_shape += [jax.ShapeDtypeStruct((t, d), BF16), jax.ShapeDtypeStruct((1, d), F32)]
        args += [y_prev, gpost3]
    return pl.pallas_call(
        body, name=name, grid=(t // tm,), in_specs=in_specs, out_specs=out_specs, out_shape=out_shape,
        compiler_params=_params("arbitrary"),
    )(*args)


def _to_steps(a):
    t = a.shape[0]
    return a.reshape(8, t // 8, -1).transpose(1, 0, 2).reshape(a.shape)


def _from_steps(a):
    t = a.shape[0]
    return a.reshape(t // 8, 8, -1).transpose(1, 0, 2).reshape(a.shape)


def _al(v):
    return v if isinstance(v, int) else pl.multiple_of(v, 8)


def _chunks(n_rows, fn, unroll=1):
    def step(r, carry):
        fn(pl.multiple_of(r * ROW_CHUNK, ROW_CHUNK))
        return carry
    lax.fori_loop(0, n_rows // ROW_CHUNK, step, 0, unroll=unroll)


def _fold8(a):
    return a.reshape(a.shape[0] // 8, 8, a.shape[1]).sum(axis=0)


def _shift_down(a):
    row = lax.broadcasted_iota(jnp.int32, a.shape, 0)
    return jnp.where(row % 8 == 0, 0.0, pltpu.roll(a, 1, 0))


def _shift_up(a):
    row = lax.broadcasted_iota(jnp.int32, a.shape, 0)
    return jnp.where(row % 8 == 7, 0.0, pltpu.roll(a, a.shape[0] - 1, 0))


def _prev_block(h, c, tt, t):
    return pl.BlockSpec((h, c), lambda i: (jnp.where(i == 0, t // h - 1, i * (tt // h) - 1), 0))


def _next_block(h, c, tt, t):
    return pl.BlockSpec((h, c), lambda i: (jnp.where(i == t // tt - 1, 0, (i + 1) * (tt // h)), 0))


def _taps(w_ref, buf, start, taps, rc, lanes=slice(None)):
    acc = w_ref[0:1, lanes] * buf[pl.ds(_al(start), rc), lanes]
    for k in range(1, taps):
        acc = acc + w_ref[k:k + 1, lanes] * buf[pl.ds(_al(start + 8 * k), rc), lanes]
    return acc


def _taps_rev(w_ref, buf, start, taps, rc, lanes=slice(None)):
    acc = w_ref[0:1, lanes] * buf[pl.ds(_al(start + 8 * (taps - 1)), rc), lanes]
    for k in range(1, taps):
        acc = acc + w_ref[k:k + 1, lanes] * buf[pl.ds(_al(start + 8 * (taps - 1 - k)), rc), lanes]
    return acc


def _mix_dims(d):
    dp = d // 4
    dc = 3 * d // 8
    ds = d - dp - dc
    oa, og = dp, dp + dc
    ob = dp + 2 * dc
    oc, ox = ob + ds, ob + 2 * ds
    return dp, dc, ds, oa, og, ob, oc, ox, ox + ds


def _pool_consts(dp):
    win = jnp.repeat(jnp.asarray(POOL_WINDOWS, F32), dp // len(POOL_WINDOWS))[None, :]
    mask = (jnp.arange(MAX_WINDOW, dtype=F32)[:, None] < win).astype(F32)
    return mask, win


def _pool_count(row0, rc, dp, seg, wl):
    r = lax.broadcasted_iota(jnp.int32, (rc, dp), 0) + row0
    return jnp.minimum(((r & 7) * seg + (r >> 3) + 1).astype(F32), wl)


def _mix_fwd(z, mbd, pscale3, wdw, bdw3, lng3, lnb3, wsc, l, name):
    t, din = z.shape
    dp, dc, ds, oa, og, ob, oc, ox, din2 = _mix_dims(din * 8 // 17)
    assert din2 == din
    d = ob
    tt = min(TILE_MIX, t)
    hp, hc, hs, rc = POOL_HALO, CONF_HALO, SHORT_HALO, ROW_CHUNK
    assert tt % hc == 0 and t % tt == 0
    seg = t // 8
    pmask, wlane = _pool_consts(dp)

    def body(z_ref, zpa_ref, zpb_ref, mbd_ref, ps_ref, pmask_ref, wl_ref, wdw_ref, bdw_ref, lng_ref, lnb_ref, wsc_ref,
             cat_ref, c_ref, pbuf, vbuf, sbuf):
        i = pl.program_id(0)
        pbuf[0:hp, :] = zpa_ref[hc - hp:hc, 0:dp]

        def prev(r0):
            rows = pl.ds(r0, rc)
            vbuf[rows, :] = zpa_ref[rows, oa:oa + dc] * _sig(zpa_ref[rows, og:og + dc])
        _chunks(hc, prev)
        sbuf[0:hs, :] = zpb_ref[:, oc:oc + ds] * zpb_ref[:, ox:ox + ds]

        @pl.when(i == 0)
        def _():
            pbuf[0:hp, :] = _shift_down(pbuf[0:hp, :])
            vbuf[0:hc, :] = _shift_down(vbuf[0:hc, :])
            sbuf[0:hs, :] = _shift_down(sbuf[0:hs, :])

        mbd_v, ps, wl = mbd_ref[...], ps_ref[...], wl_ref[...]
        bdw, lng, lnb = bdw_ref[...], lng_ref[...], lnb_ref[...]

        def step(r0):
            rows = pl.ds(r0, rc)
            zp = z_ref[rows, 0:dp]
            pbuf[pl.ds(_al(hp + r0), rc), :] = zp
            vbuf[pl.ds(_al(hc + r0), rc), :] = z_ref[rows, oa:oa + dc] * _sig(z_ref[rows, og:og + dc])
            sbuf[pl.ds(_al(hs + r0), rc), :] = z_ref[rows, oc:oc + ds] * z_ref[rows, ox:ox + ds]
            pooled = _taps_rev(pmask_ref, pbuf, r0 + hp - 8 * (MAX_WINDOW - 1), MAX_WINDOW, rc)
            pooled = pooled / _pool_count(i * tt + r0, rc, dp, seg, wl) - zp
            pm = jnp.dot(pooled.astype(BF16), mbd_v, preferred_element_type=F32)
            cat_ref[rows, 0:dp] = (pm * ps).astype(BF16)
            c = _taps(wdw_ref, vbuf, r0 + hc - 8 * (CONF_TAPS - 1), CONF_TAPS, rc) + bdw
            c_ref[rows, :] = c
            xc = c - jnp.mean(c, axis=-1, keepdims=True)
            nrm = xc * lax.rsqrt(jnp.mean(xc * xc, axis=-1, keepdims=True) + EPS)
            yln = nrm * lng + lnb
            cat_ref[rows, dp:dp + dc] = (yln * _sig(yln)).astype(BF16)
            cv = _taps(wsc_ref, sbuf, r0 + hs - 8 * (SHORT_TAPS - 1), SHORT_TAPS, rc)
            cat_ref[rows, dp + dc:d] = (z_ref[rows, ob:ob + ds] * cv).astype(BF16)
        _chunks(tt, step, unroll=2)

    return pl.pallas_call(
        body, name=name, grid=(t // tt,),
        in_specs=[_rows(tt, din), _prev_block(hc, d, tt, t), _prev_block(hs, din, tt, t),
                  _layer((dp, dp), l), _layer((1, dp), l), _whole((MAX_WINDOW, dp)), _whole((1, dp)),
                  _layer((CONF_TAPS, dc), l), _layer((1, dc), l), _layer((1, dc), l), _layer((1, dc), l),
                  _layer((SHORT_TAPS, ds), l)],
        out_specs=[_rows(tt, d), _rows(tt, dc)],
        out_shape=[jax.ShapeDtypeStruct((t, d), BF16), jax.ShapeDtypeStruct((t, dc), F32)],
        scratch_shapes=[pltpu.VMEM((hp + tt, dp), F32), pltpu.VMEM((hc + tt, dc), F32), pltpu.VMEM((hs + tt, ds), F32)],
        compiler_params=_params("parallel"),
    )(z, z, z, mbd, pscale3, pmask, wlane, wdw, bdw3, lng3, lnb3, wsc)


def _mix_bwd(dcat, z, c, mbd, pscale3, wdw, lng3, lnb3, wsc, l, name):
    t, din = z.shape
    dp, dc, ds, oa, og, ob, oc, ox, _ = _mix_dims(din * 8 // 17)
    d = ob
    tt = min(TILE_MIX, t)
    nt = t // tt
    hp, hc, hs, rc = POOL_HALO, CONF_HALO, SHORT_HALO, ROW_CHUNK
    assert tt % hc == 0 and t % tt == 0 and tt >= 8 * MAX_WINDOW
    seg = t // 8
    pmask, wlane = _pool_consts(dp)

    def body(dcat_ref, dcn_ref, z_ref, zpa_ref, zpb_ref, znb_ref, c_ref, cn_ref, mbd_ref, ps_ref, pmask_ref, wl_ref,
             wdw_ref, lng_ref, lnb_ref, wsc_ref,
             dz_ref, dmbd_ref, dps_ref, dwdw_ref, dbdw_ref, dlng_ref, dlnb_ref, dwsc_ref,
             pbuf, qbuf, dpbuf, pbf, vbuf, dcbuf, sbuf, dsbuf, dw8, ds8, ln8, ps8):
        i = pl.program_id(0)
        first, last = i == 0, i == nt - 1

        @pl.when(first)
        def _():
            for ref in (dmbd_ref, dw8, ds8, ln8, ps8):
                ref[...] = jnp.zeros_like(ref)

        mbd_v, ps, wl = mbd_ref[...], ps_ref[...], wl_ref[...]
        lng, lnb = lng_ref[...], lnb_ref[...]

        def ln_silu_bwd(cc, dyb):
            xc = cc - jnp.mean(cc, axis=-1, keepdims=True)
            rstd = lax.rsqrt(jnp.mean(xc * xc, axis=-1, keepdims=True) + EPS)
            nrm = xc * rstd
            yln = nrm * lng + lnb
            s = _sig(yln)
            dyln = dyb * (s * (1.0 + yln * (1.0 - s)))
            dn = dyln * lng
            dcc = rstd * (dn - jnp.mean(dn, axis=-1, keepdims=True) - nrm * jnp.mean(dn * nrm, axis=-1, keepdims=True))
            return dcc, dyln, nrm

        pbuf[0:hp, :] = zpa_ref[hc - hp:hc, 0:dp]

        def prev(r0):
            rows = pl.ds(r0, rc)
            vbuf[rows, :] = zpa_ref[rows, oa:oa + dc] * _sig(zpa_ref[rows, og:og + dc])
        _chunks(hc, prev)
        sbuf[0:hs, :] = zpb_ref[:, oc:oc + ds] * zpb_ref[:, ox:ox + ds]

        @pl.when(first)
        def _():
            pbuf[0:hp, :] = _shift_down(pbuf[0:hp, :])
            vbuf[0:hc, :] = _shift_down(vbuf[0:hc, :])
            sbuf[0:hs, :] = _shift_down(sbuf[0:hs, :])

        def nxt(r0):
            rows = pl.ds(r0, rc)
            dcc, _, _ = ln_silu_bwd(cn_ref[rows, :], dcn_ref[rows, dp:dp + dc])
            dcbuf[pl.ds(_al(tt + r0), rc), :] = dcc
        _chunks(hc, nxt, unroll=4)
        dpm_n = (dcn_ref[0:hp, 0:dp] * ps).astype(BF16)
        qbuf[tt:tt + hp, :] = lax.dot_general(dpm_n, mbd_v, _DIMS["nt"], preferred_element_type=F32) / wl
        dsbuf[tt:tt + hs, :] = dcn_ref[0:hs, dp + dc:d] * znb_ref[:, ob:ob + ds]

        @pl.when(last)
        def _():
            dcbuf[tt:tt + hc, :] = _shift_up(dcbuf[tt:tt + hc, :])
            qbuf[tt:tt + hp, :] = _shift_up(qbuf[tt:tt + hp, :])
            dsbuf[tt:tt + hs, :] = _shift_up(dsbuf[tt:tt + hs, :])

        def fill(r0):
            rows = pl.ds(r0, rc)
            zp = z_ref[rows, 0:dp]
            pbuf[pl.ds(_al(hp + r0), rc), :] = zp
            vbuf[pl.ds(_al(hc + r0), rc), :] = z_ref[rows, oa:oa + dc] * _sig(z_ref[rows, og:og + dc])
            sbuf[pl.ds(_al(hs + r0), rc), :] = z_ref[rows, oc:oc + ds] * z_ref[rows, ox:ox + ds]
            pooled = _taps_rev(pmask_ref, pbuf, r0 + hp - 8 * (MAX_WINDOW - 1), MAX_WINDOW, rc)
            pbf[rows, :] = (pooled / _pool_count(i * tt + r0, rc, dp, seg, wl) - zp).astype(BF16)
            dcc, dyln, nrm = ln_silu_bwd(c_ref[rows, :], dcat_ref[rows, dp:dp + dc])
            dcbuf[rows, :] = dcc
            ln8[0] += _fold8(dyln * nrm)
            ln8[1] += _fold8(dyln)
            ln8[2] += _fold8(dcc)
            dsbuf[rows, :] = dcat_ref[rows, dp + dc:d] * z_ref[rows, ob:ob + ds]
        _chunks(tt, fill, unroll=4)

        pb = pbf[...]
        dya = dcat_ref[:, 0:dp]
        ps8[...] += _fold8(dya * jnp.dot(pb, mbd_v, preferred_element_type=F32))
        dpm = (dya * ps).astype(BF16)
        dmbd_ref[...] += lax.dot_general(pb, dpm, _DIMS["tn"], preferred_element_type=F32)
        dpbuf[...] = lax.dot_general(dpm, mbd_v, _DIMS["nt"], preferred_element_type=F32)

        def quot(r0):
            rows = pl.ds(r0, rc)
            qbuf[rows, :] = dpbuf[rows, :] / _pool_count(i * tt + r0, rc, dp, seg, wl)
        _chunks(tt, quot)

        def back(r0):
            rows = pl.ds(r0, rc)
            dzp = _taps(pmask_ref, qbuf, r0, MAX_WINDOW, rc) - dpbuf[rows, :]
            dz_ref[rows, 0:dp] = dzp.astype(BF16)
            dcc = dcbuf[rows, :]
            for k in range(CONF_TAPS):
                dw8[k] += _fold8(dcc * vbuf[pl.ds(_al(r0 + hc - 8 * (CONF_TAPS - 1 - k)), rc), :])
            dv = _taps_rev(wdw_ref, dcbuf, r0, CONF_TAPS, rc)
            za = z_ref[rows, oa:oa + dc]
            sg = _sig(z_ref[rows, og:og + dc])
            dz_ref[rows, oa:oa + dc] = (dv * sg).astype(BF16)
            dz_ref[rows, og:og + dc] = (dv * za * sg * (1.0 - sg)).astype(BF16)
            cv = _taps(wsc_ref, sbuf, r0 + hs - 8 * (SHORT_TAPS - 1), SHORT_TAPS, rc)
            dz_ref[rows, ob:ob + ds] = (dcat_ref[rows, dp + dc:d] * cv).astype(BF16)
            dcv = dsbuf[rows, :]
            for k in range(SHORT_TAPS):
                ds8[k] += _fold8(dcv * sbuf[pl.ds(_al(r0 + hs - 8 * (SHORT_TAPS - 1 - k)), rc), :])
            dpv = _taps_rev(wsc_ref, dsbuf, r0, SHORT_TAPS, rc)
            dz_ref[rows, oc:oc + ds] = (dpv * z_ref[rows, ox:ox + ds]).astype(BF16)
            dz_ref[rows, ox:ox + ds] = (dpv * z_ref[rows, oc:oc + ds]).astype(BF16)
        _chunks(tt, back, unroll=2)

        @pl.when(last)
        def _():
            dps_ref[...] = jnp.sum(ps8[...], axis=0, keepdims=True)
            dwdw_ref[...] = jnp.sum(dw8[...], axis=1)
            dwsc_ref[...] = jnp.sum(ds8[...], axis=1)
            dlng_ref[...] = jnp.sum(ln8[0], axis=0, keepdims=True)
            dlnb_ref[...] = jnp.sum(ln8[1], axis=0, keepdims=True)
            dbdw_ref[...] = jnp.sum(ln8[2], axis=0, keepdims=True)

    return pl.pallas_call(
        body, name=name, grid=(nt,),
        in_specs=[_rows(tt, d), _next_block(hc, d, tt, t),
                  _rows(tt, din), _prev_block(hc, d, tt, t), _prev_block(hs, din, tt, t), _next_block(hs, din, tt, t),
                  _rows(tt, dc), _next_block(hc, dc, tt, t),
                  _layer((dp, dp), l), _layer((1, dp), l), _whole((MAX_WINDOW, dp)), _whole((1, dp)),
                  _layer((CONF_TAPS, dc), l), _layer((1, dc), l), _layer((1, dc), l), _layer((SHORT_TAPS, ds), l)],
        out_specs=[_rows(tt, din), _whole((dp, dp)), _whole((1, dp)), _whole((CONF_TAPS, dc)), _whole((1, dc)),
                   _whole((1, dc)), _whole((1, dc)), _whole((SHORT_TAPS, ds))],
        out_shape=[jax.ShapeDtypeStruct((t, din), BF16), jax.ShapeDtypeStruct((dp, dp), F32),
                   jax.ShapeDtypeStruct((1, dp), F32), jax.ShapeDtypeStruct((CONF_TAPS, dc), F32),
                   jax.ShapeDtypeStruct((1, dc), F32), jax.ShapeDtypeStruct((1, dc), F32),
                   jax.ShapeDtypeStruct((1, dc), F32), jax.ShapeDtypeStruct((SHORT_TAPS, ds), F32)],
        scratch_shapes=[pltpu.VMEM((hp + tt, dp), F32), pltpu.VMEM((tt + hp, dp), F32), pltpu.VMEM((tt, dp), F32),
                        pltpu.VMEM((tt, dp), BF16), pltpu.VMEM((hc + tt, dc), F32), pltpu.VMEM((tt + hc, dc), F32),
                        pltpu.VMEM((hs + tt, ds), F32), pltpu.VMEM((tt + hs, ds), F32),
                        pltpu.VMEM((CONF_TAPS, 8, dc), F32), pltpu.VMEM((SHORT_TAPS, 8, ds), F32),
                        pltpu.VMEM((3, 8, dc), F32), pltpu.VMEM((8, dp), F32)],
        compiler_params=_params("arbitrary"),
    )(dcat, dcat, z, z, z, z, c, c, mbd, pscale3, pmask, wlane, wdw, lng3, lnb3, wsc)


def _softmax_rows(qh, kh, scale):
    s = lax.dot_general(qh, kh, _DIMS["nt"], preferred_element_type=F32) * scale
    e = jnp.exp(s - jnp.max(s, axis=-1, keepdims=True))
    return e / jnp.sum(e, axis=-1, keepdims=True)


def _attn_fwd(q, k, v, name):
    t, d = q.shape
    m = k.shape[0]
    hd = d // HEADS
    scale = hd ** -0.5
    tt = min(TILE_ATTN, t)

    def body(q_ref, k_ref, v_ref, o_ref):
        for h in range(HEADS):
            sl = slice(h * hd, (h + 1) * hd)
            p = _softmax_rows(q_ref[:, sl], k_ref[:, sl], scale)
            o_ref[:, sl] = jnp.dot(p.astype(BF16), v_ref[:, sl], preferred_element_type=F32).astype(BF16)

    return pl.pallas_call(
        body, name=name, grid=(t // tt,),
        in_specs=[_rows(tt, d), _whole((m, d)), _whole((m, d))], out_specs=_rows(tt, d),
        out_shape=jax.ShapeDtypeStruct((t, d), BF16), compiler_params=_params("parallel"),
    )(q, k, v)


def _attn_bwd(q, k, v, do, name):
    t, d = q.shape
    m = k.shape[0]
    hd = d // HEADS
    scale = hd ** -0.5
    tt = min(TILE_ATTN, t)

    def body(q_ref, k_ref, v_ref, do_ref, dq_ref, dk_ref, dv_ref):
        @pl.when(pl.program_id(0) == 0)
        def _():
            dk_ref[...] = jnp.zeros_like(dk_ref)
            dv_ref[...] = jnp.zeros_like(dv_ref)

        for h in range(HEADS):
            sl = slice(h * hd, (h + 1) * hd)
            qh, kh, vh, doh = q_ref[:, sl], k_ref[:, sl], v_ref[:, sl], do_ref[:, sl]
            p = _softmax_rows(qh, kh, scale)
            dv_ref[:, sl] += lax.dot_general(p.astype(BF16), doh, _DIMS["tn"], preferred_element_type=F32)
            dp = lax.dot_general(doh, vh, _DIMS["nt"], preferred_element_type=F32)
            ds = (p * (dp - jnp.sum(dp * p, axis=-1, keepdims=True)) * scale).astype(BF16)
            dq_ref[:, sl] = jnp.dot(ds, kh, preferred_element_type=F32).astype(BF16)
            dk_ref[:, sl] += lax.dot_general(ds, qh, _DIMS["tn"], preferred_element_type=F32)

    return pl.pallas_call(
        body, name=name, grid=(t // tt,),
        in_specs=[_rows(tt, d), _whole((m, d)), _whole((m, d)), _rows(tt, d)],
        out_specs=[_rows(tt, d), _whole((m, d)), _whole((m, d))],
        out_shape=[jax.ShapeDtypeStruct((t, d), BF16), jax.ShapeDtypeStruct((m, d), F32),
                   jax.ShapeDtypeStruct((m, d), F32)],
        compiler_params=_params("arbitrary"),
    )(q, k, v, do)


def _lane_chunks(f):
    w = 256 if f % 256 == 0 else 128 if f % 128 == 0 else f
    return [(c0, w) for c0 in range(0, f, w)]


def _ffn_act_fwd(u, wc, l, name):
    t, f2 = u.shape
    f = f2 // 2
    tt = min(2 * TILE_FFN, t)
    hs, rc = SHORT_HALO, ROW_CHUNK
    lanes = _lane_chunks(f)

    def body(u_ref, up_ref, wc_ref, a_ref, ubuf):
        ubuf[0:hs, :] = up_ref[...]

        @pl.when(pl.program_id(0) == 0)
        def _():
            ubuf[0:hs, :] = _shift_down(ubuf[0:hs, :])

        def step(r0):
            rows = pl.ds(r0, rc)
            ubuf[pl.ds(_al(hs + r0), rc), :] = u_ref[rows, :]
            start = r0 + hs - 8 * (SHORT_TAPS - 1)
            for c0, cw in lanes:
                g = _taps(wc_ref, ubuf, start, SHORT_TAPS, rc, slice(c0, c0 + cw))
                vv = _taps(wc_ref, ubuf, start, SHORT_TAPS, rc, slice(f + c0, f + c0 + cw))
                a_ref[rows, c0:c0 + cw] = (g * _sig(g) * vv).astype(BF16)
        _chunks(tt, step, unroll=2)

    return pl.pallas_call(
        body, name=name, grid=(t // tt,),
        in_specs=[_rows(tt, f2), _prev_block(hs, f2, tt, t), _layer((SHORT_TAPS, f2), l)],
        out_specs=_rows(tt, f), out_shape=jax.ShapeDtypeStruct((t, f), BF16),
        scratch_shapes=[pltpu.VMEM((hs + tt, f2), F32)], compiler_params=_params("parallel"),
    )(u, u, wc)


def _ffn_act_bwd(u, da, wc, l, name):
    t, f2 = u.shape
    f = f2 // 2
    tt = min(TILE_FFN, t)
    nt = t // tt
    hs, rc = SHORT_HALO, ROW_CHUNK
    lanes = _lane_chunks(f)

    def body(u_ref, up_ref, un_ref, da_ref, dan_ref, wc_ref, du_ref, dwc_ref, ubuf, danbuf, dbuf, dw8):
        i = pl.program_id(0)
        first, last = i == 0, i == nt - 1
        ubuf[0:hs, :] = up_ref[...]
        ubuf[hs + tt:hs + tt + hs, :] = un_ref[...]
        danbuf[...] = dan_ref[...]

        @pl.when(first)
        def _():
            dw8[...] = jnp.zeros_like(dw8)
            ubuf[0:hs, :] = _shift_down(ubuf[0:hs, :])

        @pl.when(last)
        def _():
            ubuf[hs + tt:hs + tt + hs, :] = _shift_up(ubuf[hs + tt:hs + tt + hs, :])
            danbuf[...] = _shift_up(danbuf[...])

        def fill(r0):
            ubuf[pl.ds(_al(hs + r0), rc), :] = u_ref[pl.ds(r0, rc), :]
        _chunks(tt, fill)

        def conv_grads(r0, n, da_rows):
            start = r0 + hs - 8 * (SHORT_TAPS - 1)
            for c0, cw in lanes:
                sl_g, sl_v = slice(c0, c0 + cw), slice(f + c0, f + c0 + cw)
                g = _taps(wc_ref, ubuf, start, SHORT_TAPS, n, sl_g)
                vv = _taps(wc_ref, ubuf, start, SHORT_TAPS, n, sl_v)
                dav = da_rows(c0, cw)
                sg = _sig(g)
                dbuf[pl.ds(_al(r0), n), sl_g] = dav * vv * (sg * (1.0 + g * (1.0 - sg)))
                dbuf[pl.ds(_al(r0), n), sl_v] = dav * (g * sg)

        _chunks(tt, lambda r0: conv_grads(r0, rc, lambda c0, cw: da_ref[pl.ds(r0, rc), c0:c0 + cw]), unroll=2)
        conv_grads(tt, hs, lambda c0, cw: danbuf[:, c0:c0 + cw])

        def back(r0):
            rows = pl.ds(r0, rc)
            for c0, cw in lanes:
                for off in (c0, f + c0):
                    sl = slice(off, off + cw)
                    du_ref[rows, sl] = _taps_rev(wc_ref, dbuf, r0, SHORT_TAPS, rc, sl).astype(BF16)
                    dd = dbuf[rows, sl]
                    for k in range(SHORT_TAPS):
                        dw8[k, :, sl] += _fold8(dd * ubuf[pl.ds(_al(r0 + hs - 8 * (SHORT_TAPS - 1 - k)), rc), sl])
        _chunks(tt, back, unroll=2)

        @pl.when(last)
        def _():
            dwc_ref[...] = jnp.sum(dw8[...], axis=1)

    return pl.pallas_call(
        body, name=name, grid=(nt,),
        in_specs=[_rows(tt, f2), _prev_block(hs, f2, tt, t), _next_block(hs, f2, tt, t),
                  _rows(tt, f), _next_block(hs, f, tt, t), _layer((SHORT_TAPS, f2), l)],
        out_specs=[_rows(tt, f2), _whole((SHORT_TAPS, f2))],
        out_shape=[jax.ShapeDtypeStruct((t, f2), BF16), jax.ShapeDtypeStruct((SHORT_TAPS, f2), F32)],
        scratch_shapes=[pltpu.VMEM((hs + tt + hs, f2), F32), pltpu.VMEM((hs, f), F32), pltpu.VMEM((tt + hs, f2), F32),
                        pltpu.VMEM((SHORT_TAPS, 8, f2), F32)],
        compiler_params=_params("arbitrary"),
    )(u, u, u, da, da, wc)


def _place():
    return lax.axis_index("x"), lax.axis_index("y"), lax.axis_index("c")


def _flip(v, bit):
    return 1 - v if bit else v


def _peers(x, y, c):
    out = []
    for kk in range(1, N_DEV):
        px, py, pc = _flip(x, kk & 4), _flip(y, kk & 2), _flip(c, kk & 1)
        out.append((kk - 1, (px, py, pc), 4 * px + 2 * py + pc))
    return out


def _allgather(shards, name):
    nt = len(shards)

    def body(*refs):
        srcs, outs = refs[:nt], refs[nt:2 * nt]
        send_sems, recv_sems, local_sems = refs[2 * nt:]
        x, y, c = _place()
        me, sibling = (x, y, c), (x, y, 1 - c)
        chips = [(1 - x, y), (x, 1 - y), (1 - x, 1 - y)]

        def rows(ti, px, py, pc):
            r = srcs[ti].shape[1]
            return outs[ti].at[:, pl.ds((4 * px + 2 * py + pc) * r, r), :]

        def copy(ti, kk, block, to, src=None):
            return pltpu.make_async_remote_copy(
                src_ref=rows(ti, *block) if src is None else src, dst_ref=rows(ti, *block),
                send_sem=send_sems.at[ti, kk], recv_sem=recv_sems.at[ti, kk], device_id=to, device_id_type=MESH)

        mine = [pltpu.make_async_copy(srcs[ti], rows(ti, *me), local_sems.at[ti]) for ti in range(nt)]
        for cp in mine:
            cp.start()
        first = []
        for ti in range(nt):
            first.append(copy(ti, 0, me, sibling, src=srcs[ti]))
            first += [copy(ti, 1 + j, me, (*chip, c), src=srcs[ti]) for j, chip in enumerate(chips)]
        for cp in first:
            cp.start()
        passed = []
        for j, chip in enumerate(chips):
            for ti in range(nt):
                copy(ti, 1 + j, (*chip, c), me).wait_recv()
                fwd = copy(ti, 4 + j, (*chip, c), sibling)
                fwd.start()
                passed.append(fwd)
        for ti in range(nt):
            copy(ti, 0, sibling, me).wait_recv()
            for j, chip in enumerate(chips):
                copy(ti, 4 + j, (*chip, 1 - c), me).wait_recv()
        for cp in first + passed:
            cp.wait_send()
        for cp in mine:
            cp.wait()

    return pl.pallas_call(
        body, name=name,
        in_specs=[ANY] * nt, out_specs=[ANY] * nt,
        out_shape=[jax.ShapeDtypeStruct((s.shape[0], N_DEV * s.shape[1], s.shape[2]), s.dtype) for s in shards],
        scratch_shapes=[pltpu.SemaphoreType.DMA((nt, 7)), pltpu.SemaphoreType.DMA((nt, 7)),
                        pltpu.SemaphoreType.DMA((nt,))],
    )(*shards)


def _gather_piece(src, land, me, to):
    r = src.shape[0]
    return src, land.at[pl.ds(me * r, r), :]


def _scatter_piece(l):
    def piece(src, land, me, to):
        r = src.shape[0] // N_DEV
        return src.at[pl.ds(to * r, r), :], land.at[l, me]
    return piece


def _split_start(srcs, lands, piece, after, name):
    nt = len(srcs)

    def body(*refs):
        src_refs, land_refs = refs[:nt], refs[nt:2 * nt]
        send_sems, recv_sems, local_sems, token = refs[2 * nt + 1], refs[2 * nt + 2], refs[2 * nt + 3], refs[4 * nt + 4]
        x, y, c = _place()
        me = 4 * x + 2 * y + c
        for ti in range(nt):
            for slot, peer, flat in _peers(x, y, c):
                src, dst = piece(src_refs[ti], land_refs[ti], me, flat)
                pltpu.make_async_remote_copy(
                    src_ref=src, dst_ref=dst, send_sem=send_sems.at[7 * ti + slot], recv_sem=recv_sems.at[7 * ti + slot],
                    device_id=peer, device_id_type=MESH).start()
        for ti in range(nt):
            pltpu.make_async_copy(*piece(src_refs[ti], land_refs[ti], me, me), local_sems.at[ti]).start()
        token[...] = jnp.zeros_like(token)

    both = list(srcs) + list(lands)
    return pl.pallas_call(
        body, name=name,
        in_specs=[HBM] * (2 * nt) + [ANY],
        out_specs=[SEM, SEM, SEM] + [HBM] * (2 * nt) + [pl.BlockSpec(memory_space=pltpu.VMEM)],
        out_shape=[pltpu.SemaphoreType.DMA((7 * nt,)), pltpu.SemaphoreType.DMA((7 * nt,)), pltpu.SemaphoreType.DMA((nt,))]
        + [pltpu.HBM(a.shape, a.dtype) for a in both] + [jax.ShapeDtypeStruct((8, 128), F32)],
        input_output_aliases={i: i + 3 for i in range(2 * nt)},
        compiler_params=pltpu.CompilerParams(has_side_effects=EFFECT),
    )(*[pltpu.with_memory_space_constraint(a, pltpu.HBM) for a in both], after)


def _split_wait(started, after, piece, name):
    send_sems, recv_sems, local_sems, *both = started[:-1]
    nt = len(both) // 2

    def body(*refs):
        src_refs, land_refs = refs[:nt], refs[nt:2 * nt]
        send_ref, recv_ref, local_ref = refs[2 * nt], refs[2 * nt + 1], refs[2 * nt + 2]
        x, y, c = _place()
        me = 4 * x + 2 * y + c
        for ti in range(nt):
            src, dst = piece(src_refs[ti], land_refs[ti], me, me)
            for slot in range(N_DEV - 1):
                cp = pltpu.make_async_remote_copy(
                    src_ref=src, dst_ref=dst, send_sem=send_ref.at[7 * ti + slot], recv_sem=recv_ref.at[7 * ti + slot],
                    device_id=(x, y, c), device_id_type=MESH)
                cp.wait_send()
                cp.wait_recv()
            pltpu.make_async_copy(src, dst, local_ref.at[ti]).wait()

    outs = pl.pallas_call(
        body, name=name,
        in_specs=[HBM] * (2 * nt) + [SEM, SEM, SEM, ANY], out_specs=[HBM] * (2 * nt),
        out_shape=[pltpu.HBM(a.shape, a.dtype) for a in both],
        input_output_aliases={i: i for i in range(2 * nt)},
        compiler_params=pltpu.CompilerParams(has_side_effects=EFFECT),
    )(*both, send_sems, recv_sems, local_sems, after)
    return outs[nt:]


def _adam(w, g, m, v):
    m2 = ADAM_B1 * m + (1.0 - ADAM_B1) * g
    v2 = ADAM_B2 * v + (1.0 - ADAM_B2) * (g * g)
    m_hat = m2 / (1.0 - ADAM_B1 ** ADAM_STEP)
    v_hat = v2 / (1.0 - ADAM_B2 ** ADAM_STEP)
    return -ADAM_LR * (m_hat / (jnp.sqrt(v_hat) + ADAM_EPS) + ADAM_WD * w), m2, v2


def _adam_sharded(recv, recv_first, w, m, v, lo, hi, name, prev=None, after=None):
    nl, r, c = w.shape
    tr = max([rows for rows in range(16, min(r, TILE_ADAM) + 1, 16) if r % rows == 0] or [r])

    def body(recv_ref, w_ref, m_ref, v_ref, *rest):
        g_ref, d_ref, m2_ref, v2_ref = rest[-4:]
        g = recv_ref[0].astype(F32)
        for s in range(1, N_DEV):
            g = g + recv_ref[s].astype(F32)
        g_ref[...] = g
        d_ref[...], m2_ref[...], v2_ref[...] = _adam(w_ref[...], g, m_ref[...], v_ref[...])

    blk = pl.BlockSpec((None, tr, c), lambda li, i: (li + lo, i, 0))
    extra = [] if prev is None else list(prev)
    n_prev = len(extra)
    if after is not None:
        extra.append(after)
    return pl.pallas_call(
        body, name=name, grid=(hi - lo, r // tr),
        in_specs=[pl.BlockSpec((None, N_DEV, tr, c), lambda li, i: (li + lo - recv_first, 0, i, 0)), blk, blk, blk]
        + [ANY] * len(extra),
        out_specs=[blk] * 4, out_shape=[jax.ShapeDtypeStruct((nl, r, c), F32)] * 4,
        input_output_aliases={4 + i: i for i in range(n_prev)},
        compiler_params=_params("parallel", "parallel"),
    )(recv, w, m, v, *extra)


def _sum_sources(parts, name):
    _, r, c = parts.shape

    def body(p_ref, o_ref):
        g = p_ref[0]
        for s in range(1, N_DEV):
            g = g + p_ref[s]
        o_ref[...] = g

    return pl.pallas_call(
        body, name=name, grid=(1,), in_specs=[_whole((N_DEV, r, c))], out_specs=_whole((r, c)),
        out_shape=jax.ShapeDtypeStruct((r, c), F32), compiler_params=_params("arbitrary"),
    )(parts)


def _adam_flat(w, g, m, v, name):
    r, c = w.shape

    def body(w_ref, g_ref, m_ref, v_ref, d_ref, m2_ref, v2_ref):
        d_ref[...], m2_ref[...], v2_ref[...] = _adam(w_ref[...], g_ref[...], m_ref[...], v_ref[...])

    return pl.pallas_call(
        body, name=name, grid=(1,), in_specs=[_whole((r, c))] * 4, out_specs=[_whole((r, c))] * 3,
        out_shape=[jax.ShapeDtypeStruct((r, c), F32)] * 3, compiler_params=_params("arbitrary"),
    )(w, g, m, v)


def _pack(arrays):
    flat = jnp.concatenate([a.reshape(-1).astype(F32) for a in arrays])
    rows = -(-flat.shape[0] // 1024) * 8
    return jnp.pad(flat, (0, rows * 128 - flat.shape[0])).reshape(rows, 128)


def _unpack(slab, like):
    flat = slab.reshape(-1)
    out, at = [], 0
    for a in like:
        out.append(flat[at:at + a.size].reshape(a.shape))
        at += a.size
    return out


def kernel(x, mem, mem_norm, mix_pre_norm, mix_post_norm, w_in, pool_maps, pool_scale, conf_dw_w, conf_dw_b, conf_ln_g, conf_ln_b, sconv_w, w_out, xattn_pre_norm, xattn_post_norm, xattn_wq, xattn_wk, xattn_wv, xattn_wo, ffn_pre_norm, ffn_post_norm, ffn_w_up, ffn_conv_w, ffn_w_down, loss_target, m_mem_norm, m_mix_pre_norm, m_mix_post_norm, m_w_in, m_pool_maps, m_pool_scale, m_conf_dw_w, m_conf_dw_b, m_conf_ln_g, m_conf_ln_b, m_sconv_w, m_w_out, m_xattn_pre_norm, m_xattn_post_norm, m_xattn_wq, m_xattn_wk, m_xattn_wv, m_xattn_wo, m_ffn_pre_norm, m_ffn_post_norm, m_ffn_w_up, m_ffn_conv_w, m_ffn_w_down, v_mem_norm, v_mix_pre_norm, v_mix_post_norm, v_w_in, v_pool_maps, v_pool_scale, v_conf_dw_w, v_conf_dw_b, v_conf_ln_g, v_conf_ln_b, v_sconv_w, v_w_out, v_xattn_pre_norm, v_xattn_post_norm, v_xattn_wq, v_xattn_wk, v_xattn_wv, v_xattn_wo, v_ffn_pre_norm, v_ffn_post_norm, v_ffn_w_up, v_ffn_conv_w, v_ffn_w_down):
    weights = dict(mem_norm=mem_norm, mix_pre_norm=mix_pre_norm, mix_post_norm=mix_post_norm, w_in=w_in, pool_maps=pool_maps, pool_scale=pool_scale, conf_dw_w=conf_dw_w, conf_dw_b=conf_dw_b, conf_ln_g=conf_ln_g, conf_ln_b=conf_ln_b, sconv_w=sconv_w, w_out=w_out, xattn_pre_norm=xattn_pre_norm, xattn_post_norm=xattn_post_norm, xattn_wq=xattn_wq, xattn_wk=xattn_wk, xattn_wv=xattn_wv, xattn_wo=xattn_wo, ffn_pre_norm=ffn_pre_norm, ffn_post_norm=ffn_post_norm, ffn_w_up=ffn_w_up, ffn_conv_w=ffn_conv_w, ffn_w_down=ffn_w_down)
    mom1 = dict(mem_norm=m_mem_norm, mix_pre_norm=m_mix_pre_norm, mix_post_norm=m_mix_post_norm, w_in=m_w_in, pool_maps=m_pool_maps, pool_scale=m_pool_scale, conf_dw_w=m_conf_dw_w, conf_dw_b=m_conf_dw_b, conf_ln_g=m_conf_ln_g, conf_ln_b=m_conf_ln_b, sconv_w=m_sconv_w, w_out=m_w_out, xattn_pre_norm=m_xattn_pre_norm, xattn_post_norm=m_xattn_post_norm, xattn_wq=m_xattn_wq, xattn_wk=m_xattn_wk, xattn_wv=m_xattn_wv, xattn_wo=m_xattn_wo, ffn_pre_norm=m_ffn_pre_norm, ffn_post_norm=m_ffn_post_norm, ffn_w_up=m_ffn_w_up, ffn_conv_w=m_ffn_conv_w, ffn_w_down=m_ffn_w_down)
    mom2 = dict(mem_norm=v_mem_norm, mix_pre_norm=v_mix_pre_norm, mix_post_norm=v_mix_post_norm, w_in=v_w_in, pool_maps=v_pool_maps, pool_scale=v_pool_scale, conf_dw_w=v_conf_dw_w, conf_dw_b=v_conf_dw_b, conf_ln_g=v_conf_ln_g, conf_ln_b=v_conf_ln_b, sconv_w=v_sconv_w, w_out=v_w_out, xattn_pre_norm=v_xattn_pre_norm, xattn_post_norm=v_xattn_post_norm, xattn_wq=v_xattn_wq, xattn_wk=v_xattn_wk, xattn_wv=v_xattn_wv, xattn_wo=v_xattn_wo, ffn_pre_norm=v_ffn_pre_norm, ffn_post_norm=v_ffn_post_norm, ffn_w_up=v_ffn_w_up, ffn_conv_w=v_ffn_conv_w, ffn_w_down=v_ffn_w_down)
    names = list(weights)

    nl, d = mix_pre_norm.shape
    x0, mem0, target = _to_steps(x[0]), mem[0], _to_steps(loss_target[0])
    dp, dc, ds, *_ = _mix_dims(d)
    pg = dp // len(POOL_WINDOWS)
    me = 4 * lax.axis_index("x") + 2 * lax.axis_index("y") + lax.axis_index("c")

    big = ["w_in", "w_out", "xattn_wq", "xattn_wk", "xattn_wv", "xattn_wo", "ffn_w_up", "ffn_w_down"]
    transposed = ("w_in", "ffn_w_up")

    def row_shard(n, a):
        return a.transpose(0, 2, 1) if n in transposed else a

    shards = [row_shard(n, weights[n]).astype(BF16) for n in big]
    taps = ["conf_dw_w", "sconv_w", "ffn_conv_w"]
    tap_slab = _pack([weights[n] for n in taps])
    win0, tap_all = _allgather([shards[0][0:1], tap_slab[None]], "gather_weights0")
    layer_w = [{"w_in": win0[0]}]

    def land(s):
        return lax.empty((N_DEV * s.shape[1], s.shape[2]), BF16)

    first_a = _split_start([s[0] for s in shards[1:6]], [land(s) for s in shards[1:6]], _gather_piece, win0,
                           "gather_start0a")
    first_b = _split_start([s[0] for s in shards[6:]], [land(s) for s in shards[6:]], _gather_piece, first_a[-1],
                           "gather_start0b")
    tap_all = tap_all[0].reshape(N_DEV, *tap_slab.shape)
    tap_parts = [_unpack(tap_all[p], [weights[n] for n in taps]) for p in range(N_DEV)]
    wdw, wsc, wcf = (jnp.concatenate([tap_parts[p][i] for p in range(N_DEV)], axis=-1) for i in range(3))

    def g3(a):
        return a.reshape(a.shape[0], 1, a.shape[-1])

    mbd = jnp.zeros((nl, dp, dp), F32)
    for gi in range(len(POOL_WINDOWS)):
        mbd = mbd.at[:, gi * pg:(gi + 1) * pg, gi * pg:(gi + 1) * pg].set(pool_maps[:, gi])
    mbd = mbd.astype(BF16)
    pre1, post1, pre2, post2, pre3, post3 = (g3(weights[n]) for n in (
        "mix_pre_norm", "mix_post_norm", "xattn_pre_norm", "xattn_post_norm", "ffn_pre_norm", "ffn_post_norm"))
    pscale3, bdw3, lng3, lnb3 = g3(pool_scale), g3(conf_dw_b), g3(conf_ln_g), g3(conf_ln_b)
    memg3 = mem_norm.reshape(1, 1, d)

    def mm(a, b, mode, dt, name, tm=2048, tn=1024, tk=1024, a_outer=True):
        return _matmul(a, b, mode, dt, name, tm=tm, tn=tn, tk=tk, a_outer=a_outer)

    mem_n = _prenorm(mem0, memg3, 0, "mem_norm")
    xs = x0
    h = _prenorm(xs, pre1, 0, "pre_norm0")
    saved = []
    for l in range(nl):
        ps_l = pscale3
        if l + 1 < nl:
            flying = _split_start([s[l + 1] for s in shards], [land(s) for s in shards], _gather_piece,
                                  xs if l else first_b[-1], f"gather_start{l + 1}")
            ps_l = pscale3 + flying[-1][0, 0]
        w = layer_w[l]
        s = {"x": xs, "h": h}
        s["z"] = mm(h, w["w_in"], "nt", F32, f"z{l}", tm=1024, tn=4096)
        s["cat"], s["c"] = _mix_fwd(s["z"], mbd, ps_l, wdw, bdw3, lng3, lnb3, wsc, l, f"mix_fwd{l}")
        if l == 0:
            w.update(zip(big[1:6], _split_wait(first_a, s["cat"], _gather_piece, "gather_wait0a")))
        s["y1"], s["x1"], s["h1"] = _matmul_resnorm(s["cat"], w["w_out"], xs, post1, l, pre2, l, f"y1_{l}", tm=1024)
        s["q"] = mm(s["h1"], w["xattn_wq"], "nn", BF16, f"q{l}")
        s["k"] = mm(mem_n, w["xattn_wk"], "nn", BF16, f"k{l}")
        s["v"] = mm(mem_n, w["xattn_wv"], "nn", BF16, f"v{l}")
        s["o"] = _attn_fwd(s["q"], s["k"], s["v"], f"attn_fwd{l}")
        s["y2"], s["x2"], s["h2"] = _matmul_resnorm(s["o"], w["xattn_wo"], s["x1"], post2, l, pre3, l, f"y2_{l}", tm=1024)
        if l == 0:
            w.update(zip(big[6:], _split_wait(first_b, s["h2"], _gather_piece, "gather_wait0b")))
        s["u"] = mm(s["h2"], w["ffn_w_up"], "nt", F32, f"u{l}", tn=1408, a_outer=False)
        s["a"] = _ffn_act_fwd(s["u"], wcf, l, f"ffn_act{l}")
        if l + 1 < nl:
            s["y3"], xs, h = _matmul_resnorm(s["a"], w["ffn_w_down"], s["x2"], post3, l, pre1, l + 1, f"y3_{l}", tm=1024)
            layer_w.append(dict(zip(big, _split_wait(flying, xs, _gather_piece, f"gather_wait{l + 1}"))))
        saved.append(s)

    last = saved[-1]
    dxn, dy3, dg_post3, loss_lanes = _matmul_loss(
        last["a"], layer_w[-1]["ffn_w_down"], last["x2"], post3, nl - 1, target, "y3_loss", tm=1024)
    loss = lax.psum(loss_lanes[0, 0], ("x", "y", "c"))

    recvs = [lax.empty((max(nl - 1, 1), N_DEV, s.shape[1], d), BF16) for s in shards]
    recv0 = [lax.empty((1, N_DEV, s.shape[1], d), BF16) for s in shards]
    small = {n: [None] * nl for n in names if n not in big and n != "mem_norm"}
    small["ffn_post_norm"][nl - 1] = dg_post3
    dmem_n = jnp.zeros(mem0.shape, F32)
    flying = None
    for l in reversed(range(nl)):
        s, w = saved[l], layer_w[l]
        wc_l = wcf if flying is None else wcf + flying[-1][0, 0]
        gw = {}
        da = mm(dy3, w["ffn_w_down"], "nt", F32, f"da{l}", tn=1408, a_outer=False)
        gw["ffn_w_down"] = mm(s["a"], dy3, "tn", BF16, f"dw_down{l}", tm=1408, tk=2048)
        du, small["ffn_conv_w"][l] = _ffn_act_bwd(s["u"], da, wc_l, l, f"ffn_act_bwd{l}")
        gw["ffn_w_up"] = mm(du, s["h2"], "tn", BF16, f"dw_up{l}", tm=1408, tk=2048)
        dx2, small["ffn_pre_norm"][l], dy2, small["xattn_post_norm"][l] = _matmul_norm_bwd(
            du, w["ffn_w_up"], "nn", dxn, s["x2"], pre3, l, f"dh2_{l}", s["y2"], post2, l)
        do = mm(dy2, w["xattn_wo"], "nt", BF16, f"do{l}")
        gw["xattn_wo"] = mm(s["o"], dy2, "tn", BF16, f"dw_o{l}", tk=4096)
        dq, dk, dv = _attn_bwd(s["q"], s["k"], s["v"], do, f"attn_bwd{l}")
        dkb, dvb = dk.astype(BF16), dv.astype(BF16)
        gw["xattn_wq"] = mm(s["h1"], dq, "tn", BF16, f"dw_q{l}", tk=4096)
        gw["xattn_wk"] = mm(mem_n, dkb, "tn", BF16, f"dw_k{l}")
        gw["xattn_wv"] = mm(mem_n, dvb, "tn", BF16, f"dw_v{l}")
        dmem_n = dmem_n + mm(dkb, w["xattn_wk"], "nt", F32, f"dmem_k{l}") \
            + mm(dvb, w["xattn_wv"], "nt", F32, f"dmem_v{l}")
        pre2_l = pre2
        if l == 0:
            flying0 = _split_start([gw[n] for n in big[2:]], recv0[2:], _scatter_piece(0), dmem_n, "scatter_start0a")
            pre2_l = pre2 + flying0[-1][0, 0]
        dx1, small["xattn_pre_norm"][l], dy1, small["mix_post_norm"][l] = _matmul_norm_bwd(
            dq, w["xattn_wq"], "nt", dx2, s["x1"], pre2_l, l, f"dh1_{l}", s["y1"], post1, l, tm=1024)
        dcat = mm(dy1, w["w_out"], "nt", F32, f"dcat{l}")
        gw["w_out"] = mm(s["cat"], dy1, "tn", BF16, f"dw_out{l}", tk=4096)
        dz, dmbd, dps, dwdw, dbdw, dlng, dlnb, dwsc = _mix_bwd(
            dcat, s["z"], s["c"], mbd, pscale3, wdw, lng3, lnb3, wsc, l, f"mix_bwd{l}")
        small["pool_maps"][l] = jnp.stack([dmbd[gi * pg:(gi + 1) * pg, gi * pg:(gi + 1) * pg]
                                           for gi in range(len(POOL_WINDOWS))])
        small["pool_scale"][l], small["conf_dw_w"][l], small["conf_dw_b"][l] = dps, dwdw, dbdw
        small["conf_ln_g"][l], small["conf_ln_b"][l], small["sconv_w"][l] = dlng, dlnb, dwsc
        gw["w_in"] = mm(dz, s["h"], "tn", BF16, f"dw_in{l}", tm=4096, tk=2048)
        if l > 0:
            dxn, small["mix_pre_norm"][l], dy3, small["ffn_post_norm"][l - 1] = _matmul_norm_bwd(
                dz, w["w_in"], "nn", dx1, s["x"], pre1, l, f"dh{l}", saved[l - 1]["y3"], post3, l - 1)
            if flying is not None:
                recvs = _split_wait(flying, dxn, _scatter_piece(l), f"scatter_wait{l + 1}")
            flying = _split_start([gw[n] for n in big], recvs, _scatter_piece(l - 1), dxn, f"scatter_start{l}")
        else:
            if flying is not None:
                recvs = _split_wait(flying, dx1, _scatter_piece(0), "scatter_wait1")
            flying = _split_start([gw[n] for n in big[:2]], recv0[:2], _scatter_piece(0), dx1, "scatter_start0b")
            dxn, small["mix_pre_norm"][l] = _matmul_norm_bwd(
                dz, w["w_in"], "nn", dx1, s["x"], pre1 + flying[-1][0, 0], l, "dh0")
    grad_x = _from_steps(dxn)[None]
    _, dg_mem = _norm_bwd(jnp.zeros(mem0.shape, F32), dmem_n, mem0, memg3, 0, "norm_bwd_mem")

    small_names = [n for n in names if n not in big]
    partial = {n: (dg_mem.reshape(d) if n == "mem_norm" else
                   jnp.stack([g.reshape(g.shape[-1]) if g.shape[0] == 1 and weights[n].ndim == 2 else g
                              for g in small[n]])) for n in small_names}
    slab = _pack([partial[n] for n in small_names])
    small_fly = _split_start([slab], [lax.empty((N_DEV * slab.shape[0], slab.shape[1]), F32)], _gather_piece, dg_mem,
                             "gather_small_start")
    wmv = {n: [row_shard(n, a[n]) for a in (weights, mom1, mom2)] for n in big}
    upper = {n: _adam_sharded(recv, 1, *wmv[n], 1, nl, f"adam_{n}", after=small_fly[-1])
             for n, recv in zip(big, recvs)} if nl > 1 else {}
    upper_done = sum(r[3][0, 0, :1] for r in upper.values()) + small_fly[-1][0, :1]
    gathered = _split_wait(small_fly, upper_done, _gather_piece, "gather_small_wait")[0].reshape(N_DEV, *slab.shape)
    summed = dict(zip(small_names, _unpack(_sum_sources(gathered, "sum_small_grads"), [partial[n] for n in small_names])))
    grad = {}
    for n in small_names:
        g = summed[n]
        if n in taps:
            width = weights[n].shape[-1]
            g = lax.dynamic_slice_in_dim(g, me * width, width, axis=g.ndim - 1)
        grad[n] = g

    delta, new_m, new_v = {}, {}, {}
    upd = _adam_flat(_pack([weights[n] for n in small_names]), _pack([grad[n] for n in small_names]),
                     _pack([mom1[n] for n in small_names]), _pack([mom2[n] for n in small_names]), "adam_small")
    for out, slab_o in zip((delta, new_m, new_v), upd):
        out.update(zip(small_names, _unpack(slab_o, [weights[n] for n in small_names])))
    done = upper_done + upd[0][0, :1]
    recv0 = _split_wait(flying, done, _scatter_piece(0), "scatter_wait0b") \
        + _split_wait(flying0, done, _scatter_piece(0), "scatter_wait0a")
    for n, recv in zip(big, recv0):
        res = _adam_sharded(recv, 0, *wmv[n], 0, 1, f"adam0_{n}", prev=upper.get(n))
        grad[n], delta[n], new_m[n], new_v[n] = (row_shard(n, r) for r in res)

    return (loss, grad_x, *[grad[n] for n in names], *[delta[n] for n in names],
            *[new_m[n] for n in names], *[new_v[n] for n in names])
```
